```python
import jax, jax.numpy as jnp
from jax import lax
import numpy as np

D_MODEL = 1024
BATCH = 32
SEQ = 2048
DEPTH = 2

MLA_HEADS = 8
MLA_NOPE = 64
MLA_ROPE = 32
MLA_V = 64
MLA_Q_RANK = 768
MLA_KV_RANK = 256
DSA_HEADS = 4
DSA_DIM = 64
IDX_HEADS = 8
IDX_DIM = 32
DSA_TOPK_MAX = 256
RET_HEADS = 4
RET_DK = 64
RET_DV = 64
RET_CHUNK = 128
D_FF = 4 * D_MODEL
ROPE_THETA = 10000.0
EPS = 1e-6
Q_BLOCK = 128

MIX_WIDTH = MLA_HEADS * MLA_V + DSA_HEADS * DSA_DIM + RET_HEADS * RET_DV
MLA_COLS = (MLA_Q_RANK, MLA_KV_RANK, MLA_ROPE)
DSA_COLS = (DSA_HEADS * DSA_DIM, DSA_DIM, DSA_DIM, IDX_HEADS * IDX_DIM, IDX_DIM, IDX_HEADS)
RET_COLS = (RET_HEADS * RET_DK, RET_HEADS * RET_DK, RET_HEADS * RET_DV, RET_HEADS * RET_DV)
SPLITS = MLA_COLS + DSA_COLS + RET_COLS
IN_WIDTH = sum(SPLITS)

kernel_name = 'hymba_mla_dsa_retention_block'


def rmsnorm(x, g):
    xf = x.astype(jnp.float32)
    y = xf * lax.rsqrt(jnp.mean(xf * xf, axis=-1, keepdims=True) + EPS)
    return (y * g.astype(jnp.float32)).astype(x.dtype)


def rope_tables(seq, dim):
    pos = jnp.arange(seq, dtype=jnp.float32)
    inv = ROPE_THETA ** (-jnp.arange(0, dim, 2, dtype=jnp.float32) / dim)
    ang = pos[:, None] * inv[None, :]
    return jnp.cos(ang), jnp.sin(ang)


def apply_rope(x, cos, sin):
    d2 = x.shape[-1] // 2
    x1, x2 = x[..., :d2], x[..., d2:]
    c = cos[:, None, :].astype(x.dtype)
    s = sin[:, None, :].astype(x.dtype)
    return jnp.concatenate([x1 * c - x2 * s, x1 * s + x2 * c], axis=-1)


def to_blocks(t):
    b, s = t.shape[0], t.shape[1]
    return t.reshape((b, s // Q_BLOCK, Q_BLOCK) + t.shape[2:]).swapaxes(0, 1)


def from_blocks(t):
    t = t.swapaxes(0, 1)
    return t.reshape(t.shape[0], t.shape[1] * t.shape[2], -1)


def mla_attention(q_nope, q_rope, k_nope, k_rope, v):
    S = q_nope.shape[1]
    nb = S // Q_BLOCK
    scale = (MLA_NOPE + MLA_ROPE) ** -0.5
    key_pos = jnp.arange(S)
    neg = jnp.finfo(jnp.float32).min

    def block(args):
        qn, qr, start = args
        s = jnp.einsum('bqhd,bkhd->bhqk', qn, k_nope) + jnp.einsum('bqhd,bkd->bhqk', qr, k_rope)
        s = s.astype(jnp.float32) * scale
        qpos = start + jnp.arange(Q_BLOCK)
        causal = key_pos[None, :] <= qpos[:, None]
        p = jax.nn.softmax(jnp.where(causal, s, neg), axis=-1).astype(v.dtype)
        return jnp.einsum('bhqk,bkhd->bqhd', p, v)

    starts = jnp.arange(nb, dtype=jnp.int32) * Q_BLOCK
    out = lax.map(block, (to_blocks(q_nope), to_blocks(q_rope), starts))
    return from_blocks(out)


def dsa_attention(q, k, v, q_idx, k_idx, w_idx, topk):
    S = q.shape[1]
    nb = S // Q_BLOCK
    scale = DSA_DIM ** -0.5
    key_pos = jnp.arange(S)
    neg = jnp.finfo(jnp.float32).min
    gather = jax.vmap(lambda table, idx: table[idx])

    def block(args):
        qb, qib, wb, start = args
        qpos = start + jnp.arange(Q_BLOCK)
        causal = key_pos[None, :] <= qpos[:, None]
        rel = jax.nn.relu(jnp.einsum('bqhd,bkd->bqhk', qib, k_idx))
        score = jnp.einsum('bqh,bqhk->bqk', wb, rel).astype(jnp.float32)
        score = jnp.where(causal[None], score, -jnp.inf)
        _, sel = lax.top_k(score, topk)
        valid = sel <= qpos[None, :, None]
        k_sel = gather(k, sel)
        v_sel = gather(v, sel)
        s = jnp.einsum('bqhd,bqjd->bhqj', qb, k_sel).astype(jnp.float32) * scale
        p = jax.nn.softmax(jnp.where(valid[:, None], s, neg), axis=-1).astype(v.dtype)
        return jnp.einsum('bhqj,bqjd->bqhd', p, v_sel)

    starts = jnp.arange(nb, dtype=jnp.int32) * Q_BLOCK
    out = lax.map(block, (to_blocks(q), to_blocks(q_idx), to_blocks(w_idx), starts))
    return from_blocks(out)


def retention(q, k, v, log_gamma):
    B, S, H, dk = q.shape
    dv = v.shape[-1]
    C = RET_CHUNK
    n = S // C
    qc = q.astype(jnp.float32).reshape(B, n, C, H, dk)
    kc = k.astype(jnp.float32).reshape(B, n, C, H, dk)
    vc = v.astype(jnp.float32).reshape(B, n, C, H, dv)
    pos = jnp.arange(C, dtype=jnp.float32)
    rel = pos[:, None] - pos[None, :]
    decay = jnp.where(rel[None] >= 0, jnp.exp(jnp.maximum(rel, 0.0)[None] * log_gamma[:, None, None]), 0.0)
    inner = jnp.einsum('bnqhd,bnchd->bnhqc', qc, kc) * decay[None, None]
    inner_out = jnp.einsum('bnhqc,bnche->bnqhe', inner, vc)
    zeta = jnp.exp((C - 1.0 - pos)[None, :] * log_gamma[:, None])
    chunk_state = jnp.einsum('bnchd,hc,bnche->bnhde', kc, zeta, vc)
    chunk_decay = jnp.exp(C * log_gamma)[None, :, None, None]

    def step(R, s_i):
        return R * chunk_decay + s_i, R

    R0 = jnp.zeros((B, H, dk, dv), jnp.float32)
    _, R_prev = lax.scan(step, R0, chunk_state.swapaxes(0, 1))
    R_prev = R_prev.swapaxes(0, 1)
    xi = jnp.exp((pos + 1.0)[None, :] * log_gamma[:, None])
    cross = jnp.einsum('bnqhd,bnhde,hq->bnqhe', qc, R_prev, xi)
    return (inner_out + cross).reshape(B, S, H, dv)


def head_groupnorm(x, g):
    mu = jnp.mean(x, axis=-1, keepdims=True)
    xc = x - mu
    var = jnp.mean(xc * xc, axis=-1, keepdims=True)
    return xc * lax.rsqrt(var + EPS) * g


def setup_inputs(seed: int = 0) -> dict:
    key = jax.random.key(seed)
    ks = jax.random.split(key, 16)

    def nrm(k, shape, scale):
        return jax.random.normal(k, shape, jnp.float32) * scale

    def gain(k, shape):
        return 1.0 + 0.05 * jax.random.normal(k, shape, jnp.float32)

    return {
        'x': nrm(ks[0], (BATCH, SEQ, D_MODEL), 1.0),
        'g_mix': gain(ks[1], (DEPTH, D_MODEL)),
        'w_in': nrm(ks[2], (DEPTH, D_MODEL, IN_WIDTH), D_MODEL ** -0.5),
        'g_q': gain(ks[3], (DEPTH, MLA_Q_RANK)),
        'w_uq': nrm(ks[4], (DEPTH, MLA_Q_RANK, MLA_HEADS * (MLA_NOPE + MLA_ROPE)), MLA_Q_RANK ** -0.5),
        'g_kv': gain(ks[5], (DEPTH, MLA_KV_RANK)),
        'w_ukv': nrm(ks[6], (DEPTH, MLA_KV_RANK, MLA_HEADS * (MLA_NOPE + MLA_V)), MLA_KV_RANK ** -0.5),
        'g_ret': gain(ks[7], (DEPTH, RET_HEADS * RET_DV)),
        'w_out': nrm(ks[8], (DEPTH, MIX_WIDTH, D_MODEL), MIX_WIDTH ** -0.5),
        'g_mlp': gain(ks[9], (DEPTH, D_MODEL)),
        'w_ff1': nrm(ks[10], (DEPTH, D_MODEL, D_FF), D_MODEL ** -0.5),
        'w_ff2': nrm(ks[11], (DEPTH, D_FF, D_MODEL), D_FF ** -0.5),
        'g_final': gain(ks[12], (D_MODEL,)),
    }


def reference(x, g_mix, w_in, g_q, w_uq, g_kv, w_ukv, g_ret, w_out, g_mlp, w_ff1, w_ff2, g_final):
    B, S, _ = x.shape
    topk = min(DSA_TOPK_MAX, S // 4)
    cos_r, sin_r = rope_tables(S, MLA_ROPE)
    cos_a, sin_a = rope_tables(S, DSA_DIM)
    cos_i, sin_i = rope_tables(S, IDX_DIM)
    cos_t, sin_t = rope_tables(S, RET_DK)
    log_gamma = jnp.log1p(-jnp.exp2(-5.0 - jnp.arange(RET_HEADS, dtype=jnp.float32)))
    split_at = np.cumsum(SPLITS)[:-1].tolist()
    idx_scale = (IDX_HEADS ** -0.5) * (IDX_DIM ** -0.5)

    for l in range(DEPTH):
        h = rmsnorm(x, g_mix[l])
        (cq, ckv, kr, aq, ak, av, qi, ki, wi, rq, rk, rv, rg) = jnp.split(h @ w_in[l], split_at, axis=-1)

        q = (rmsnorm(cq, g_q[l]) @ w_uq[l]).reshape(B, S, MLA_HEADS, MLA_NOPE + MLA_ROPE)
        q_nope = q[..., :MLA_NOPE]
        q_rope = apply_rope(q[..., MLA_NOPE:], cos_r, sin_r)
        kv = (rmsnorm(ckv, g_kv[l]) @ w_ukv[l]).reshape(B, S, MLA_HEADS, MLA_NOPE + MLA_V)
        k_nope, v_mla = kv[..., :MLA_NOPE], kv[..., MLA_NOPE:]
        k_rope = apply_rope(kr[:, :, None, :], cos_r, sin_r)[:, :, 0]
        o_a = mla_attention(q_nope, q_rope, k_nope, k_rope, v_mla)

        q_b = apply_rope(aq.reshape(B, S, DSA_HEADS, DSA_DIM), cos_a, sin_a)
        k_b = apply_rope(ak[:, :, None, :], cos_a, sin_a)[:, :, 0]
        q_i = apply_rope(qi.reshape(B, S, IDX_HEADS, IDX_DIM), cos_i, sin_i)
        k_i = apply_rope(ki[:, :, None, :], cos_i, sin_i)[:, :, 0]
        o_b = dsa_attention(q_b, k_b, av, q_i, k_i, wi * idx_scale, topk)

        q_c = apply_rope(rq.reshape(B, S, RET_HEADS, RET_DK), cos_t, sin_t)
        k_c = apply_rope(rk.reshape(B, S, RET_HEADS, RET_DK), cos_t, sin_t) * (RET_DK ** -0.5)
        ret = retention(q_c, k_c, rv.reshape(B, S, RET_HEADS, RET_DV), log_gamma)
        ret = head_groupnorm(ret, g_ret[l].reshape(RET_HEADS, RET_DV).astype(jnp.float32))
        o_c = (jax.nn.silu(rg.astype(jnp.float32)) * ret.reshape(B, S, RET_HEADS * RET_DV)).astype(x.dtype)

        x = x + jnp.concatenate([o_a, o_b, o_c], axis=-1) @ w_out[l]

        hf = rmsnorm(x, g_mlp[l])
        x = x + jnp.square(jax.nn.relu(hf @ w_ff1[l])) @ w_ff2[l]

    return rmsnorm(x, g_final)
```

```python
import functools

import numpy as np
import jax
import jax.numpy as jnp
from jax import lax
from jax.experimental import pallas as pl
from jax.experimental.pallas import tpu as pltpu

D_MODEL = 1024
DEPTH = 2
MLA_HEADS = 8
MLA_NOPE = 64
MLA_ROPE = 32
MLA_V = 64
MLA_Q_RANK = 768
MLA_KV_RANK = 256
DSA_HEADS = 4
DSA_DIM = 64
IDX_HEADS = 8
IDX_DIM = 32
DSA_TOPK_MAX = 256
RET_HEADS = 4
RET_DK = 64
RET_DV = 64
D_FF = 4 * D_MODEL
ROPE_THETA = 10000.0
EPS = 1e-6

MLA_COLS = (MLA_Q_RANK, MLA_KV_RANK, MLA_ROPE)
DSA_COLS = (DSA_HEADS * DSA_DIM, DSA_DIM, DSA_DIM, IDX_HEADS * IDX_DIM, IDX_DIM, IDX_HEADS)
RET_COLS = (RET_HEADS * RET_DK, RET_HEADS * RET_DK, RET_HEADS * RET_DV, RET_HEADS * RET_DV)
SPLITS = MLA_COLS + DSA_COLS + RET_COLS
_OFF = np.concatenate([[0], np.cumsum(SPLITS)]).tolist()
(O_CQ, O_CKV, O_KR, O_AQ, O_AK, O_AV, O_QI, O_KI, O_WI, O_RQ, O_RK, O_RV, O_RG, O_END) = _OFF

LANE = 128
VMEM_LIMIT = 56 * 1024 * 1024

BF16 = jnp.bfloat16
F32 = jnp.float32
NEG = -1e30
INT_MIN = np.int32(-2**31)
INT_MIN1 = np.int32(-2**31 + 1)

N_CQ = 0
N_CKV = N_CQ + MLA_Q_RANK
N_KR = N_CKV + MLA_KV_RANK
N_AK = N_KR + LANE
N_KI = N_AK + LANE
N_RK = N_KI + LANE
N_NAT = N_RK + RET_HEADS * LANE
T_AQ = 0
T_QI = T_AQ + DSA_HEADS * DSA_DIM
T_AV = T_QI + IDX_HEADS * IDX_DIM
T_RQ = T_AV + DSA_DIM
T_RV = T_RQ + RET_HEADS * RET_DK
T_RG = T_RV + RET_HEADS * RET_DV
T_WI = T_RG + RET_HEADS * RET_DV
N_TR = T_WI + IDX_HEADS

TM_PROJ = 256
TQ_MLA = 256
TQ_DSA = 256
CK_DSA = 128
C_RET = 256
TM_MLP = 512
FF_CHUNK = 1024

_NT = (((1,), (1,)), ((), ()))


def _rms(x, g):
    return x * lax.rsqrt(jnp.mean(x * x, axis=-1, keepdims=True) + EPS) * g


def _dot(a, b):
    return jnp.dot(a, b, preferred_element_type=F32)


def _dot_nt(a, b):
    return lax.dot_general(a, b, _NT, preferred_element_type=F32)


def _rope_rows(x1, x2, c, s):
    return x1 * c - x2 * s, x1 * s + x2 * c


def _rope_lanes(x, c, s_signed, half, first_lo):
    lane = lax.broadcasted_iota(jnp.int32, x.shape, 1)
    first = (lane >= first_lo) & (lane < first_lo + half)
    rot = jnp.where(first, pltpu.roll(x, LANE - half, 1), pltpu.roll(x, half, 1))
    return x * c + rot * s_signed


def _proj_kernel(x_ref, gmix_ref, gq_ref, gkv_ref, wnat_ref, wtr_ref, wuq_ref, wuk_ref, wuv_ref,
                 tabt_ref, tabn_ref,
                 qat_ref, ka_ref, vat_ref, qit_ref, wit_ref, ki_ref, qbt_ref, kb_ref, vbt_ref,
                 rqt_ref, rk_ref, rvt_ref, rgt_ref):
    tm = x_ref.shape[0]
    hb = _rms(x_ref[...], gmix_ref[...]).astype(BF16)

    c64t = tabt_ref[0:32, :]
    s64t = tabt_ref[32:64, :]
    c32t = tabt_ref[64:80, :]
    s32t = tabt_ref[80:96, :]

    cq = _dot(hb, wnat_ref[:, N_CQ:N_CQ + MLA_Q_RANK])
    nq = _rms(cq, gq_ref[...]).astype(BF16)
    qt = _dot_nt(wuq_ref[...], nq)
    scale_a = (MLA_NOPE + MLA_ROPE) ** -0.5
    for h in range(MLA_HEADS):
        r0 = h * LANE
        qat_ref[r0:r0 + 64, :] = (qt[r0:r0 + 64, :] * scale_a).astype(BF16)
        o1, o2 = _rope_rows(qt[r0 + 64:r0 + 80, :], qt[r0 + 80:r0 + 96, :], c32t, s32t)
        qat_ref[r0 + 64:r0 + 80, :] = (o1 * scale_a).astype(BF16)
        qat_ref[r0 + 80:r0 + 96, :] = (o2 * scale_a).astype(BF16)
        qat_ref[r0 + 96:r0 + 128, :] = jnp.zeros((32, tm), BF16)

    ckv = _dot(hb, wnat_ref[:, N_CKV:N_CKV + MLA_KV_RANK])
    nkv = _rms(ckv, gkv_ref[...]).astype(BF16)
    kn = _dot(nkv, wuk_ref[...])
    krp = _dot(hb, wnat_ref[:, N_KR:N_KR + LANE])
    krp = _rope_lanes(krp, tabn_ref[0], tabn_ref[1], 16, 64)
    for h in range(MLA_HEADS):
        ka_ref[:, h * LANE:(h + 1) * LANE] = (kn[:, h * LANE:(h + 1) * LANE] + krp).astype(BF16)
    vt = _dot_nt(wuv_ref[...], nkv).astype(BF16)
    for j in range(tm // LANE):
        vat_ref[j] = vt[:, j * LANE:(j + 1) * LANE]

    akp = _dot(hb, wnat_ref[:, N_AK:N_AK + LANE])
    kb_ref[...] = _rope_lanes(akp, tabn_ref[2], tabn_ref[3], 32, 0).astype(BF16)
    kip = _dot(hb, wnat_ref[:, N_KI:N_KI + LANE])
    ki_ref[...] = _rope_lanes(kip, tabn_ref[4], tabn_ref[5], 16, 0).astype(BF16)

    for h in range(RET_HEADS):
        rkp = _dot(hb, wnat_ref[:, N_RK + h * LANE:N_RK + (h + 1) * LANE])
        rkp = _rope_lanes(rkp, tabn_ref[2], tabn_ref[3], 32, 0) * (RET_DK ** -0.5)
        rk_ref[:, h * LANE:(h + 1) * LANE] = rkp.astype(BF16)

    aqt = _dot_nt(wtr_ref[T_AQ:T_AQ + 256, :], hb)
    scale_b = DSA_DIM ** -0.5
    for h in range(DSA_HEADS):
        o1, o2 = _rope_rows(aqt[h * 64:h * 64 + 32, :], aqt[h * 64 + 32:h * 64 + 64, :], c64t, s64t)
        qbt_ref[h * LANE:h * LANE + 32, :] = (o1 * scale_b).astype(BF16)
        qbt_ref[h * LANE + 32:h * LANE + 64, :] = (o2 * scale_b).astype(BF16)
        qbt_ref[h * LANE + 64:(h + 1) * LANE, :] = jnp.zeros((64, tm), BF16)

    qit = _dot_nt(wtr_ref[T_QI:T_QI + 256, :], hb)
    for h in range(IDX_HEADS):
        o1, o2 = _rope_rows(qit[h * 32:h * 32 + 16, :], qit[h * 32 + 16:h * 32 + 32, :], c32t, s32t)
        qit_ref[h * LANE:h * LANE + 16, :] = o1.astype(BF16)
        qit_ref[h * LANE + 16:h * LANE + 32, :] = o2.astype(BF16)
        qit_ref[h * LANE + 32:(h + 1) * LANE, :] = jnp.zeros((96, tm), BF16)

    idx_scale = (IDX_HEADS ** -0.5) * (IDX_DIM ** -0.5)
    wit_ref[...] = _dot_nt(wtr_ref[T_WI:T_WI + IDX_HEADS, :], hb) * idx_scale

    avt = _dot_nt(wtr_ref[T_AV:T_AV + DSA_DIM, :], hb).astype(BF16)
    for j in range(tm // LANE):
        vbt_ref[j] = avt[:, j * LANE:(j + 1) * LANE]

    rqt = _dot_nt(wtr_ref[T_RQ:T_RQ + 256, :], hb)
    for h in range(RET_HEADS):
        o1, o2 = _rope_rows(rqt[h * 64:h * 64 + 32, :], rqt[h * 64 + 32:h * 64 + 64, :], c64t, s64t)
        rqt_ref[h * LANE:h * LANE + 32, :] = o1.astype(BF16)
        rqt_ref[h * LANE + 32:h * LANE + 64, :] = o2.astype(BF16)
        rqt_ref[h * LANE + 64:(h + 1) * LANE, :] = jnp.zeros((64, tm), BF16)

    rvt_ref[...] = _dot_nt(wtr_ref[T_RV:T_RV + 256, :], hb).astype(BF16)
    rgt_ref[...] = _dot_nt(wtr_ref[T_RG:T_RG + 256, :], hb)


def _full(shape):
    nd = len(shape)
    return pl.BlockSpec(shape, lambda b, i: (0,) * nd)


def _proj(x, gmix, gq, gkv, wnat, wtr, wuq, wuk, wuv, tabt, tabn):
    B, S, D = x.shape
    tm = TM_PROJ
    nch = tm // LANE
    outs = [
        (jax.ShapeDtypeStruct((B, MLA_HEADS * LANE, S), BF16), pl.BlockSpec((None, MLA_HEADS * LANE, tm), lambda b, i: (b, 0, i))),
        (jax.ShapeDtypeStruct((B, S, MLA_HEADS * LANE), BF16), pl.BlockSpec((None, tm, MLA_HEADS * LANE), lambda b, i: (b, i, 0))),
        (jax.ShapeDtypeStruct((B, S // LANE, MLA_HEADS * MLA_V, LANE), BF16), pl.BlockSpec((None, nch, MLA_HEADS * MLA_V, LANE), lambda b, i: (b, i, 0, 0))),
        (jax.ShapeDtypeStruct((B, IDX_HEADS * LANE, S), BF16), pl.BlockSpec((None, IDX_HEADS * LANE, tm), lambda b, i: (b, 0, i))),
        (jax.ShapeDtypeStruct((B, IDX_HEADS, S), F32), pl.BlockSpec((None, IDX_HEADS, tm), lambda b, i: (b, 0, i))),
        (jax.ShapeDtypeStruct((B, S, LANE), BF16), pl.BlockSpec((None, tm, LANE), lambda b, i: (b, i, 0))),
        (jax.ShapeDtypeStruct((B, DSA_HEADS * LANE, S), BF16), pl.BlockSpec((None, DSA_HEADS * LANE, tm), lambda b, i: (b, 0, i))),
        (jax.ShapeDtypeStruct((B, S, LANE), BF16), pl.BlockSpec((None, tm, LANE), lambda b, i: (b, i, 0))),
        (jax.ShapeDtypeStruct((B, S // LANE, DSA_DIM, LANE), BF16), pl.BlockSpec((None, nch, DSA_DIM, LANE), lambda b, i: (b, i, 0, 0))),
        (jax.ShapeDtypeStruct((B, RET_HEADS * LANE, S), BF16), pl.BlockSpec((None, RET_HEADS * LANE, tm), lambda b, i: (b, 0, i))),
        (jax.ShapeDtypeStruct((B, S, RET_HEADS * LANE), BF16), pl.BlockSpec((None, tm, RET_HEADS * LANE), lambda b, i: (b, i, 0))),
        (jax.ShapeDtypeStruct((B, RET_HEADS * RET_DV, S), BF16), pl.BlockSpec((None, RET_HEADS * RET_DV, tm), lambda b, i: (b, 0, i))),
        (jax.ShapeDtypeStruct((B, RET_HEADS * RET_DV, S), F32), pl.BlockSpec((None, RET_HEADS * RET_DV, tm), lambda b, i: (b, 0, i))),
    ]
    in_specs = [
        pl.BlockSpec((None, tm, D), lambda b, i: (b, i, 0)),
        _full(gmix.shape), _full(gq.shape), _full(gkv.shape),
        _full(wnat.shape), _full(wtr.shape), _full(wuq.shape), _full(wuk.shape), _full(wuv.shape),
        pl.BlockSpec((tabt.shape[0], tm), lambda b, i: (0, i)),
        pl.BlockSpec((tabn.shape[0], tm, LANE), lambda b, i: (0, i, 0)),
    ]
    return pl.pallas_call(
        _proj_kernel,
        out_shape=[o[0] for o in outs],
        grid=(B, S // tm),
        in_specs=in_specs,
        out_specs=[o[1] for o in outs],
        compiler_params=pltpu.CompilerParams(
            dimension_semantics=("parallel", "parallel"), vmem_limit_bytes=VMEM_LIMIT),
        name="proj",
    )(x, gmix, gq, gkv, wnat, wtr, wuq, wuk, wuv, tabt, tabn)


def _mla_kernel(q_ref, k_ref, v_ref, o_ref):
    tq = q_ref.shape[1]
    tk = tq
    i = pl.program_id(2)
    q = q_ref[...]
    nv = tk // LANE

    def step(kt, carry, masked):
        m, l, acc = carry
        k = k_ref[pl.ds(pl.multiple_of(kt * tk, tk), tk), :]
        s = _dot(k, q)
        if masked:
            kidx = lax.broadcasted_iota(jnp.int32, (tk, tq), 0)
            qidx = lax.broadcasted_iota(jnp.int32, (tk, tq), 1)
            s = jnp.where(kidx <= qidx, s, NEG)
        m_new = jnp.maximum(m, jnp.max(s, axis=0, keepdims=True))
        alpha = jnp.exp(m - m_new)
        p = jnp.exp(s - m_new)
        l = alpha * l + jnp.sum(p, axis=0, keepdims=True)
        vt = jnp.concatenate([v_ref[kt * nv + j] for j in range(nv)], axis=1)
        acc = alpha * acc + _dot(vt, p.astype(BF16))
        return m_new, l, acc

    init = (jnp.full((1, tq), NEG, F32), jnp.zeros((1, tq), F32), jnp.zeros((MLA_V, tq), F32))
    carry = lax.fori_loop(0, i, lambda kt, c: step(kt, c, False), init)
    m, l, acc = step(i, carry, True)
    o_ref[...] = (acc / l).astype(BF16)


def _mla(qat, ka, vat):
    B, _, S = qat.shape
    tq = TQ_MLA
    return pl.pallas_call(
        _mla_kernel,
        out_shape=jax.ShapeDtypeStruct((B, MLA_HEADS * MLA_V, S), BF16),
        grid=(B, MLA_HEADS, S // tq),
        in_specs=[
            pl.BlockSpec((None, LANE, tq), lambda b, h, i: (b, h, i)),
            pl.BlockSpec((None, S, LANE), lambda b, h, i: (b, 0, h)),
            pl.BlockSpec((None, S // LANE, MLA_V, LANE), lambda b, h, i: (b, 0, h, 0)),
        ],
        out_specs=pl.BlockSpec((None, MLA_V, tq), lambda b, h, i: (b, h, i)),
        compiler_params=pltpu.CompilerParams(
            dimension_semantics=("parallel", "parallel", "arbitrary"), vmem_limit_bytes=VMEM_LIMIT),
        name="mla",
    )(qat, ka, vat)


def _count_rows(mask):
    r, t = mask.shape
    return jnp.sum(mask.astype(jnp.int32).reshape(r // 8, 8, t), axis=0)


def _dsa_kernel(topk, qit_ref, wit_ref, ki_ref, qbt_ref, kb_ref, vbt_ref, o_ref,
                key_sc, bias_sc, thr_sc, jsel_sc):
    tq = o_ref.shape[1]
    ck = CK_DSA
    i = pl.program_id(1)
    nck = (i + 1) * (tq // ck)
    qpos = i * tq + lax.broadcasted_iota(jnp.int32, (ck, tq), 1)
    krow = lax.broadcasted_iota(jnp.int32, (ck, tq), 0)

    def chunk(ref, c):
        return ref[pl.ds(pl.multiple_of(c * ck, ck), ck), :]

    def score_body(c, _):
        kc = chunk(ki_ref, c)
        acc = jnp.zeros((ck, tq), F32)
        for h in range(IDX_HEADS):
            rel = _dot(kc, qit_ref[h * LANE:(h + 1) * LANE, :])
            acc = acc + wit_ref[h:h + 1, :] * jnp.maximum(rel, 0.0)
        acc = jnp.where(acc == 0.0, 0.0, acc)
        bits = pltpu.bitcast(acc, jnp.int32)
        key = jnp.where(bits < 0, bits ^ jnp.int32(0x7FFFFFFF), bits)
        key = jnp.where(c * ck + krow <= qpos, key, INT_MIN)
        key_sc[pl.ds(pl.multiple_of(c * ck, ck), ck), :] = key
        return 0

    lax.fori_loop(0, nck, score_body, 0)

    def count(pred):
        def body(c, a):
            return a + _count_rows(pred(chunk(key_sc, c), c))
        part = lax.fori_loop(0, nck, body, jnp.zeros((8, tq), jnp.int32))
        return jnp.sum(part, axis=0, keepdims=True)

    thr_sc[...] = jnp.full((1, tq), INT_MIN1, jnp.int32)
    jsel_sc[...] = jnp.full((1, tq), key_sc.shape[0], jnp.int32)

    @pl.when(i * tq >= topk)
    def _():
        cand0 = jnp.zeros((1, tq), jnp.int32)
        ans0 = jnp.where(count(lambda k, c: k >= cand0) >= topk, cand0, INT_MIN)

        def bit_body(t, ans):
            cand = ans + jnp.left_shift(jnp.int32(1), 30 - t)
            return jnp.where(count(lambda k, c: k >= cand) >= topk, cand, ans)

        thr = lax.fori_loop(0, 31, bit_body, ans0)
        thr_sc[...] = thr
        n_ge = count(lambda k, c: k >= thr)
        excess = jnp.max(n_ge) > topk

        @pl.when(excess)
        def _():
            need = topk - count(lambda k, c: k > thr)

            def jbit(t, j):
                cand = j + jnp.left_shift(jnp.int32(1), 10 - t)
                n = count(lambda k, c: (k == thr) & (c * ck + krow < cand))
                return jnp.where(n < need, cand, j)

            jsel_sc[...] = lax.fori_loop(0, 11, jbit, jnp.zeros((1, tq), jnp.int32))

    thr = thr_sc[...]
    jsel = jsel_sc[...]

    def bias_body(c, _):
        k = chunk(key_sc, c)
        sel = (k > thr) | ((k == thr) & (c * ck + krow <= jsel))
        bias_sc[pl.ds(pl.multiple_of(c * ck, ck), ck), :] = jnp.where(sel, 0.0, NEG)
        return 0

    lax.fori_loop(0, nck, bias_body, 0)

    for h in range(DSA_HEADS):
        qh = qbt_ref[h * LANE:(h + 1) * LANE, :]

        def att_body(c, carry):
            m, l, acc = carry
            s = _dot(chunk(kb_ref, c), qh) + chunk(bias_sc, c)
            m_new = jnp.maximum(m, jnp.max(s, axis=0, keepdims=True))
            alpha = jnp.exp(m - m_new)
            p = jnp.exp(s - m_new)
            l = alpha * l + jnp.sum(p, axis=0, keepdims=True)
            acc = alpha * acc + _dot(vbt_ref[c], p.astype(BF16))
            return m_new, l, acc

        init = (jnp.full((1, tq), NEG, F32), jnp.zeros((1, tq), F32), jnp.zeros((DSA_DIM, tq), F32))
        m, l, acc = lax.fori_loop(0, nck, att_body, init)
        o_ref[h * DSA_DIM:(h + 1) * DSA_DIM, :] = (acc / l).astype(BF16)


def _dsa(qit, wit, ki, qbt, kb, vbt, topk):
    B, _, S = qit.shape
    tq = TQ_DSA
    assert topk % tq == 0 or topk >= S, "query tiles must not straddle the top-k boundary"
    return pl.pallas_call(
        functools.partial(_dsa_kernel, topk),
        out_shape=jax.ShapeDtypeStruct((B, DSA_HEADS * DSA_DIM, S), BF16),
        grid=(B, S // tq),
        in_specs=[
            pl.BlockSpec((None, IDX_HEADS * LANE, tq), lambda b, i: (b, 0, i)),
            pl.BlockSpec((None, IDX_HEADS, tq), lambda b, i: (b, 0, i)),
            pl.BlockSpec((None, S, LANE), lambda b, i: (b, 0, 0)),
            pl.BlockSpec((None, DSA_HEADS * LANE, tq), lambda b, i: (b, 0, i)),
            pl.BlockSpec((None, S, LANE), lambda b, i: (b, 0, 0)),
            pl.BlockSpec((None, S // LANE, DSA_DIM, LANE), lambda b, i: (b, 0, 0, 0)),
        ],
        out_specs=pl.BlockSpec((None, DSA_HEADS * DSA_DIM, tq), lambda b, i: (b, 0, i)),
        scratch_shapes=[
            pltpu.VMEM((S, tq), jnp.int32),
            pltpu.VMEM((S, tq), F32),
            pltpu.VMEM((1, tq), jnp.int32),
            pltpu.VMEM((1, tq), jnp.int32),
        ],
        compiler_params=pltpu.CompilerParams(
            dimension_semantics=("parallel", "arbitrary"), vmem_limit_bytes=VMEM_LIMIT),
        name="dsa",
    )(qit, wit, ki, qbt, kb, vbt)


def _ret_kernel(rqt_ref, rk_ref, rvt_ref, rgt_ref, dt_ref, xi_ref, zeta_ref, cdec_ref, gret_ref,
                o_ref, state_sc):
    n = pl.program_id(1)

    @pl.when(n == 0)
    def _():
        state_sc[...] = jnp.zeros_like(state_sc)

    for h in range(RET_HEADS):
        qt = rqt_ref[h * LANE:(h + 1) * LANE, :]
        k = rk_ref[:, h * LANE:(h + 1) * LANE]
        vt = rvt_ref[h * RET_DV:(h + 1) * RET_DV, :]
        inner_t = _dot(k, qt) * dt_ref[h]
        rt = state_sc[h]
        qx = (qt.astype(F32) * xi_ref[h:h + 1, :]).astype(BF16)
        out_t = _dot(vt, inner_t.astype(BF16)) + _dot(rt.astype(BF16), qx)
        vz = (vt.astype(F32) * zeta_ref[h:h + 1, :]).astype(BF16)
        state_sc[h] = rt * cdec_ref[h] + _dot(vz, k)

        mu = jnp.mean(out_t, axis=0, keepdims=True)
        xc = out_t - mu
        var = jnp.mean(xc * xc, axis=0, keepdims=True)
        y = xc * lax.rsqrt(var + EPS) * gret_ref[h * RET_DV:(h + 1) * RET_DV, :]
        g = rgt_ref[h * RET_DV:(h + 1) * RET_DV, :]
        o_ref[h * RET_DV:(h + 1) * RET_DV, :] = (g * jax.nn.sigmoid(g) * y).astype(BF16)


def _retention(rqt, rk, rvt, rgt, dt, xi, zeta, cdec, gret):
    B, _, S = rqt.shape
    c = C_RET
    return pl.pallas_call(
        _ret_kernel,
        out_shape=jax.ShapeDtypeStruct((B, RET_HEADS * RET_DV, S), BF16),
        grid=(B, S // c),
        in_specs=[
            pl.BlockSpec((None, RET_HEADS * LANE, c), lambda b, n: (b, 0, n)),
            pl.BlockSpec((None, c, RET_HEADS * LANE), lambda b, n: (b, n, 0)),
            pl.BlockSpec((None, RET_HEADS * RET_DV, c), lambda b, n: (b, 0, n)),
            pl.BlockSpec((None, RET_HEADS * RET_DV, c), lambda b, n: (b, 0, n)),
            _full(dt.shape), _full(xi.shape), _full(zeta.shape), _full(cdec.shape), _full(gret.shape),
        ],
        out_specs=pl.BlockSpec((None, RET_HEADS * RET_DV, c), lambda b, n: (b, 0, n)),
        scratch_shapes=[pltpu.VMEM((RET_HEADS, RET_DV, LANE), F32)],
        compiler_params=pltpu.CompilerParams(
            dimension_semantics=("parallel", "arbitrary"), vmem_limit_bytes=VMEM_LIMIT),
        name="retention",
    )(rqt, rk, rvt, rgt, dt, xi, zeta, cdec, gret)


def _mlp_kernel(final, x_ref, oat_ref, obt_ref, oct_ref, wout_ref, gmlp_ref, w1_ref, w2_ref, gfin_ref,
                y_ref):
    def tr(ref):
        return ref[...].astype(F32).T.astype(BF16)

    na = MLA_HEADS * MLA_V
    nb = DSA_HEADS * DSA_DIM
    attn = (_dot(tr(oat_ref), wout_ref[0:na, :])
            + _dot(tr(obt_ref), wout_ref[na:na + nb, :])
            + _dot(tr(oct_ref), wout_ref[na + nb:, :]))
    x1 = x_ref[...] + attn
    hf = _rms(x1, gmlp_ref[...]).astype(BF16)
    acc = x1
    for f in range(D_FF // FF_CHUNK):
        u = jnp.maximum(_dot(hf, w1_ref[:, f * FF_CHUNK:(f + 1) * FF_CHUNK]), 0.0)
        acc = acc + _dot((u * u).astype(BF16), w2_ref[f * FF_CHUNK:(f + 1) * FF_CHUNK, :])
    if final:
        acc = _rms(acc, gfin_ref[...])
    y_ref[...] = acc


def _mlp(x, oat, obt, oct, wout, gmlp, w1, w2, gfin, final):
    B, S, D = x.shape
    tm = TM_MLP

    def const(shape):
        nd = len(shape)
        return pl.BlockSpec(shape, lambda b, i: (0,) * nd, pipeline_mode=pl.Buffered(1))

    return pl.pallas_call(
        functools.partial(_mlp_kernel, final),
        out_shape=jax.ShapeDtypeStruct((B, S, D), F32),
        grid=(B, S // tm),
        in_specs=[
            pl.BlockSpec((None, tm, D), lambda b, i: (b, i, 0)),
            pl.BlockSpec((None, oat.shape[1], tm), lambda b, i: (b, 0, i)),
            pl.BlockSpec((None, obt.shape[1], tm), lambda b, i: (b, 0, i)),
            pl.BlockSpec((None, oct.shape[1], tm), lambda b, i: (b, 0, i)),
            const(wout.shape), const(gmlp.shape), const(w1.shape), const(w2.shape), const(gfin.shape),
        ],
        out_specs=pl.BlockSpec((None, tm, D), lambda b, i: (b, i, 0)),
        compiler_params=pltpu.CompilerParams(
            dimension_semantics=("parallel", "parallel"), vmem_limit_bytes=VMEM_LIMIT),
        name="mlp",
    )(x, oat, obt, oct, wout, gmlp, w1, w2, gfin)


def _rope_tables(seq, dim):
    pos = jnp.arange(seq, dtype=F32)
    inv = ROPE_THETA ** (-jnp.arange(0, dim, 2, dtype=F32) / dim)
    ang = pos[:, None] * inv[None, :]
    return jnp.cos(ang), jnp.sin(ang)


def _lane_table(c, s, lo, seq):
    half = c.shape[1]
    ct = jnp.zeros((seq, LANE), F32).at[:, lo:lo + 2 * half].set(jnp.concatenate([c, c], axis=1))
    st = jnp.zeros((seq, LANE), F32).at[:, lo:lo + 2 * half].set(jnp.concatenate([-s, s], axis=1))
    return ct, st


def _pad_cols(w, width):
    return jnp.pad(w, ((0, 0), (0, width - w.shape[1])))


def _prep_layer(w_in, w_uq, w_ukv):
    z = lambda n: jnp.zeros((D_MODEL, n), F32)
    col = lambda a, b: w_in[:, a:b]
    nat = [col(O_CQ, O_CKV), col(O_CKV, O_KR),
           jnp.concatenate([z(64), col(O_KR, O_AQ), z(32)], axis=1),
           _pad_cols(col(O_AK, O_AV), LANE),
           _pad_cols(col(O_KI, O_WI), LANE)]
    for h in range(RET_HEADS):
        nat.append(_pad_cols(col(O_RK + h * RET_DK, O_RK + (h + 1) * RET_DK), LANE))
    wnat = jnp.concatenate(nat, axis=1).astype(BF16)
    wtr = jnp.concatenate([col(O_AQ, O_AK), col(O_QI, O_KI), col(O_AV, O_QI), col(O_RQ, O_RK),
                           col(O_RV, O_RG), col(O_RG, O_END), col(O_WI, O_RQ)], axis=1).T.astype(BF16)
    dq = MLA_NOPE + MLA_ROPE
    wuq = jnp.pad(w_uq.reshape(MLA_Q_RANK, MLA_HEADS, dq), ((0, 0), (0, 0), (0, LANE - dq)))
    wuq = wuq.reshape(MLA_Q_RANK, MLA_HEADS * LANE).T.astype(BF16)
    wkv = w_ukv.reshape(MLA_KV_RANK, MLA_HEADS, MLA_NOPE + MLA_V)
    wuk = jnp.pad(wkv[:, :, :MLA_NOPE], ((0, 0), (0, 0), (0, LANE - MLA_NOPE)))
    wuk = wuk.reshape(MLA_KV_RANK, MLA_HEADS * LANE).astype(BF16)
    wuv = wkv[:, :, MLA_NOPE:].reshape(MLA_KV_RANK, MLA_HEADS * MLA_V).T.astype(BF16)
    return wnat, wtr, wuq, wuk, wuv


def kernel(x, g_mix, w_in, g_q, w_uq, g_kv, w_ukv, g_ret, w_out, g_mlp, w_ff1, w_ff2, g_final):
    B, S, _ = x.shape
    topk = min(DSA_TOPK_MAX, S // 4)

    cos32, sin32 = _rope_tables(S, 32)
    cos64, sin64 = _rope_tables(S, 64)
    tabt = jnp.concatenate([cos64.T, sin64.T, cos32.T, sin32.T], axis=0)
    tabn = jnp.stack(_lane_table(cos32, sin32, 64, S) + _lane_table(cos64, sin64, 0, S)
                     + _lane_table(cos32, sin32, 0, S))

    log_gamma = jnp.log1p(-jnp.exp2(-5.0 - jnp.arange(RET_HEADS, dtype=F32)))
    c = C_RET
    pos = jnp.arange(c, dtype=F32)
    rel = pos[None, :] - pos[:, None]
    dt = jnp.where(rel[None] >= 0, jnp.exp(jnp.maximum(rel, 0.0)[None] * log_gamma[:, None, None]), 0.0)
    xi = jnp.exp((pos + 1.0)[None, :] * log_gamma[:, None])
    zeta = jnp.exp((c - 1.0 - pos)[None, :] * log_gamma[:, None])
    cdec = jnp.broadcast_to(jnp.exp(c * log_gamma)[:, None, None], (RET_HEADS, RET_DV, LANE))

    for l in range(DEPTH):
        wnat, wtr, wuq, wuk, wuv = _prep_layer(w_in[l], w_uq[l], w_ukv[l])
        (qat, ka, vat, qit, wit, ki, qbt, kb, vbt, rqt, rk, rvt, rgt) = _proj(
            x, g_mix[l][None, :], g_q[l][None, :], g_kv[l][None, :], wnat, wtr, wuq, wuk, wuv, tabt, tabn)
        oat = _mla(qat, ka, vat)
        obt = _dsa(qit, wit, ki, qbt, kb, vbt, topk)
        gret = jnp.broadcast_to(g_ret[l][:, None], (RET_HEADS * RET_DV, c))
        oct = _retention(rqt, rk, rvt, rgt, dt, xi, zeta, cdec, gret)
        x = _mlp(x, oat, obt, oct, w_out[l].astype(BF16), g_mlp[l][None, :],
                 w_ff1[l].astype(BF16), w_ff2[l].astype(BF16), g_final[None, :], l == DEPTH - 1)
    return x
```

```python
import functools

import numpy as np
import jax
import jax.numpy as jnp
from jax import lax
from jax.experimental import pallas as pl
from jax.experimental.pallas import tpu as pltpu

D_MODEL = 1024
DEPTH = 2
MLA_HEADS = 8
MLA_NOPE = 64
MLA_ROPE = 32
MLA_V = 64
MLA_Q_RANK = 768
MLA_KV_RANK = 256
DSA_HEADS = 4
DSA_DIM = 64
IDX_HEADS = 8
IDX_DIM = 32
DSA_TOPK_MAX = 256
RET_HEADS = 4
RET_DK = 64
RET_DV = 64
D_FF = 4 * D_MODEL
ROPE_THETA = 10000.0
EPS = 1e-6

MLA_COLS = (MLA_Q_RANK, MLA_KV_RANK, MLA_ROPE)
DSA_COLS = (DSA_HEADS * DSA_DIM, DSA_DIM, DSA_DIM, IDX_HEADS * IDX_DIM, IDX_DIM, IDX_HEADS)
RET_COLS = (RET_HEADS * RET_DK, RET_HEADS * RET_DK, RET_HEADS * RET_DV, RET_HEADS * RET_DV)
SPLITS = MLA_COLS + DSA_COLS + RET_COLS
_OFF = np.concatenate([[0], np.cumsum(SPLITS)]).tolist()
(O_CQ, O_CKV, O_KR, O_AQ, O_AK, O_AV, O_QI, O_KI, O_WI, O_RQ, O_RK, O_RV, O_RG, O_END) = _OFF

LANE = 128
VMEM_LIMIT = 56 * 1024 * 1024

BF16 = jnp.bfloat16
F32 = jnp.float32
NEG = -1e30
INT_MIN = np.int32(-2**31)

N_CQ = 0
N_CKV = N_CQ + MLA_Q_RANK
N_KR = N_CKV + MLA_KV_RANK
N_AK = N_KR + LANE
N_KI = N_AK + LANE
N_RK = N_KI + LANE
N_NAT = N_RK + RET_HEADS * LANE
T_AQ = 0
T_QI = T_AQ + DSA_HEADS * DSA_DIM
T_AV = T_QI + IDX_HEADS * IDX_DIM
T_RQ = T_AV + DSA_DIM
T_RV = T_RQ + RET_HEADS * RET_DK
T_RG = T_RV + RET_HEADS * RET_DV
T_WI = T_RG + RET_HEADS * RET_DV
N_TR = T_WI + IDX_HEADS

TM_PROJ = 256
TQ_MLA = 256
G_MLA = 8
TQ_DSA = 256
CK_DSA = 128
C_RET = 256
TM_MLP = 512
FF_CHUNK = 1024

_NT = (((1,), (1,)), ((), ()))


def _rms(x, g):
    return x * lax.rsqrt(jnp.mean(x * x, axis=-1, keepdims=True) + EPS) * g


def _dot(a, b):
    return jnp.dot(a, b, preferred_element_type=F32)


def _dot_nt(a, b):
    return lax.dot_general(a, b, _NT, preferred_element_type=F32)


def _rope_rows(x1, x2, c, s):
    return x1 * c - x2 * s, x1 * s + x2 * c


def _rope_lanes(x, c, s_signed, half, first_lo):
    lane = lax.broadcasted_iota(jnp.int32, x.shape, 1)
    first = (lane >= first_lo) & (lane < first_lo + half)
    rot = jnp.where(first, pltpu.roll(x, LANE - half, 1), pltpu.roll(x, half, 1))
    return x * c + rot * s_signed


def _proj_kernel(x_ref, gmix_ref, gq_ref, gkv_ref, wnat_ref, wtr_ref, wuq_ref, wuk_ref, wuv_ref,
                 tabt_ref, tabn_ref,
                 qat_ref, ka_ref, vat_ref, qit_ref, wit_ref, ki_ref, qbt_ref, kb_ref, vbt_ref,
                 rqt_ref, rk_ref, rvt_ref, rgt_ref):
    tm = x_ref.shape[0]
    hb = _rms(x_ref[...], gmix_ref[...]).astype(BF16)

    c64t = tabt_ref[0:32, :]
    s64t = tabt_ref[32:64, :]
    c32t = tabt_ref[64:80, :]
    s32t = tabt_ref[80:96, :]

    cq = _dot(hb, wnat_ref[:, N_CQ:N_CQ + MLA_Q_RANK])
    nq = _rms(cq, gq_ref[...]).astype(BF16)
    qt = _dot_nt(wuq_ref[...], nq)
    scale_a = (MLA_NOPE + MLA_ROPE) ** -0.5
    for h in range(MLA_HEADS):
        r0 = h * LANE
        qat_ref[r0:r0 + 64, :] = (qt[r0:r0 + 64, :] * scale_a).astype(BF16)
        o1, o2 = _rope_rows(qt[r0 + 64:r0 + 80, :], qt[r0 + 80:r0 + 96, :], c32t, s32t)
        qat_ref[r0 + 64:r0 + 80, :] = (o1 * scale_a).astype(BF16)
        qat_ref[r0 + 80:r0 + 96, :] = (o2 * scale_a).astype(BF16)
        qat_ref[r0 + 96:r0 + 128, :] = jnp.zeros((32, tm), BF16)

    ckv = _dot(hb, wnat_ref[:, N_CKV:N_CKV + MLA_KV_RANK])
    nkv = _rms(ckv, gkv_ref[...]).astype(BF16)
    kn = _dot(nkv, wuk_ref[...])
    keys = _dot(hb, wnat_ref[:, N_KR:N_NAT])
    slab = lambda off: keys[:, off - N_KR:off - N_KR + LANE]
    krp = _rope_lanes(slab(N_KR), tabn_ref[0], tabn_ref[1], 16, 64)
    for h in range(MLA_HEADS):
        ka_ref[:, h * LANE:(h + 1) * LANE] = (kn[:, h * LANE:(h + 1) * LANE] + krp).astype(BF16)
    vt = _dot_nt(wuv_ref[...], nkv).astype(BF16)
    for j in range(tm // LANE):
        vat_ref[j] = vt[:, j * LANE:(j + 1) * LANE]

    kb_ref[...] = _rope_lanes(slab(N_AK), tabn_ref[2], tabn_ref[3], 32, 0).astype(BF16)
    ki_ref[...] = _rope_lanes(slab(N_KI), tabn_ref[4], tabn_ref[5], 16, 0).astype(BF16)

    for h in range(RET_HEADS):
        rkp = _rope_lanes(slab(N_RK + h * LANE), tabn_ref[2], tabn_ref[3], 32, 0) * (RET_DK ** -0.5)
        rk_ref[:, h * LANE:(h + 1) * LANE] = rkp.astype(BF16)

    aqt = _dot_nt(wtr_ref[T_AQ:T_AQ + 256, :], hb)
    scale_b = DSA_DIM ** -0.5
    for h in range(DSA_HEADS):
        o1, o2 = _rope_rows(aqt[h * 64:h * 64 + 32, :], aqt[h * 64 + 32:h * 64 + 64, :], c64t, s64t)
        qbt_ref[h * LANE:h * LANE + 32, :] = (o1 * scale_b).astype(BF16)
        qbt_ref[h * LANE + 32:h * LANE + 64, :] = (o2 * scale_b).astype(BF16)
        qbt_ref[h * LANE + 64:(h + 1) * LANE, :] = jnp.zeros((64, tm), BF16)

    qit = _dot_nt(wtr_ref[T_QI:T_QI + 256, :], hb)
    for h in range(IDX_HEADS):
        o1, o2 = _rope_rows(qit[h * 32:h * 32 + 16, :], qit[h * 32 + 16:h * 32 + 32, :], c32t, s32t)
        qit_ref[h * LANE:h * LANE + 16, :] = o1.astype(BF16)
        qit_ref[h * LANE + 16:h * LANE + 32, :] = o2.astype(BF16)
        qit_ref[h * LANE + 32:(h + 1) * LANE, :] = jnp.zeros((96, tm), BF16)

    idx_scale = (IDX_HEADS ** -0.5) * (IDX_DIM ** -0.5)
    wit_ref[...] = _dot_nt(wtr_ref[T_WI:T_WI + IDX_HEADS, :], hb) * idx_scale

    avt = _dot_nt(wtr_ref[T_AV:T_AV + DSA_DIM, :], hb).astype(BF16)
    for j in range(tm // LANE):
        vbt_ref[j] = avt[:, j * LANE:(j + 1) * LANE]

    rqt = _dot_nt(wtr_ref[T_RQ:T_RQ + 256, :], hb)
    for h in range(RET_HEADS):
        o1, o2 = _rope_rows(rqt[h * 64:h * 64 + 32, :], rqt[h * 64 + 32:h * 64 + 64, :], c64t, s64t)
        rqt_ref[h * LANE:h * LANE + 32, :] = o1.astype(BF16)
        rqt_ref[h * LANE + 32:h * LANE + 64, :] = o2.astype(BF16)
        rqt_ref[h * LANE + 64:(h + 1) * LANE, :] = jnp.zeros((64, tm), BF16)

    rvt_ref[...] = _dot_nt(wtr_ref[T_RV:T_RV + 256, :], hb).astype(BF16)
    rgt_ref[...] = _dot_nt(wtr_ref[T_RG:T_RG + 256, :], hb)


def _full(shape):
    nd = len(shape)
    return pl.BlockSpec(shape, lambda b, i: (0,) * nd)


def _proj(x, gmix, gq, gkv, wnat, wtr, wuq, wuk, wuv, tabt, tabn):
    B, S, D = x.shape
    tm = TM_PROJ
    nch = tm // LANE
    outs = [
        (jax.ShapeDtypeStruct((B, MLA_HEADS * LANE, S), BF16), pl.BlockSpec((None, MLA_HEADS * LANE, tm), lambda b, i: (b, 0, i))),
        (jax.ShapeDtypeStruct((B, S, MLA_HEADS * LANE), BF16), pl.BlockSpec((None, tm, MLA_HEADS * LANE), lambda b, i: (b, i, 0))),
        (jax.ShapeDtypeStruct((B, S // LANE, MLA_HEADS * MLA_V, LANE), BF16), pl.BlockSpec((None, nch, MLA_HEADS * MLA_V, LANE), lambda b, i: (b, i, 0, 0))),
        (jax.ShapeDtypeStruct((B, IDX_HEADS * LANE, S), BF16), pl.BlockSpec((None, IDX_HEADS * LANE, tm), lambda b, i: (b, 0, i))),
        (jax.ShapeDtypeStruct((B, IDX_HEADS, S), F32), pl.BlockSpec((None, IDX_HEADS, tm), lambda b, i: (b, 0, i))),
        (jax.ShapeDtypeStruct((B, S, LANE), BF16), pl.BlockSpec((None, tm, LANE), lambda b, i: (b, i, 0))),
        (jax.ShapeDtypeStruct((B, DSA_HEADS * LANE, S), BF16), pl.BlockSpec((None, DSA_HEADS * LANE, tm), lambda b, i: (b, 0, i))),
        (jax.ShapeDtypeStruct((B, S, LANE), BF16), pl.BlockSpec((None, tm, LANE), lambda b, i: (b, i, 0))),
        (jax.ShapeDtypeStruct((B, S // LANE, DSA_DIM, LANE), BF16), pl.BlockSpec((None, nch, DSA_DIM, LANE), lambda b, i: (b, i, 0, 0))),
        (jax.ShapeDtypeStruct((B, RET_HEADS * LANE, S), BF16), pl.BlockSpec((None, RET_HEADS * LANE, tm), lambda b, i: (b, 0, i))),
        (jax.ShapeDtypeStruct((B, S, RET_HEADS * LANE), BF16), pl.BlockSpec((None, tm, RET_HEADS * LANE), lambda b, i: (b, i, 0))),
        (jax.ShapeDtypeStruct((B, RET_HEADS * RET_DV, S), BF16), pl.BlockSpec((None, RET_HEADS * RET_DV, tm), lambda b, i: (b, 0, i))),
        (jax.ShapeDtypeStruct((B, RET_HEADS * RET_DV, S), F32), pl.BlockSpec((None, RET_HEADS * RET_DV, tm), lambda b, i: (b, 0, i))),
    ]
    in_specs = [
        pl.BlockSpec((None, tm, D), lambda b, i: (b, i, 0)),
        _full(gmix.shape), _full(gq.shape), _full(gkv.shape),
        _full(wnat.shape), _full(wtr.shape), _full(wuq.shape), _full(wuk.shape), _full(wuv.shape),
        pl.BlockSpec((tabt.shape[0], tm), lambda b, i: (0, i)),
        pl.BlockSpec((tabn.shape[0], tm, LANE), lambda b, i: (0, i, 0)),
    ]
    return pl.pallas_call(
        _proj_kernel,
        out_shape=[o[0] for o in outs],
        grid=(B, S // tm),
        in_specs=in_specs,
        out_specs=[o[1] for o in outs],
        compiler_params=pltpu.CompilerParams(
            dimension_semantics=("parallel", "parallel"), vmem_limit_bytes=VMEM_LIMIT),
        name="proj",
    )(x, gmix, gq, gkv, wnat, wtr, wuq, wuk, wuv, tabt, tabn)


def _mla_kernel(q_ref, k_ref, v_ref, o_ref, acc_sc):
    g_heads = acc_sc.shape[0]
    tq = q_ref.shape[1]
    tk = tq
    i = pl.program_id(2)
    nv = tk // LANE
    acc_sc[...] = jnp.zeros_like(acc_sc)

    def step(kt, carry, masked):
        ss = []
        for g in range(g_heads):
            k = k_ref[pl.ds(pl.multiple_of(kt * tk, tk), tk), g * LANE:(g + 1) * LANE]
            ss.append(_dot(k, q_ref[g * LANE:(g + 1) * LANE, :]))
        out = []
        for g in range(g_heads):
            m, l = carry[g]
            s = ss[g]
            if masked:
                kidx = lax.broadcasted_iota(jnp.int32, (tk, tq), 0)
                qidx = lax.broadcasted_iota(jnp.int32, (tk, tq), 1)
                s = jnp.where(kidx <= qidx, s, NEG)
            m_new = jnp.maximum(m, jnp.max(s, axis=0, keepdims=True))
            alpha = jnp.exp(m - m_new)
            p = jnp.exp(s - m_new)
            l = alpha * l + jnp.sum(p, axis=0, keepdims=True)
            vt = jnp.concatenate([v_ref[kt * nv + j, g * MLA_V:(g + 1) * MLA_V, :] for j in range(nv)],
                                 axis=1)
            acc_sc[g] = alpha * acc_sc[g] + _dot(vt, p.astype(BF16))
            out.append((m_new, l))
        return tuple(out)

    init = tuple((jnp.full((1, tq), NEG, F32), jnp.zeros((1, tq), F32)) for _ in range(g_heads))
    carry = lax.fori_loop(0, i, lambda kt, c: step(kt, c, False), init)
    carry = step(i, carry, True)
    for g in range(g_heads):
        o_ref[g * MLA_V:(g + 1) * MLA_V, :] = (acc_sc[g] / carry[g][1]).astype(BF16)


def _mla(qat, ka, vat):
    B, _, S = qat.shape
    tq = TQ_MLA
    g = G_MLA
    return pl.pallas_call(
        _mla_kernel,
        out_shape=jax.ShapeDtypeStruct((B, MLA_HEADS * MLA_V, S), BF16),
        grid=(B, MLA_HEADS // g, S // tq),
        in_specs=[
            pl.BlockSpec((None, g * LANE, tq), lambda b, h, i: (b, h, i)),
            pl.BlockSpec((None, S, g * LANE), lambda b, h, i: (b, 0, h)),
            pl.BlockSpec((None, S // LANE, g * MLA_V, LANE), lambda b, h, i: (b, 0, h, 0)),
        ],
        out_specs=pl.BlockSpec((None, g * MLA_V, tq), lambda b, h, i: (b, h, i)),
        scratch_shapes=[pltpu.VMEM((g, MLA_V, tq), F32)],
        compiler_params=pltpu.CompilerParams(
            dimension_semantics=("parallel", "parallel", "arbitrary"), vmem_limit_bytes=VMEM_LIMIT),
        name="mla",
    )(qat, ka, vat)


def _count_rows(mask):
    r, t = mask.shape
    return jnp.sum(mask.astype(jnp.int32).reshape(r // 8, 8, t), axis=0)


def _dsa_kernel(topk, qit_ref, wit_ref, ki_ref, qbt_ref, kb_ref, vbt_ref, o_ref,
                sc_sc, bias_sc, thr_sc, jsel_sc, acc_sc):
    tq = o_ref.shape[1]
    ck = CK_DSA
    sub = tq // ck
    i = pl.program_id(1)
    nblk = i + 1

    def rows(ref, r0, n):
        return ref[pl.ds(pl.multiple_of(r0, n), n), :]

    qpos_c = i * tq + lax.broadcasted_iota(jnp.int32, (ck, tq), 1)
    krow_c = lax.broadcasted_iota(jnp.int32, (ck, tq), 0)

    def score_body(cb, _):
        for j in range(sub):
            r0 = cb * tq + j * ck
            kc = rows(ki_ref, r0, ck)
            acc = jnp.zeros((ck, tq), F32)
            for h in range(IDX_HEADS):
                rel = _dot(kc, qit_ref[h * LANE:(h + 1) * LANE, :])
                acc = acc + wit_ref[h:h + 1, :] * jnp.maximum(rel, 0.0)
            sc_sc[pl.ds(pl.multiple_of(r0, ck), ck), :] = jnp.where(r0 + krow_c <= qpos_c, acc, -jnp.inf)
        return 0

    lax.fori_loop(0, nblk, score_body, 0)

    krow = lax.broadcasted_iota(jnp.int32, (tq, tq), 0)
    qpos = i * tq + lax.broadcasted_iota(jnp.int32, (tq, tq), 1)

    def count(pred):
        def body(cb, a):
            return a + _count_rows(pred(rows(sc_sc, cb * tq, tq), cb * tq + krow))
        part = lax.fori_loop(0, nblk, body, jnp.zeros((8, tq), jnp.int32))
        return jnp.sum(part, axis=0, keepdims=True)

    def key_to_score(key):
        return pltpu.bitcast(jnp.where(key < 0, key ^ jnp.int32(0x7FFFFFFF), key), F32)

    thr_sc[...] = jnp.full((1, tq), -jnp.inf, F32)
    jsel_sc[...] = jnp.full((1, tq), sc_sc.shape[0], jnp.int32)

    @pl.when(i * tq >= topk)
    def _():
        def bit_body(t, ans):
            cand = jnp.where(t == 0, jnp.zeros_like(ans), ans + jnp.left_shift(jnp.int32(1), 31 - t))
            cf = key_to_score(cand)
            return jnp.where(count(lambda s, kidx: s >= cf) >= topk, cand, ans)

        thr = key_to_score(lax.fori_loop(0, 32, bit_body, jnp.full((1, tq), INT_MIN, jnp.int32)))
        thr_sc[...] = thr
        n_ge = count(lambda s, kidx: s >= thr)
        excess = jnp.max(n_ge) > topk

        @pl.when(excess)
        def _():
            need = topk - count(lambda s, kidx: s > thr)

            def jbit(t, j):
                cand = j + jnp.left_shift(jnp.int32(1), 10 - t)
                n = count(lambda s, kidx: (s == thr) & (kidx < cand))
                return jnp.where(n < need, cand, j)

            jsel_sc[...] = lax.fori_loop(0, 11, jbit, jnp.zeros((1, tq), jnp.int32))

    thr = thr_sc[...]
    jsel = jsel_sc[...]

    def bias_body(cb, _):
        s = rows(sc_sc, cb * tq, tq)
        kidx = cb * tq + krow
        sel = ((s > thr) | ((s == thr) & (kidx <= jsel))) & (kidx <= qpos)
        bias_sc[pl.ds(pl.multiple_of(cb * tq, tq), tq), :] = jnp.where(sel, 0.0, NEG)
        return 0

    lax.fori_loop(0, nblk, bias_body, 0)

    nh = DSA_HEADS
    qall = jnp.concatenate([qbt_ref[h * LANE:(h + 1) * LANE, :] for h in range(nh)], axis=1)
    acc_sc[...] = jnp.zeros_like(acc_sc)

    def att_body(cb, carry):
        m, l = carry
        for j in range(sub):
            r0 = cb * tq + j * ck
            b = rows(bias_sc, r0, ck)
            s = _dot(rows(kb_ref, r0, ck), qall) + jnp.concatenate([b] * nh, axis=1)
            m_new = jnp.maximum(m, jnp.max(s, axis=0, keepdims=True))
            alpha = jnp.exp(m - m_new)
            p = jnp.exp(s - m_new)
            l = alpha * l + jnp.sum(p, axis=0, keepdims=True)
            acc_sc[...] = alpha * acc_sc[...] + _dot(vbt_ref[cb * sub + j], p.astype(BF16))
            m = m_new
        return m, l

    init = (jnp.full((1, nh * tq), NEG, F32), jnp.zeros((1, nh * tq), F32))
    m, l = lax.fori_loop(0, nblk, att_body, init)
    out = acc_sc[...] / l
    for h in range(nh):
        o_ref[h * DSA_DIM:(h + 1) * DSA_DIM, :] = out[:, h * tq:(h + 1) * tq].astype(BF16)


def _dsa(qit, wit, ki, qbt, kb, vbt, topk):
    B, _, S = qit.shape
    tq = TQ_DSA
    assert topk % tq == 0 or topk >= S, "query tiles must not straddle the top-k boundary"
    return pl.pallas_call(
        functools.partial(_dsa_kernel, topk),
        out_shape=jax.ShapeDtypeStruct((B, DSA_HEADS * DSA_DIM, S), BF16),
        grid=(B, S // tq),
        in_specs=[
            pl.BlockSpec((None, IDX_HEADS * LANE, tq), lambda b, i: (b, 0, i)),
            pl.BlockSpec((None, IDX_HEADS, tq), lambda b, i: (b, 0, i)),
            pl.BlockSpec((None, S, LANE), lambda b, i: (b, 0, 0)),
            pl.BlockSpec((None, DSA_HEADS * LANE, tq), lambda b, i: (b, 0, i)),
            pl.BlockSpec((None, S, LANE), lambda b, i: (b, 0, 0)),
            pl.BlockSpec((None, S // LANE, DSA_DIM, LANE), lambda b, i: (b, 0, 0, 0)),
        ],
        out_specs=pl.BlockSpec((None, DSA_HEADS * DSA_DIM, tq), lambda b, i: (b, 0, i)),
        scratch_shapes=[
            pltpu.VMEM((S, tq), F32),
            pltpu.VMEM((S, tq), F32),
            pltpu.VMEM((1, tq), F32),
            pltpu.VMEM((1, tq), jnp.int32),
            pltpu.VMEM((DSA_DIM, DSA_HEADS * tq), F32),
        ],
        compiler_params=pltpu.CompilerParams(
            dimension_semantics=("parallel", "arbitrary"), vmem_limit_bytes=VMEM_LIMIT),
        name="dsa",
    )(qit, wit, ki, qbt, kb, vbt)


def _ret_kernel(rqt_ref, rk_ref, rvt_ref, rgt_ref, dt_ref, xi_ref, zeta_ref, cdec_ref, gret_ref,
                o_ref, state_sc):
    n = pl.program_id(1)

    @pl.when(n == 0)
    def _():
        state_sc[...] = jnp.zeros_like(state_sc)

    for h in range(RET_HEADS):
        qt = rqt_ref[h * LANE:(h + 1) * LANE, :]
        k = rk_ref[:, h * LANE:(h + 1) * LANE]
        vt = rvt_ref[h * RET_DV:(h + 1) * RET_DV, :]
        inner_t = _dot(k, qt) * dt_ref[h]
        rt = state_sc[h]
        qx = (qt.astype(F32) * xi_ref[h:h + 1, :]).astype(BF16)
        out_t = _dot(vt, inner_t.astype(BF16)) + _dot(rt.astype(BF16), qx)
        vz = (vt.astype(F32) * zeta_ref[h:h + 1, :]).astype(BF16)
        state_sc[h] = rt * cdec_ref[h] + _dot(vz, k)

        mu = jnp.mean(out_t, axis=0, keepdims=True)
        xc = out_t - mu
        var = jnp.mean(xc * xc, axis=0, keepdims=True)
        y = xc * lax.rsqrt(var + EPS) * gret_ref[h * RET_DV:(h + 1) * RET_DV, :]
        g = rgt_ref[h * RET_DV:(h + 1) * RET_DV, :]
        o_ref[h * RET_DV:(h + 1) * RET_DV, :] = (g * jax.nn.sigmoid(g) * y).astype(BF16)


def _retention(rqt, rk, rvt, rgt, dt, xi, zeta, cdec, gret):
    B, _, S = rqt.shape
    c = C_RET
    return pl.pallas_call(
        _ret_kernel,
        out_shape=jax.ShapeDtypeStruct((B, RET_HEADS * RET_DV, S), BF16),
        grid=(B, S // c),
        in_specs=[
            pl.BlockSpec((None, RET_HEADS * LANE, c), lambda b, n: (b, 0, n)),
            pl.BlockSpec((None, c, RET_HEADS * LANE), lambda b, n: (b, n, 0)),
            pl.BlockSpec((None, RET_HEADS * RET_DV, c), lambda b, n: (b, 0, n)),
            pl.BlockSpec((None, RET_HEADS * RET_DV, c), lambda b, n: (b, 0, n)),
            _full(dt.shape), _full(xi.shape), _full(zeta.shape), _full(cdec.shape), _full(gret.shape),
        ],
        out_specs=pl.BlockSpec((None, RET_HEADS * RET_DV, c), lambda b, n: (b, 0, n)),
        scratch_shapes=[pltpu.VMEM((RET_HEADS, RET_DV, LANE), F32)],
        compiler_params=pltpu.CompilerParams(
            dimension_semantics=("parallel", "arbitrary"), vmem_limit_bytes=VMEM_LIMIT),
        name="retention",
    )(rqt, rk, rvt, rgt, dt, xi, zeta, cdec, gret)


def _mlp_kernel(final, x_ref, oat_ref, obt_ref, oct_ref, wout_ref, gmlp_ref, w1_ref, w2_ref, gfin_ref,
                y_ref):
    def tr(ref):
        return ref[...].astype(F32).T.astype(BF16)

    na = MLA_HEADS * MLA_V
    nb = DSA_HEADS * DSA_DIM
    attn = (_dot(tr(oat_ref), wout_ref[0:na, :])
            + _dot(tr(obt_ref), wout_ref[na:na + nb, :])
            + _dot(tr(oct_ref), wout_ref[na + nb:, :]))
    x1 = x_ref[...] + attn
    hf = _rms(x1, gmlp_ref[...]).astype(BF16)
    acc = x1
    for f in range(D_FF // FF_CHUNK):
        u = jnp.maximum(_dot(hf, w1_ref[:, f * FF_CHUNK:(f + 1) * FF_CHUNK]), 0.0)
        acc = acc + _dot((u * u).astype(BF16), w2_ref[f * FF_CHUNK:(f + 1) * FF_CHUNK, :])
    if final:
        acc = _rms(acc, gfin_ref[...])
    y_ref[...] = acc


def _mlp(x, oat, obt, oct, wout, gmlp, w1, w2, gfin, final):
    B, S, D = x.shape
    tm = TM_MLP

    def const(shape):
        nd = len(shape)
        return pl.BlockSpec(shape, lambda b, i: (0,) * nd, pipeline_mode=pl.Buffered(1))

    return pl.pallas_call(
        functools.partial(_mlp_kernel, final),
        out_shape=jax.ShapeDtypeStruct((B, S, D), F32),
        grid=(B, S // tm),
        in_specs=[
            pl.BlockSpec((None, tm, D), lambda b, i: (b, i, 0)),
            pl.BlockSpec((None, oat.shape[1], tm), lambda b, i: (b, 0, i)),
            pl.BlockSpec((None, obt.shape[1], tm), lambda b, i: (b, 0, i)),
            pl.BlockSpec((None, oct.shape[1], tm), lambda b, i: (b, 0, i)),
            const(wout.shape), const(gmlp.shape), const(w1.shape), const(w2.shape), const(gfin.shape),
        ],
        out_specs=pl.BlockSpec((None, tm, D), lambda b, i: (b, i, 0)),
        compiler_params=pltpu.CompilerParams(
            dimension_semantics=("parallel", "parallel"), vmem_limit_bytes=VMEM_LIMIT),
        name="mlp",
    )(x, oat, obt, oct, wout, gmlp, w1, w2, gfin)


def _rope_tables(seq, dim):
    pos = jnp.arange(seq, dtype=F32)
    inv = ROPE_THETA ** (-jnp.arange(0, dim, 2, dtype=F32) / dim)
    ang = pos[:, None] * inv[None, :]
    return jnp.cos(ang), jnp.sin(ang)


def _lane_table(c, s, lo, seq):
    half = c.shape[1]
    ct = jnp.zeros((seq, LANE), F32).at[:, lo:lo + 2 * half].set(jnp.concatenate([c, c], axis=1))
    st = jnp.zeros((seq, LANE), F32).at[:, lo:lo + 2 * half].set(jnp.concatenate([-s, s], axis=1))
    return ct, st


def _pad_cols(w, width):
    return jnp.pad(w, ((0, 0), (0, width - w.shape[1])))


def _prep_layer(w_in, w_uq, w_ukv):
    z = lambda n: jnp.zeros((D_MODEL, n), F32)
    col = lambda a, b: w_in[:, a:b]
    nat = [col(O_CQ, O_CKV), col(O_CKV, O_KR),
           jnp.concatenate([z(64), col(O_KR, O_AQ), z(32)], axis=1),
           _pad_cols(col(O_AK, O_AV), LANE),
           _pad_cols(col(O_KI, O_WI), LANE)]
    for h in range(RET_HEADS):
        nat.append(_pad_cols(col(O_RK + h * RET_DK, O_RK + (h + 1) * RET_DK), LANE))
    wnat = jnp.concatenate(nat, axis=1).astype(BF16)
    wtr = jnp.concatenate([col(O_AQ, O_AK), col(O_QI, O_KI), col(O_AV, O_QI), col(O_RQ, O_RK),
                           col(O_RV, O_RG), col(O_RG, O_END), col(O_WI, O_RQ)], axis=1).T.astype(BF16)
    dq = MLA_NOPE + MLA_ROPE
    wuq = jnp.pad(w_uq.reshape(MLA_Q_RANK, MLA_HEADS, dq), ((0, 0), (0, 0), (0, LANE - dq)))
    wuq = wuq.reshape(MLA_Q_RANK, MLA_HEADS * LANE).T.astype(BF16)
    wkv = w_ukv.reshape(MLA_KV_RANK, MLA_HEADS, MLA_NOPE + MLA_V)
    wuk = jnp.pad(wkv[:, :, :MLA_NOPE], ((0, 0), (0, 0), (0, LANE - MLA_NOPE)))
    wuk = wuk.reshape(MLA_KV_RANK, MLA_HEADS * LANE).astype(BF16)
    wuv = wkv[:, :, MLA_NOPE:].reshape(MLA_KV_RANK, MLA_HEADS * MLA_V).T.astype(BF16)
    return wnat, wtr, wuq, wuk, wuv


def kernel(x, g_mix, w_in, g_q, w_uq, g_kv, w_ukv, g_ret, w_out, g_mlp, w_ff1, w_ff2, g_final):
    B, S, _ = x.shape
    topk = min(DSA_TOPK_MAX, S // 4)

    cos32, sin32 = _rope_tables(S, 32)
    cos64, sin64 = _rope_tables(S, 64)
    tabt = jnp.concatenate([cos64.T, sin64.T, cos32.T, sin32.T], axis=0)
    tabn = jnp.stack(_lane_table(cos32, sin32, 64, S) + _lane_table(cos64, sin64, 0, S)
                     + _lane_table(cos32, sin32, 0, S))

    log_gamma = jnp.log1p(-jnp.exp2(-5.0 - jnp.arange(RET_HEADS, dtype=F32)))
    c = C_RET
    pos = jnp.arange(c, dtype=F32)
    rel = pos[None, :] - pos[:, None]
    dt = jnp.where(rel[None] >= 0, jnp.exp(jnp.maximum(rel, 0.0)[None] * log_gamma[:, None, None]), 0.0)
    xi = jnp.exp((pos + 1.0)[None, :] * log_gamma[:, None])
    zeta = jnp.exp((c - 1.0 - pos)[None, :] * log_gamma[:, None])
    cdec = jnp.broadcast_to(jnp.exp(c * log_gamma)[:, None, None], (RET_HEADS, RET_DV, LANE))

    for l in range(DEPTH):
        wnat, wtr, wuq, wuk, wuv = _prep_layer(w_in[l], w_uq[l], w_ukv[l])
        (qat, ka, vat, qit, wit, ki, qbt, kb, vbt, rqt, rk, rvt, rgt) = _proj(
            x, g_mix[l][None, :], g_q[l][None, :], g_kv[l][None, :], wnat, wtr, wuq, wuk, wuv, tabt, tabn)
        oat = _mla(qat, ka, vat)
        obt = _dsa(qit, wit, ki, qbt, kb, vbt, topk)
        gret = jnp.broadcast_to(g_ret[l][:, None], (RET_HEADS * RET_DV, c))
        oct = _retention(rqt, rk, rvt, rgt, dt, xi, zeta, cdec, gret)
        x = _mlp(x, oat, obt, oct, w_out[l].astype(BF16), g_mlp[l][None, :],
                 w_ff1[l].astype(BF16), w_ff2[l].astype(BF16), g_final[None, :], l == DEPTH - 1)
    return x
```

```python
import functools

import numpy as np
import jax
import jax.numpy as jnp
from jax import lax
from jax.experimental import pallas as pl
from jax.experimental.pallas import tpu as pltpu

D_MODEL = 1024
DEPTH = 2
MLA_HEADS = 8
MLA_NOPE = 64
MLA_ROPE = 32
MLA_V = 64
MLA_Q_RANK = 768
MLA_KV_RANK = 256
DSA_HEADS = 4
DSA_DIM = 64
IDX_HEADS = 8
IDX_DIM = 32
DSA_TOPK_MAX = 256
RET_HEADS = 4
RET_DK = 64
RET_DV = 64
D_FF = 4 * D_MODEL
ROPE_THETA = 10000.0
EPS = 1e-6

MLA_COLS = (MLA_Q_RANK, MLA_KV_RANK, MLA_ROPE)
DSA_COLS = (DSA_HEADS * DSA_DIM, DSA_DIM, DSA_DIM, IDX_HEADS * IDX_DIM, IDX_DIM, IDX_HEADS)
RET_COLS = (RET_HEADS * RET_DK, RET_HEADS * RET_DK, RET_HEADS * RET_DV, RET_HEADS * RET_DV)
SPLITS = MLA_COLS + DSA_COLS + RET_COLS
_OFF = np.concatenate([[0], np.cumsum(SPLITS)]).tolist()
(O_CQ, O_CKV, O_KR, O_AQ, O_AK, O_AV, O_QI, O_KI, O_WI, O_RQ, O_RK, O_RV, O_RG, O_END) = _OFF

LANE = 128
VMEM_LIMIT = 56 * 1024 * 1024

BF16 = jnp.bfloat16
F32 = jnp.float32
NEG = -1e30
LOG2E = float(np.log2(np.e))
INT_MIN = np.int32(-2**31)

N_CQ = 0
N_CKV = N_CQ + MLA_Q_RANK
N_KR = N_CKV + MLA_KV_RANK
N_AK = N_KR + LANE
N_KI = N_AK + LANE
N_RK = N_KI + LANE
N_NAT = N_RK + RET_HEADS * LANE
T_AQ = 0
T_QI = T_AQ + DSA_HEADS * DSA_DIM
T_AV = T_QI + IDX_HEADS * IDX_DIM
T_RQ = T_AV + DSA_DIM
T_RV = T_RQ + RET_HEADS * RET_DK
T_RG = T_RV + RET_HEADS * RET_DV
T_WI = T_RG + RET_HEADS * RET_DV
N_TR = T_WI + IDX_HEADS

TM_PROJ = 256
TQ_MLA = 256
G_MLA = 8
TQ_DSA = 256
CK_DSA = 128
C_RET = 256
TM_MLP = 512
FF_CHUNK = 1024

_NT = (((1,), (1,)), ((), ()))


def _rms(x, g):
    return x * lax.rsqrt(jnp.mean(x * x, axis=-1, keepdims=True) + EPS) * g


def _dot(a, b):
    return jnp.dot(a, b, preferred_element_type=F32)


def _dot_nt(a, b):
    return lax.dot_general(a, b, _NT, preferred_element_type=F32)


def _rope_rows(x1, x2, c, s):
    return x1 * c - x2 * s, x1 * s + x2 * c


def _rope_lanes(x, c, s_signed, half, first_lo):
    lane = lax.broadcasted_iota(jnp.int32, x.shape, 1)
    first = (lane >= first_lo) & (lane < first_lo + half)
    rot = jnp.where(first, pltpu.roll(x, LANE - half, 1), pltpu.roll(x, half, 1))
    return x * c + rot * s_signed


def _proj_kernel(x_ref, gmix_ref, gq_ref, gkv_ref, wnat_ref, wtr_ref, wuq_ref, wuk_ref, wuv_ref,
                 tabt_ref, tabn_ref,
                 qat_ref, ka_ref, vat_ref, qit_ref, wit_ref, ki_ref, qbt_ref, kb_ref, vbt_ref,
                 rqt_ref, rk_ref, rvt_ref, rgt_ref):
    tm = x_ref.shape[0]
    hb = _rms(x_ref[...], gmix_ref[...]).astype(BF16)

    c64t = tabt_ref[0:32, :]
    s64t = tabt_ref[32:64, :]
    c32t = tabt_ref[64:80, :]
    s32t = tabt_ref[80:96, :]

    cq = _dot(hb, wnat_ref[:, N_CQ:N_CQ + MLA_Q_RANK])
    nq = _rms(cq, gq_ref[...]).astype(BF16)
    qt = _dot_nt(wuq_ref[...], nq)
    scale_a = (MLA_NOPE + MLA_ROPE) ** -0.5 * LOG2E
    for h in range(MLA_HEADS):
        r0 = h * LANE
        qat_ref[r0:r0 + 64, :] = (qt[r0:r0 + 64, :] * scale_a).astype(BF16)
        o1, o2 = _rope_rows(qt[r0 + 64:r0 + 80, :], qt[r0 + 80:r0 + 96, :], c32t, s32t)
        qat_ref[r0 + 64:r0 + 80, :] = (o1 * scale_a).astype(BF16)
        qat_ref[r0 + 80:r0 + 96, :] = (o2 * scale_a).astype(BF16)
        qat_ref[r0 + 96:r0 + 128, :] = jnp.zeros((32, tm), BF16)

    ckv = _dot(hb, wnat_ref[:, N_CKV:N_CKV + MLA_KV_RANK])
    nkv = _rms(ckv, gkv_ref[...]).astype(BF16)
    kn = _dot(nkv, wuk_ref[...])
    keys = _dot(hb, wnat_ref[:, N_KR:N_NAT])
    slab = lambda off: keys[:, off - N_KR:off - N_KR + LANE]
    krp = _rope_lanes(slab(N_KR), tabn_ref[0], tabn_ref[1], 16, 64)
    for h in range(MLA_HEADS):
        ka_ref[:, h * LANE:(h + 1) * LANE] = (kn[:, h * LANE:(h + 1) * LANE] + krp).astype(BF16)
    vt = _dot_nt(wuv_ref[...], nkv).astype(BF16)
    for j in range(tm // LANE):
        vat_ref[j] = vt[:, j * LANE:(j + 1) * LANE]

    kb_ref[...] = _rope_lanes(slab(N_AK), tabn_ref[2], tabn_ref[3], 32, 0).astype(BF16)
    ki_ref[...] = _rope_lanes(slab(N_KI), tabn_ref[4], tabn_ref[5], 16, 0).astype(BF16)

    for h in range(RET_HEADS):
        rkp = _rope_lanes(slab(N_RK + h * LANE), tabn_ref[2], tabn_ref[3], 32, 0) * (RET_DK ** -0.5)
        rk_ref[:, h * LANE:(h + 1) * LANE] = rkp.astype(BF16)

    aqt = _dot_nt(wtr_ref[T_AQ:T_AQ + 256, :], hb)
    scale_b = DSA_DIM ** -0.5 * LOG2E
    for h in range(DSA_HEADS):
        o1, o2 = _rope_rows(aqt[h * 64:h * 64 + 32, :], aqt[h * 64 + 32:h * 64 + 64, :], c64t, s64t)
        qbt_ref[h * LANE:h * LANE + 32, :] = (o1 * scale_b).astype(BF16)
        qbt_ref[h * LANE + 32:h * LANE + 64, :] = (o2 * scale_b).astype(BF16)
        qbt_ref[h * LANE + 64:(h + 1) * LANE, :] = jnp.zeros((64, tm), BF16)

    qit = _dot_nt(wtr_ref[T_QI:T_QI + 256, :], hb)
    for h in range(IDX_HEADS):
        o1, o2 = _rope_rows(qit[h * 32:h * 32 + 16, :], qit[h * 32 + 16:h * 32 + 32, :], c32t, s32t)
        qit_ref[h * LANE:h * LANE + 16, :] = o1.astype(BF16)
        qit_ref[h * LANE + 16:h * LANE + 32, :] = o2.astype(BF16)
        qit_ref[h * LANE + 32:(h + 1) * LANE, :] = jnp.zeros((96, tm), BF16)

    idx_scale = (IDX_HEADS ** -0.5) * (IDX_DIM ** -0.5)
    wit_ref[...] = _dot_nt(wtr_ref[T_WI:T_WI + IDX_HEADS, :], hb) * idx_scale

    avt = _dot_nt(wtr_ref[T_AV:T_AV + DSA_DIM, :], hb).astype(BF16)
    for j in range(tm // LANE):
        vbt_ref[j] = avt[:, j * LANE:(j + 1) * LANE]

    rqt = _dot_nt(wtr_ref[T_RQ:T_RQ + 256, :], hb)
    for h in range(RET_HEADS):
        o1, o2 = _rope_rows(rqt[h * 64:h * 64 + 32, :], rqt[h * 64 + 32:h * 64 + 64, :], c64t, s64t)
        rqt_ref[h * LANE:h * LANE + 32, :] = o1.astype(BF16)
        rqt_ref[h * LANE + 32:h * LANE + 64, :] = o2.astype(BF16)
        rqt_ref[h * LANE + 64:(h + 1) * LANE, :] = jnp.zeros((64, tm), BF16)

    rvt_ref[...] = _dot_nt(wtr_ref[T_RV:T_RV + 256, :], hb).astype(BF16)
    rgt_ref[...] = _dot_nt(wtr_ref[T_RG:T_RG + 256, :], hb)


def _full(shape):
    nd = len(shape)
    return pl.BlockSpec(shape, lambda b, i: (0,) * nd)


def _proj(x, gmix, gq, gkv, wnat, wtr, wuq, wuk, wuv, tabt, tabn):
    B, S, D = x.shape
    tm = TM_PROJ
    nch = tm // LANE
    outs = [
        (jax.ShapeDtypeStruct((B, MLA_HEADS * LANE, S), BF16), pl.BlockSpec((None, MLA_HEADS * LANE, tm), lambda b, i: (b, 0, i))),
        (jax.ShapeDtypeStruct((B, S, MLA_HEADS * LANE), BF16), pl.BlockSpec((None, tm, MLA_HEADS * LANE), lambda b, i: (b, i, 0))),
        (jax.ShapeDtypeStruct((B, S // LANE, MLA_HEADS * MLA_V, LANE), BF16), pl.BlockSpec((None, nch, MLA_HEADS * MLA_V, LANE), lambda b, i: (b, i, 0, 0))),
        (jax.ShapeDtypeStruct((B, IDX_HEADS * LANE, S), BF16), pl.BlockSpec((None, IDX_HEADS * LANE, tm), lambda b, i: (b, 0, i))),
        (jax.ShapeDtypeStruct((B, IDX_HEADS, S), F32), pl.BlockSpec((None, IDX_HEADS, tm), lambda b, i: (b, 0, i))),
        (jax.ShapeDtypeStruct((B, S, LANE), BF16), pl.BlockSpec((None, tm, LANE), lambda b, i: (b, i, 0))),
        (jax.ShapeDtypeStruct((B, DSA_HEADS * LANE, S), BF16), pl.BlockSpec((None, DSA_HEADS * LANE, tm), lambda b, i: (b, 0, i))),
        (jax.ShapeDtypeStruct((B, S, LANE), BF16), pl.BlockSpec((None, tm, LANE), lambda b, i: (b, i, 0))),
        (jax.ShapeDtypeStruct((B, S // LANE, DSA_DIM, LANE), BF16), pl.BlockSpec((None, nch, DSA_DIM, LANE), lambda b, i: (b, i, 0, 0))),
        (jax.ShapeDtypeStruct((B, RET_HEADS * LANE, S), BF16), pl.BlockSpec((None, RET_HEADS * LANE, tm), lambda b, i: (b, 0, i))),
        (jax.ShapeDtypeStruct((B, S, RET_HEADS * LANE), BF16), pl.BlockSpec((None, tm, RET_HEADS * LANE), lambda b, i: (b, i, 0))),
        (jax.ShapeDtypeStruct((B, RET_HEADS * RET_DV, S), BF16), pl.BlockSpec((None, RET_HEADS * RET_DV, tm), lambda b, i: (b, 0, i))),
        (jax.ShapeDtypeStruct((B, RET_HEADS * RET_DV, S), F32), pl.BlockSpec((None, RET_HEADS * RET_DV, tm), lambda b, i: (b, 0, i))),
    ]
    in_specs = [
        pl.BlockSpec((None, tm, D), lambda b, i: (b, i, 0)),
        _full(gmix.shape), _full(gq.shape), _full(gkv.shape),
        _full(wnat.shape), _full(wtr.shape), _full(wuq.shape), _full(wuk.shape), _full(wuv.shape),
        pl.BlockSpec((tabt.shape[0], tm), lambda b, i: (0, i)),
        pl.BlockSpec((tabn.shape[0], tm, LANE), lambda b, i: (0, i, 0)),
    ]
    return pl.pallas_call(
        _proj_kernel,
        out_shape=[o[0] for o in outs],
        grid=(B, S // tm),
        in_specs=in_specs,
        out_specs=[o[1] for o in outs],
        compiler_params=pltpu.CompilerParams(
            dimension_semantics=("parallel", "parallel"), vmem_limit_bytes=VMEM_LIMIT),
        name="proj",
    )(x, gmix, gq, gkv, wnat, wtr, wuq, wuk, wuv, tabt, tabn)


def _mla_kernel(q_ref, k_ref, v_ref, o_ref, s_sc, acc_sc):
    g_heads = acc_sc.shape[0]
    tq = q_ref.shape[1]
    tk = tq
    i = pl.program_id(2)
    nv = tk // LANE

    def tile(kt):
        return pl.ds(pl.multiple_of(kt * tk, tk), tk)

    def scores(kt, ms, masked):
        out = []
        for g in range(g_heads):
            s = _dot(k_ref[tile(kt), g * LANE:(g + 1) * LANE], q_ref[g * LANE:(g + 1) * LANE, :])
            if masked:
                kidx = lax.broadcasted_iota(jnp.int32, (tk, tq), 0)
                qidx = lax.broadcasted_iota(jnp.int32, (tk, tq), 1)
                s = jnp.where(kidx <= qidx, s, NEG)
            s_sc[g, tile(kt), :] = s
            out.append(jnp.maximum(ms[g], jnp.max(s, axis=0, keepdims=True)))
        return tuple(out)

    ms = tuple(jnp.full((1, tq), NEG, F32) for _ in range(g_heads))
    ms = lax.fori_loop(0, i, lambda kt, c: scores(kt, c, False), ms)
    ms = scores(i, ms, True)

    acc_sc[...] = jnp.zeros_like(acc_sc)

    def accum(kt, ls):
        out = []
        for g in range(g_heads):
            p = jnp.exp2(s_sc[g, tile(kt), :] - ms[g])
            vt = jnp.concatenate([v_ref[kt * nv + j, g * MLA_V:(g + 1) * MLA_V, :] for j in range(nv)],
                                 axis=1)
            acc_sc[g] += _dot(vt, p.astype(BF16))
            out.append(ls[g] + jnp.sum(p, axis=0, keepdims=True))
        return tuple(out)

    ls = lax.fori_loop(0, i + 1, accum, tuple(jnp.zeros((1, tq), F32) for _ in range(g_heads)))
    for g in range(g_heads):
        o_ref[g * MLA_V:(g + 1) * MLA_V, :] = (acc_sc[g] / ls[g]).astype(BF16)


def _mla(qat, ka, vat):
    B, _, S = qat.shape
    tq = TQ_MLA
    g = G_MLA
    return pl.pallas_call(
        _mla_kernel,
        out_shape=jax.ShapeDtypeStruct((B, MLA_HEADS * MLA_V, S), BF16),
        grid=(B, MLA_HEADS // g, S // tq),
        in_specs=[
            pl.BlockSpec((None, g * LANE, tq), lambda b, h, i: (b, h, i)),
            pl.BlockSpec((None, S, g * LANE), lambda b, h, i: (b, 0, h)),
            pl.BlockSpec((None, S // LANE, g * MLA_V, LANE), lambda b, h, i: (b, 0, h, 0)),
        ],
        out_specs=pl.BlockSpec((None, g * MLA_V, tq), lambda b, h, i: (b, h, i)),
        scratch_shapes=[pltpu.VMEM((g, S, tq), F32), pltpu.VMEM((g, MLA_V, tq), F32)],
        compiler_params=pltpu.CompilerParams(
            dimension_semantics=("parallel", "parallel", "arbitrary"), vmem_limit_bytes=VMEM_LIMIT),
        name="mla",
    )(qat, ka, vat)


def _count_rows(mask):
    r, t = mask.shape
    return jnp.sum(mask.astype(jnp.int32).reshape(r // 8, 8, t), axis=0)


def _dsa_kernel(topk, qit_ref, wit_ref, ki_ref, qbt_ref, kb_ref, vbt_ref, o_ref,
                sc_sc, att_sc, thr_sc, jsel_sc, acc_sc):
    tq = o_ref.shape[1]
    ck = CK_DSA
    sub = tq // ck
    i = pl.program_id(1)
    nblk = i + 1

    def rows(ref, r0, n):
        return ref[pl.ds(pl.multiple_of(r0, n), n), :]

    qpos_c = i * tq + lax.broadcasted_iota(jnp.int32, (ck, tq), 1)
    krow_c = lax.broadcasted_iota(jnp.int32, (ck, tq), 0)

    def score_body(cb, _):
        for j in range(sub):
            r0 = cb * tq + j * ck
            kc = rows(ki_ref, r0, ck)
            acc = jnp.zeros((ck, tq), F32)
            for h in range(IDX_HEADS):
                rel = _dot(kc, qit_ref[h * LANE:(h + 1) * LANE, :])
                acc = acc + wit_ref[h:h + 1, :] * jnp.maximum(rel, 0.0)
            sc_sc[pl.ds(pl.multiple_of(r0, ck), ck), :] = jnp.where(r0 + krow_c <= qpos_c, acc, -jnp.inf)
        return 0

    lax.fori_loop(0, nblk, score_body, 0)

    krow = lax.broadcasted_iota(jnp.int32, (tq, tq), 0)
    qpos = i * tq + lax.broadcasted_iota(jnp.int32, (tq, tq), 1)

    def count(pred):
        def body(cb, a):
            return a + _count_rows(pred(rows(sc_sc, cb * tq, tq), cb * tq + krow))
        part = lax.fori_loop(0, nblk, body, jnp.zeros((8, tq), jnp.int32))
        return jnp.sum(part, axis=0, keepdims=True)

    def key_to_score(key):
        return pltpu.bitcast(jnp.where(key < 0, key ^ jnp.int32(0x7FFFFFFF), key), F32)

    thr_sc[...] = jnp.full((1, tq), -jnp.inf, F32)
    jsel_sc[...] = jnp.full((1, tq), sc_sc.shape[0], jnp.int32)

    @pl.when(i * tq >= topk)
    def _():
        def bit_body(t, ans):
            cand = jnp.where(t == 0, jnp.zeros_like(ans), ans + jnp.left_shift(jnp.int32(1), 31 - t))
            cf = key_to_score(cand)
            return jnp.where(count(lambda s, kidx: s >= cf) >= topk, cand, ans)

        thr = key_to_score(lax.fori_loop(0, 32, bit_body, jnp.full((1, tq), INT_MIN, jnp.int32)))
        thr_sc[...] = thr
        n_ge = count(lambda s, kidx: s >= thr)
        excess = jnp.max(n_ge) > topk

        @pl.when(excess)
        def _():
            need = topk - count(lambda s, kidx: s > thr)

            def jbit(t, j):
                cand = j + jnp.left_shift(jnp.int32(1), 10 - t)
                n = count(lambda s, kidx: (s == thr) & (kidx < cand))
                return jnp.where(n < need, cand, j)

            jsel_sc[...] = lax.fori_loop(0, 11, jbit, jnp.zeros((1, tq), jnp.int32))

    thr = thr_sc[...]
    jsel = jsel_sc[...]
    nh = DSA_HEADS
    qall = jnp.concatenate([qbt_ref[h * LANE:(h + 1) * LANE, :] for h in range(nh)], axis=1)

    def logit_body(cb, m):
        sc = rows(sc_sc, cb * tq, tq)
        kidx = cb * tq + krow
        sel = ((sc > thr) | ((sc == thr) & (kidx <= jsel))) & (kidx <= qpos)
        bias = jnp.where(sel, 0.0, NEG)
        for j in range(sub):
            r0 = cb * tq + j * ck
            b = bias[j * ck:(j + 1) * ck, :]
            s = _dot(rows(kb_ref, r0, ck), qall) + jnp.concatenate([b] * nh, axis=1)
            att_sc[pl.ds(pl.multiple_of(r0, ck), ck), :] = s
            m = jnp.maximum(m, jnp.max(s, axis=0, keepdims=True))
        return m

    m = lax.fori_loop(0, nblk, logit_body, jnp.full((1, nh * tq), NEG, F32))

    acc_sc[...] = jnp.zeros_like(acc_sc)

    def att_body(cb, l):
        for j in range(sub):
            r0 = cb * tq + j * ck
            p = jnp.exp2(rows(att_sc, r0, ck) - m)
            l = l + jnp.sum(p, axis=0, keepdims=True)
            acc_sc[...] += _dot(vbt_ref[cb * sub + j], p.astype(BF16))
        return l

    l = lax.fori_loop(0, nblk, att_body, jnp.zeros((1, nh * tq), F32))
    out = acc_sc[...] / l
    for h in range(nh):
        o_ref[h * DSA_DIM:(h + 1) * DSA_DIM, :] = out[:, h * tq:(h + 1) * tq].astype(BF16)


def _dsa(qit, wit, ki, qbt, kb, vbt, topk):
    B, _, S = qit.shape
    tq = TQ_DSA
    assert topk % tq == 0 or topk >= S, "query tiles must not straddle the top-k boundary"
    return pl.pallas_call(
        functools.partial(_dsa_kernel, topk),
        out_shape=jax.ShapeDtypeStruct((B, DSA_HEADS * DSA_DIM, S), BF16),
        grid=(B, S // tq),
        in_specs=[
            pl.BlockSpec((None, IDX_HEADS * LANE, tq), lambda b, i: (b, 0, i)),
            pl.BlockSpec((None, IDX_HEADS, tq), lambda b, i: (b, 0, i)),
            pl.BlockSpec((None, S, LANE), lambda b, i: (b, 0, 0)),
            pl.BlockSpec((None, DSA_HEADS * LANE, tq), lambda b, i: (b, 0, i)),
            pl.BlockSpec((None, S, LANE), lambda b, i: (b, 0, 0)),
            pl.BlockSpec((None, S // LANE, DSA_DIM, LANE), lambda b, i: (b, 0, 0, 0)),
        ],
        out_specs=pl.BlockSpec((None, DSA_HEADS * DSA_DIM, tq), lambda b, i: (b, 0, i)),
        scratch_shapes=[
            pltpu.VMEM((S, tq), F32),
            pltpu.VMEM((S, DSA_HEADS * tq), F32),
            pltpu.VMEM((1, tq), F32),
            pltpu.VMEM((1, tq), jnp.int32),
            pltpu.VMEM((DSA_DIM, DSA_HEADS * tq), F32),
        ],
        compiler_params=pltpu.CompilerParams(
            dimension_semantics=("parallel", "arbitrary"), vmem_limit_bytes=VMEM_LIMIT),
        name="dsa",
    )(qit, wit, ki, qbt, kb, vbt)


def _ret_kernel(rqt_ref, rk_ref, rvt_ref, rgt_ref, dt_ref, xi_ref, zeta_ref, cdec_ref, gret_ref,
                o_ref, state_sc):
    n = pl.program_id(1)

    @pl.when(n == 0)
    def _():
        state_sc[...] = jnp.zeros_like(state_sc)

    for h in range(RET_HEADS):
        qt = rqt_ref[h * LANE:(h + 1) * LANE, :]
        k = rk_ref[:, h * LANE:(h + 1) * LANE]
        vt = rvt_ref[h * RET_DV:(h + 1) * RET_DV, :]
        inner_t = _dot(k, qt) * dt_ref[h]
        rt = state_sc[h]
        qx = (qt.astype(F32) * xi_ref[h:h + 1, :]).astype(BF16)
        out_t = _dot(vt, inner_t.astype(BF16)) + _dot(rt.astype(BF16), qx)
        vz = (vt.astype(F32) * zeta_ref[h:h + 1, :]).astype(BF16)
        state_sc[h] = rt * cdec_ref[h] + _dot(vz, k)

        mu = jnp.mean(out_t, axis=0, keepdims=True)
        xc = out_t - mu
        var = jnp.mean(xc * xc, axis=0, keepdims=True)
        y = xc * lax.rsqrt(var + EPS) * gret_ref[h * RET_DV:(h + 1) * RET_DV, :]
        g = rgt_ref[h * RET_DV:(h + 1) * RET_DV, :]
        o_ref[h * RET_DV:(h + 1) * RET_DV, :] = (g * jax.nn.sigmoid(g) * y).astype(BF16)


def _retention(rqt, rk, rvt, rgt, dt, xi, zeta, cdec, gret):
    B, _, S = rqt.shape
    c = C_RET
    return pl.pallas_call(
        _ret_kernel,
        out_shape=jax.ShapeDtypeStruct((B, RET_HEADS * RET_DV, S), BF16),
        grid=(B, S // c),
        in_specs=[
            pl.BlockSpec((None, RET_HEADS * LANE, c), lambda b, n: (b, 0, n)),
            pl.BlockSpec((None, c, RET_HEADS * LANE), lambda b, n: (b, n, 0)),
            pl.BlockSpec((None, RET_HEADS * RET_DV, c), lambda b, n: (b, 0, n)),
            pl.BlockSpec((None, RET_HEADS * RET_DV, c), lambda b, n: (b, 0, n)),
            _full(dt.shape), _full(xi.shape), _full(zeta.shape), _full(cdec.shape), _full(gret.shape),
        ],
        out_specs=pl.BlockSpec((None, RET_HEADS * RET_DV, c), lambda b, n: (b, 0, n)),
        scratch_shapes=[pltpu.VMEM((RET_HEADS, RET_DV, LANE), F32)],
        compiler_params=pltpu.CompilerParams(
            dimension_semantics=("parallel", "arbitrary"), vmem_limit_bytes=VMEM_LIMIT),
        name="retention",
    )(rqt, rk, rvt, rgt, dt, xi, zeta, cdec, gret)


def _mlp_kernel(final, x_ref, oat_ref, obt_ref, oct_ref, wout_ref, gmlp_ref, w1_ref, w2_ref, gfin_ref,
                y_ref):
    def tr(ref):
        return ref[...].astype(F32).T.astype(BF16)

    na = MLA_HEADS * MLA_V
    nb = DSA_HEADS * DSA_DIM
    attn = (_dot(tr(oat_ref), wout_ref[0:na, :])
            + _dot(tr(obt_ref), wout_ref[na:na + nb, :])
            + _dot(tr(oct_ref), wout_ref[na + nb:, :]))
    x1 = x_ref[...] + attn
    hf = _rms(x1, gmlp_ref[...]).astype(BF16)
    acc = x1
    for f in range(D_FF // FF_CHUNK):
        u = jnp.maximum(_dot(hf, w1_ref[:, f * FF_CHUNK:(f + 1) * FF_CHUNK]), 0.0)
        acc = acc + _dot((u * u).astype(BF16), w2_ref[f * FF_CHUNK:(f + 1) * FF_CHUNK, :])
    if final:
        acc = _rms(acc, gfin_ref[...])
    y_ref[...] = acc


def _mlp(x, oat, obt, oct, wout, gmlp, w1, w2, gfin, final):
    B, S, D = x.shape
    tm = TM_MLP

    def const(shape):
        nd = len(shape)
        return pl.BlockSpec(shape, lambda b, i: (0,) * nd, pipeline_mode=pl.Buffered(1))

    return pl.pallas_call(
        functools.partial(_mlp_kernel, final),
        out_shape=jax.ShapeDtypeStruct((B, S, D), F32),
        grid=(B, S // tm),
        in_specs=[
            pl.BlockSpec((None, tm, D), lambda b, i: (b, i, 0)),
            pl.BlockSpec((None, oat.shape[1], tm), lambda b, i: (b, 0, i)),
            pl.BlockSpec((None, obt.shape[1], tm), lambda b, i: (b, 0, i)),
            pl.BlockSpec((None, oct.shape[1], tm), lambda b, i: (b, 0, i)),
            const(wout.shape), const(gmlp.shape), const(w1.shape), const(w2.shape), const(gfin.shape),
        ],
        out_specs=pl.BlockSpec((None, tm, D), lambda b, i: (b, i, 0)),
        compiler_params=pltpu.CompilerParams(
            dimension_semantics=("parallel", "parallel"), vmem_limit_bytes=VMEM_LIMIT),
        name="mlp",
    )(x, oat, obt, oct, wout, gmlp, w1, w2, gfin)


def _rope_tables(seq, dim):
    pos = jnp.arange(seq, dtype=F32)
    inv = ROPE_THETA ** (-jnp.arange(0, dim, 2, dtype=F32) / dim)
    ang = pos[:, None] * inv[None, :]
    return jnp.cos(ang), jnp.sin(ang)


def _lane_table(c, s, lo, seq):
    half = c.shape[1]
    ct = jnp.zeros((seq, LANE), F32).at[:, lo:lo + 2 * half].set(jnp.concatenate([c, c], axis=1))
    st = jnp.zeros((seq, LANE), F32).at[:, lo:lo + 2 * half].set(jnp.concatenate([-s, s], axis=1))
    return ct, st


def _pad_cols(w, width):
    return jnp.pad(w, ((0, 0), (0, width - w.shape[1])))


def _prep_layer(w_in, w_uq, w_ukv):
    z = lambda n: jnp.zeros((D_MODEL, n), F32)
    col = lambda a, b: w_in[:, a:b]
    nat = [col(O_CQ, O_CKV), col(O_CKV, O_KR),
           jnp.concatenate([z(64), col(O_KR, O_AQ), z(32)], axis=1),
           _pad_cols(col(O_AK, O_AV), LANE),
           _pad_cols(col(O_KI, O_WI), LANE)]
    for h in range(RET_HEADS):
        nat.append(_pad_cols(col(O_RK + h * RET_DK, O_RK + (h + 1) * RET_DK), LANE))
    wnat = jnp.concatenate(nat, axis=1).astype(BF16)
    wtr = jnp.concatenate([col(O_AQ, O_AK), col(O_QI, O_KI), col(O_AV, O_QI), col(O_RQ, O_RK),
                           col(O_RV, O_RG), col(O_RG, O_END), col(O_WI, O_RQ)], axis=1).T.astype(BF16)
    dq = MLA_NOPE + MLA_ROPE
    wuq = jnp.pad(w_uq.reshape(MLA_Q_RANK, MLA_HEADS, dq), ((0, 0), (0, 0), (0, LANE - dq)))
    wuq = wuq.reshape(MLA_Q_RANK, MLA_HEADS * LANE).T.astype(BF16)
    wkv = w_ukv.reshape(MLA_KV_RANK, MLA_HEADS, MLA_NOPE + MLA_V)
    wuk = jnp.pad(wkv[:, :, :MLA_NOPE], ((0, 0), (0, 0), (0, LANE - MLA_NOPE)))
    wuk = wuk.reshape(MLA_KV_RANK, MLA_HEADS * LANE).astype(BF16)
    wuv = wkv[:, :, MLA_NOPE:].reshape(MLA_KV_RANK, MLA_HEADS * MLA_V).T.astype(BF16)
    return wnat, wtr, wuq, wuk, wuv


def kernel(x, g_mix, w_in, g_q, w_uq, g_kv, w_ukv, g_ret, w_out, g_mlp, w_ff1, w_ff2, g_final):
    B, S, _ = x.shape
    topk = min(DSA_TOPK_MAX, S // 4)

    cos32, sin32 = _rope_tables(S, 32)
    cos64, sin64 = _rope_tables(S, 64)
    tabt = jnp.concatenate([cos64.T, sin64.T, cos32.T, sin32.T], axis=0)
    tabn = jnp.stack(_lane_table(cos32, sin32, 64, S) + _lane_table(cos64, sin64, 0, S)
                     + _lane_table(cos32, sin32, 0, S))

    log_gamma = jnp.log1p(-jnp.exp2(-5.0 - jnp.arange(RET_HEADS, dtype=F32)))
    c = C_RET
    pos = jnp.arange(c, dtype=F32)
    rel = pos[None, :] - pos[:, None]
    dt = jnp.where(rel[None] >= 0, jnp.exp(jnp.maximum(rel, 0.0)[None] * log_gamma[:, None, None]), 0.0)
    xi = jnp.exp((pos + 1.0)[None, :] * log_gamma[:, None])
    zeta = jnp.exp((c - 1.0 - pos)[None, :] * log_gamma[:, None])
    cdec = jnp.broadcast_to(jnp.exp(c * log_gamma)[:, None, None], (RET_HEADS, RET_DV, LANE))

    for l in range(DEPTH):
        wnat, wtr, wuq, wuk, wuv = _prep_layer(w_in[l], w_uq[l], w_ukv[l])
        (qat, ka, vat, qit, wit, ki, qbt, kb, vbt, rqt, rk, rvt, rgt) = _proj(
            x, g_mix[l][None, :], g_q[l][None, :], g_kv[l][None, :], wnat, wtr, wuq, wuk, wuv, tabt, tabn)
        oat = _mla(qat, ka, vat)
        obt = _dsa(qit, wit, ki, qbt, kb, vbt, topk)
        gret = jnp.broadcast_to(g_ret[l][:, None], (RET_HEADS * RET_DV, c))
        oct = _retention(rqt, rk, rvt, rgt, dt, xi, zeta, cdec, gret)
        x = _mlp(x, oat, obt, oct, w_out[l].astype(BF16), g_mlp[l][None, :],
                 w_ff1[l].astype(BF16), w_ff2[l].astype(BF16), g_final[None, :], l == DEPTH - 1)
    return x
```

```python
import functools

import numpy as np
import jax
import jax.numpy as jnp
from jax import lax
from jax.experimental import pallas as pl
from jax.experimental.pallas import tpu as pltpu

D_MODEL = 1024
DEPTH = 2
MLA_HEADS = 8
MLA_NOPE = 64
MLA_ROPE = 32
MLA_V = 64
MLA_Q_RANK = 768
MLA_KV_RANK = 256
DSA_HEADS = 4
DSA_DIM = 64
IDX_HEADS = 8
IDX_DIM = 32
DSA_TOPK_MAX = 256
RET_HEADS = 4
RET_DK = 64
RET_DV = 64
D_FF = 4 * D_MODEL
ROPE_THETA = 10000.0
EPS = 1e-6

MLA_COLS = (MLA_Q_RANK, MLA_KV_RANK, MLA_ROPE)
DSA_COLS = (DSA_HEADS * DSA_DIM, DSA_DIM, DSA_DIM, IDX_HEADS * IDX_DIM, IDX_DIM, IDX_HEADS)
RET_COLS = (RET_HEADS * RET_DK, RET_HEADS * RET_DK, RET_HEADS * RET_DV, RET_HEADS * RET_DV)
SPLITS = MLA_COLS + DSA_COLS + RET_COLS
_OFF = np.concatenate([[0], np.cumsum(SPLITS)]).tolist()
(O_CQ, O_CKV, O_KR, O_AQ, O_AK, O_AV, O_QI, O_KI, O_WI, O_RQ, O_RK, O_RV, O_RG, O_END) = _OFF

LANE = 128
VMEM_LIMIT = 56 * 1024 * 1024

BF16 = jnp.bfloat16
F32 = jnp.float32
NEG = -1e30
LOG2E = float(np.log2(np.e))
INT_MIN = np.int32(-2**31)

N_CQ = 0
N_CKV = N_CQ + MLA_Q_RANK
N_KR = N_CKV + MLA_KV_RANK
N_AK = N_KR + LANE
N_KI = N_AK + LANE
N_RK = N_KI + LANE
N_NAT = N_RK + RET_HEADS * LANE
T_AQ = 0
T_QI = T_AQ + DSA_HEADS * DSA_DIM
T_AV = T_QI + IDX_HEADS * IDX_DIM
T_RQ = T_AV + DSA_DIM
T_RV = T_RQ + RET_HEADS * RET_DK
T_RG = T_RV + RET_HEADS * RET_DV
T_WI = T_RG + RET_HEADS * RET_DV
N_TR = T_WI + IDX_HEADS

TM_PROJ = 256
TQ_MLA = 256
G_MLA = 8
TQ_DSA = 256
CK_DSA = 128
BRACKET_BITS = 25
C_RET = 256
TM_MLP = 512
FF_CHUNK = 1024

_NT = (((1,), (1,)), ((), ()))


def _rms(x, g):
    return x * lax.rsqrt(jnp.mean(x * x, axis=-1, keepdims=True) + EPS) * g


def _dot(a, b):
    return jnp.dot(a, b, preferred_element_type=F32)


def _dot_nt(a, b):
    return lax.dot_general(a, b, _NT, preferred_element_type=F32)


def _rope_rows(x1, x2, c, s):
    return x1 * c - x2 * s, x1 * s + x2 * c


def _rope_lanes(x, c, s_signed, half, first_lo):
    lane = lax.broadcasted_iota(jnp.int32, x.shape, 1)
    first = (lane >= first_lo) & (lane < first_lo + half)
    rot = jnp.where(first, pltpu.roll(x, LANE - half, 1), pltpu.roll(x, half, 1))
    return x * c + rot * s_signed


def _proj_kernel(x_ref, gmix_ref, gq_ref, gkv_ref, wnat_ref, wtr_ref, wuq_ref, wuk_ref, wuv_ref,
                 tabt_ref, tabn_ref,
                 qat_ref, ka_ref, vat_ref, qit_ref, wit_ref, ki_ref, qbt_ref, kb_ref, vbt_ref,
                 rqt_ref, rk_ref, rvt_ref, rgt_ref):
    tm = x_ref.shape[0]
    hb = _rms(x_ref[...], gmix_ref[...]).astype(BF16)

    c64t = tabt_ref[0:32, :]
    s64t = tabt_ref[32:64, :]
    c32t = tabt_ref[64:80, :]
    s32t = tabt_ref[80:96, :]

    cq = _dot(hb, wnat_ref[:, N_CQ:N_CQ + MLA_Q_RANK])
    nq = _rms(cq, gq_ref[...]).astype(BF16)
    qt = _dot_nt(wuq_ref[...], nq)
    scale_a = (MLA_NOPE + MLA_ROPE) ** -0.5 * LOG2E
    for h in range(MLA_HEADS):
        r0 = h * LANE
        qat_ref[r0:r0 + 64, :] = (qt[r0:r0 + 64, :] * scale_a).astype(BF16)
        o1, o2 = _rope_rows(qt[r0 + 64:r0 + 80, :], qt[r0 + 80:r0 + 96, :], c32t, s32t)
        qat_ref[r0 + 64:r0 + 80, :] = (o1 * scale_a).astype(BF16)
        qat_ref[r0 + 80:r0 + 96, :] = (o2 * scale_a).astype(BF16)
        qat_ref[r0 + 96:r0 + 128, :] = jnp.zeros((32, tm), BF16)

    ckv = _dot(hb, wnat_ref[:, N_CKV:N_CKV + MLA_KV_RANK])
    nkv = _rms(ckv, gkv_ref[...]).astype(BF16)
    kn = _dot(nkv, wuk_ref[...])
    keys = _dot(hb, wnat_ref[:, N_KR:N_NAT])
    slab = lambda off: keys[:, off - N_KR:off - N_KR + LANE]
    krp = _rope_lanes(slab(N_KR), tabn_ref[0], tabn_ref[1], 16, 64)
    for h in range(MLA_HEADS):
        ka_ref[:, h * LANE:(h + 1) * LANE] = (kn[:, h * LANE:(h + 1) * LANE] + krp).astype(BF16)
    vt = _dot_nt(wuv_ref[...], nkv).astype(BF16)
    for j in range(tm // LANE):
        vat_ref[j] = vt[:, j * LANE:(j + 1) * LANE]

    kb_ref[...] = _rope_lanes(slab(N_AK), tabn_ref[2], tabn_ref[3], 32, 0).astype(BF16)
    ki_ref[...] = _rope_lanes(slab(N_KI), tabn_ref[4], tabn_ref[5], 16, 0).astype(BF16)

    for h in range(RET_HEADS):
        rkp = _rope_lanes(slab(N_RK + h * LANE), tabn_ref[2], tabn_ref[3], 32, 0) * (RET_DK ** -0.5)
        rk_ref[:, h * LANE:(h + 1) * LANE] = rkp.astype(BF16)

    aqt = _dot_nt(wtr_ref[T_AQ:T_AQ + 256, :], hb)
    scale_b = DSA_DIM ** -0.5 * LOG2E
    for h in range(DSA_HEADS):
        o1, o2 = _rope_rows(aqt[h * 64:h * 64 + 32, :], aqt[h * 64 + 32:h * 64 + 64, :], c64t, s64t)
        qbt_ref[h * LANE:h * LANE + 32, :] = (o1 * scale_b).astype(BF16)
        qbt_ref[h * LANE + 32:h * LANE + 64, :] = (o2 * scale_b).astype(BF16)
        qbt_ref[h * LANE + 64:(h + 1) * LANE, :] = jnp.zeros((64, tm), BF16)

    qit = _dot_nt(wtr_ref[T_QI:T_QI + 256, :], hb)
    for h in range(IDX_HEADS):
        o1, o2 = _rope_rows(qit[h * 32:h * 32 + 16, :], qit[h * 32 + 16:h * 32 + 32, :], c32t, s32t)
        qit_ref[h * LANE:h * LANE + 16, :] = o1.astype(BF16)
        qit_ref[h * LANE + 16:h * LANE + 32, :] = o2.astype(BF16)
        qit_ref[h * LANE + 32:(h + 1) * LANE, :] = jnp.zeros((96, tm), BF16)

    idx_scale = (IDX_HEADS ** -0.5) * (IDX_DIM ** -0.5)
    wit_ref[...] = _dot_nt(wtr_ref[T_WI:T_WI + IDX_HEADS, :], hb) * idx_scale

    avt = _dot_nt(wtr_ref[T_AV:T_AV + DSA_DIM, :], hb).astype(BF16)
    for j in range(tm // LANE):
        vbt_ref[j] = avt[:, j * LANE:(j + 1) * LANE]

    rqt = _dot_nt(wtr_ref[T_RQ:T_RQ + 256, :], hb)
    for h in range(RET_HEADS):
        o1, o2 = _rope_rows(rqt[h * 64:h * 64 + 32, :], rqt[h * 64 + 32:h * 64 + 64, :], c64t, s64t)
        rqt_ref[h * LANE:h * LANE + 32, :] = o1.astype(BF16)
        rqt_ref[h * LANE + 32:h * LANE + 64, :] = o2.astype(BF16)
        rqt_ref[h * LANE + 64:(h + 1) * LANE, :] = jnp.zeros((64, tm), BF16)

    rvt_ref[...] = _dot_nt(wtr_ref[T_RV:T_RV + 256, :], hb).astype(BF16)
    rgt_ref[...] = _dot_nt(wtr_ref[T_RG:T_RG + 256, :], hb)


def _full(shape):
    nd = len(shape)
    return pl.BlockSpec(shape, lambda b, i: (0,) * nd)


def _proj(x, gmix, gq, gkv, wnat, wtr, wuq, wuk, wuv, tabt, tabn):
    B, S, D = x.shape
    tm = TM_PROJ
    nch = tm // LANE
    outs = [
        (jax.ShapeDtypeStruct((B, MLA_HEADS * LANE, S), BF16), pl.BlockSpec((None, MLA_HEADS * LANE, tm), lambda b, i: (b, 0, i))),
        (jax.ShapeDtypeStruct((B, S, MLA_HEADS * LANE), BF16), pl.BlockSpec((None, tm, MLA_HEADS * LANE), lambda b, i: (b, i, 0))),
        (jax.ShapeDtypeStruct((B, S // LANE, MLA_HEADS * MLA_V, LANE), BF16), pl.BlockSpec((None, nch, MLA_HEADS * MLA_V, LANE), lambda b, i: (b, i, 0, 0))),
        (jax.ShapeDtypeStruct((B, IDX_HEADS * LANE, S), BF16), pl.BlockSpec((None, IDX_HEADS * LANE, tm), lambda b, i: (b, 0, i))),
        (jax.ShapeDtypeStruct((B, IDX_HEADS, S), F32), pl.BlockSpec((None, IDX_HEADS, tm), lambda b, i: (b, 0, i))),
        (jax.ShapeDtypeStruct((B, S, LANE), BF16), pl.BlockSpec((None, tm, LANE), lambda b, i: (b, i, 0))),
        (jax.ShapeDtypeStruct((B, DSA_HEADS * LANE, S), BF16), pl.BlockSpec((None, DSA_HEADS * LANE, tm), lambda b, i: (b, 0, i))),
        (jax.ShapeDtypeStruct((B, S, LANE), BF16), pl.BlockSpec((None, tm, LANE), lambda b, i: (b, i, 0))),
        (jax.ShapeDtypeStruct((B, S // LANE, DSA_DIM, LANE), BF16), pl.BlockSpec((None, nch, DSA_DIM, LANE), lambda b, i: (b, i, 0, 0))),
        (jax.ShapeDtypeStruct((B, RET_HEADS * LANE, S), BF16), pl.BlockSpec((None, RET_HEADS * LANE, tm), lambda b, i: (b, 0, i))),
        (jax.ShapeDtypeStruct((B, S, RET_HEADS * LANE), BF16), pl.BlockSpec((None, tm, RET_HEADS * LANE), lambda b, i: (b, i, 0))),
        (jax.ShapeDtypeStruct((B, RET_HEADS * RET_DV, S), BF16), pl.BlockSpec((None, RET_HEADS * RET_DV, tm), lambda b, i: (b, 0, i))),
        (jax.ShapeDtypeStruct((B, RET_HEADS * RET_DV, S), F32), pl.BlockSpec((None, RET_HEADS * RET_DV, tm), lambda b, i: (b, 0, i))),
    ]
    in_specs = [
        pl.BlockSpec((None, tm, D), lambda b, i: (b, i, 0)),
        _full(gmix.shape), _full(gq.shape), _full(gkv.shape),
        _full(wnat.shape), _full(wtr.shape), _full(wuq.shape), _full(wuk.shape), _full(wuv.shape),
        pl.BlockSpec((tabt.shape[0], tm), lambda b, i: (0, i)),
        pl.BlockSpec((tabn.shape[0], tm, LANE), lambda b, i: (0, i, 0)),
    ]
    return pl.pallas_call(
        _proj_kernel,
        out_shape=[o[0] for o in outs],
        grid=(B, S // tm),
        in_specs=in_specs,
        out_specs=[o[1] for o in outs],
        compiler_params=pltpu.CompilerParams(
            dimension_semantics=("parallel", "parallel"), vmem_limit_bytes=VMEM_LIMIT),
        name="proj",
    )(x, gmix, gq, gkv, wnat, wtr, wuq, wuk, wuv, tabt, tabn)


def _mla_kernel(q_ref, k_ref, v_ref, o_ref, s_sc, acc_sc):
    g_heads = acc_sc.shape[0]
    tq = q_ref.shape[1]
    tk = tq
    i = pl.program_id(2)
    nv = tk // LANE

    def tile(kt):
        return pl.ds(pl.multiple_of(kt * tk, tk), tk)

    def scores(kt, ms, masked):
        out = []
        for g in range(g_heads):
            s = _dot(k_ref[tile(kt), g * LANE:(g + 1) * LANE], q_ref[g * LANE:(g + 1) * LANE, :])
            if masked:
                kidx = lax.broadcasted_iota(jnp.int32, (tk, tq), 0)
                qidx = lax.broadcasted_iota(jnp.int32, (tk, tq), 1)
                s = jnp.where(kidx <= qidx, s, NEG)
            s_sc[g, tile(kt), :] = s
            out.append(jnp.maximum(ms[g], jnp.max(s, axis=0, keepdims=True)))
        return tuple(out)

    ms = tuple(jnp.full((1, tq), NEG, F32) for _ in range(g_heads))
    ms = lax.fori_loop(0, i, lambda kt, c: scores(kt, c, False), ms)
    ms = scores(i, ms, True)

    acc_sc[...] = jnp.zeros_like(acc_sc)

    def accum(kt, ls):
        out = []
        for g in range(g_heads):
            p = jnp.exp2(s_sc[g, tile(kt), :] - ms[g])
            vt = jnp.concatenate([v_ref[kt * nv + j, g * MLA_V:(g + 1) * MLA_V, :] for j in range(nv)],
                                 axis=1)
            acc_sc[g] += _dot(vt, p.astype(BF16))
            out.append(ls[g] + jnp.sum(p, axis=0, keepdims=True))
        return tuple(out)

    ls = lax.fori_loop(0, i + 1, accum, tuple(jnp.zeros((1, tq), F32) for _ in range(g_heads)))
    for g in range(g_heads):
        o_ref[g * MLA_V:(g + 1) * MLA_V, :] = (acc_sc[g] / ls[g]).astype(BF16)


def _mla(qat, ka, vat):
    B, _, S = qat.shape
    tq = TQ_MLA
    g = G_MLA
    return pl.pallas_call(
        _mla_kernel,
        out_shape=jax.ShapeDtypeStruct((B, MLA_HEADS * MLA_V, S), BF16),
        grid=(B, MLA_HEADS // g, S // tq),
        in_specs=[
            pl.BlockSpec((None, g * LANE, tq), lambda b, h, i: (b, h, i)),
            pl.BlockSpec((None, S, g * LANE), lambda b, h, i: (b, 0, h)),
            pl.BlockSpec((None, S // LANE, g * MLA_V, LANE), lambda b, h, i: (b, 0, h, 0)),
        ],
        out_specs=pl.BlockSpec((None, g * MLA_V, tq), lambda b, h, i: (b, h, i)),
        scratch_shapes=[pltpu.VMEM((g, S, tq), F32), pltpu.VMEM((g, MLA_V, tq), F32)],
        compiler_params=pltpu.CompilerParams(
            dimension_semantics=("parallel", "parallel", "arbitrary"), vmem_limit_bytes=VMEM_LIMIT),
        name="mla",
    )(qat, ka, vat)


def _count_rows(mask):
    r, t = mask.shape
    return jnp.sum(mask.astype(jnp.int32).reshape(r // 8, 8, t), axis=0)


def _dsa_kernel(topk, qit_ref, wit_ref, ki_ref, qbt_ref, kb_ref, vbt_ref, o_ref,
                sc_sc, att_sc, thr_sc, jsel_sc, acc_sc):
    tq = o_ref.shape[1]
    ck = CK_DSA
    sub = tq // ck
    i = pl.program_id(1)
    nblk = i + 1

    def rows(ref, r0, n):
        if isinstance(r0, int):
            return ref[r0:r0 + n, :]
        return ref[pl.ds(pl.multiple_of(r0, n), n), :]

    qpos_c = i * tq + lax.broadcasted_iota(jnp.int32, (ck, tq), 1)
    krow_c = lax.broadcasted_iota(jnp.int32, (ck, tq), 0)

    def score_body(cb, mx):
        for j in range(sub):
            r0 = cb * tq + j * ck
            kc = rows(ki_ref, r0, ck)
            acc = jnp.zeros((ck, tq), F32)
            for h in range(IDX_HEADS):
                rel = _dot(kc, qit_ref[h * LANE:(h + 1) * LANE, :])
                acc = acc + wit_ref[h:h + 1, :] * jnp.maximum(rel, 0.0)
            acc = jnp.where(r0 + krow_c <= qpos_c, acc, -jnp.inf)
            sc_sc[pl.ds(pl.multiple_of(r0, ck), ck), :] = acc
            mx = jnp.maximum(mx, jnp.max(acc.reshape(ck // 8, 8, tq), axis=0))
        return mx

    mx = lax.fori_loop(0, nblk, score_body, jnp.full((8, tq), -jnp.inf, F32))
    smax = jnp.max(mx, axis=0, keepdims=True)

    krow = lax.broadcasted_iota(jnp.int32, (tq, tq), 0)
    qpos = i * tq + lax.broadcasted_iota(jnp.int32, (tq, tq), 1)

    def count(pred, nb=None):
        def body(cb, a):
            return a + _count_rows(pred(rows(sc_sc, cb * tq, tq), cb * tq + krow))
        part = jnp.zeros((8, tq), jnp.int32)
        if nb is None:
            part = lax.fori_loop(0, nblk, body, part)
        else:
            for cb in range(nb):
                part = body(cb, part)
        return jnp.sum(part, axis=0, keepdims=True)

    def flip(word):
        return jnp.where(word < 0, word ^ jnp.int32(0x7FFFFFFF), word)

    def key_to_score(key):
        return pltpu.bitcast(flip(key), F32)

    thr_sc[...] = jnp.full((1, tq), -jnp.inf, F32)
    jsel_sc[...] = jnp.full((1, tq), sc_sc.shape[0], jnp.int32)

    def bisect(nb):
        n_ge = lambda key: count(lambda s, kidx: s >= key_to_score(key), nb)
        hi = flip(pltpu.bitcast(smax, jnp.int32)) + 1
        key_ninf = INT_MIN + np.int32(0x7FFFFF)
        width = np.int32(1 << BRACKET_BITS)
        near = jnp.where(hi >= key_ninf + width, hi - width, key_ninf)
        ok = n_ge(near) >= topk
        lo = jnp.where(ok, near, key_ninf)
        steps = jnp.where(jnp.min(ok.astype(jnp.int32)) > 0, BRACKET_BITS, 32)

        def body(_, lohi):
            lo, hi = lohi
            mid = (lo & hi) + ((lo ^ hi) >> 1)
            up = n_ge(mid) >= topk
            return jnp.where(up, mid, lo), jnp.where(up, hi, mid)

        lo, hi = lax.fori_loop(0, steps, body, (lo, hi))
        thr_sc[...] = key_to_score(lo)

    for v in range(topk // tq, sc_sc.shape[0] // tq):
        pl.when(i == v)(functools.partial(bisect, v + 1))

    @pl.when(i * tq >= topk)
    def _():
        thr = thr_sc[...]
        n_ge = count(lambda s, kidx: s >= thr)
        excess = jnp.max(n_ge) > topk

        @pl.when(excess)
        def _():
            need = topk - count(lambda s, kidx: s > thr)

            def jbit(t, j):
                cand = j + jnp.left_shift(jnp.int32(1), 10 - t)
                n = count(lambda s, kidx: (s == thr) & (kidx < cand))
                return jnp.where(n < need, cand, j)

            jsel_sc[...] = lax.fori_loop(0, 11, jbit, jnp.zeros((1, tq), jnp.int32))

    thr = thr_sc[...]
    jsel = jsel_sc[...]
    nh = DSA_HEADS
    qall = jnp.concatenate([qbt_ref[h * LANE:(h + 1) * LANE, :] for h in range(nh)], axis=1)

    def logit_body(cb, m):
        sc = rows(sc_sc, cb * tq, tq)
        kidx = cb * tq + krow
        sel = ((sc > thr) | ((sc == thr) & (kidx <= jsel))) & (kidx <= qpos)
        bias = jnp.where(sel, 0.0, NEG)
        for j in range(sub):
            r0 = cb * tq + j * ck
            b = bias[j * ck:(j + 1) * ck, :]
            s = _dot(rows(kb_ref, r0, ck), qall) + jnp.concatenate([b] * nh, axis=1)
            att_sc[pl.ds(pl.multiple_of(r0, ck), ck), :] = s
            m = jnp.maximum(m, jnp.max(s, axis=0, keepdims=True))
        return m

    m = lax.fori_loop(0, nblk, logit_body, jnp.full((1, nh * tq), NEG, F32))

    acc_sc[...] = jnp.zeros_like(acc_sc)

    def att_body(cb, l):
        for j in range(sub):
            r0 = cb * tq + j * ck
            p = jnp.exp2(rows(att_sc, r0, ck) - m)
            l = l + jnp.sum(p, axis=0, keepdims=True)
            acc_sc[...] += _dot(vbt_ref[cb * sub + j], p.astype(BF16))
        return l

    l = lax.fori_loop(0, nblk, att_body, jnp.zeros((1, nh * tq), F32))
    out = acc_sc[...] / l
    for h in range(nh):
        o_ref[h * DSA_DIM:(h + 1) * DSA_DIM, :] = out[:, h * tq:(h + 1) * tq].astype(BF16)


def _dsa(qit, wit, ki, qbt, kb, vbt, topk):
    B, _, S = qit.shape
    tq = TQ_DSA
    assert topk % tq == 0 or topk >= S, "query tiles must not straddle the top-k boundary"
    return pl.pallas_call(
        functools.partial(_dsa_kernel, topk),
        out_shape=jax.ShapeDtypeStruct((B, DSA_HEADS * DSA_DIM, S), BF16),
        grid=(B, S // tq),
        in_specs=[
            pl.BlockSpec((None, IDX_HEADS * LANE, tq), lambda b, i: (b, 0, i)),
            pl.BlockSpec((None, IDX_HEADS, tq), lambda b, i: (b, 0, i)),
            pl.BlockSpec((None, S, LANE), lambda b, i: (b, 0, 0)),
            pl.BlockSpec((None, DSA_HEADS * LANE, tq), lambda b, i: (b, 0, i)),
            pl.BlockSpec((None, S, LANE), lambda b, i: (b, 0, 0)),
            pl.BlockSpec((None, S // LANE, DSA_DIM, LANE), lambda b, i: (b, 0, 0, 0)),
        ],
        out_specs=pl.BlockSpec((None, DSA_HEADS * DSA_DIM, tq), lambda b, i: (b, 0, i)),
        scratch_shapes=[
            pltpu.VMEM((S, tq), F32),
            pltpu.VMEM((S, DSA_HEADS * tq), F32),
            pltpu.VMEM((1, tq), F32),
            pltpu.VMEM((1, tq), jnp.int32),
            pltpu.VMEM((DSA_DIM, DSA_HEADS * tq), F32),
        ],
        compiler_params=pltpu.CompilerParams(
            dimension_semantics=("parallel", "arbitrary"), vmem_limit_bytes=VMEM_LIMIT),
        name="dsa",
    )(qit, wit, ki, qbt, kb, vbt)


def _ret_kernel(rqt_ref, rk_ref, rvt_ref, rgt_ref, dt_ref, xi_ref, zeta_ref, cdec_ref, gret_ref,
                o_ref, state_sc):
    n = pl.program_id(1)

    @pl.when(n == 0)
    def _():
        state_sc[...] = jnp.zeros_like(state_sc)

    for h in range(RET_HEADS):
        qt = rqt_ref[h * LANE:(h + 1) * LANE, :]
        k = rk_ref[:, h * LANE:(h + 1) * LANE]
        vt = rvt_ref[h * RET_DV:(h + 1) * RET_DV, :]
        inner_t = _dot(k, qt) * dt_ref[h]
        rt = state_sc[h]
        qx = (qt.astype(F32) * xi_ref[h:h + 1, :]).astype(BF16)
        out_t = _dot(vt, inner_t.astype(BF16)) + _dot(rt.astype(BF16), qx)
        vz = (vt.astype(F32) * zeta_ref[h:h + 1, :]).astype(BF16)
        state_sc[h] = rt * cdec_ref[h] + _dot(vz, k)

        mu = jnp.mean(out_t, axis=0, keepdims=True)
        xc = out_t - mu
        var = jnp.mean(xc * xc, axis=0, keepdims=True)
        y = xc * lax.rsqrt(var + EPS) * gret_ref[h * RET_DV:(h + 1) * RET_DV, :]
        g = rgt_ref[h * RET_DV:(h + 1) * RET_DV, :]
        o_ref[h * RET_DV:(h + 1) * RET_DV, :] = (g * jax.nn.sigmoid(g) * y).astype(BF16)


def _retention(rqt, rk, rvt, rgt, dt, xi, zeta, cdec, gret):
    B, _, S = rqt.shape
    c = C_RET
    return pl.pallas_call(
        _ret_kernel,
        out_shape=jax.ShapeDtypeStruct((B, RET_HEADS * RET_DV, S), BF16),
        grid=(B, S // c),
        in_specs=[
            pl.BlockSpec((None, RET_HEADS * LANE, c), lambda b, n: (b, 0, n)),
            pl.BlockSpec((None, c, RET_HEADS * LANE), lambda b, n: (b, n, 0)),
            pl.BlockSpec((None, RET_HEADS * RET_DV, c), lambda b, n: (b, 0, n)),
            pl.BlockSpec((None, RET_HEADS * RET_DV, c), lambda b, n: (b, 0, n)),
            _full(dt.shape), _full(xi.shape), _full(zeta.shape), _full(cdec.shape), _full(gret.shape),
        ],
        out_specs=pl.BlockSpec((None, RET_HEADS * RET_DV, c), lambda b, n: (b, 0, n)),
        scratch_shapes=[pltpu.VMEM((RET_HEADS, RET_DV, LANE), F32)],
        compiler_params=pltpu.CompilerParams(
            dimension_semantics=("parallel", "arbitrary"), vmem_limit_bytes=VMEM_LIMIT),
        name="retention",
    )(rqt, rk, rvt, rgt, dt, xi, zeta, cdec, gret)


def _mlp_kernel(final, x_ref, oat_ref, obt_ref, oct_ref, wout_ref, gmlp_ref, w1_ref, w2_ref, gfin_ref,
                y_ref):
    def tr(ref):
        return ref[...].astype(F32).T.astype(BF16)

    na = MLA_HEADS * MLA_V
    nb = DSA_HEADS * DSA_DIM
    attn = (_dot(tr(oat_ref), wout_ref[0:na, :])
            + _dot(tr(obt_ref), wout_ref[na:na + nb, :])
            + _dot(tr(oct_ref), wout_ref[na + nb:, :]))
    x1 = x_ref[...] + attn
    hf = _rms(x1, gmlp_ref[...]).astype(BF16)
    acc = x1
    for f in range(D_FF // FF_CHUNK):
        u = jnp.maximum(_dot(hf, w1_ref[:, f * FF_CHUNK:(f + 1) * FF_CHUNK]), 0.0)
        acc = acc + _dot((u * u).astype(BF16), w2_ref[f * FF_CHUNK:(f + 1) * FF_CHUNK, :])
    if final:
        acc = _rms(acc, gfin_ref[...])
    y_ref[...] = acc


def _mlp(x, oat, obt, oct, wout, gmlp, w1, w2, gfin, final):
    B, S, D = x.shape
    tm = TM_MLP

    def const(shape):
        nd = len(shape)
        return pl.BlockSpec(shape, lambda b, i: (0,) * nd, pipeline_mode=pl.Buffered(1))

    return pl.pallas_call(
        functools.partial(_mlp_kernel, final),
        out_shape=jax.ShapeDtypeStruct((B, S, D), F32),
        grid=(B, S // tm),
        in_specs=[
            pl.BlockSpec((None, tm, D), lambda b, i: (b, i, 0)),
            pl.BlockSpec((None, oat.shape[1], tm), lambda b, i: (b, 0, i)),
            pl.BlockSpec((None, obt.shape[1], tm), lambda b, i: (b, 0, i)),
            pl.BlockSpec((None, oct.shape[1], tm), lambda b, i: (b, 0, i)),
            const(wout.shape), const(gmlp.shape), const(w1.shape), const(w2.shape), const(gfin.shape),
        ],
        out_specs=pl.BlockSpec((None, tm, D), lambda b, i: (b, i, 0)),
        compiler_params=pltpu.CompilerParams(
            dimension_semantics=("parallel", "parallel"), vmem_limit_bytes=VMEM_LIMIT),
        name="mlp",
    )(x, oat, obt, oct, wout, gmlp, w1, w2, gfin)


def _rope_tables(seq, dim):
    pos = jnp.arange(seq, dtype=F32)
    inv = ROPE_THETA ** (-jnp.arange(0, dim, 2, dtype=F32) / dim)
    ang = pos[:, None] * inv[None, :]
    return jnp.cos(ang), jnp.sin(ang)


def _lane_table(c, s, lo, seq):
    half = c.shape[1]
    ct = jnp.zeros((seq, LANE), F32).at[:, lo:lo + 2 * half].set(jnp.concatenate([c, c], axis=1))
    st = jnp.zeros((seq, LANE), F32).at[:, lo:lo + 2 * half].set(jnp.concatenate([-s, s], axis=1))
    return ct, st


def _pad_cols(w, width):
    return jnp.pad(w, ((0, 0), (0, width - w.shape[1])))


def _prep_layer(w_in, w_uq, w_ukv):
    z = lambda n: jnp.zeros((D_MODEL, n), F32)
    col = lambda a, b: w_in[:, a:b]
    nat = [col(O_CQ, O_CKV), col(O_CKV, O_KR),
           jnp.concatenate([z(64), col(O_KR, O_AQ), z(32)], axis=1),
           _pad_cols(col(O_AK, O_AV), LANE),
           _pad_cols(col(O_KI, O_WI), LANE)]
    for h in range(RET_HEADS):
        nat.append(_pad_cols(col(O_RK + h * RET_DK, O_RK + (h + 1) * RET_DK), LANE))
    wnat = jnp.concatenate(nat, axis=1).astype(BF16)
    wtr = jnp.concatenate([col(O_AQ, O_AK), col(O_QI, O_KI), col(O_AV, O_QI), col(O_RQ, O_RK),
                           col(O_RV, O_RG), col(O_RG, O_END), col(O_WI, O_RQ)], axis=1).T.astype(BF16)
    dq = MLA_NOPE + MLA_ROPE
    wuq = jnp.pad(w_uq.reshape(MLA_Q_RANK, MLA_HEADS, dq), ((0, 0), (0, 0), (0, LANE - dq)))
    wuq = wuq.reshape(MLA_Q_RANK, MLA_HEADS * LANE).T.astype(BF16)
    wkv = w_ukv.reshape(MLA_KV_RANK, MLA_HEADS, MLA_NOPE + MLA_V)
    wuk = jnp.pad(wkv[:, :, :MLA_NOPE], ((0, 0), (0, 0), (0, LANE - MLA_NOPE)))
    wuk = wuk.reshape(MLA_KV_RANK, MLA_HEADS * LANE).astype(BF16)
    wuv = wkv[:, :, MLA_NOPE:].reshape(MLA_KV_RANK, MLA_HEADS * MLA_V).T.astype(BF16)
    return wnat, wtr, wuq, wuk, wuv


def kernel(x, g_mix, w_in, g_q, w_uq, g_kv, w_ukv, g_ret, w_out, g_mlp, w_ff1, w_ff2, g_final):
    B, S, _ = x.shape
    topk = min(DSA_TOPK_MAX, S // 4)

    cos32, sin32 = _rope_tables(S, 32)
    cos64, sin64 = _rope_tables(S, 64)
    tabt = jnp.concatenate([cos64.T, sin64.T, cos32.T, sin32.T], axis=0)
    tabn = jnp.stack(_lane_table(cos32, sin32, 64, S) + _lane_table(cos64, sin64, 0, S)
                     + _lane_table(cos32, sin32, 0, S))

    log_gamma = jnp.log1p(-jnp.exp2(-5.0 - jnp.arange(RET_HEADS, dtype=F32)))
    c = C_RET
    pos = jnp.arange(c, dtype=F32)
    rel = pos[None, :] - pos[:, None]
    dt = jnp.where(rel[None] >= 0, jnp.exp(jnp.maximum(rel, 0.0)[None] * log_gamma[:, None, None]), 0.0)
    xi = jnp.exp((pos + 1.0)[None, :] * log_gamma[:, None])
    zeta = jnp.exp((c - 1.0 - pos)[None, :] * log_gamma[:, None])
    cdec = jnp.broadcast_to(jnp.exp(c * log_gamma)[:, None, None], (RET_HEADS, RET_DV, LANE))

    for l in range(DEPTH):
        wnat, wtr, wuq, wuk, wuv = _prep_layer(w_in[l], w_uq[l], w_ukv[l])
        (qat, ka, vat, qit, wit, ki, qbt, kb, vbt, rqt, rk, rvt, rgt) = _proj(
            x, g_mix[l][None, :], g_q[l][None, :], g_kv[l][None, :], wnat, wtr, wuq, wuk, wuv, tabt, tabn)
        oat = _mla(qat, ka, vat)
        obt = _dsa(qit, wit, ki, qbt, kb, vbt, topk)
        gret = jnp.broadcast_to(g_ret[l][:, None], (RET_HEADS * RET_DV, c))
        oct = _retention(rqt, rk, rvt, rgt, dt, xi, zeta, cdec, gret)
        x = _mlp(x, oat, obt, oct, w_out[l].astype(BF16), g_mlp[l][None, :],
                 w_ff1[l].astype(BF16), w_ff2[l].astype(BF16), g_final[None, :], l == DEPTH - 1)
    return x
```

```python
import functools

import numpy as np
import jax
import jax.numpy as jnp
from jax import lax
from jax.experimental import pallas as pl
from jax.experimental.pallas import tpu as pltpu

D_MODEL = 1024
DEPTH = 2
MLA_HEADS = 8
MLA_NOPE = 64
MLA_ROPE = 32
MLA_V = 64
MLA_Q_RANK = 768
MLA_KV_RANK = 256
DSA_HEADS = 4
DSA_DIM = 64
IDX_HEADS = 8
IDX_DIM = 32
DSA_TOPK_MAX = 256
RET_HEADS = 4
RET_DK = 64
RET_DV = 64
D_FF = 4 * D_MODEL
ROPE_THETA = 10000.0
EPS = 1e-6

MLA_COLS = (MLA_Q_RANK, MLA_KV_RANK, MLA_ROPE)
DSA_COLS = (DSA_HEADS * DSA_DIM, DSA_DIM, DSA_DIM, IDX_HEADS * IDX_DIM, IDX_DIM, IDX_HEADS)
RET_COLS = (RET_HEADS * RET_DK, RET_HEADS * RET_DK, RET_HEADS * RET_DV, RET_HEADS * RET_DV)
SPLITS = MLA_COLS + DSA_COLS + RET_COLS
_OFF = np.concatenate([[0], np.cumsum(SPLITS)]).tolist()
(O_CQ, O_CKV, O_KR, O_AQ, O_AK, O_AV, O_QI, O_KI, O_WI, O_RQ, O_RK, O_RV, O_RG, O_END) = _OFF

LANE = 128
VMEM_LIMIT = 56 * 1024 * 1024

BF16 = jnp.bfloat16
F32 = jnp.float32
NEG = -1e30
LOG2E = float(np.log2(np.e))
INT_MIN = np.int32(-2**31)

N_CQ = 0
N_CKV = N_CQ + MLA_Q_RANK
N_KR = N_CKV + MLA_KV_RANK
N_AK = N_KR + LANE
N_KI = N_AK + LANE
N_RK = N_KI + LANE
N_NAT = N_RK + RET_HEADS * LANE
T_AQ = 0
T_QI = T_AQ + DSA_HEADS * DSA_DIM
T_AV = T_QI + IDX_HEADS * IDX_DIM
T_RQ = T_AV + DSA_DIM
T_RV = T_RQ + RET_HEADS * RET_DK
T_RG = T_RV + RET_HEADS * RET_DV
T_WI = T_RG + RET_HEADS * RET_DV
N_TR = T_WI + IDX_HEADS

TM_PROJ = 256
TQ_MLA = 256
G_MLA = 8
TQ_DSA = 256
CK_DSA = 128
C_RET = 256
TM_MLP = 512
FF_CHUNK = 1024

_NT = (((1,), (1,)), ((), ()))


def _rms(x, g):
    return x * lax.rsqrt(jnp.mean(x * x, axis=-1, keepdims=True) + EPS) * g


def _dot(a, b):
    return jnp.dot(a, b, preferred_element_type=F32)


def _dot_nt(a, b):
    return lax.dot_general(a, b, _NT, preferred_element_type=F32)


def _rope_rows(x1, x2, c, s):
    return x1 * c - x2 * s, x1 * s + x2 * c


def _rope_lanes(x, c, s_signed, half, first_lo):
    lane = lax.broadcasted_iota(jnp.int32, x.shape, 1)
    first = (lane >= first_lo) & (lane < first_lo + half)
    rot = jnp.where(first, pltpu.roll(x, LANE - half, 1), pltpu.roll(x, half, 1))
    return x * c + rot * s_signed


def _proj_kernel(x_ref, gmix_ref, gq_ref, gkv_ref, wnat_ref, wtr_ref, wuq_ref, wuk_ref, wuv_ref,
                 tabt_ref, tabn_ref,
                 qat_ref, ka_ref, vat_ref, qit_ref, wit_ref, ki_ref, qbt_ref, kb_ref, vbt_ref,
                 rqt_ref, rk_ref, rvt_ref, rgt_ref):
    tm = x_ref.shape[0]
    hb = _rms(x_ref[...], gmix_ref[...]).astype(BF16)

    c64t = tabt_ref[0:32, :]
    s64t = tabt_ref[32:64, :]
    c32t = tabt_ref[64:80, :]
    s32t = tabt_ref[80:96, :]

    cq = _dot(hb, wnat_ref[:, N_CQ:N_CQ + MLA_Q_RANK])
    nq = _rms(cq, gq_ref[...]).astype(BF16)
    qt = _dot_nt(wuq_ref[...], nq)
    scale_a = (MLA_NOPE + MLA_ROPE) ** -0.5 * LOG2E
    for h in range(MLA_HEADS):
        r0 = h * LANE
        qat_ref[r0:r0 + 64, :] = (qt[r0:r0 + 64, :] * scale_a).astype(BF16)
        o1, o2 = _rope_rows(qt[r0 + 64:r0 + 80, :], qt[r0 + 80:r0 + 96, :], c32t, s32t)
        qat_ref[r0 + 64:r0 + 80, :] = (o1 * scale_a).astype(BF16)
        qat_ref[r0 + 80:r0 + 96, :] = (o2 * scale_a).astype(BF16)
        qat_ref[r0 + 96:r0 + 128, :] = jnp.zeros((32, tm), BF16)

    ckv = _dot(hb, wnat_ref[:, N_CKV:N_CKV + MLA_KV_RANK])
    nkv = _rms(ckv, gkv_ref[...]).astype(BF16)
    kn = _dot(nkv, wuk_ref[...])
    keys = _dot(hb, wnat_ref[:, N_KR:N_NAT])
    slab = lambda off: keys[:, off - N_KR:off - N_KR + LANE]
    krp = _rope_lanes(slab(N_KR), tabn_ref[0], tabn_ref[1], 16, 64)
    for h in range(MLA_HEADS):
        ka_ref[:, h * LANE:(h + 1) * LANE] = (kn[:, h * LANE:(h + 1) * LANE] + krp).astype(BF16)
    vt = _dot_nt(wuv_ref[...], nkv).astype(BF16)
    for j in range(tm // LANE):
        vat_ref[j] = vt[:, j * LANE:(j + 1) * LANE]

    kb_ref[...] = _rope_lanes(slab(N_AK), tabn_ref[2], tabn_ref[3], 32, 0).astype(BF16)
    ki_ref[...] = _rope_lanes(slab(N_KI), tabn_ref[4], tabn_ref[5], 16, 0).astype(BF16)

    for h in range(RET_HEADS):
        rkp = _rope_lanes(slab(N_RK + h * LANE), tabn_ref[2], tabn_ref[3], 32, 0) * (RET_DK ** -0.5)
        rk_ref[:, h * LANE:(h + 1) * LANE] = rkp.astype(BF16)

    aqt = _dot_nt(wtr_ref[T_AQ:T_AQ + 256, :], hb)
    scale_b = DSA_DIM ** -0.5 * LOG2E
    for h in range(DSA_HEADS):
        o1, o2 = _rope_rows(aqt[h * 64:h * 64 + 32, :], aqt[h * 64 + 32:h * 64 + 64, :], c64t, s64t)
        qbt_ref[h * LANE:h * LANE + 32, :] = (o1 * scale_b).astype(BF16)
        qbt_ref[h * LANE + 32:h * LANE + 64, :] = (o2 * scale_b).astype(BF16)
        qbt_ref[h * LANE + 64:(h + 1) * LANE, :] = jnp.zeros((64, tm), BF16)

    qit = _dot_nt(wtr_ref[T_QI:T_QI + 256, :], hb)
    for h in range(IDX_HEADS):
        o1, o2 = _rope_rows(qit[h * 32:h * 32 + 16, :], qit[h * 32 + 16:h * 32 + 32, :], c32t, s32t)
        qit_ref[h * LANE:h * LANE + 16, :] = o1.astype(BF16)
        qit_ref[h * LANE + 16:h * LANE + 32, :] = o2.astype(BF16)
        qit_ref[h * LANE + 32:(h + 1) * LANE, :] = jnp.zeros((96, tm), BF16)

    idx_scale = (IDX_HEADS ** -0.5) * (IDX_DIM ** -0.5)
    wit_ref[...] = _dot_nt(wtr_ref[T_WI:T_WI + IDX_HEADS, :], hb) * idx_scale

    avt = _dot_nt(wtr_ref[T_AV:T_AV + DSA_DIM, :], hb).astype(BF16)
    for j in range(tm // LANE):
        vbt_ref[j] = avt[:, j * LANE:(j + 1) * LANE]

    rqt = _dot_nt(wtr_ref[T_RQ:T_RQ + 256, :], hb)
    for h in range(RET_HEADS):
        o1, o2 = _rope_rows(rqt[h * 64:h * 64 + 32, :], rqt[h * 64 + 32:h * 64 + 64, :], c64t, s64t)
        rqt_ref[h * LANE:h * LANE + 32, :] = o1.astype(BF16)
        rqt_ref[h * LANE + 32:h * LANE + 64, :] = o2.astype(BF16)
        rqt_ref[h * LANE + 64:(h + 1) * LANE, :] = jnp.zeros((64, tm), BF16)

    rvt_ref[...] = _dot_nt(wtr_ref[T_RV:T_RV + 256, :], hb).astype(BF16)
    rgt_ref[...] = _dot_nt(wtr_ref[T_RG:T_RG + 256, :], hb)


def _full(shape):
    nd = len(shape)
    return pl.BlockSpec(shape, lambda b, i: (0,) * nd)


def _proj(x, gmix, gq, gkv, wnat, wtr, wuq, wuk, wuv, tabt, tabn):
    B, S, D = x.shape
    tm = TM_PROJ
    nch = tm // LANE
    outs = [
        (jax.ShapeDtypeStruct((B, MLA_HEADS * LANE, S), BF16), pl.BlockSpec((None, MLA_HEADS * LANE, tm), lambda b, i: (b, 0, i))),
        (jax.ShapeDtypeStruct((B, S, MLA_HEADS * LANE), BF16), pl.BlockSpec((None, tm, MLA_HEADS * LANE), lambda b, i: (b, i, 0))),
        (jax.ShapeDtypeStruct((B, S // LANE, MLA_HEADS * MLA_V, LANE), BF16), pl.BlockSpec((None, nch, MLA_HEADS * MLA_V, LANE), lambda b, i: (b, i, 0, 0))),
        (jax.ShapeDtypeStruct((B, IDX_HEADS * LANE, S), BF16), pl.BlockSpec((None, IDX_HEADS * LANE, tm), lambda b, i: (b, 0, i))),
        (jax.ShapeDtypeStruct((B, IDX_HEADS, S), F32), pl.BlockSpec((None, IDX_HEADS, tm), lambda b, i: (b, 0, i))),
        (jax.ShapeDtypeStruct((B, S, LANE), BF16), pl.BlockSpec((None, tm, LANE), lambda b, i: (b, i, 0))),
        (jax.ShapeDtypeStruct((B, DSA_HEADS * LANE, S), BF16), pl.BlockSpec((None, DSA_HEADS * LANE, tm), lambda b, i: (b, 0, i))),
        (jax.ShapeDtypeStruct((B, S, LANE), BF16), pl.BlockSpec((None, tm, LANE), lambda b, i: (b, i, 0))),
        (jax.ShapeDtypeStruct((B, S // LANE, DSA_DIM, LANE), BF16), pl.BlockSpec((None, nch, DSA_DIM, LANE), lambda b, i: (b, i, 0, 0))),
        (jax.ShapeDtypeStruct((B, RET_HEADS * LANE, S), BF16), pl.BlockSpec((None, RET_HEADS * LANE, tm), lambda b, i: (b, 0, i))),
        (jax.ShapeDtypeStruct((B, S, RET_HEADS * LANE), BF16), pl.BlockSpec((None, tm, RET_HEADS * LANE), lambda b, i: (b, i, 0))),
        (jax.ShapeDtypeStruct((B, RET_HEADS * RET_DV, S), BF16), pl.BlockSpec((None, RET_HEADS * RET_DV, tm), lambda b, i: (b, 0, i))),
        (jax.ShapeDtypeStruct((B, RET_HEADS * RET_DV, S), F32), pl.BlockSpec((None, RET_HEADS * RET_DV, tm), lambda b, i: (b, 0, i))),
    ]
    in_specs = [
        pl.BlockSpec((None, tm, D), lambda b, i: (b, i, 0)),
        _full(gmix.shape), _full(gq.shape), _full(gkv.shape),
        _full(wnat.shape), _full(wtr.shape), _full(wuq.shape), _full(wuk.shape), _full(wuv.shape),
        pl.BlockSpec((tabt.shape[0], tm), lambda b, i: (0, i)),
        pl.BlockSpec((tabn.shape[0], tm, LANE), lambda b, i: (0, i, 0)),
    ]
    return pl.pallas_call(
        _proj_kernel,
        out_shape=[o[0] for o in outs],
        grid=(B, S // tm),
        in_specs=in_specs,
        out_specs=[o[1] for o in outs],
        compiler_params=pltpu.CompilerParams(
            dimension_semantics=("parallel", "parallel"), vmem_limit_bytes=VMEM_LIMIT),
        name="proj",
    )(x, gmix, gq, gkv, wnat, wtr, wuq, wuk, wuv, tabt, tabn)


def _mla_kernel(q_ref, k_ref, v_ref, o_ref, s0_sc, s1_sc, m_sc, l_sc, acc_sc):
    g_heads = acc_sc.shape[0]
    tq = q_ref.shape[1]
    tk = tq
    i = pl.program_id(2)
    nv = tk // LANE
    kidx = lax.broadcasted_iota(jnp.int32, (tk, tq), 0)
    qidx = lax.broadcasted_iota(jnp.int32, (tk, tq), 1)
    bufs = (s0_sc, s1_sc)

    def scores(kt, buf, causal=None):
        for g in range(g_heads):
            s = _dot(k_ref[pl.ds(pl.multiple_of(kt * tk, tk), tk), g * LANE:(g + 1) * LANE],
                     q_ref[g * LANE:(g + 1) * LANE, :])
            if causal is not None:
                s = jnp.where(causal, s, NEG)
            buf[g] = s

    def softmax(kt, buf):
        for g in range(g_heads):
            s = buf[g]
            m_old = m_sc[g]
            m_new = jnp.maximum(m_old, jnp.max(s, axis=0, keepdims=True))
            a = jnp.exp2(m_old - m_new)
            p = jnp.exp2(s - m_new)
            vt = jnp.concatenate([v_ref[kt * nv + j, g * MLA_V:(g + 1) * MLA_V, :] for j in range(nv)],
                                 axis=1)
            m_sc[g] = m_new
            l_sc[g] = a * l_sc[g] + jnp.sum(p, axis=0, keepdims=True)
            acc_sc[g] = a * acc_sc[g] + _dot(vt, p.astype(BF16))

    m_sc[...] = jnp.full(m_sc.shape, NEG, F32)
    l_sc[...] = jnp.zeros_like(l_sc)
    acc_sc[...] = jnp.zeros_like(acc_sc)

    def step(k, parity, diagonal=False):
        scores(k + 1, bufs[1 - parity], (kidx <= qidx) if diagonal else None)
        softmax(k, bufs[parity])

    scores(0, bufs[0], kidx <= i * tq + qidx)

    def pair(p, _):
        step(2 * p, 0)
        step(2 * p + 1, 1)
        return 0

    lax.fori_loop(0, (i - 1) // 2, pair, 0)

    @pl.when(i % 2 == 0)
    def _():
        @pl.when(i >= 2)
        def _():
            step(i - 2, 0)
            step(i - 1, 1, diagonal=True)
        softmax(i, bufs[0])

    @pl.when(i % 2 == 1)
    def _():
        step(i - 1, 0, diagonal=True)
        softmax(i, bufs[1])

    for g in range(g_heads):
        o_ref[g * MLA_V:(g + 1) * MLA_V, :] = (acc_sc[g] / l_sc[g]).astype(BF16)


def _mla(qat, ka, vat):
    B, _, S = qat.shape
    tq = TQ_MLA
    g = G_MLA
    return pl.pallas_call(
        _mla_kernel,
        out_shape=jax.ShapeDtypeStruct((B, MLA_HEADS * MLA_V, S), BF16),
        grid=(B, MLA_HEADS // g, S // tq),
        in_specs=[
            pl.BlockSpec((None, g * LANE, tq), lambda b, h, i: (b, h, i)),
            pl.BlockSpec((None, S, g * LANE), lambda b, h, i: (b, 0, h)),
            pl.BlockSpec((None, S // LANE, g * MLA_V, LANE), lambda b, h, i: (b, 0, h, 0)),
        ],
        out_specs=pl.BlockSpec((None, g * MLA_V, tq), lambda b, h, i: (b, h, i)),
        scratch_shapes=[pltpu.VMEM((g, tq, tq), F32), pltpu.VMEM((g, tq, tq), F32),
                        pltpu.VMEM((g, 1, tq), F32), pltpu.VMEM((g, 1, tq), F32),
                        pltpu.VMEM((g, MLA_V, tq), F32)],
        compiler_params=pltpu.CompilerParams(
            dimension_semantics=("parallel", "parallel", "arbitrary"), vmem_limit_bytes=VMEM_LIMIT),
        name="mla",
    )(qat, ka, vat)


def _count_rows(mask):
    r, t = mask.shape
    return jnp.sum(mask.astype(jnp.int32).reshape(r // 8, 8, t), axis=0)


def _dsa_kernel(topk, qit_ref, wit_ref, ki_ref, qbt_ref, kb_ref, vbt_ref, o_ref,
                sc_sc, att_sc, thr_sc, jsel_sc, acc_sc):
    tq = o_ref.shape[1]
    ck = CK_DSA
    sub = tq // ck
    i = pl.program_id(1)
    nblk = i + 1

    def rows(ref, r0, n):
        if isinstance(r0, int):
            return ref[r0:r0 + n, :]
        return ref[pl.ds(pl.multiple_of(r0, n), n), :]

    qpos_c = i * tq + lax.broadcasted_iota(jnp.int32, (ck, tq), 1)
    krow_c = lax.broadcasted_iota(jnp.int32, (ck, tq), 0)

    def score_body(cb, _):
        for j in range(sub):
            r0 = cb * tq + j * ck
            kc = rows(ki_ref, r0, ck)
            acc = jnp.zeros((ck, tq), F32)
            for h in range(IDX_HEADS):
                rel = _dot(kc, qit_ref[h * LANE:(h + 1) * LANE, :])
                acc = acc + wit_ref[h:h + 1, :] * jnp.maximum(rel, 0.0)
            sc_sc[pl.ds(pl.multiple_of(r0, ck), ck), :] = jnp.where(r0 + krow_c <= qpos_c, acc, -jnp.inf)
        return 0

    lax.fori_loop(0, nblk, score_body, 0)

    krow = lax.broadcasted_iota(jnp.int32, (tq, tq), 0)
    qpos = i * tq + lax.broadcasted_iota(jnp.int32, (tq, tq), 1)

    def count(pred, nb=None):
        def body(cb, a):
            return a + _count_rows(pred(rows(sc_sc, cb * tq, tq), cb * tq + krow))
        part = jnp.zeros((8, tq), jnp.int32)
        if nb is None:
            part = lax.fori_loop(0, nblk, body, part)
        else:
            for cb in range(nb):
                part = body(cb, part)
        return jnp.sum(part, axis=0, keepdims=True)

    def flip(word):
        return jnp.where(word < 0, word ^ jnp.int32(0x7FFFFFFF), word)

    def key_to_score(key):
        return pltpu.bitcast(flip(key), F32)

    thr_sc[...] = jnp.full((1, tq), -jnp.inf, F32)
    jsel_sc[...] = jnp.full((1, tq), sc_sc.shape[0], jnp.int32)

    def bisect(nb):
        assert tq == 2 * LANE
        la, lb = slice(0, LANE), slice(LANE, 2 * LANE)

        def part_count(lanes, cf):
            part = jnp.zeros((8, LANE), jnp.int32)
            for cb in range(nb):
                part = part + _count_rows(sc_sc[cb * tq:(cb + 1) * tq, lanes] >= cf[0:1, :])
            return part

        def accept(part, cand, ans):
            return jnp.where(jnp.sum(part, axis=0, keepdims=True) >= topk, cand, ans)

        def cand_of(t, ans):
            return jnp.where(t == 0, jnp.zeros_like(ans), ans + jnp.left_shift(jnp.int32(1), 31 - t))

        def body(t, st):
            ans_a, ans_b, cand_b, part_b = st
            ans_b = accept(part_b, cand_b, ans_b)
            cand_b = cand_of(t, ans_b)
            cand_a = cand_of(t, ans_a)
            ans_a = accept(part_count(la, key_to_score(cand_a)), cand_a, ans_a)
            return ans_a, ans_b, cand_b, part_count(lb, key_to_score(cand_b))

        start = jnp.full((8, LANE), INT_MIN, jnp.int32)
        first = body(jnp.int32(0), (start, start, start, jnp.zeros((8, LANE), jnp.int32)))
        ans_a, ans_b, cand_b, part_b = lax.fori_loop(1, 32, body, first)
        thr_sc[:, la] = key_to_score(ans_a)[0:1, :]
        thr_sc[:, lb] = key_to_score(accept(part_b, cand_b, ans_b))[0:1, :]

    for v in range(topk // tq, sc_sc.shape[0] // tq):
        pl.when(i == v)(functools.partial(bisect, v + 1))

    @pl.when(i * tq >= topk)
    def _():
        thr = thr_sc[...]
        n_ge = count(lambda s, kidx: s >= thr)
        excess = jnp.max(n_ge) > topk

        @pl.when(excess)
        def _():
            need = topk - count(lambda s, kidx: s > thr)

            def jbit(t, j):
                cand = j + jnp.left_shift(jnp.int32(1), 10 - t)
                n = count(lambda s, kidx: (s == thr) & (kidx < cand))
                return jnp.where(n < need, cand, j)

            jsel_sc[...] = lax.fori_loop(0, 11, jbit, jnp.zeros((1, tq), jnp.int32))

    thr = thr_sc[...]
    jsel = jsel_sc[...]
    nh = DSA_HEADS
    qall = jnp.concatenate([qbt_ref[h * LANE:(h + 1) * LANE, :] for h in range(nh)], axis=1)

    def logit_body(cb, m):
        sc = rows(sc_sc, cb * tq, tq)
        kidx = cb * tq + krow
        sel = ((sc > thr) | ((sc == thr) & (kidx <= jsel))) & (kidx <= qpos)
        bias = jnp.where(sel, 0.0, NEG)
        for j in range(sub):
            r0 = cb * tq + j * ck
            b = bias[j * ck:(j + 1) * ck, :]
            s = _dot(rows(kb_ref, r0, ck), qall) + jnp.concatenate([b] * nh, axis=1)
            att_sc[pl.ds(pl.multiple_of(r0, ck), ck), :] = s
            m = jnp.maximum(m, jnp.max(s, axis=0, keepdims=True))
        return m

    m = lax.fori_loop(0, nblk, logit_body, jnp.full((1, nh * tq), NEG, F32))

    acc_sc[...] = jnp.zeros_like(acc_sc)

    def att_body(cb, l):
        for j in range(sub):
            r0 = cb * tq + j * ck
            p = jnp.exp2(rows(att_sc, r0, ck) - m)
            l = l + jnp.sum(p, axis=0, keepdims=True)
            acc_sc[...] += _dot(vbt_ref[cb * sub + j], p.astype(BF16))
        return l

    l = lax.fori_loop(0, nblk, att_body, jnp.zeros((1, nh * tq), F32))
    out = acc_sc[...] / l
    for h in range(nh):
        o_ref[h * DSA_DIM:(h + 1) * DSA_DIM, :] = out[:, h * tq:(h + 1) * tq].astype(BF16)


def _dsa(qit, wit, ki, qbt, kb, vbt, topk):
    B, _, S = qit.shape
    tq = TQ_DSA
    assert topk % tq == 0 or topk >= S, "query tiles must not straddle the top-k boundary"
    return pl.pallas_call(
        functools.partial(_dsa_kernel, topk),
        out_shape=jax.ShapeDtypeStruct((B, DSA_HEADS * DSA_DIM, S), BF16),
        grid=(B, S // tq),
        in_specs=[
            pl.BlockSpec((None, IDX_HEADS * LANE, tq), lambda b, i: (b, 0, i)),
            pl.BlockSpec((None, IDX_HEADS, tq), lambda b, i: (b, 0, i)),
            pl.BlockSpec((None, S, LANE), lambda b, i: (b, 0, 0)),
            pl.BlockSpec((None, DSA_HEADS * LANE, tq), lambda b, i: (b, 0, i)),
            pl.BlockSpec((None, S, LANE), lambda b, i: (b, 0, 0)),
            pl.BlockSpec((None, S // LANE, DSA_DIM, LANE), lambda b, i: (b, 0, 0, 0)),
        ],
        out_specs=pl.BlockSpec((None, DSA_HEADS * DSA_DIM, tq), lambda b, i: (b, 0, i)),
        scratch_shapes=[
            pltpu.VMEM((S, tq), F32),
            pltpu.VMEM((S, DSA_HEADS * tq), F32),
            pltpu.VMEM((1, tq), F32),
            pltpu.VMEM((1, tq), jnp.int32),
            pltpu.VMEM((DSA_DIM, DSA_HEADS * tq), F32),
        ],
        compiler_params=pltpu.CompilerParams(
            dimension_semantics=("parallel", "arbitrary"), vmem_limit_bytes=VMEM_LIMIT),
        name="dsa",
    )(qit, wit, ki, qbt, kb, vbt)


def _ret_kernel(rqt_ref, rk_ref, rvt_ref, rgt_ref, dt_ref, xi_ref, zeta_ref, cdec_ref, gret_ref,
                o_ref, state_sc):
    n = pl.program_id(1)

    @pl.when(n == 0)
    def _():
        state_sc[...] = jnp.zeros_like(state_sc)

    for h in range(RET_HEADS):
        qt = rqt_ref[h * LANE:(h + 1) * LANE, :]
        k = rk_ref[:, h * LANE:(h + 1) * LANE]
        vt = rvt_ref[h * RET_DV:(h + 1) * RET_DV, :]
        inner_t = _dot(k, qt) * dt_ref[h]
        rt = state_sc[h]
        qx = (qt.astype(F32) * xi_ref[h:h + 1, :]).astype(BF16)
        out_t = _dot(vt, inner_t.astype(BF16)) + _dot(rt.astype(BF16), qx)
        vz = (vt.astype(F32) * zeta_ref[h:h + 1, :]).astype(BF16)
        state_sc[h] = rt * cdec_ref[h] + _dot(vz, k)

        mu = jnp.mean(out_t, axis=0, keepdims=True)
        xc = out_t - mu
        var = jnp.mean(xc * xc, axis=0, keepdims=True)
        y = xc * lax.rsqrt(var + EPS) * gret_ref[h * RET_DV:(h + 1) * RET_DV, :]
        g = rgt_ref[h * RET_DV:(h + 1) * RET_DV, :]
        o_ref[h * RET_DV:(h + 1) * RET_DV, :] = (g * jax.nn.sigmoid(g) * y).astype(BF16)


def _retention(rqt, rk, rvt, rgt, dt, xi, zeta, cdec, gret):
    B, _, S = rqt.shape
    c = C_RET
    return pl.pallas_call(
        _ret_kernel,
        out_shape=jax.ShapeDtypeStruct((B, RET_HEADS * RET_DV, S), BF16),
        grid=(B, S // c),
        in_specs=[
            pl.BlockSpec((None, RET_HEADS * LANE, c), lambda b, n: (b, 0, n)),
            pl.BlockSpec((None, c, RET_HEADS * LANE), lambda b, n: (b, n, 0)),
            pl.BlockSpec((None, RET_HEADS * RET_DV, c), lambda b, n: (b, 0, n)),
            pl.BlockSpec((None, RET_HEADS * RET_DV, c), lambda b, n: (b, 0, n)),
            _full(dt.shape), _full(xi.shape), _full(zeta.shape), _full(cdec.shape), _full(gret.shape),
        ],
        out_specs=pl.BlockSpec((None, RET_HEADS * RET_DV, c), lambda b, n: (b, 0, n)),
        scratch_shapes=[pltpu.VMEM((RET_HEADS, RET_DV, LANE), F32)],
        compiler_params=pltpu.CompilerParams(
            dimension_semantics=("parallel", "arbitrary"), vmem_limit_bytes=VMEM_LIMIT),
        name="retention",
    )(rqt, rk, rvt, rgt, dt, xi, zeta, cdec, gret)


def _mlp_kernel(final, x_ref, oat_ref, obt_ref, oct_ref, wout_ref, gmlp_ref, w1_ref, w2_ref, gfin_ref,
                y_ref):
    def tr(ref):
        return ref[...].astype(F32).T.astype(BF16)

    na = MLA_HEADS * MLA_V
    nb = DSA_HEADS * DSA_DIM
    attn = (_dot(tr(oat_ref), wout_ref[0:na, :])
            + _dot(tr(obt_ref), wout_ref[na:na + nb, :])
            + _dot(tr(oct_ref), wout_ref[na + nb:, :]))
    x1 = x_ref[...] + attn
    hf = _rms(x1, gmlp_ref[...]).astype(BF16)
    acc = x1
    for f in range(D_FF // FF_CHUNK):
        u = jnp.maximum(_dot(hf, w1_ref[:, f * FF_CHUNK:(f + 1) * FF_CHUNK]), 0.0)
        acc = acc + _dot((u * u).astype(BF16), w2_ref[f * FF_CHUNK:(f + 1) * FF_CHUNK, :])
    if final:
        acc = _rms(acc, gfin_ref[...])
    y_ref[...] = acc


def _mlp(x, oat, obt, oct, wout, gmlp, w1, w2, gfin, final):
    B, S, D = x.shape
    tm = TM_MLP

    def const(shape):
        nd = len(shape)
        return pl.BlockSpec(shape, lambda b, i: (0,) * nd, pipeline_mode=pl.Buffered(1))

    return pl.pallas_call(
        functools.partial(_mlp_kernel, final),
        out_shape=jax.ShapeDtypeStruct((B, S, D), F32),
        grid=(B, S // tm),
        in_specs=[
            pl.BlockSpec((None, tm, D), lambda b, i: (b, i, 0)),
            pl.BlockSpec((None, oat.shape[1], tm), lambda b, i: (b, 0, i)),
            pl.BlockSpec((None, obt.shape[1], tm), lambda b, i: (b, 0, i)),
            pl.BlockSpec((None, oct.shape[1], tm), lambda b, i: (b, 0, i)),
            const(wout.shape), const(gmlp.shape), const(w1.shape), const(w2.shape), const(gfin.shape),
        ],
        out_specs=pl.BlockSpec((None, tm, D), lambda b, i: (b, i, 0)),
        compiler_params=pltpu.CompilerParams(
            dimension_semantics=("parallel", "parallel"), vmem_limit_bytes=VMEM_LIMIT),
        name="mlp",
    )(x, oat, obt, oct, wout, gmlp, w1, w2, gfin)


def _rope_tables(seq, dim):
    pos = jnp.arange(seq, dtype=F32)
    inv = ROPE_THETA ** (-jnp.arange(0, dim, 2, dtype=F32) / dim)
    ang = pos[:, None] * inv[None, :]
    return jnp.cos(ang), jnp.sin(ang)


def _lane_table(c, s, lo, seq):
    half = c.shape[1]
    ct = jnp.zeros((seq, LANE), F32).at[:, lo:lo + 2 * half].set(jnp.concatenate([c, c], axis=1))
    st = jnp.zeros((seq, LANE), F32).at[:, lo:lo + 2 * half].set(jnp.concatenate([-s, s], axis=1))
    return ct, st


def _pad_cols(w, width):
    return jnp.pad(w, ((0, 0), (0, width - w.shape[1])))


def _prep_layer(w_in, w_uq, w_ukv):
    z = lambda n: jnp.zeros((D_MODEL, n), F32)
    col = lambda a, b: w_in[:, a:b]
    nat = [col(O_CQ, O_CKV), col(O_CKV, O_KR),
           jnp.concatenate([z(64), col(O_KR, O_AQ), z(32)], axis=1),
           _pad_cols(col(O_AK, O_AV), LANE),
           _pad_cols(col(O_KI, O_WI), LANE)]
    for h in range(RET_HEADS):
        nat.append(_pad_cols(col(O_RK + h * RET_DK, O_RK + (h + 1) * RET_DK), LANE))
    wnat = jnp.concatenate(nat, axis=1).astype(BF16)
    wtr = jnp.concatenate([col(O_AQ, O_AK), col(O_QI, O_KI), col(O_AV, O_QI), col(O_RQ, O_RK),
                           col(O_RV, O_RG), col(O_RG, O_END), col(O_WI, O_RQ)], axis=1).T.astype(BF16)
    dq = MLA_NOPE + MLA_ROPE
    wuq = jnp.pad(w_uq.reshape(MLA_Q_RANK, MLA_HEADS, dq), ((0, 0), (0, 0), (0, LANE - dq)))
    wuq = wuq.reshape(MLA_Q_RANK, MLA_HEADS * LANE).T.astype(BF16)
    wkv = w_ukv.reshape(MLA_KV_RANK, MLA_HEADS, MLA_NOPE + MLA_V)
    wuk = jnp.pad(wkv[:, :, :MLA_NOPE], ((0, 0), (0, 0), (0, LANE - MLA_NOPE)))
    wuk = wuk.reshape(MLA_KV_RANK, MLA_HEADS * LANE).astype(BF16)
    wuv = wkv[:, :, MLA_NOPE:].reshape(MLA_KV_RANK, MLA_HEADS * MLA_V).T.astype(BF16)
    return wnat, wtr, wuq, wuk, wuv


def kernel(x, g_mix, w_in, g_q, w_uq, g_kv, w_ukv, g_ret, w_out, g_mlp, w_ff1, w_ff2, g_final):
    B, S, _ = x.shape
    topk = min(DSA_TOPK_MAX, S // 4)

    cos32, sin32 = _rope_tables(S, 32)
    cos64, sin64 = _rope_tables(S, 64)
    tabt = jnp.concatenate([cos64.T, sin64.T, cos32.T, sin32.T], axis=0)
    tabn = jnp.stack(_lane_table(cos32, sin32, 64, S) + _lane_table(cos64, sin64, 0, S)
                     + _lane_table(cos32, sin32, 0, S))

    log_gamma = jnp.log1p(-jnp.exp2(-5.0 - jnp.arange(RET_HEADS, dtype=F32)))
    c = C_RET
    pos = jnp.arange(c, dtype=F32)
    rel = pos[None, :] - pos[:, None]
    dt = jnp.where(rel[None] >= 0, jnp.exp(jnp.maximum(rel, 0.0)[None] * log_gamma[:, None, None]), 0.0)
    xi = jnp.exp((pos + 1.0)[None, :] * log_gamma[:, None])
    zeta = jnp.exp((c - 1.0 - pos)[None, :] * log_gamma[:, None])
    cdec = jnp.broadcast_to(jnp.exp(c * log_gamma)[:, None, None], (RET_HEADS, RET_DV, LANE))

    for l in range(DEPTH):
        wnat, wtr, wuq, wuk, wuv = _prep_layer(w_in[l], w_uq[l], w_ukv[l])
        (qat, ka, vat, qit, wit, ki, qbt, kb, vbt, rqt, rk, rvt, rgt) = _proj(
            x, g_mix[l][None, :], g_q[l][None, :], g_kv[l][None, :], wnat, wtr, wuq, wuk, wuv, tabt, tabn)
        oat = _mla(qat, ka, vat)
        obt = _dsa(qit, wit, ki, qbt, kb, vbt, topk)
        gret = jnp.broadcast_to(g_ret[l][:, None], (RET_HEADS * RET_DV, c))
        oct = _retention(rqt, rk, rvt, rgt, dt, xi, zeta, cdec, gret)
        x = _mlp(x, oat, obt, oct, w_out[l].astype(BF16), g_mlp[l][None, :],
                 w_ff1[l].astype(BF16), w_ff2[l].astype(BF16), g_final[None, :], l == DEPTH - 1)
    return x
```

```python
import functools

import numpy as np
import jax
import jax.numpy as jnp
from jax import lax
from jax.experimental import pallas as pl
from jax.experimental.pallas import tpu as pltpu

D_MODEL = 1024
DEPTH = 2
MLA_HEADS = 8
MLA_NOPE = 64
MLA_ROPE = 32
MLA_V = 64
MLA_Q_RANK = 768
MLA_KV_RANK = 256
DSA_HEADS = 4
DSA_DIM = 64
IDX_HEADS = 8
IDX_DIM = 32
DSA_TOPK_MAX = 256
RET_HEADS = 4
RET_DK = 64
RET_DV = 64
D_FF = 4 * D_MODEL
ROPE_THETA = 10000.0
EPS = 1e-6

MLA_COLS = (MLA_Q_RANK, MLA_KV_RANK, MLA_ROPE)
DSA_COLS = (DSA_HEADS * DSA_DIM, DSA_DIM, DSA_DIM, IDX_HEADS * IDX_DIM, IDX_DIM, IDX_HEADS)
RET_COLS = (RET_HEADS * RET_DK, RET_HEADS * RET_DK, RET_HEADS * RET_DV, RET_HEADS * RET_DV)
SPLITS = MLA_COLS + DSA_COLS + RET_COLS
_OFF = np.concatenate([[0], np.cumsum(SPLITS)]).tolist()
(O_CQ, O_CKV, O_KR, O_AQ, O_AK, O_AV, O_QI, O_KI, O_WI, O_RQ, O_RK, O_RV, O_RG, O_END) = _OFF

LANE = 128
VMEM_LIMIT = 56 * 1024 * 1024

BF16 = jnp.bfloat16
F32 = jnp.float32
NEG = -1e30
LOG2E = float(np.log2(np.e))
INT_MIN = np.int32(-2**31)

N_CQ = 0
N_CKV = N_CQ + MLA_Q_RANK
N_KR = N_CKV + MLA_KV_RANK
N_AK = N_KR + LANE
N_KI = N_AK + LANE
N_RK = N_KI + LANE
N_NAT = N_RK + RET_HEADS * LANE
T_AQ = 0
T_QI = T_AQ + DSA_HEADS * DSA_DIM
T_AV = T_QI + IDX_HEADS * IDX_DIM
T_RQ = T_AV + DSA_DIM
T_RV = T_RQ + RET_HEADS * RET_DK
T_RG = T_RV + RET_HEADS * RET_DV
T_WI = T_RG + RET_HEADS * RET_DV
N_TR = T_WI + IDX_HEADS

TM_PROJ = 256
TQ_MLA = 256
G_MLA = 8
TQ_DSA = 256
CK_DSA = 128
C_RET = 256
TM_MLP = 512
FF_CHUNK = 1024

_NT = (((1,), (1,)), ((), ()))


def _rms(x, g):
    return x * lax.rsqrt(jnp.mean(x * x, axis=-1, keepdims=True) + EPS) * g


def _dot(a, b):
    return jnp.dot(a, b, preferred_element_type=F32)


def _dot_nt(a, b):
    return lax.dot_general(a, b, _NT, preferred_element_type=F32)


def _rope_rows(x1, x2, c, s):
    return x1 * c - x2 * s, x1 * s + x2 * c


def _rope_lanes(x, c, s_signed, half, first_lo):
    lane = lax.broadcasted_iota(jnp.int32, x.shape, 1)
    first = (lane >= first_lo) & (lane < first_lo + half)
    rot = jnp.where(first, pltpu.roll(x, LANE - half, 1), pltpu.roll(x, half, 1))
    return x * c + rot * s_signed


def _proj_kernel(x_ref, gmix_ref, gq_ref, gkv_ref, wnat_ref, wtr_ref, wuq_ref, wuk_ref, wuv_ref,
                 tabt_ref, tabn_ref,
                 qat_ref, ka_ref, vat_ref, qit_ref, wit_ref, ki_ref, qbt_ref, kb_ref, vbt_ref,
                 rqt_ref, rk_ref, rvt_ref, rgt_ref):
    tm = x_ref.shape[0]
    hb = _rms(x_ref[...], gmix_ref[...]).astype(BF16)

    c64t = tabt_ref[0:32, :]
    s64t = tabt_ref[32:64, :]
    c32t = tabt_ref[64:80, :]
    s32t = tabt_ref[80:96, :]

    cq = _dot(hb, wnat_ref[:, N_CQ:N_CQ + MLA_Q_RANK])
    nq = _rms(cq, gq_ref[...]).astype(BF16)
    qt = _dot_nt(wuq_ref[...], nq)
    scale_a = (MLA_NOPE + MLA_ROPE) ** -0.5 * LOG2E
    for h in range(MLA_HEADS):
        r0 = h * LANE
        qat_ref[r0:r0 + 64, :] = (qt[r0:r0 + 64, :] * scale_a).astype(BF16)
        o1, o2 = _rope_rows(qt[r0 + 64:r0 + 80, :], qt[r0 + 80:r0 + 96, :], c32t, s32t)
        qat_ref[r0 + 64:r0 + 80, :] = (o1 * scale_a).astype(BF16)
        qat_ref[r0 + 80:r0 + 96, :] = (o2 * scale_a).astype(BF16)
        qat_ref[r0 + 96:r0 + 128, :] = jnp.zeros((32, tm), BF16)

    ckv = _dot(hb, wnat_ref[:, N_CKV:N_CKV + MLA_KV_RANK])
    nkv = _rms(ckv, gkv_ref[...]).astype(BF16)
    kn = _dot(nkv, wuk_ref[...])
    keys = _dot(hb, wnat_ref[:, N_KR:N_NAT])
    slab = lambda off: keys[:, off - N_KR:off - N_KR + LANE]
    krp = _rope_lanes(slab(N_KR), tabn_ref[0], tabn_ref[1], 16, 64)
    for h in range(MLA_HEADS):
        ka_ref[:, h * LANE:(h + 1) * LANE] = (kn[:, h * LANE:(h + 1) * LANE] + krp).astype(BF16)
    vt = _dot_nt(wuv_ref[...], nkv).astype(BF16)
    for j in range(tm // LANE):
        vat_ref[j] = vt[:, j * LANE:(j + 1) * LANE]

    kb_ref[...] = _rope_lanes(slab(N_AK), tabn_ref[2], tabn_ref[3], 32, 0).astype(BF16)
    ki_ref[...] = _rope_lanes(slab(N_KI), tabn_ref[4], tabn_ref[5], 16, 0).astype(BF16)

    for h in range(RET_HEADS):
        rkp = _rope_lanes(slab(N_RK + h * LANE), tabn_ref[2], tabn_ref[3], 32, 0) * (RET_DK ** -0.5)
        rk_ref[:, h * LANE:(h + 1) * LANE] = rkp.astype(BF16)

    aqt = _dot_nt(wtr_ref[T_AQ:T_AQ + 256, :], hb)
    scale_b = DSA_DIM ** -0.5 * LOG2E
    for h in range(DSA_HEADS):
        o1, o2 = _rope_rows(aqt[h * 64:h * 64 + 32, :], aqt[h * 64 + 32:h * 64 + 64, :], c64t, s64t)
        qbt_ref[h * LANE:h * LANE + 32, :] = (o1 * scale_b).astype(BF16)
        qbt_ref[h * LANE + 32:h * LANE + 64, :] = (o2 * scale_b).astype(BF16)
        qbt_ref[h * LANE + 64:(h + 1) * LANE, :] = jnp.zeros((64, tm), BF16)

    qit = _dot_nt(wtr_ref[T_QI:T_QI + 256, :], hb)
    for h in range(IDX_HEADS):
        o1, o2 = _rope_rows(qit[h * 32:h * 32 + 16, :], qit[h * 32 + 16:h * 32 + 32, :], c32t, s32t)
        qit_ref[h * LANE:h * LANE + 16, :] = o1.astype(BF16)
        qit_ref[h * LANE + 16:h * LANE + 32, :] = o2.astype(BF16)
        qit_ref[h * LANE + 32:(h + 1) * LANE, :] = jnp.zeros((96, tm), BF16)

    idx_scale = (IDX_HEADS ** -0.5) * (IDX_DIM ** -0.5)
    wit_ref[...] = _dot_nt(wtr_ref[T_WI:T_WI + IDX_HEADS, :], hb) * idx_scale

    avt = _dot_nt(wtr_ref[T_AV:T_AV + DSA_DIM, :], hb).astype(BF16)
    for j in range(tm // LANE):
        vbt_ref[j] = avt[:, j * LANE:(j + 1) * LANE]

    rqt = _dot_nt(wtr_ref[T_RQ:T_RQ + 256, :], hb)
    for h in range(RET_HEADS):
        o1, o2 = _rope_rows(rqt[h * 64:h * 64 + 32, :], rqt[h * 64 + 32:h * 64 + 64, :], c64t, s64t)
        rqt_ref[h * LANE:h * LANE + 32, :] = o1.astype(BF16)
        rqt_ref[h * LANE + 32:h * LANE + 64, :] = o2.astype(BF16)
        rqt_ref[h * LANE + 64:(h + 1) * LANE, :] = jnp.zeros((64, tm), BF16)

    rvt_ref[...] = _dot_nt(wtr_ref[T_RV:T_RV + 256, :], hb).astype(BF16)
    rgt_ref[...] = _dot_nt(wtr_ref[T_RG:T_RG + 256, :], hb)


def _full(shape):
    nd = len(shape)
    return pl.BlockSpec(shape, lambda b, i: (0,) * nd)


def _proj(x, gmix, gq, gkv, wnat, wtr, wuq, wuk, wuv, tabt, tabn):
    B, S, D = x.shape
    tm = TM_PROJ
    nch = tm // LANE
    outs = [
        (jax.ShapeDtypeStruct((B, MLA_HEADS * LANE, S), BF16), pl.BlockSpec((None, MLA_HEADS * LANE, tm), lambda b, i: (b, 0, i))),
        (jax.ShapeDtypeStruct((B, S, MLA_HEADS * LANE), BF16), pl.BlockSpec((None, tm, MLA_HEADS * LANE), lambda b, i: (b, i, 0))),
        (jax.ShapeDtypeStruct((B, S // LANE, MLA_HEADS * MLA_V, LANE), BF16), pl.BlockSpec((None, nch, MLA_HEADS * MLA_V, LANE), lambda b, i: (b, i, 0, 0))),
        (jax.ShapeDtypeStruct((B, IDX_HEADS * LANE, S), BF16), pl.BlockSpec((None, IDX_HEADS * LANE, tm), lambda b, i: (b, 0, i))),
        (jax.ShapeDtypeStruct((B, IDX_HEADS, S), F32), pl.BlockSpec((None, IDX_HEADS, tm), lambda b, i: (b, 0, i))),
        (jax.ShapeDtypeStruct((B, S, LANE), BF16), pl.BlockSpec((None, tm, LANE), lambda b, i: (b, i, 0))),
        (jax.ShapeDtypeStruct((B, DSA_HEADS * LANE, S), BF16), pl.BlockSpec((None, DSA_HEADS * LANE, tm), lambda b, i: (b, 0, i))),
        (jax.ShapeDtypeStruct((B, S, LANE), BF16), pl.BlockSpec((None, tm, LANE), lambda b, i: (b, i, 0))),
        (jax.ShapeDtypeStruct((B, S // LANE, DSA_DIM, LANE), BF16), pl.BlockSpec((None, nch, DSA_DIM, LANE), lambda b, i: (b, i, 0, 0))),
        (jax.ShapeDtypeStruct((B, RET_HEADS * LANE, S), BF16), pl.BlockSpec((None, RET_HEADS * LANE, tm), lambda b, i: (b, 0, i))),
        (jax.ShapeDtypeStruct((B, S, RET_HEADS * LANE), BF16), pl.BlockSpec((None, tm, RET_HEADS * LANE), lambda b, i: (b, i, 0))),
        (jax.ShapeDtypeStruct((B, RET_HEADS * RET_DV, S), BF16), pl.BlockSpec((None, RET_HEADS * RET_DV, tm), lambda b, i: (b, 0, i))),
        (jax.ShapeDtypeStruct((B, RET_HEADS * RET_DV, S), F32), pl.BlockSpec((None, RET_HEADS * RET_DV, tm), lambda b, i: (b, 0, i))),
    ]
    in_specs = [
        pl.BlockSpec((None, tm, D), lambda b, i: (b, i, 0)),
        _full(gmix.shape), _full(gq.shape), _full(gkv.shape),
        _full(wnat.shape), _full(wtr.shape), _full(wuq.shape), _full(wuk.shape), _full(wuv.shape),
        pl.BlockSpec((tabt.shape[0], tm), lambda b, i: (0, i)),
        pl.BlockSpec((tabn.shape[0], tm, LANE), lambda b, i: (0, i, 0)),
    ]
    return pl.pallas_call(
        _proj_kernel,
        out_shape=[o[0] for o in outs],
        grid=(B, S // tm),
        in_specs=in_specs,
        out_specs=[o[1] for o in outs],
        compiler_params=pltpu.CompilerParams(
            dimension_semantics=("parallel", "parallel"), vmem_limit_bytes=VMEM_LIMIT),
        name="proj",
    )(x, gmix, gq, gkv, wnat, wtr, wuq, wuk, wuv, tabt, tabn)


def _mla_kernel(q_ref, k_ref, v_ref, o_ref, s0_sc, s1_sc, m_sc, l_sc, acc_sc):
    g_heads = acc_sc.shape[0]
    tq = q_ref.shape[1]
    tk = tq
    i = pl.program_id(2)
    nv = tk // LANE
    kidx = lax.broadcasted_iota(jnp.int32, (tk, tq), 0)
    qidx = lax.broadcasted_iota(jnp.int32, (tk, tq), 1)
    bufs = (s0_sc, s1_sc)

    def scores(kt, buf, causal=None):
        for g in range(g_heads):
            s = _dot(k_ref[pl.ds(pl.multiple_of(kt * tk, tk), tk), g * LANE:(g + 1) * LANE],
                     q_ref[g * LANE:(g + 1) * LANE, :])
            if causal is not None:
                s = jnp.where(causal, s, NEG)
            buf[g] = s

    def softmax(kt, buf):
        for g in range(g_heads):
            s = buf[g]
            m_old = m_sc[g]
            m_new = jnp.maximum(m_old, jnp.max(s, axis=0, keepdims=True))
            a = jnp.exp2(m_old - m_new)
            p = jnp.exp2(s - m_new)
            vt = jnp.concatenate([v_ref[kt * nv + j, g * MLA_V:(g + 1) * MLA_V, :] for j in range(nv)],
                                 axis=1)
            m_sc[g] = m_new
            l_sc[g] = a * l_sc[g] + jnp.sum(p, axis=0, keepdims=True)
            acc_sc[g] = a * acc_sc[g] + _dot(vt, p.astype(BF16))

    m_sc[...] = jnp.full(m_sc.shape, NEG, F32)
    l_sc[...] = jnp.zeros_like(l_sc)
    acc_sc[...] = jnp.zeros_like(acc_sc)

    def step(k, parity, diagonal=False):
        scores(k + 1, bufs[1 - parity], (kidx <= qidx) if diagonal else None)
        softmax(k, bufs[parity])

    scores(0, bufs[0], kidx <= i * tq + qidx)

    def pair(p, _):
        step(2 * p, 0)
        step(2 * p + 1, 1)
        return 0

    lax.fori_loop(0, (i - 1) // 2, pair, 0)

    @pl.when(i % 2 == 0)
    def _():
        @pl.when(i >= 2)
        def _():
            step(i - 2, 0)
            step(i - 1, 1, diagonal=True)
        softmax(i, bufs[0])

    @pl.when(i % 2 == 1)
    def _():
        step(i - 1, 0, diagonal=True)
        softmax(i, bufs[1])

    for g in range(g_heads):
        o_ref[g * MLA_V:(g + 1) * MLA_V, :] = (acc_sc[g] / l_sc[g]).astype(BF16)


def _mla(qat, ka, vat):
    B, _, S = qat.shape
    tq = TQ_MLA
    g = G_MLA
    return pl.pallas_call(
        _mla_kernel,
        out_shape=jax.ShapeDtypeStruct((B, MLA_HEADS * MLA_V, S), BF16),
        grid=(B, MLA_HEADS // g, S // tq),
        in_specs=[
            pl.BlockSpec((None, g * LANE, tq), lambda b, h, i: (b, h, i)),
            pl.BlockSpec((None, S, g * LANE), lambda b, h, i: (b, 0, h)),
            pl.BlockSpec((None, S // LANE, g * MLA_V, LANE), lambda b, h, i: (b, 0, h, 0)),
        ],
        out_specs=pl.BlockSpec((None, g * MLA_V, tq), lambda b, h, i: (b, h, i)),
        scratch_shapes=[pltpu.VMEM((g, tq, tq), F32), pltpu.VMEM((g, tq, tq), F32),
                        pltpu.VMEM((g, 1, tq), F32), pltpu.VMEM((g, 1, tq), F32),
                        pltpu.VMEM((g, MLA_V, tq), F32)],
        compiler_params=pltpu.CompilerParams(
            dimension_semantics=("parallel", "parallel", "arbitrary"), vmem_limit_bytes=VMEM_LIMIT),
        name="mla",
    )(qat, ka, vat)


def _count_rows(mask):
    r, t = mask.shape
    return jnp.sum(mask.astype(jnp.int32).reshape(r // 8, 8, t), axis=0)


def _dsa_kernel(topk, qit_ref, wit_ref, ki_ref, qbt_ref, kb_ref, vbt_ref, o_ref,
                sc_sc, att0_sc, att1_sc, thr_sc, jsel_sc, m_sc, l_sc, acc_sc):
    tq = o_ref.shape[1]
    ck = CK_DSA
    sub = tq // ck
    i = pl.program_id(1)
    nblk = i + 1

    def rows(ref, r0, n):
        if isinstance(r0, int):
            return ref[r0:r0 + n, :]
        return ref[pl.ds(pl.multiple_of(r0, n), n), :]

    qpos_c = i * tq + lax.broadcasted_iota(jnp.int32, (ck, tq), 1)
    krow_c = lax.broadcasted_iota(jnp.int32, (ck, tq), 0)

    def score_block(cb):
        for j in range(sub):
            r0 = cb * tq + j * ck
            kc = rows(ki_ref, r0, ck)
            acc = jnp.zeros((ck, tq), F32)
            for h in range(IDX_HEADS):
                rel = _dot(kc, qit_ref[h * LANE:(h + 1) * LANE, :])
                acc = acc + wit_ref[h:h + 1, :] * jnp.maximum(rel, 0.0)
            sc_sc[pl.ds(pl.multiple_of(r0, ck), ck), :] = jnp.where(r0 + krow_c <= qpos_c, acc, -jnp.inf)

    def score_pair(p, _):
        score_block(2 * p)
        score_block(2 * p + 1)
        return 0

    lax.fori_loop(0, nblk // 2, score_pair, 0)

    @pl.when(nblk % 2 == 1)
    def _():
        score_block(nblk - 1)

    krow = lax.broadcasted_iota(jnp.int32, (tq, tq), 0)
    qpos = i * tq + lax.broadcasted_iota(jnp.int32, (tq, tq), 1)

    def count(pred, nb=None):
        def body(cb, a):
            return a + _count_rows(pred(rows(sc_sc, cb * tq, tq), cb * tq + krow))
        part = jnp.zeros((8, tq), jnp.int32)
        if nb is None:
            part = lax.fori_loop(0, nblk, body, part)
        else:
            for cb in range(nb):
                part = body(cb, part)
        return jnp.sum(part, axis=0, keepdims=True)

    def flip(word):
        return jnp.where(word < 0, word ^ jnp.int32(0x7FFFFFFF), word)

    def key_to_score(key):
        return pltpu.bitcast(flip(key), F32)

    thr_sc[...] = jnp.full((1, tq), -jnp.inf, F32)
    jsel_sc[...] = jnp.full((1, tq), sc_sc.shape[0], jnp.int32)

    def bisect(nb):
        assert tq == 2 * LANE
        la, lb = slice(0, LANE), slice(LANE, 2 * LANE)

        def part_count(lanes, cf):
            part = jnp.zeros((8, LANE), jnp.int32)
            for cb in range(nb):
                part = part + _count_rows(sc_sc[cb * tq:(cb + 1) * tq, lanes] >= cf[0:1, :])
            return part

        def accept(part, cand, ans):
            return jnp.where(jnp.sum(part, axis=0, keepdims=True) >= topk, cand, ans)

        def cand_of(t, ans):
            return jnp.where(t == 0, jnp.zeros_like(ans), ans + jnp.left_shift(jnp.int32(1), 31 - t))

        def body(t, st):
            ans_a, ans_b, cand_b, part_b = st
            ans_b = accept(part_b, cand_b, ans_b)
            cand_b = cand_of(t, ans_b)
            cand_a = cand_of(t, ans_a)
            ans_a = accept(part_count(la, key_to_score(cand_a)), cand_a, ans_a)
            return ans_a, ans_b, cand_b, part_count(lb, key_to_score(cand_b))

        start = jnp.full((8, LANE), INT_MIN, jnp.int32)
        first = body(jnp.int32(0), (start, start, start, jnp.zeros((8, LANE), jnp.int32)))
        ans_a, ans_b, cand_b, part_b = lax.fori_loop(1, 32, body, first)
        thr_sc[:, la] = key_to_score(ans_a)[0:1, :]
        thr_sc[:, lb] = key_to_score(accept(part_b, cand_b, ans_b))[0:1, :]

    for v in range(topk // tq, sc_sc.shape[0] // tq):
        pl.when(i == v)(functools.partial(bisect, v + 1))

    @pl.when(i * tq >= topk)
    def _():
        thr = thr_sc[...]
        n_ge = count(lambda s, kidx: s >= thr)
        excess = jnp.max(n_ge) > topk

        @pl.when(excess)
        def _():
            need = topk - count(lambda s, kidx: s > thr)

            def jbit(t, j):
                cand = j + jnp.left_shift(jnp.int32(1), 10 - t)
                n = count(lambda s, kidx: (s == thr) & (kidx < cand))
                return jnp.where(n < need, cand, j)

            jsel_sc[...] = lax.fori_loop(0, 11, jbit, jnp.zeros((1, tq), jnp.int32))

    thr = thr_sc[...]
    jsel = jsel_sc[...]
    nh = DSA_HEADS
    qall = jnp.concatenate([qbt_ref[h * LANE:(h + 1) * LANE, :] for h in range(nh)], axis=1)
    bufs = (att0_sc, att1_sc)

    def logits(cb, buf):
        sc = rows(sc_sc, cb * tq, tq)
        kidx = cb * tq + krow
        sel = ((sc > thr) | ((sc == thr) & (kidx <= jsel))) & (kidx <= qpos)
        bias = jnp.where(sel, 0.0, NEG)
        for j in range(sub):
            b = bias[j * ck:(j + 1) * ck, :]
            buf[j * ck:(j + 1) * ck, :] = (_dot(rows(kb_ref, cb * tq + j * ck, ck), qall)
                                           + jnp.concatenate([b] * nh, axis=1))

    def attend(cb, buf):
        s = buf[...]
        m_old = m_sc[...]
        m_new = jnp.maximum(m_old, jnp.max(s, axis=0, keepdims=True))
        a = jnp.exp2(m_old - m_new)
        p = jnp.exp2(s - m_new)
        vt = jnp.concatenate([vbt_ref[cb * sub + j] for j in range(sub)], axis=1)
        m_sc[...] = m_new
        l_sc[...] = a * l_sc[...] + jnp.sum(p, axis=0, keepdims=True)
        acc_sc[...] = a * acc_sc[...] + _dot(vt, p.astype(BF16))

    def step(k, parity):
        logits(k + 1, bufs[1 - parity])
        attend(k, bufs[parity])

    m_sc[...] = jnp.full(m_sc.shape, NEG, F32)
    l_sc[...] = jnp.zeros_like(l_sc)
    acc_sc[...] = jnp.zeros_like(acc_sc)
    logits(0, bufs[0])

    def pair(p, _):
        step(2 * p, 0)
        step(2 * p + 1, 1)
        return 0

    lax.fori_loop(0, i // 2, pair, 0)

    @pl.when(i % 2 == 1)
    def _():
        step(i - 1, 0)
        attend(i, bufs[1])

    @pl.when(i % 2 == 0)
    def _():
        attend(i, bufs[0])

    out = acc_sc[...] / l_sc[...]
    for h in range(nh):
        o_ref[h * DSA_DIM:(h + 1) * DSA_DIM, :] = out[:, h * tq:(h + 1) * tq].astype(BF16)


def _dsa(qit, wit, ki, qbt, kb, vbt, topk):
    B, _, S = qit.shape
    tq = TQ_DSA
    assert topk % tq == 0 or topk >= S, "query tiles must not straddle the top-k boundary"
    return pl.pallas_call(
        functools.partial(_dsa_kernel, topk),
        out_shape=jax.ShapeDtypeStruct((B, DSA_HEADS * DSA_DIM, S), BF16),
        grid=(B, S // tq),
        in_specs=[
            pl.BlockSpec((None, IDX_HEADS * LANE, tq), lambda b, i: (b, 0, i)),
            pl.BlockSpec((None, IDX_HEADS, tq), lambda b, i: (b, 0, i)),
            pl.BlockSpec((None, S, LANE), lambda b, i: (b, 0, 0)),
            pl.BlockSpec((None, DSA_HEADS * LANE, tq), lambda b, i: (b, 0, i)),
            pl.BlockSpec((None, S, LANE), lambda b, i: (b, 0, 0)),
            pl.BlockSpec((None, S // LANE, DSA_DIM, LANE), lambda b, i: (b, 0, 0, 0)),
        ],
        out_specs=pl.BlockSpec((None, DSA_HEADS * DSA_DIM, tq), lambda b, i: (b, 0, i)),
        scratch_shapes=[
            pltpu.VMEM((S, tq), F32),
            pltpu.VMEM((tq, DSA_HEADS * tq), F32),
            pltpu.VMEM((tq, DSA_HEADS * tq), F32),
            pltpu.VMEM((1, tq), F32),
            pltpu.VMEM((1, tq), jnp.int32),
            pltpu.VMEM((1, DSA_HEADS * tq), F32),
            pltpu.VMEM((1, DSA_HEADS * tq), F32),
            pltpu.VMEM((DSA_DIM, DSA_HEADS * tq), F32),
        ],
        compiler_params=pltpu.CompilerParams(
            dimension_semantics=("parallel", "arbitrary"), vmem_limit_bytes=VMEM_LIMIT),
        name="dsa",
    )(qit, wit, ki, qbt, kb, vbt)


def _ret_kernel(rqt_ref, rk_ref, rvt_ref, rgt_ref, dt_ref, xi_ref, zeta_ref, cdec_ref, gret_ref,
                o_ref, state_sc):
    n = pl.program_id(1)

    @pl.when(n == 0)
    def _():
        state_sc[...] = jnp.zeros_like(state_sc)

    for h in range(RET_HEADS):
        qt = rqt_ref[h * LANE:(h + 1) * LANE, :]
        k = rk_ref[:, h * LANE:(h + 1) * LANE]
        vt = rvt_ref[h * RET_DV:(h + 1) * RET_DV, :]
        inner_t = _dot(k, qt) * dt_ref[h]
        rt = state_sc[h]
        qx = (qt.astype(F32) * xi_ref[h:h + 1, :]).astype(BF16)
        out_t = _dot(vt, inner_t.astype(BF16)) + _dot(rt.astype(BF16), qx)
        vz = (vt.astype(F32) * zeta_ref[h:h + 1, :]).astype(BF16)
        state_sc[h] = rt * cdec_ref[h] + _dot(vz, k)

        mu = jnp.mean(out_t, axis=0, keepdims=True)
        xc = out_t - mu
        var = jnp.mean(xc * xc, axis=0, keepdims=True)
        y = xc * lax.rsqrt(var + EPS) * gret_ref[h * RET_DV:(h + 1) * RET_DV, :]
        g = rgt_ref[h * RET_DV:(h + 1) * RET_DV, :]
        o_ref[h * RET_DV:(h + 1) * RET_DV, :] = (g * jax.nn.sigmoid(g) * y).astype(BF16)


def _retention(rqt, rk, rvt, rgt, dt, xi, zeta, cdec, gret):
    B, _, S = rqt.shape
    c = C_RET
    return pl.pallas_call(
        _ret_kernel,
        out_shape=jax.ShapeDtypeStruct((B, RET_HEADS * RET_DV, S), BF16),
        grid=(B, S // c),
        in_specs=[
            pl.BlockSpec((None, RET_HEADS * LANE, c), lambda b, n: (b, 0, n)),
            pl.BlockSpec((None, c, RET_HEADS * LANE), lambda b, n: (b, n, 0)),
            pl.BlockSpec((None, RET_HEADS * RET_DV, c), lambda b, n: (b, 0, n)),
            pl.BlockSpec((None, RET_HEADS * RET_DV, c), lambda b, n: (b, 0, n)),
            _full(dt.shape), _full(xi.shape), _full(zeta.shape), _full(cdec.shape), _full(gret.shape),
        ],
        out_specs=pl.BlockSpec((None, RET_HEADS * RET_DV, c), lambda b, n: (b, 0, n)),
        scratch_shapes=[pltpu.VMEM((RET_HEADS, RET_DV, LANE), F32)],
        compiler_params=pltpu.CompilerParams(
            dimension_semantics=("parallel", "arbitrary"), vmem_limit_bytes=VMEM_LIMIT),
        name="retention",
    )(rqt, rk, rvt, rgt, dt, xi, zeta, cdec, gret)


def _mlp_kernel(final, x_ref, oat_ref, obt_ref, oct_ref, wout_ref, gmlp_ref, w1_ref, w2_ref, gfin_ref,
                y_ref):
    def tr(ref):
        return ref[...].astype(F32).T.astype(BF16)

    na = MLA_HEADS * MLA_V
    nb = DSA_HEADS * DSA_DIM
    attn = (_dot(tr(oat_ref), wout_ref[0:na, :])
            + _dot(tr(obt_ref), wout_ref[na:na + nb, :])
            + _dot(tr(oct_ref), wout_ref[na + nb:, :]))
    x1 = x_ref[...] + attn
    hf = _rms(x1, gmlp_ref[...]).astype(BF16)
    acc = x1
    for f in range(D_FF // FF_CHUNK):
        u = jnp.maximum(_dot(hf, w1_ref[:, f * FF_CHUNK:(f + 1) * FF_CHUNK]), 0.0)
        acc = acc + _dot((u * u).astype(BF16), w2_ref[f * FF_CHUNK:(f + 1) * FF_CHUNK, :])
    if final:
        acc = _rms(acc, gfin_ref[...])
    y_ref[...] = acc


def _mlp(x, oat, obt, oct, wout, gmlp, w1, w2, gfin, final):
    B, S, D = x.shape
    tm = TM_MLP

    def const(shape):
        nd = len(shape)
        return pl.BlockSpec(shape, lambda b, i: (0,) * nd, pipeline_mode=pl.Buffered(1))

    return pl.pallas_call(
        functools.partial(_mlp_kernel, final),
        out_shape=jax.ShapeDtypeStruct((B, S, D), F32),
        grid=(B, S // tm),
        in_specs=[
            pl.BlockSpec((None, tm, D), lambda b, i: (b, i, 0)),
            pl.BlockSpec((None, oat.shape[1], tm), lambda b, i: (b, 0, i)),
            pl.BlockSpec((None, obt.shape[1], tm), lambda b, i: (b, 0, i)),
            pl.BlockSpec((None, oct.shape[1], tm), lambda b, i: (b, 0, i)),
            const(wout.shape), const(gmlp.shape), const(w1.shape), const(w2.shape), const(gfin.shape),
        ],
        out_specs=pl.BlockSpec((None, tm, D), lambda b, i: (b, i, 0)),
        compiler_params=pltpu.CompilerParams(
            dimension_semantics=("parallel", "parallel"), vmem_limit_bytes=VMEM_LIMIT),
        name="mlp",
    )(x, oat, obt, oct, wout, gmlp, w1, w2, gfin)


def _rope_tables(seq, dim):
    pos = jnp.arange(seq, dtype=F32)
    inv = ROPE_THETA ** (-jnp.arange(0, dim, 2, dtype=F32) / dim)
    ang = pos[:, None] * inv[None, :]
    return jnp.cos(ang), jnp.sin(ang)


def _lane_table(c, s, lo, seq):
    half = c.shape[1]
    ct = jnp.zeros((seq, LANE), F32).at[:, lo:lo + 2 * half].set(jnp.concatenate([c, c], axis=1))
    st = jnp.zeros((seq, LANE), F32).at[:, lo:lo + 2 * half].set(jnp.concatenate([-s, s], axis=1))
    return ct, st


def _pad_cols(w, width):
    return jnp.pad(w, ((0, 0), (0, width - w.shape[1])))


def _prep_layer(w_in, w_uq, w_ukv):
    z = lambda n: jnp.zeros((D_MODEL, n), F32)
    col = lambda a, b: w_in[:, a:b]
    nat = [col(O_CQ, O_CKV), col(O_CKV, O_KR),
           jnp.concatenate([z(64), col(O_KR, O_AQ), z(32)], axis=1),
           _pad_cols(col(O_AK, O_AV), LANE),
           _pad_cols(col(O_KI, O_WI), LANE)]
    for h in range(RET_HEADS):
        nat.append(_pad_cols(col(O_RK + h * RET_DK, O_RK + (h + 1) * RET_DK), LANE))
    wnat = jnp.concatenate(nat, axis=1).astype(BF16)
    wtr = jnp.concatenate([col(O_AQ, O_AK), col(O_QI, O_KI), col(O_AV, O_QI), col(O_RQ, O_RK),
                           col(O_RV, O_RG), col(O_RG, O_END), col(O_WI, O_RQ)], axis=1).T.astype(BF16)
    dq = MLA_NOPE + MLA_ROPE
    wuq = jnp.pad(w_uq.reshape(MLA_Q_RANK, MLA_HEADS, dq), ((0, 0), (0, 0), (0, LANE - dq)))
    wuq = wuq.reshape(MLA_Q_RANK, MLA_HEADS * LANE).T.astype(BF16)
    wkv = w_ukv.reshape(MLA_KV_RANK, MLA_HEADS, MLA_NOPE + MLA_V)
    wuk = jnp.pad(wkv[:, :, :MLA_NOPE], ((0, 0), (0, 0), (0, LANE - MLA_NOPE)))
    wuk = wuk.reshape(MLA_KV_RANK, MLA_HEADS * LANE).astype(BF16)
    wuv = wkv[:, :, MLA_NOPE:].reshape(MLA_KV_RANK, MLA_HEADS * MLA_V).T.astype(BF16)
    return wnat, wtr, wuq, wuk, wuv


def kernel(x, g_mix, w_in, g_q, w_uq, g_kv, w_ukv, g_ret, w_out, g_mlp, w_ff1, w_ff2, g_final):
    B, S, _ = x.shape
    topk = min(DSA_TOPK_MAX, S // 4)

    cos32, sin32 = _rope_tables(S, 32)
    cos64, sin64 = _rope_tables(S, 64)
    tabt = jnp.concatenate([cos64.T, sin64.T, cos32.T, sin32.T], axis=0)
    tabn = jnp.stack(_lane_table(cos32, sin32, 64, S) + _lane_table(cos64, sin64, 0, S)
                     + _lane_table(cos32, sin32, 0, S))

    log_gamma = jnp.log1p(-jnp.exp2(-5.0 - jnp.arange(RET_HEADS, dtype=F32)))
    c = C_RET
    pos = jnp.arange(c, dtype=F32)
    rel = pos[None, :] - pos[:, None]
    dt = jnp.where(rel[None] >= 0, jnp.exp(jnp.maximum(rel, 0.0)[None] * log_gamma[:, None, None]), 0.0)
    xi = jnp.exp((pos + 1.0)[None, :] * log_gamma[:, None])
    zeta = jnp.exp((c - 1.0 - pos)[None, :] * log_gamma[:, None])
    cdec = jnp.broadcast_to(jnp.exp(c * log_gamma)[:, None, None], (RET_HEADS, RET_DV, LANE))

    for l in range(DEPTH):
        wnat, wtr, wuq, wuk, wuv = _prep_layer(w_in[l], w_uq[l], w_ukv[l])
        (qat, ka, vat, qit, wit, ki, qbt, kb, vbt, rqt, rk, rvt, rgt) = _proj(
            x, g_mix[l][None, :], g_q[l][None, :], g_kv[l][None, :], wnat, wtr, wuq, wuk, wuv, tabt, tabn)
        oat = _mla(qat, ka, vat)
        obt = _dsa(qit, wit, ki, qbt, kb, vbt, topk)
        gret = jnp.broadcast_to(g_ret[l][:, None], (RET_HEADS * RET_DV, c))
        oct = _retention(rqt, rk, rvt, rgt, dt, xi, zeta, cdec, gret)
        x = _mlp(x, oat, obt, oct, w_out[l].astype(BF16), g_mlp[l][None, :],
                 w_ff1[l].astype(BF16), w_ff2[l].astype(BF16), g_final[None, :], l == DEPTH - 1)
    return x
```

```python
import functools

import numpy as np
import jax
import jax.numpy as jnp
from jax import lax
from jax.experimental import pallas as pl
from jax.experimental.pallas import tpu as pltpu

D_MODEL = 1024
DEPTH = 2
MLA_HEADS = 8
MLA_NOPE = 64
MLA_ROPE = 32
MLA_V = 64
MLA_Q_RANK = 768
MLA_KV_RANK = 256
DSA_HEADS = 4
DSA_DIM = 64
IDX_HEADS = 8
IDX_DIM = 32
DSA_TOPK_MAX = 256
RET_HEADS = 4
RET_DK = 64
RET_DV = 64
D_FF = 4 * D_MODEL
ROPE_THETA = 10000.0
EPS = 1e-6

MLA_COLS = (MLA_Q_RANK, MLA_KV_RANK, MLA_ROPE)
DSA_COLS = (DSA_HEADS * DSA_DIM, DSA_DIM, DSA_DIM, IDX_HEADS * IDX_DIM, IDX_DIM, IDX_HEADS)
RET_COLS = (RET_HEADS * RET_DK, RET_HEADS * RET_DK, RET_HEADS * RET_DV, RET_HEADS * RET_DV)
SPLITS = MLA_COLS + DSA_COLS + RET_COLS
_OFF = np.concatenate([[0], np.cumsum(SPLITS)]).tolist()
(O_CQ, O_CKV, O_KR, O_AQ, O_AK, O_AV, O_QI, O_KI, O_WI, O_RQ, O_RK, O_RV, O_RG, O_END) = _OFF

LANE = 128
BF16_ROWS = 16
VMEM_LIMIT = 56 * 1024 * 1024

BF16 = jnp.bfloat16
F32 = jnp.float32
NEG = -1e30
LOG2E = float(np.log2(np.e))
INT_MIN = np.int32(-2**31)

N_CQ = 0
N_CKV = N_CQ + MLA_Q_RANK
N_KR = N_CKV + MLA_KV_RANK
N_AK = N_KR + LANE
N_KI = N_AK + LANE
N_RK = N_KI + LANE
N_NAT = N_RK + RET_HEADS * LANE
T_AQ = 0
T_QI = T_AQ + DSA_HEADS * DSA_DIM
T_AV = T_QI + IDX_HEADS * IDX_DIM
T_RQ = T_AV + DSA_DIM
T_RV = T_RQ + RET_HEADS * RET_DK
T_RG = T_RV + RET_HEADS * RET_DV
T_WI = T_RG + RET_HEADS * RET_DV
N_TR = T_WI + IDX_HEADS

TM_PROJ = 512
TQ_MLA = 256
G_MLA = 8
TQ_DSA = 256
CK_DSA = 128
C_RET = 256
TM_MLP = 512
FF_CHUNK = 1024

_NT = (((1,), (1,)), ((), ()))


def _rms(x, g):
    return x * lax.rsqrt(jnp.mean(x * x, axis=-1, keepdims=True) + EPS) * g


def _dot(a, b):
    return jnp.dot(a, b, preferred_element_type=F32)


def _dot_nt(a, b):
    return lax.dot_general(a, b, _NT, preferred_element_type=F32)


def _rope_rows(x1, x2, c, s):
    return x1 * c - x2 * s, x1 * s + x2 * c


def _rope_lanes(x, c, s_signed, half, first_lo):
    lane = lax.broadcasted_iota(jnp.int32, x.shape, 1)
    first = (lane >= first_lo) & (lane < first_lo + half)
    rot = jnp.where(first, pltpu.roll(x, LANE - half, 1), pltpu.roll(x, half, 1))
    return x * c + rot * s_signed


def _proj_kernel(x_ref, gmix_ref, gq_ref, gkv_ref, wnat_ref, wtr_ref, wuq_ref, wuk_ref, wuv_ref,
                 tabt_ref, tabn_ref,
                 qat_ref, ka_ref, vat_ref, qit_ref, wit_ref, ki_ref, qbt_ref, kb_ref, vbt_ref,
                 rqt_ref, rk_ref, rvt_ref, rgt_ref):
    tm = x_ref.shape[0]
    hb = _rms(x_ref[...], gmix_ref[...]).astype(BF16)

    c64t = tabt_ref[0:32, :]
    s64t = tabt_ref[32:64, :]
    c32t = tabt_ref[64:80, :]
    s32t = tabt_ref[80:96, :]

    cq = _dot(hb, wnat_ref[:, N_CQ:N_CQ + MLA_Q_RANK])
    nq = _rms(cq, gq_ref[...]).astype(BF16)
    qt = _dot_nt(wuq_ref[...], nq)
    scale_a = (MLA_NOPE + MLA_ROPE) ** -0.5 * LOG2E
    for h in range(MLA_HEADS):
        r0 = h * LANE
        qat_ref[r0:r0 + 64, :] = (qt[r0:r0 + 64, :] * scale_a).astype(BF16)
        o1, o2 = _rope_rows(qt[r0 + 64:r0 + 80, :], qt[r0 + 80:r0 + 96, :], c32t, s32t)
        qat_ref[r0 + 64:r0 + 80, :] = (o1 * scale_a).astype(BF16)
        qat_ref[r0 + 80:r0 + 96, :] = (o2 * scale_a).astype(BF16)
        qat_ref[r0 + 96:r0 + 128, :] = jnp.zeros((32, tm), BF16)

    ckv = _dot(hb, wnat_ref[:, N_CKV:N_CKV + MLA_KV_RANK])
    nkv = _rms(ckv, gkv_ref[...]).astype(BF16)
    kn = _dot(nkv, wuk_ref[...])
    keys = _dot(hb, wnat_ref[:, N_KR:N_NAT])
    slab = lambda off: keys[:, off - N_KR:off - N_KR + LANE]
    krp = _rope_lanes(slab(N_KR), tabn_ref[0], tabn_ref[1], 16, 64)
    for h in range(MLA_HEADS):
        ka_ref[:, h * LANE:(h + 1) * LANE] = (kn[:, h * LANE:(h + 1) * LANE] + krp).astype(BF16)
    vt = _dot_nt(wuv_ref[...], nkv).astype(BF16)
    for j in range(tm // LANE):
        vat_ref[j] = vt[:, j * LANE:(j + 1) * LANE]

    kb_ref[...] = _rope_lanes(slab(N_AK), tabn_ref[2], tabn_ref[3], 32, 0).astype(BF16)
    ki_ref[...] = _rope_lanes(slab(N_KI), tabn_ref[4], tabn_ref[5], 16, 0).astype(BF16)

    for h in range(RET_HEADS):
        rkp = _rope_lanes(slab(N_RK + h * LANE), tabn_ref[2], tabn_ref[3], 32, 0) * (RET_DK ** -0.5)
        rk_ref[:, h * LANE:(h + 1) * LANE] = rkp.astype(BF16)

    aqt = _dot_nt(wtr_ref[T_AQ:T_AQ + 256, :], hb)
    scale_b = DSA_DIM ** -0.5 * LOG2E
    for h in range(DSA_HEADS):
        o1, o2 = _rope_rows(aqt[h * 64:h * 64 + 32, :], aqt[h * 64 + 32:h * 64 + 64, :], c64t, s64t)
        qbt_ref[h * LANE:h * LANE + 32, :] = (o1 * scale_b).astype(BF16)
        qbt_ref[h * LANE + 32:h * LANE + 64, :] = (o2 * scale_b).astype(BF16)
        qbt_ref[h * LANE + 64:(h + 1) * LANE, :] = jnp.zeros((64, tm), BF16)

    qit = _dot_nt(wtr_ref[T_QI:T_QI + 256, :], hb)
    for h in range(IDX_HEADS):
        o1, o2 = _rope_rows(qit[h * 32:h * 32 + 16, :], qit[h * 32 + 16:h * 32 + 32, :], c32t, s32t)
        qit_ref[h * LANE:h * LANE + 16, :] = o1.astype(BF16)
        qit_ref[h * LANE + 16:h * LANE + 32, :] = o2.astype(BF16)
        qit_ref[h * LANE + 32:(h + 1) * LANE, :] = jnp.zeros((96, tm), BF16)

    idx_scale = (IDX_HEADS ** -0.5) * (IDX_DIM ** -0.5)
    wit_ref[...] = _dot_nt(wtr_ref[T_WI:T_WI + IDX_HEADS, :], hb) * idx_scale

    avt = _dot_nt(wtr_ref[T_AV:T_AV + DSA_DIM, :], hb).astype(BF16)
    for j in range(tm // LANE):
        vbt_ref[j] = avt[:, j * LANE:(j + 1) * LANE]

    rqt = _dot_nt(wtr_ref[T_RQ:T_RQ + 256, :], hb)
    for h in range(RET_HEADS):
        o1, o2 = _rope_rows(rqt[h * 64:h * 64 + 32, :], rqt[h * 64 + 32:h * 64 + 64, :], c64t, s64t)
        rqt_ref[h * LANE:h * LANE + 32, :] = o1.astype(BF16)
        rqt_ref[h * LANE + 32:h * LANE + 64, :] = o2.astype(BF16)
        rqt_ref[h * LANE + 64:(h + 1) * LANE, :] = jnp.zeros((64, tm), BF16)

    rvt_ref[...] = _dot_nt(wtr_ref[T_RV:T_RV + 256, :], hb).astype(BF16)
    rgt_ref[...] = _dot_nt(wtr_ref[T_RG:T_RG + 256, :], hb)


def _full(shape):
    nd = len(shape)
    return pl.BlockSpec(shape, lambda b, i: (0,) * nd)


def _proj(x, gmix, gq, gkv, wnat, wtr, wuq, wuk, wuv, tabt, tabn):
    B, S, D = x.shape
    tm = TM_PROJ
    nch = tm // LANE
    outs = [
        (jax.ShapeDtypeStruct((B, MLA_HEADS * LANE, S), BF16), pl.BlockSpec((None, MLA_HEADS * LANE, tm), lambda b, i: (b, 0, i))),
        (jax.ShapeDtypeStruct((B, S, MLA_HEADS * LANE), BF16), pl.BlockSpec((None, tm, MLA_HEADS * LANE), lambda b, i: (b, i, 0))),
        (jax.ShapeDtypeStruct((B, S // LANE, MLA_HEADS * MLA_V, LANE), BF16), pl.BlockSpec((None, nch, MLA_HEADS * MLA_V, LANE), lambda b, i: (b, i, 0, 0))),
        (jax.ShapeDtypeStruct((B, IDX_HEADS * LANE, S), BF16), pl.BlockSpec((None, IDX_HEADS * LANE, tm), lambda b, i: (b, 0, i))),
        (jax.ShapeDtypeStruct((B, IDX_HEADS, S), F32), pl.BlockSpec((None, IDX_HEADS, tm), lambda b, i: (b, 0, i))),
        (jax.ShapeDtypeStruct((B, S, LANE), BF16), pl.BlockSpec((None, tm, LANE), lambda b, i: (b, i, 0))),
        (jax.ShapeDtypeStruct((B, DSA_HEADS * LANE, S), BF16), pl.BlockSpec((None, DSA_HEADS * LANE, tm), lambda b, i: (b, 0, i))),
        (jax.ShapeDtypeStruct((B, S, LANE), BF16), pl.BlockSpec((None, tm, LANE), lambda b, i: (b, i, 0))),
        (jax.ShapeDtypeStruct((B, S // LANE, DSA_DIM, LANE), BF16), pl.BlockSpec((None, nch, DSA_DIM, LANE), lambda b, i: (b, i, 0, 0))),
        (jax.ShapeDtypeStruct((B, RET_HEADS * LANE, S), BF16), pl.BlockSpec((None, RET_HEADS * LANE, tm), lambda b, i: (b, 0, i))),
        (jax.ShapeDtypeStruct((B, S, RET_HEADS * LANE), BF16), pl.BlockSpec((None, tm, RET_HEADS * LANE), lambda b, i: (b, i, 0))),
        (jax.ShapeDtypeStruct((B, RET_HEADS * RET_DV, S), BF16), pl.BlockSpec((None, RET_HEADS * RET_DV, tm), lambda b, i: (b, 0, i))),
        (jax.ShapeDtypeStruct((B, RET_HEADS * RET_DV, S), F32), pl.BlockSpec((None, RET_HEADS * RET_DV, tm), lambda b, i: (b, 0, i))),
    ]
    in_specs = [
        pl.BlockSpec((None, tm, D), lambda b, i: (b, i, 0)),
        _full(gmix.shape), _full(gq.shape), _full(gkv.shape),
        _full(wnat.shape), _full(wtr.shape), _full(wuq.shape), _full(wuk.shape), _full(wuv.shape),
        pl.BlockSpec((tabt.shape[0], tm), lambda b, i: (0, i)),
        pl.BlockSpec((tabn.shape[0], tm, LANE), lambda b, i: (0, i, 0)),
    ]
    return pl.pallas_call(
        _proj_kernel,
        out_shape=[o[0] for o in outs],
        grid=(B, S // tm),
        in_specs=in_specs,
        out_specs=[o[1] for o in outs],
        compiler_params=pltpu.CompilerParams(
            dimension_semantics=("parallel", "parallel"), vmem_limit_bytes=VMEM_LIMIT),
        name="proj",
    )(x, gmix, gq, gkv, wnat, wtr, wuq, wuk, wuv, tabt, tabn)


def _with_ones_rows(vt):
    return jnp.concatenate([vt, jnp.ones((BF16_ROWS, vt.shape[1]), BF16)], axis=0)


def _mla_kernel(q_ref, k_ref, v_ref, o_ref, s0_sc, s1_sc, m_sc, acc_sc):
    g_heads = acc_sc.shape[0]
    tq = q_ref.shape[1]
    tk = tq
    i = pl.program_id(2)
    nv = tk // LANE
    kidx = lax.broadcasted_iota(jnp.int32, (tk, tq), 0)
    qidx = lax.broadcasted_iota(jnp.int32, (tk, tq), 1)
    bufs = (s0_sc, s1_sc)

    def scores(kt, buf, causal=None):
        for g in range(g_heads):
            s = _dot(k_ref[pl.ds(pl.multiple_of(kt * tk, tk), tk), g * LANE:(g + 1) * LANE],
                     q_ref[g * LANE:(g + 1) * LANE, :])
            if causal is not None:
                s = jnp.where(causal, s, NEG)
            buf[g] = s

    def softmax(kt, buf):
        for g in range(g_heads):
            s = buf[g]
            m_old = m_sc[g]
            m_new = jnp.maximum(m_old, jnp.max(s, axis=0, keepdims=True))
            a = jnp.exp2(m_old - m_new)
            p = jnp.exp2(s - m_new)
            vt = jnp.concatenate([v_ref[kt * nv + j, g * MLA_V:(g + 1) * MLA_V, :] for j in range(nv)],
                                 axis=1)
            m_sc[g] = m_new
            acc_sc[g] = a * acc_sc[g] + _dot(_with_ones_rows(vt), p.astype(BF16))

    m_sc[...] = jnp.full(m_sc.shape, NEG, F32)
    acc_sc[...] = jnp.zeros_like(acc_sc)

    def step(k, parity, diagonal=False):
        scores(k + 1, bufs[1 - parity], (kidx <= qidx) if diagonal else None)
        softmax(k, bufs[parity])

    scores(0, bufs[0], kidx <= i * tq + qidx)

    def pair(p, _):
        step(2 * p, 0)
        step(2 * p + 1, 1)
        return 0

    lax.fori_loop(0, (i - 1) // 2, pair, 0)

    @pl.when(i % 2 == 0)
    def _():
        @pl.when(i >= 2)
        def _():
            step(i - 2, 0)
            step(i - 1, 1, diagonal=True)
        softmax(i, bufs[0])

    @pl.when(i % 2 == 1)
    def _():
        step(i - 1, 0, diagonal=True)
        softmax(i, bufs[1])

    for g in range(g_heads):
        acc = acc_sc[g]
        o_ref[g * MLA_V:(g + 1) * MLA_V, :] = (acc[:MLA_V] / acc[MLA_V:MLA_V + 1]).astype(BF16)


def _mla(qat, ka, vat):
    B, _, S = qat.shape
    tq = TQ_MLA
    g = G_MLA
    return pl.pallas_call(
        _mla_kernel,
        out_shape=jax.ShapeDtypeStruct((B, MLA_HEADS * MLA_V, S), BF16),
        grid=(B, MLA_HEADS // g, S // tq),
        in_specs=[
            pl.BlockSpec((None, g * LANE, tq), lambda b, h, i: (b, h, i)),
            pl.BlockSpec((None, S, g * LANE), lambda b, h, i: (b, 0, h)),
            pl.BlockSpec((None, S // LANE, g * MLA_V, LANE), lambda b, h, i: (b, 0, h, 0)),
        ],
        out_specs=pl.BlockSpec((None, g * MLA_V, tq), lambda b, h, i: (b, h, i)),
        scratch_shapes=[pltpu.VMEM((g, tq, tq), F32), pltpu.VMEM((g, tq, tq), F32),
                        pltpu.VMEM((g, 1, tq), F32),
                        pltpu.VMEM((g, MLA_V + BF16_ROWS, tq), F32)],
        compiler_params=pltpu.CompilerParams(
            dimension_semantics=("parallel", "parallel", "arbitrary"), vmem_limit_bytes=VMEM_LIMIT),
        name="mla",
    )(qat, ka, vat)


def _count_rows(mask):
    r, t = mask.shape
    return jnp.sum(mask.astype(jnp.int32).reshape(r // 8, 8, t), axis=0)


def _dsa_kernel(topk, qit_ref, wit_ref, ki_ref, qbt_ref, kb_ref, vbt_ref, o_ref,
                sc_sc, att0_sc, att1_sc, thr_sc, jsel_sc, m_sc, acc_sc):
    tq = o_ref.shape[1]
    ck = CK_DSA
    sub = tq // ck
    i = pl.program_id(1)
    nblk = i + 1

    def rows(ref, r0, n):
        if isinstance(r0, int):
            return ref[r0:r0 + n, :]
        return ref[pl.ds(pl.multiple_of(r0, n), n), :]

    qpos_c = i * tq + lax.broadcasted_iota(jnp.int32, (ck, tq), 1)
    krow_c = lax.broadcasted_iota(jnp.int32, (ck, tq), 0)

    def score_block(cb):
        for j in range(sub):
            r0 = cb * tq + j * ck
            kc = rows(ki_ref, r0, ck)
            acc = jnp.zeros((ck, tq), F32)
            for h in range(IDX_HEADS):
                rel = _dot(kc, qit_ref[h * LANE:(h + 1) * LANE, :])
                acc = acc + wit_ref[h:h + 1, :] * jnp.maximum(rel, 0.0)
            sc_sc[pl.ds(pl.multiple_of(r0, ck), ck), :] = jnp.where(r0 + krow_c <= qpos_c, acc, -jnp.inf)

    def score_pair(p, _):
        score_block(2 * p)
        score_block(2 * p + 1)
        return 0

    lax.fori_loop(0, nblk // 2, score_pair, 0)

    @pl.when(nblk % 2 == 1)
    def _():
        score_block(nblk - 1)

    krow = lax.broadcasted_iota(jnp.int32, (tq, tq), 0)
    qpos = i * tq + lax.broadcasted_iota(jnp.int32, (tq, tq), 1)

    def count(pred, nb=None):
        def body(cb, a):
            return a + _count_rows(pred(rows(sc_sc, cb * tq, tq), cb * tq + krow))
        part = jnp.zeros((8, tq), jnp.int32)
        if nb is None:
            part = lax.fori_loop(0, nblk, body, part)
        else:
            for cb in range(nb):
                part = body(cb, part)
        return jnp.sum(part, axis=0, keepdims=True)

    def flip(word):
        return jnp.where(word < 0, word ^ jnp.int32(0x7FFFFFFF), word)

    def key_to_score(key):
        return pltpu.bitcast(flip(key), F32)

    thr_sc[...] = jnp.full((1, tq), -jnp.inf, F32)
    jsel_sc[...] = jnp.full((1, tq), sc_sc.shape[0], jnp.int32)

    def bisect(nb):
        assert tq == 2 * LANE
        la, lb = slice(0, LANE), slice(LANE, 2 * LANE)

        def part_count(lanes, cf):
            part = jnp.zeros((8, LANE), jnp.int32)
            for cb in range(nb):
                part = part + _count_rows(sc_sc[cb * tq:(cb + 1) * tq, lanes] >= cf[0:1, :])
            return part

        def accept(part, cand, ans):
            return jnp.where(jnp.sum(part, axis=0, keepdims=True) >= topk, cand, ans)

        def cand_of(t, ans):
            return jnp.where(t == 0, jnp.zeros_like(ans), ans + jnp.left_shift(jnp.int32(1), 31 - t))

        def body(t, st):
            ans_a, ans_b, cand_b, part_b = st
            ans_b = accept(part_b, cand_b, ans_b)
            cand_b = cand_of(t, ans_b)
            cand_a = cand_of(t, ans_a)
            ans_a = accept(part_count(la, key_to_score(cand_a)), cand_a, ans_a)
            return ans_a, ans_b, cand_b, part_count(lb, key_to_score(cand_b))

        start = jnp.full((8, LANE), INT_MIN, jnp.int32)
        first = body(jnp.int32(0), (start, start, start, jnp.zeros((8, LANE), jnp.int32)))
        ans_a, ans_b, cand_b, part_b = lax.fori_loop(1, 32, body, first)
        thr_sc[:, la] = key_to_score(ans_a)[0:1, :]
        thr_sc[:, lb] = key_to_score(accept(part_b, cand_b, ans_b))[0:1, :]

    for v in range(topk // tq, sc_sc.shape[0] // tq):
        pl.when(i == v)(functools.partial(bisect, v + 1))

    @pl.when(i * tq >= topk)
    def _():
        thr = thr_sc[...]
        n_ge = count(lambda s, kidx: s >= thr)
        excess = jnp.max(n_ge) > topk

        @pl.when(excess)
        def _():
            need = topk - count(lambda s, kidx: s > thr)

            def jbit(t, j):
                cand = j + jnp.left_shift(jnp.int32(1), 10 - t)
                n = count(lambda s, kidx: (s == thr) & (kidx < cand))
                return jnp.where(n < need, cand, j)

            jsel_sc[...] = lax.fori_loop(0, 11, jbit, jnp.zeros((1, tq), jnp.int32))

    thr = thr_sc[...]
    jsel = jsel_sc[...]
    nh = DSA_HEADS
    qall = jnp.concatenate([qbt_ref[h * LANE:(h + 1) * LANE, :] for h in range(nh)], axis=1)
    bufs = (att0_sc, att1_sc)

    def logits(cb, buf):
        sc = rows(sc_sc, cb * tq, tq)
        kidx = cb * tq + krow
        sel = ((sc > thr) | ((sc == thr) & (kidx <= jsel))) & (kidx <= qpos)
        bias = jnp.where(sel, 0.0, NEG)
        for j in range(sub):
            b = bias[j * ck:(j + 1) * ck, :]
            buf[j * ck:(j + 1) * ck, :] = (_dot(rows(kb_ref, cb * tq + j * ck, ck), qall)
                                           + jnp.concatenate([b] * nh, axis=1))

    def attend(cb, buf):
        vt = _with_ones_rows(jnp.concatenate([vbt_ref[cb * sub + j] for j in range(sub)], axis=1))
        for h in range(nh):
            cols = slice(h * tq, (h + 1) * tq)
            s = buf[:, cols]
            m_old = m_sc[:, cols]
            m_new = jnp.maximum(m_old, jnp.max(s, axis=0, keepdims=True))
            a = jnp.exp2(m_old - m_new)
            p = jnp.exp2(s - m_new)
            m_sc[:, cols] = m_new
            acc_sc[:, cols] = a * acc_sc[:, cols] + _dot(vt, p.astype(BF16))

    def step(k, parity):
        logits(k + 1, bufs[1 - parity])
        attend(k, bufs[parity])

    m_sc[...] = jnp.full(m_sc.shape, NEG, F32)
    acc_sc[...] = jnp.zeros_like(acc_sc)
    logits(0, bufs[0])

    def pair(p, _):
        step(2 * p, 0)
        step(2 * p + 1, 1)
        return 0

    lax.fori_loop(0, i // 2, pair, 0)

    @pl.when(i % 2 == 1)
    def _():
        step(i - 1, 0)
        attend(i, bufs[1])

    @pl.when(i % 2 == 0)
    def _():
        attend(i, bufs[0])

    out = acc_sc[0:DSA_DIM, :] / acc_sc[DSA_DIM:DSA_DIM + 1, :]
    for h in range(nh):
        o_ref[h * DSA_DIM:(h + 1) * DSA_DIM, :] = out[:, h * tq:(h + 1) * tq].astype(BF16)


def _dsa(qit, wit, ki, qbt, kb, vbt, topk):
    B, _, S = qit.shape
    tq = TQ_DSA
    assert topk % tq == 0 or topk >= S, "query tiles must not straddle the top-k boundary"
    return pl.pallas_call(
        functools.partial(_dsa_kernel, topk),
        out_shape=jax.ShapeDtypeStruct((B, DSA_HEADS * DSA_DIM, S), BF16),
        grid=(B, S // tq),
        in_specs=[
            pl.BlockSpec((None, IDX_HEADS * LANE, tq), lambda b, i: (b, 0, i)),
            pl.BlockSpec((None, IDX_HEADS, tq), lambda b, i: (b, 0, i)),
            pl.BlockSpec((None, S, LANE), lambda b, i: (b, 0, 0)),
            pl.BlockSpec((None, DSA_HEADS * LANE, tq), lambda b, i: (b, 0, i)),
            pl.BlockSpec((None, S, LANE), lambda b, i: (b, 0, 0)),
            pl.BlockSpec((None, S // LANE, DSA_DIM, LANE), lambda b, i: (b, 0, 0, 0)),
        ],
        out_specs=pl.BlockSpec((None, DSA_HEADS * DSA_DIM, tq), lambda b, i: (b, 0, i)),
        scratch_shapes=[
            pltpu.VMEM((S, tq), F32),
            pltpu.VMEM((tq, DSA_HEADS * tq), F32),
            pltpu.VMEM((tq, DSA_HEADS * tq), F32),
            pltpu.VMEM((1, tq), F32),
            pltpu.VMEM((1, tq), jnp.int32),
            pltpu.VMEM((1, DSA_HEADS * tq), F32),
            pltpu.VMEM((DSA_DIM + BF16_ROWS, DSA_HEADS * tq), F32),
        ],
        compiler_params=pltpu.CompilerParams(
            dimension_semantics=("parallel", "arbitrary"), vmem_limit_bytes=VMEM_LIMIT),
        name="dsa",
    )(qit, wit, ki, qbt, kb, vbt)


def _ret_kernel(rqt_ref, rk_ref, rvt_ref, rgt_ref, dt_ref, xi_ref, zeta_ref, cdec_ref, gret_ref,
                o_ref, state_sc):
    n = pl.program_id(1)

    @pl.when(n == 0)
    def _():
        state_sc[...] = jnp.zeros_like(state_sc)

    for h in range(RET_HEADS):
        qt = rqt_ref[h * LANE:(h + 1) * LANE, :]
        k = rk_ref[:, h * LANE:(h + 1) * LANE]
        vt = rvt_ref[h * RET_DV:(h + 1) * RET_DV, :]
        inner_t = _dot(k, qt) * dt_ref[h]
        rt = state_sc[h]
        qx = (qt.astype(F32) * xi_ref[h:h + 1, :]).astype(BF16)
        out_t = _dot(vt, inner_t.astype(BF16)) + _dot(rt.astype(BF16), qx)
        vz = (vt.astype(F32) * zeta_ref[h:h + 1, :]).astype(BF16)
        state_sc[h] = rt * cdec_ref[h] + _dot(vz, k)

        mu = jnp.mean(out_t, axis=0, keepdims=True)
        xc = out_t - mu
        var = jnp.mean(xc * xc, axis=0, keepdims=True)
        y = xc * lax.rsqrt(var + EPS) * gret_ref[h * RET_DV:(h + 1) * RET_DV, :]
        g = rgt_ref[h * RET_DV:(h + 1) * RET_DV, :]
        o_ref[h * RET_DV:(h + 1) * RET_DV, :] = (g * jax.nn.sigmoid(g) * y).astype(BF16)


def _retention(rqt, rk, rvt, rgt, dt, xi, zeta, cdec, gret):
    B, _, S = rqt.shape
    c = C_RET
    return pl.pallas_call(
        _ret_kernel,
        out_shape=jax.ShapeDtypeStruct((B, RET_HEADS * RET_DV, S), BF16),
        grid=(B, S // c),
        in_specs=[
            pl.BlockSpec((None, RET_HEADS * LANE, c), lambda b, n: (b, 0, n)),
            pl.BlockSpec((None, c, RET_HEADS * LANE), lambda b, n: (b, n, 0)),
            pl.BlockSpec((None, RET_HEADS * RET_DV, c), lambda b, n: (b, 0, n)),
            pl.BlockSpec((None, RET_HEADS * RET_DV, c), lambda b, n: (b, 0, n)),
            _full(dt.shape), _full(xi.shape), _full(zeta.shape), _full(cdec.shape), _full(gret.shape),
        ],
        out_specs=pl.BlockSpec((None, RET_HEADS * RET_DV, c), lambda b, n: (b, 0, n)),
        scratch_shapes=[pltpu.VMEM((RET_HEADS, RET_DV, LANE), F32)],
        compiler_params=pltpu.CompilerParams(
            dimension_semantics=("parallel", "arbitrary"), vmem_limit_bytes=VMEM_LIMIT),
        name="retention",
    )(rqt, rk, rvt, rgt, dt, xi, zeta, cdec, gret)


def _mlp_kernel(final, x_ref, oat_ref, obt_ref, oct_ref, wout_ref, gmlp_ref, w1_ref, w2_ref, gfin_ref,
                y_ref):
    def tr(ref):
        return ref[...].astype(F32).T.astype(BF16)

    na = MLA_HEADS * MLA_V
    nb = DSA_HEADS * DSA_DIM
    attn = (_dot(tr(oat_ref), wout_ref[0:na, :])
            + _dot(tr(obt_ref), wout_ref[na:na + nb, :])
            + _dot(tr(oct_ref), wout_ref[na + nb:, :]))
    x1 = x_ref[...] + attn
    hf = _rms(x1, gmlp_ref[...]).astype(BF16)
    acc = x1
    for f in range(D_FF // FF_CHUNK):
        u = jnp.maximum(_dot(hf, w1_ref[:, f * FF_CHUNK:(f + 1) * FF_CHUNK]), 0.0)
        acc = acc + _dot((u * u).astype(BF16), w2_ref[f * FF_CHUNK:(f + 1) * FF_CHUNK, :])
    if final:
        acc = _rms(acc, gfin_ref[...])
    y_ref[...] = acc


def _mlp(x, oat, obt, oct, wout, gmlp, w1, w2, gfin, final):
    B, S, D = x.shape
    tm = TM_MLP

    def const(shape):
        nd = len(shape)
        return pl.BlockSpec(shape, lambda b, i: (0,) * nd, pipeline_mode=pl.Buffered(1))

    return pl.pallas_call(
        functools.partial(_mlp_kernel, final),
        out_shape=jax.ShapeDtypeStruct((B, S, D), F32),
        grid=(B, S // tm),
        in_specs=[
            pl.BlockSpec((None, tm, D), lambda b, i: (b, i, 0)),
            pl.BlockSpec((None, oat.shape[1], tm), lambda b, i: (b, 0, i)),
            pl.BlockSpec((None, obt.shape[1], tm), lambda b, i: (b, 0, i)),
            pl.BlockSpec((None, oct.shape[1], tm), lambda b, i: (b, 0, i)),
            const(wout.shape), const(gmlp.shape), const(w1.shape), const(w2.shape), const(gfin.shape),
        ],
        out_specs=pl.BlockSpec((None, tm, D), lambda b, i: (b, i, 0)),
        compiler_params=pltpu.CompilerParams(
            dimension_semantics=("parallel", "parallel"), vmem_limit_bytes=VMEM_LIMIT),
        name="mlp",
    )(x, oat, obt, oct, wout, gmlp, w1, w2, gfin)


def _rope_tables(seq, dim):
    pos = jnp.arange(seq, dtype=F32)
    inv = ROPE_THETA ** (-jnp.arange(0, dim, 2, dtype=F32) / dim)
    ang = pos[:, None] * inv[None, :]
    return jnp.cos(ang), jnp.sin(ang)


def _lane_table(c, s, lo, seq):
    half = c.shape[1]
    ct = jnp.zeros((seq, LANE), F32).at[:, lo:lo + 2 * half].set(jnp.concatenate([c, c], axis=1))
    st = jnp.zeros((seq, LANE), F32).at[:, lo:lo + 2 * half].set(jnp.concatenate([-s, s], axis=1))
    return ct, st


def _pad_cols(w, width):
    return jnp.pad(w, ((0, 0), (0, width - w.shape[1])))


def _prep_layer(w_in, w_uq, w_ukv):
    z = lambda n: jnp.zeros((D_MODEL, n), F32)
    col = lambda a, b: w_in[:, a:b]
    nat = [col(O_CQ, O_CKV), col(O_CKV, O_KR),
           jnp.concatenate([z(64), col(O_KR, O_AQ), z(32)], axis=1),
           _pad_cols(col(O_AK, O_AV), LANE),
           _pad_cols(col(O_KI, O_WI), LANE)]
    for h in range(RET_HEADS):
        nat.append(_pad_cols(col(O_RK + h * RET_DK, O_RK + (h + 1) * RET_DK), LANE))
    wnat = jnp.concatenate(nat, axis=1).astype(BF16)
    wtr = jnp.concatenate([col(O_AQ, O_AK), col(O_QI, O_KI), col(O_AV, O_QI), col(O_RQ, O_RK),
                           col(O_RV, O_RG), col(O_RG, O_END), col(O_WI, O_RQ)], axis=1).T.astype(BF16)
    dq = MLA_NOPE + MLA_ROPE
    wuq = jnp.pad(w_uq.reshape(MLA_Q_RANK, MLA_HEADS, dq), ((0, 0), (0, 0), (0, LANE - dq)))
    wuq = wuq.reshape(MLA_Q_RANK, MLA_HEADS * LANE).T.astype(BF16)
    wkv = w_ukv.reshape(MLA_KV_RANK, MLA_HEADS, MLA_NOPE + MLA_V)
    wuk = jnp.pad(wkv[:, :, :MLA_NOPE], ((0, 0), (0, 0), (0, LANE - MLA_NOPE)))
    wuk = wuk.reshape(MLA_KV_RANK, MLA_HEADS * LANE).astype(BF16)
    wuv = wkv[:, :, MLA_NOPE:].reshape(MLA_KV_RANK, MLA_HEADS * MLA_V).T.astype(BF16)
    return wnat, wtr, wuq, wuk, wuv


def kernel(x, g_mix, w_in, g_q, w_uq, g_kv, w_ukv, g_ret, w_out, g_mlp, w_ff1, w_ff2, g_final):
    B, S, _ = x.shape
    topk = min(DSA_TOPK_MAX, S // 4)

    cos32, sin32 = _rope_tables(S, 32)
    cos64, sin64 = _rope_tables(S, 64)
    tabt = jnp.concatenate([cos64.T, sin64.T, cos32.T, sin32.T], axis=0)
    tabn = jnp.stack(_lane_table(cos32, sin32, 64, S) + _lane_table(cos64, sin64, 0, S)
                     + _lane_table(cos32, sin32, 0, S))

    log_gamma = jnp.log1p(-jnp.exp2(-5.0 - jnp.arange(RET_HEADS, dtype=F32)))
    c = C_RET
    pos = jnp.arange(c, dtype=F32)
    rel = pos[None, :] - pos[:, None]
    dt = jnp.where(rel[None] >= 0, jnp.exp(jnp.maximum(rel, 0.0)[None] * log_gamma[:, None, None]), 0.0)
    xi = jnp.exp((pos + 1.0)[None, :] * log_gamma[:, None])
    zeta = jnp.exp((c - 1.0 - pos)[None, :] * log_gamma[:, None])
    cdec = jnp.broadcast_to(jnp.exp(c * log_gamma)[:, None, None], (RET_HEADS, RET_DV, LANE))

    for l in range(DEPTH):
        wnat, wtr, wuq, wuk, wuv = _prep_layer(w_in[l], w_uq[l], w_ukv[l])
        (qat, ka, vat, qit, wit, ki, qbt, kb, vbt, rqt, rk, rvt, rgt) = _proj(
            x, g_mix[l][None, :], g_q[l][None, :], g_kv[l][None, :], wnat, wtr, wuq, wuk, wuv, tabt, tabn)
        oat = _mla(qat, ka, vat)
        obt = _dsa(qit, wit, ki, qbt, kb, vbt, topk)
        gret = jnp.broadcast_to(g_ret[l][:, None], (RET_HEADS * RET_DV, c))
        oct = _retention(rqt, rk, rvt, rgt, dt, xi, zeta, cdec, gret)
        x = _mlp(x, oat, obt, oct, w_out[l].astype(BF16), g_mlp[l][None, :],
                 w_ff1[l].astype(BF16), w_ff2[l].astype(BF16), g_final[None, :], l == DEPTH - 1)
    return x
```

```python
import functools

import numpy as np
import jax
import jax.numpy as jnp
from jax import lax
from jax.experimental import pallas as pl
from jax.experimental.pallas import tpu as pltpu

D_MODEL = 1024
DEPTH = 2
MLA_HEADS = 8
MLA_NOPE = 64
MLA_ROPE = 32
MLA_V = 64
MLA_Q_RANK = 768
MLA_KV_RANK = 256
DSA_HEADS = 4
DSA_DIM = 64
IDX_HEADS = 8
IDX_DIM = 32
DSA_TOPK_MAX = 256
RET_HEADS = 4
RET_DK = 64
RET_DV = 64
D_FF = 4 * D_MODEL
ROPE_THETA = 10000.0
EPS = 1e-6

MLA_COLS = (MLA_Q_RANK, MLA_KV_RANK, MLA_ROPE)
DSA_COLS = (DSA_HEADS * DSA_DIM, DSA_DIM, DSA_DIM, IDX_HEADS * IDX_DIM, IDX_DIM, IDX_HEADS)
RET_COLS = (RET_HEADS * RET_DK, RET_HEADS * RET_DK, RET_HEADS * RET_DV, RET_HEADS * RET_DV)
SPLITS = MLA_COLS + DSA_COLS + RET_COLS
_OFF = np.concatenate([[0], np.cumsum(SPLITS)]).tolist()
(O_CQ, O_CKV, O_KR, O_AQ, O_AK, O_AV, O_QI, O_KI, O_WI, O_RQ, O_RK, O_RV, O_RG, O_END) = _OFF

LANE = 128
BF16_ROWS = 16
VMEM_LIMIT = 56 * 1024 * 1024

BF16 = jnp.bfloat16
F32 = jnp.float32
NEG = -1e30
LOG2E = float(np.log2(np.e))
INT_MIN = np.int32(-2**31)

N_CQ = 0
N_CKV = N_CQ + MLA_Q_RANK
N_KR = N_CKV + MLA_KV_RANK
N_AK = N_KR + LANE
N_KI = N_AK + LANE
N_RK = N_KI + LANE
N_NAT = N_RK + RET_HEADS * LANE
T_AQ = 0
T_QI = T_AQ + DSA_HEADS * DSA_DIM
T_AV = T_QI + IDX_HEADS * IDX_DIM
T_RQ = T_AV + DSA_DIM
T_RV = T_RQ + RET_HEADS * RET_DK
T_RG = T_RV + RET_HEADS * RET_DV
T_WI = T_RG + RET_HEADS * RET_DV
N_TR = T_WI + IDX_HEADS

TM_PROJ = 512
TQ_MLA = 256
G_MLA = 8
TQ_DSA = 256
CK_DSA = 128
C_RET = 256
TM_MLP = 512
FF_CHUNK = 1024

_NT = (((1,), (1,)), ((), ()))


def _rms(x, g):
    return x * lax.rsqrt(jnp.mean(x * x, axis=-1, keepdims=True) + EPS) * g


def _dot(a, b):
    return jnp.dot(a, b, preferred_element_type=F32)


def _dot_nt(a, b):
    return lax.dot_general(a, b, _NT, preferred_element_type=F32)


def _rope_rows(x1, x2, c, s):
    return x1 * c - x2 * s, x1 * s + x2 * c


def _rope_lanes(x, c, s_signed, half, first_lo):
    lane = lax.broadcasted_iota(jnp.int32, x.shape, 1)
    first = (lane >= first_lo) & (lane < first_lo + half)
    rot = jnp.where(first, pltpu.roll(x, LANE - half, 1), pltpu.roll(x, half, 1))
    return x * c + rot * s_signed


def _proj_kernel(x_ref, gmix_ref, gq_ref, gkv_ref, wnat_ref, wtr_ref, wuq_ref, wuk_ref, wuv_ref,
                 tabt_ref, tabn_ref,
                 qat_ref, ka_ref, vat_ref, qit_ref, wit_ref, ki_ref, qbt_ref, kb_ref, vbt_ref,
                 rqt_ref, rk_ref, rvt_ref, rgt_ref):
    tm = x_ref.shape[0]
    hb = _rms(x_ref[...], gmix_ref[...]).astype(BF16)

    c64t = tabt_ref[0:32, :]
    s64t = tabt_ref[32:64, :]
    c32t = tabt_ref[64:80, :]
    s32t = tabt_ref[80:96, :]

    cq = _dot(hb, wnat_ref[:, N_CQ:N_CQ + MLA_Q_RANK])
    nq = _rms(cq, gq_ref[...]).astype(BF16)
    qt = _dot_nt(wuq_ref[...], nq)
    scale_a = (MLA_NOPE + MLA_ROPE) ** -0.5 * LOG2E
    for h in range(MLA_HEADS):
        r0 = h * LANE
        qat_ref[r0:r0 + 64, :] = (qt[r0:r0 + 64, :] * scale_a).astype(BF16)
        o1, o2 = _rope_rows(qt[r0 + 64:r0 + 80, :], qt[r0 + 80:r0 + 96, :], c32t, s32t)
        qat_ref[r0 + 64:r0 + 80, :] = (o1 * scale_a).astype(BF16)
        qat_ref[r0 + 80:r0 + 96, :] = (o2 * scale_a).astype(BF16)
        qat_ref[r0 + 96:r0 + 128, :] = jnp.zeros((32, tm), BF16)

    ckv = _dot(hb, wnat_ref[:, N_CKV:N_CKV + MLA_KV_RANK])
    nkv = _rms(ckv, gkv_ref[...]).astype(BF16)
    kn = _dot(nkv, wuk_ref[...])
    keys = _dot(hb, wnat_ref[:, N_KR:N_NAT])
    slab = lambda off: keys[:, off - N_KR:off - N_KR + LANE]
    krp = _rope_lanes(slab(N_KR), tabn_ref[0], tabn_ref[1], 16, 64)
    for h in range(MLA_HEADS):
        ka_ref[:, h * LANE:(h + 1) * LANE] = (kn[:, h * LANE:(h + 1) * LANE] + krp).astype(BF16)
    vt = _dot_nt(wuv_ref[...], nkv).astype(BF16)
    for j in range(tm // LANE):
        vat_ref[j] = vt[:, j * LANE:(j + 1) * LANE]

    kb_ref[...] = _rope_lanes(slab(N_AK), tabn_ref[2], tabn_ref[3], 32, 0).astype(BF16)
    ki_ref[...] = _rope_lanes(slab(N_KI), tabn_ref[4], tabn_ref[5], 16, 0).astype(BF16)

    for h in range(RET_HEADS):
        rkp = _rope_lanes(slab(N_RK + h * LANE), tabn_ref[2], tabn_ref[3], 32, 0) * (RET_DK ** -0.5)
        rk_ref[:, h * LANE:(h + 1) * LANE] = rkp.astype(BF16)

    aqt = _dot_nt(wtr_ref[T_AQ:T_AQ + 256, :], hb)
    scale_b = DSA_DIM ** -0.5 * LOG2E
    for h in range(DSA_HEADS):
        o1, o2 = _rope_rows(aqt[h * 64:h * 64 + 32, :], aqt[h * 64 + 32:h * 64 + 64, :], c64t, s64t)
        qbt_ref[h * LANE:h * LANE + 32, :] = (o1 * scale_b).astype(BF16)
        qbt_ref[h * LANE + 32:h * LANE + 64, :] = (o2 * scale_b).astype(BF16)
        qbt_ref[h * LANE + 64:(h + 1) * LANE, :] = jnp.zeros((64, tm), BF16)

    qit = _dot_nt(wtr_ref[T_QI:T_QI + 256, :], hb)
    for h in range(IDX_HEADS):
        o1, o2 = _rope_rows(qit[h * 32:h * 32 + 16, :], qit[h * 32 + 16:h * 32 + 32, :], c32t, s32t)
        qit_ref[h * LANE:h * LANE + 16, :] = o1.astype(BF16)
        qit_ref[h * LANE + 16:h * LANE + 32, :] = o2.astype(BF16)
        qit_ref[h * LANE + 32:(h + 1) * LANE, :] = jnp.zeros((96, tm), BF16)

    idx_scale = (IDX_HEADS ** -0.5) * (IDX_DIM ** -0.5)
    wit_ref[...] = _dot_nt(wtr_ref[T_WI:T_WI + IDX_HEADS, :], hb) * idx_scale

    avt = _dot_nt(wtr_ref[T_AV:T_AV + DSA_DIM, :], hb).astype(BF16)
    for j in range(tm // LANE):
        vbt_ref[j] = avt[:, j * LANE:(j + 1) * LANE]

    rqt = _dot_nt(wtr_ref[T_RQ:T_RQ + 256, :], hb)
    for h in range(RET_HEADS):
        o1, o2 = _rope_rows(rqt[h * 64:h * 64 + 32, :], rqt[h * 64 + 32:h * 64 + 64, :], c64t, s64t)
        rqt_ref[h * LANE:h * LANE + 32, :] = o1.astype(BF16)
        rqt_ref[h * LANE + 32:h * LANE + 64, :] = o2.astype(BF16)
        rqt_ref[h * LANE + 64:(h + 1) * LANE, :] = jnp.zeros((64, tm), BF16)

    rvt_ref[...] = _dot_nt(wtr_ref[T_RV:T_RV + 256, :], hb).astype(BF16)
    rgt_ref[...] = _dot_nt(wtr_ref[T_RG:T_RG + 256, :], hb)


def _full(shape):
    nd = len(shape)
    return pl.BlockSpec(shape, lambda b, i: (0,) * nd)


def _proj(x, gmix, gq, gkv, wnat, wtr, wuq, wuk, wuv, tabt, tabn):
    B, S, D = x.shape
    tm = TM_PROJ
    nch = tm // LANE
    outs = [
        (jax.ShapeDtypeStruct((B, MLA_HEADS * LANE, S), BF16), pl.BlockSpec((None, MLA_HEADS * LANE, tm), lambda b, i: (b, 0, i))),
        (jax.ShapeDtypeStruct((B, S, MLA_HEADS * LANE), BF16), pl.BlockSpec((None, tm, MLA_HEADS * LANE), lambda b, i: (b, i, 0))),
        (jax.ShapeDtypeStruct((B, S // LANE, MLA_HEADS * MLA_V, LANE), BF16), pl.BlockSpec((None, nch, MLA_HEADS * MLA_V, LANE), lambda b, i: (b, i, 0, 0))),
        (jax.ShapeDtypeStruct((B, IDX_HEADS * LANE, S), BF16), pl.BlockSpec((None, IDX_HEADS * LANE, tm), lambda b, i: (b, 0, i))),
        (jax.ShapeDtypeStruct((B, IDX_HEADS, S), F32), pl.BlockSpec((None, IDX_HEADS, tm), lambda b, i: (b, 0, i))),
        (jax.ShapeDtypeStruct((B, S, LANE), BF16), pl.BlockSpec((None, tm, LANE), lambda b, i: (b, i, 0))),
        (jax.ShapeDtypeStruct((B, DSA_HEADS * LANE, S), BF16), pl.BlockSpec((None, DSA_HEADS * LANE, tm), lambda b, i: (b, 0, i))),
        (jax.ShapeDtypeStruct((B, S, LANE), BF16), pl.BlockSpec((None, tm, LANE), lambda b, i: (b, i, 0))),
        (jax.ShapeDtypeStruct((B, S // LANE, DSA_DIM, LANE), BF16), pl.BlockSpec((None, nch, DSA_DIM, LANE), lambda b, i: (b, i, 0, 0))),
        (jax.ShapeDtypeStruct((B, RET_HEADS * LANE, S), BF16), pl.BlockSpec((None, RET_HEADS * LANE, tm), lambda b, i: (b, 0, i))),
        (jax.ShapeDtypeStruct((B, S, RET_HEADS * LANE), BF16), pl.BlockSpec((None, tm, RET_HEADS * LANE), lambda b, i: (b, i, 0))),
        (jax.ShapeDtypeStruct((B, RET_HEADS * RET_DV, S), BF16), pl.BlockSpec((None, RET_HEADS * RET_DV, tm), lambda b, i: (b, 0, i))),
        (jax.ShapeDtypeStruct((B, RET_HEADS * RET_DV, S), F32), pl.BlockSpec((None, RET_HEADS * RET_DV, tm), lambda b, i: (b, 0, i))),
    ]
    in_specs = [
        pl.BlockSpec((None, tm, D), lambda b, i: (b, i, 0)),
        _full(gmix.shape), _full(gq.shape), _full(gkv.shape),
        _full(wnat.shape), _full(wtr.shape), _full(wuq.shape), _full(wuk.shape), _full(wuv.shape),
        pl.BlockSpec((tabt.shape[0], tm), lambda b, i: (0, i)),
        pl.BlockSpec((tabn.shape[0], tm, LANE), lambda b, i: (0, i, 0)),
    ]
    return pl.pallas_call(
        _proj_kernel,
        out_shape=[o[0] for o in outs],
        grid=(B, S // tm),
        in_specs=in_specs,
        out_specs=[o[1] for o in outs],
        compiler_params=pltpu.CompilerParams(
            dimension_semantics=("parallel", "parallel"), vmem_limit_bytes=VMEM_LIMIT),
        name="proj",
    )(x, gmix, gq, gkv, wnat, wtr, wuq, wuk, wuv, tabt, tabn)


def _with_ones_rows(vt):
    return jnp.concatenate([vt, jnp.ones((BF16_ROWS, vt.shape[1]), BF16)], axis=0)


def _mla_kernel(q_ref, k_ref, v_ref, o_ref, s0_sc, s1_sc, m_sc, acc_sc):
    g_heads = acc_sc.shape[0]
    tq = q_ref.shape[1]
    tk = tq
    i = pl.program_id(2)
    nv = tk // LANE
    kidx = lax.broadcasted_iota(jnp.int32, (tk, tq), 0)
    qidx = lax.broadcasted_iota(jnp.int32, (tk, tq), 1)
    bufs = (s0_sc, s1_sc)

    def scores(kt, buf, causal=None):
        for g in range(g_heads):
            s = _dot(k_ref[pl.ds(pl.multiple_of(kt * tk, tk), tk), g * LANE:(g + 1) * LANE],
                     q_ref[g * LANE:(g + 1) * LANE, :])
            if causal is not None:
                s = jnp.where(causal, s, NEG)
            buf[g] = s

    def softmax(kt, buf):
        for g in range(g_heads):
            s = buf[g]
            m_old = m_sc[g]
            m_new = jnp.maximum(m_old, jnp.max(s, axis=0, keepdims=True))
            a = jnp.exp2(m_old - m_new)
            p = jnp.exp2(s - m_new)
            vt = jnp.concatenate([v_ref[kt * nv + j, g * MLA_V:(g + 1) * MLA_V, :] for j in range(nv)],
                                 axis=1)
            m_sc[g] = m_new
            acc_sc[g] = a * acc_sc[g] + _dot(_with_ones_rows(vt), p.astype(BF16))

    m_sc[...] = jnp.full(m_sc.shape, NEG, F32)
    acc_sc[...] = jnp.zeros_like(acc_sc)

    def step(k, parity, diagonal=False):
        scores(k + 1, bufs[1 - parity], (kidx <= qidx) if diagonal else None)
        softmax(k, bufs[parity])

    scores(0, bufs[0], kidx <= i * tq + qidx)

    def pair(p, _):
        step(2 * p, 0)
        step(2 * p + 1, 1)
        return 0

    lax.fori_loop(0, (i - 1) // 2, pair, 0)

    @pl.when(i % 2 == 0)
    def _():
        @pl.when(i >= 2)
        def _():
            step(i - 2, 0)
            step(i - 1, 1, diagonal=True)
        softmax(i, bufs[0])

    @pl.when(i % 2 == 1)
    def _():
        step(i - 1, 0, diagonal=True)
        softmax(i, bufs[1])

    for g in range(g_heads):
        acc = acc_sc[g]
        o_ref[g * MLA_V:(g + 1) * MLA_V, :] = (acc[:MLA_V] / acc[MLA_V:MLA_V + 1]).astype(BF16)


def _mla(qat, ka, vat):
    B, _, S = qat.shape
    tq = TQ_MLA
    g = G_MLA
    return pl.pallas_call(
        _mla_kernel,
        out_shape=jax.ShapeDtypeStruct((B, MLA_HEADS * MLA_V, S), BF16),
        grid=(B, MLA_HEADS // g, S // tq),
        in_specs=[
            pl.BlockSpec((None, g * LANE, tq), lambda b, h, i: (b, h, i)),
            pl.BlockSpec((None, S, g * LANE), lambda b, h, i: (b, 0, h)),
            pl.BlockSpec((None, S // LANE, g * MLA_V, LANE), lambda b, h, i: (b, 0, h, 0)),
        ],
        out_specs=pl.BlockSpec((None, g * MLA_V, tq), lambda b, h, i: (b, h, i)),
        scratch_shapes=[pltpu.VMEM((g, tq, tq), F32), pltpu.VMEM((g, tq, tq), F32),
                        pltpu.VMEM((g, 1, tq), F32),
                        pltpu.VMEM((g, MLA_V + BF16_ROWS, tq), F32)],
        compiler_params=pltpu.CompilerParams(
            dimension_semantics=("parallel", "parallel", "arbitrary"), vmem_limit_bytes=VMEM_LIMIT),
        name="mla",
    )(qat, ka, vat)


def _count_rows(mask):
    r, t = mask.shape
    return jnp.sum(mask.astype(jnp.int32).reshape(r // 8, 8, t), axis=0)


def _dsa_kernel(topk, qit_ref, wit_ref, ki_ref, qbt_ref, kb_ref, vbt_ref, o_ref,
                sc_sc, att0_sc, att1_sc, thr_sc, jsel_sc, m_sc, acc_sc):
    tq = o_ref.shape[1]
    ck = CK_DSA
    sub = tq // ck
    i = pl.program_id(1)
    nblk = i + 1

    def rows(ref, r0, n):
        if isinstance(r0, int):
            return ref[r0:r0 + n, :]
        return ref[pl.ds(pl.multiple_of(r0, n), n), :]

    qpos_c = i * tq + lax.broadcasted_iota(jnp.int32, (ck, tq), 1)
    krow_c = lax.broadcasted_iota(jnp.int32, (ck, tq), 0)

    def score_block(cb):
        for j in range(sub):
            r0 = cb * tq + j * ck
            kc = rows(ki_ref, r0, ck)
            acc = jnp.zeros((ck, tq), F32)
            for h in range(IDX_HEADS):
                rel = _dot(kc, qit_ref[h * LANE:(h + 1) * LANE, :])
                acc = acc + wit_ref[h:h + 1, :] * jnp.maximum(rel, 0.0)
            sc_sc[pl.ds(pl.multiple_of(r0, ck), ck), :] = jnp.where(r0 + krow_c <= qpos_c, acc, -jnp.inf)

    def score_pair(p, _):
        score_block(2 * p)
        score_block(2 * p + 1)
        return 0

    lax.fori_loop(0, nblk // 2, score_pair, 0)

    @pl.when(nblk % 2 == 1)
    def _():
        score_block(nblk - 1)

    krow = lax.broadcasted_iota(jnp.int32, (tq, tq), 0)
    qpos = i * tq + lax.broadcasted_iota(jnp.int32, (tq, tq), 1)

    def count(pred, nb=None):
        def body(cb, a):
            return a + _count_rows(pred(rows(sc_sc, cb * tq, tq), cb * tq + krow))
        part = jnp.zeros((8, tq), jnp.int32)
        if nb is None:
            part = lax.fori_loop(0, nblk, body, part)
        else:
            for cb in range(nb):
                part = body(cb, part)
        return jnp.sum(part, axis=0, keepdims=True)

    def flip(word):
        return jnp.where(word < 0, word ^ jnp.int32(0x7FFFFFFF), word)

    def key_to_score(key):
        return pltpu.bitcast(flip(key), F32)

    thr_sc[...] = jnp.full((1, tq), -jnp.inf, F32)
    jsel_sc[...] = jnp.full((1, tq), sc_sc.shape[0], jnp.int32)

    def bisect(nb):
        assert tq == 2 * LANE
        la, lb = slice(0, LANE), slice(LANE, 2 * LANE)

        def part_count(lanes, cf):
            part = jnp.zeros((8, LANE), jnp.int32)
            for cb in range(nb):
                part = part + _count_rows(sc_sc[cb * tq:(cb + 1) * tq, lanes] >= cf[0:1, :])
            return part

        def accept(part, cand, ans):
            return jnp.where(jnp.sum(part, axis=0, keepdims=True) >= topk, cand, ans)

        def cand_of(t, ans):
            return jnp.where(t == 0, jnp.zeros_like(ans), ans + jnp.left_shift(jnp.int32(1), 31 - t))

        def body(t, st):
            ans_a, ans_b, cand_b, part_b = st
            ans_b = accept(part_b, cand_b, ans_b)
            cand_b = cand_of(t, ans_b)
            cand_a = cand_of(t, ans_a)
            ans_a = accept(part_count(la, key_to_score(cand_a)), cand_a, ans_a)
            return ans_a, ans_b, cand_b, part_count(lb, key_to_score(cand_b))

        start = jnp.full((8, LANE), INT_MIN, jnp.int32)
        first = body(jnp.int32(0), (start, start, start, jnp.zeros((8, LANE), jnp.int32)))
        ans_a, ans_b, cand_b, part_b = lax.fori_loop(1, 32, body, first)
        thr_sc[:, la] = key_to_score(ans_a)[0:1, :]
        thr_sc[:, lb] = key_to_score(accept(part_b, cand_b, ans_b))[0:1, :]

    for v in range(topk // tq, sc_sc.shape[0] // tq):
        pl.when(i == v)(functools.partial(bisect, v + 1))

    @pl.when(i * tq >= topk)
    def _():
        thr = thr_sc[...]
        n_ge = count(lambda s, kidx: s >= thr)
        excess = jnp.max(n_ge) > topk

        @pl.when(excess)
        def _():
            need = topk - count(lambda s, kidx: s > thr)

            def jbit(t, j):
                cand = j + jnp.left_shift(jnp.int32(1), 10 - t)
                n = count(lambda s, kidx: (s == thr) & (kidx < cand))
                return jnp.where(n < need, cand, j)

            jsel_sc[...] = lax.fori_loop(0, 11, jbit, jnp.zeros((1, tq), jnp.int32))

    thr = thr_sc[...]
    jsel = jsel_sc[...]
    nh = DSA_HEADS
    qall = jnp.concatenate([qbt_ref[h * LANE:(h + 1) * LANE, :] for h in range(nh)], axis=1)
    bufs = (att0_sc, att1_sc)

    def logits(cb, buf):
        sc = rows(sc_sc, cb * tq, tq)
        kidx = cb * tq + krow
        sel = ((sc > thr) | ((sc == thr) & (kidx <= jsel))) & (kidx <= qpos)
        bias = jnp.where(sel, 0.0, NEG)
        for j in range(sub):
            b = bias[j * ck:(j + 1) * ck, :]
            buf[j * ck:(j + 1) * ck, :] = (_dot(rows(kb_ref, cb * tq + j * ck, ck), qall)
                                           + jnp.concatenate([b] * nh, axis=1))

    def attend(cb, buf):
        vt = _with_ones_rows(jnp.concatenate([vbt_ref[cb * sub + j] for j in range(sub)], axis=1))
        for h in range(nh):
            cols = slice(h * tq, (h + 1) * tq)
            s = buf[:, cols]
            m_old = m_sc[:, cols]
            m_new = jnp.maximum(m_old, jnp.max(s, axis=0, keepdims=True))
            a = jnp.exp2(m_old - m_new)
            p = jnp.exp2(s - m_new)
            m_sc[:, cols] = m_new
            acc_sc[:, cols] = a * acc_sc[:, cols] + _dot(vt, p.astype(BF16))

    def step(k, parity):
        logits(k + 1, bufs[1 - parity])
        attend(k, bufs[parity])

    m_sc[...] = jnp.full(m_sc.shape, NEG, F32)
    acc_sc[...] = jnp.zeros_like(acc_sc)
    logits(0, bufs[0])

    def pair(p, _):
        step(2 * p, 0)
        step(2 * p + 1, 1)
        return 0

    lax.fori_loop(0, i // 2, pair, 0)

    @pl.when(i % 2 == 1)
    def _():
        step(i - 1, 0)
        attend(i, bufs[1])

    @pl.when(i % 2 == 0)
    def _():
        attend(i, bufs[0])

    out = acc_sc[0:DSA_DIM, :] / acc_sc[DSA_DIM:DSA_DIM + 1, :]
    for h in range(nh):
        o_ref[h * DSA_DIM:(h + 1) * DSA_DIM, :] = out[:, h * tq:(h + 1) * tq].astype(BF16)


def _dsa(qit, wit, ki, qbt, kb, vbt, topk):
    B, _, S = qit.shape
    tq = TQ_DSA
    assert topk % tq == 0 or topk >= S, "query tiles must not straddle the top-k boundary"
    return pl.pallas_call(
        functools.partial(_dsa_kernel, topk),
        out_shape=jax.ShapeDtypeStruct((B, DSA_HEADS * DSA_DIM, S), BF16),
        grid=(B, S // tq),
        in_specs=[
            pl.BlockSpec((None, IDX_HEADS * LANE, tq), lambda b, i: (b, 0, i)),
            pl.BlockSpec((None, IDX_HEADS, tq), lambda b, i: (b, 0, i)),
            pl.BlockSpec((None, S, LANE), lambda b, i: (b, 0, 0)),
            pl.BlockSpec((None, DSA_HEADS * LANE, tq), lambda b, i: (b, 0, i)),
            pl.BlockSpec((None, S, LANE), lambda b, i: (b, 0, 0)),
            pl.BlockSpec((None, S // LANE, DSA_DIM, LANE), lambda b, i: (b, 0, 0, 0)),
        ],
        out_specs=pl.BlockSpec((None, DSA_HEADS * DSA_DIM, tq), lambda b, i: (b, 0, i)),
        scratch_shapes=[
            pltpu.VMEM((S, tq), F32),
            pltpu.VMEM((tq, DSA_HEADS * tq), F32),
            pltpu.VMEM((tq, DSA_HEADS * tq), F32),
            pltpu.VMEM((1, tq), F32),
            pltpu.VMEM((1, tq), jnp.int32),
            pltpu.VMEM((1, DSA_HEADS * tq), F32),
            pltpu.VMEM((DSA_DIM + BF16_ROWS, DSA_HEADS * tq), F32),
        ],
        compiler_params=pltpu.CompilerParams(
            dimension_semantics=("parallel", "arbitrary"), vmem_limit_bytes=VMEM_LIMIT),
        name="dsa",
    )(qit, wit, ki, qbt, kb, vbt)


def _ret_kernel(rqt_ref, rk_ref, rvt_ref, rgt_ref, dt_ref, xi_ref, zeta_ref, cdec_ref, gret_ref, o_ref):
    c = dt_ref.shape[1]
    n_chunks = o_ref.shape[1] // c
    state = [jnp.zeros((RET_DV, LANE), F32) for _ in range(RET_HEADS)]
    for n in range(n_chunks):
        cols = slice(n * c, (n + 1) * c)
        for h in range(RET_HEADS):
            rt = state[h]
            qt = rqt_ref[h * LANE:(h + 1) * LANE, cols]
            k = rk_ref[cols, h * LANE:(h + 1) * LANE]
            vt = rvt_ref[h * RET_DV:(h + 1) * RET_DV, cols]
            inner_t = _dot(k, qt) * dt_ref[h]
            qx = (qt.astype(F32) * xi_ref[h:h + 1, :]).astype(BF16)
            out_t = _dot(vt, inner_t.astype(BF16)) + _dot(rt.astype(BF16), qx)
            vz = (vt.astype(F32) * zeta_ref[h:h + 1, :]).astype(BF16)
            state[h] = rt * cdec_ref[h] + _dot(vz, k)

            mu = jnp.mean(out_t, axis=0, keepdims=True)
            xc = out_t - mu
            var = jnp.mean(xc * xc, axis=0, keepdims=True)
            y = xc * lax.rsqrt(var + EPS) * gret_ref[h * RET_DV:(h + 1) * RET_DV, :]
            g = rgt_ref[h * RET_DV:(h + 1) * RET_DV, cols]
            o_ref[h * RET_DV:(h + 1) * RET_DV, cols] = (g * jax.nn.sigmoid(g) * y).astype(BF16)


def _retention(rqt, rk, rvt, rgt, dt, xi, zeta, cdec, gret):
    B, _, S = rqt.shape

    def full1(shape):
        nd = len(shape)
        return pl.BlockSpec(shape, lambda b: (0,) * nd)

    def per_batch(rows, cols):
        return pl.BlockSpec((None, rows, cols), lambda b: (b, 0, 0))

    return pl.pallas_call(
        _ret_kernel,
        out_shape=jax.ShapeDtypeStruct((B, RET_HEADS * RET_DV, S), BF16),
        grid=(B,),
        in_specs=[
            per_batch(RET_HEADS * LANE, S), per_batch(S, RET_HEADS * LANE),
            per_batch(RET_HEADS * RET_DV, S), per_batch(RET_HEADS * RET_DV, S),
            full1(dt.shape), full1(xi.shape), full1(zeta.shape), full1(cdec.shape), full1(gret.shape),
        ],
        out_specs=per_batch(RET_HEADS * RET_DV, S),
        compiler_params=pltpu.CompilerParams(
            dimension_semantics=("parallel",), vmem_limit_bytes=VMEM_LIMIT),
        name="retention",
    )(rqt, rk, rvt, rgt, dt, xi, zeta, cdec, gret)


def _mlp_kernel(final, x_ref, oat_ref, obt_ref, oct_ref, wout_ref, gmlp_ref, w1_ref, w2_ref, gfin_ref,
                y_ref):
    def tr(ref):
        return ref[...].astype(F32).T.astype(BF16)

    na = MLA_HEADS * MLA_V
    nb = DSA_HEADS * DSA_DIM
    attn = (_dot(tr(oat_ref), wout_ref[0:na, :])
            + _dot(tr(obt_ref), wout_ref[na:na + nb, :])
            + _dot(tr(oct_ref), wout_ref[na + nb:, :]))
    x1 = x_ref[...] + attn
    hf = _rms(x1, gmlp_ref[...]).astype(BF16)
    acc = x1
    for f in range(D_FF // FF_CHUNK):
        u = jnp.maximum(_dot(hf, w1_ref[:, f * FF_CHUNK:(f + 1) * FF_CHUNK]), 0.0)
        acc = acc + _dot((u * u).astype(BF16), w2_ref[f * FF_CHUNK:(f + 1) * FF_CHUNK, :])
    if final:
        acc = _rms(acc, gfin_ref[...])
    y_ref[...] = acc


def _mlp(x, oat, obt, oct, wout, gmlp, w1, w2, gfin, final):
    B, S, D = x.shape
    tm = TM_MLP

    def const(shape):
        nd = len(shape)
        return pl.BlockSpec(shape, lambda b, i: (0,) * nd, pipeline_mode=pl.Buffered(1))

    return pl.pallas_call(
        functools.partial(_mlp_kernel, final),
        out_shape=jax.ShapeDtypeStruct((B, S, D), F32),
        grid=(B, S // tm),
        in_specs=[
            pl.BlockSpec((None, tm, D), lambda b, i: (b, i, 0)),
            pl.BlockSpec((None, oat.shape[1], tm), lambda b, i: (b, 0, i)),
            pl.BlockSpec((None, obt.shape[1], tm), lambda b, i: (b, 0, i)),
            pl.BlockSpec((None, oct.shape[1], tm), lambda b, i: (b, 0, i)),
            const(wout.shape), const(gmlp.shape), const(w1.shape), const(w2.shape), const(gfin.shape),
        ],
        out_specs=pl.BlockSpec((None, tm, D), lambda b, i: (b, i, 0)),
        compiler_params=pltpu.CompilerParams(
            dimension_semantics=("parallel", "parallel"), vmem_limit_bytes=VMEM_LIMIT),
        name="mlp",
    )(x, oat, obt, oct, wout, gmlp, w1, w2, gfin)


def _rope_tables(seq, dim):
    pos = jnp.arange(seq, dtype=F32)
    inv = ROPE_THETA ** (-jnp.arange(0, dim, 2, dtype=F32) / dim)
    ang = pos[:, None] * inv[None, :]
    return jnp.cos(ang), jnp.sin(ang)


def _lane_table(c, s, lo, seq):
    half = c.shape[1]
    ct = jnp.zeros((seq, LANE), F32).at[:, lo:lo + 2 * half].set(jnp.concatenate([c, c], axis=1))
    st = jnp.zeros((seq, LANE), F32).at[:, lo:lo + 2 * half].set(jnp.concatenate([-s, s], axis=1))
    return ct, st


def _pad_cols(w, width):
    return jnp.pad(w, ((0, 0), (0, width - w.shape[1])))


def _prep_layer(w_in, w_uq, w_ukv):
    z = lambda n: jnp.zeros((D_MODEL, n), F32)
    col = lambda a, b: w_in[:, a:b]
    nat = [col(O_CQ, O_CKV), col(O_CKV, O_KR),
           jnp.concatenate([z(64), col(O_KR, O_AQ), z(32)], axis=1),
           _pad_cols(col(O_AK, O_AV), LANE),
           _pad_cols(col(O_KI, O_WI), LANE)]
    for h in range(RET_HEADS):
        nat.append(_pad_cols(col(O_RK + h * RET_DK, O_RK + (h + 1) * RET_DK), LANE))
    wnat = jnp.concatenate(nat, axis=1).astype(BF16)
    wtr = jnp.concatenate([col(O_AQ, O_AK), col(O_QI, O_KI), col(O_AV, O_QI), col(O_RQ, O_RK),
                           col(O_RV, O_RG), col(O_RG, O_END), col(O_WI, O_RQ)], axis=1).T.astype(BF16)
    dq = MLA_NOPE + MLA_ROPE
    wuq = jnp.pad(w_uq.reshape(MLA_Q_RANK, MLA_HEADS, dq), ((0, 0), (0, 0), (0, LANE - dq)))
    wuq = wuq.reshape(MLA_Q_RANK, MLA_HEADS * LANE).T.astype(BF16)
    wkv = w_ukv.reshape(MLA_KV_RANK, MLA_HEADS, MLA_NOPE + MLA_V)
    wuk = jnp.pad(wkv[:, :, :MLA_NOPE], ((0, 0), (0, 0), (0, LANE - MLA_NOPE)))
    wuk = wuk.reshape(MLA_KV_RANK, MLA_HEADS * LANE).astype(BF16)
    wuv = wkv[:, :, MLA_NOPE:].reshape(MLA_KV_RANK, MLA_HEADS * MLA_V).T.astype(BF16)
    return wnat, wtr, wuq, wuk, wuv


def kernel(x, g_mix, w_in, g_q, w_uq, g_kv, w_ukv, g_ret, w_out, g_mlp, w_ff1, w_ff2, g_final):
    B, S, _ = x.shape
    topk = min(DSA_TOPK_MAX, S // 4)

    cos32, sin32 = _rope_tables(S, 32)
    cos64, sin64 = _rope_tables(S, 64)
    tabt = jnp.concatenate([cos64.T, sin64.T, cos32.T, sin32.T], axis=0)
    tabn = jnp.stack(_lane_table(cos32, sin32, 64, S) + _lane_table(cos64, sin64, 0, S)
                     + _lane_table(cos32, sin32, 0, S))

    log_gamma = jnp.log1p(-jnp.exp2(-5.0 - jnp.arange(RET_HEADS, dtype=F32)))
    c = C_RET
    pos = jnp.arange(c, dtype=F32)
    rel = pos[None, :] - pos[:, None]
    dt = jnp.where(rel[None] >= 0, jnp.exp(jnp.maximum(rel, 0.0)[None] * log_gamma[:, None, None]), 0.0)
    xi = jnp.exp((pos + 1.0)[None, :] * log_gamma[:, None])
    zeta = jnp.exp((c - 1.0 - pos)[None, :] * log_gamma[:, None])
    cdec = jnp.broadcast_to(jnp.exp(c * log_gamma)[:, None, None], (RET_HEADS, RET_DV, LANE))

    for l in range(DEPTH):
        wnat, wtr, wuq, wuk, wuv = _prep_layer(w_in[l], w_uq[l], w_ukv[l])
        (qat, ka, vat, qit, wit, ki, qbt, kb, vbt, rqt, rk, rvt, rgt) = _proj(
            x, g_mix[l][None, :], g_q[l][None, :], g_kv[l][None, :], wnat, wtr, wuq, wuk, wuv, tabt, tabn)
        oat = _mla(qat, ka, vat)
        obt = _dsa(qit, wit, ki, qbt, kb, vbt, topk)
        gret = jnp.broadcast_to(g_ret[l][:, None], (RET_HEADS * RET_DV, c))
        oct = _retention(rqt, rk, rvt, rgt, dt, xi, zeta, cdec, gret)
        x = _mlp(x, oat, obt, oct, w_out[l].astype(BF16), g_mlp[l][None, :],
                 w_ff1[l].astype(BF16), w_ff2[l].astype(BF16), g_final[None, :], l == DEPTH - 1)
    return x
```

```python
import functools

import numpy as np
import jax
import jax.numpy as jnp
from jax import lax
from jax.experimental import pallas as pl
from jax.experimental.pallas import tpu as pltpu

D_MODEL = 1024
DEPTH = 2
MLA_HEADS = 8
MLA_NOPE = 64
MLA_ROPE = 32
MLA_V = 64
MLA_Q_RANK = 768
MLA_KV_RANK = 256
DSA_HEADS = 4
DSA_DIM = 64
IDX_HEADS = 8
IDX_DIM = 32
DSA_TOPK_MAX = 256
RET_HEADS = 4
RET_DK = 64
RET_DV = 64
D_FF = 4 * D_MODEL
ROPE_THETA = 10000.0
EPS = 1e-6

MLA_COLS = (MLA_Q_RANK, MLA_KV_RANK, MLA_ROPE)
DSA_COLS = (DSA_HEADS * DSA_DIM, DSA_DIM, DSA_DIM, IDX_HEADS * IDX_DIM, IDX_DIM, IDX_HEADS)
RET_COLS = (RET_HEADS * RET_DK, RET_HEADS * RET_DK, RET_HEADS * RET_DV, RET_HEADS * RET_DV)
SPLITS = MLA_COLS + DSA_COLS + RET_COLS
_OFF = np.concatenate([[0], np.cumsum(SPLITS)]).tolist()
(O_CQ, O_CKV, O_KR, O_AQ, O_AK, O_AV, O_QI, O_KI, O_WI, O_RQ, O_RK, O_RV, O_RG, O_END) = _OFF

LANE = 128
BF16_ROWS = 16
VMEM_LIMIT = 56 * 1024 * 1024

BF16 = jnp.bfloat16
F32 = jnp.float32
NEG = -1e30
LOG2E = float(np.log2(np.e))
INT_MIN = np.int32(-2**31)

N_CQ = 0
N_CKV = N_CQ + MLA_Q_RANK
N_KR = N_CKV + MLA_KV_RANK
N_AK = N_KR + LANE
N_KI = N_AK + LANE
N_RK = N_KI + LANE
N_NAT = N_RK + RET_HEADS * LANE
T_AQ = 0
T_QI = T_AQ + DSA_HEADS * DSA_DIM
T_AV = T_QI + IDX_HEADS * IDX_DIM
T_RQ = T_AV + DSA_DIM
T_RV = T_RQ + RET_HEADS * RET_DK
T_RG = T_RV + RET_HEADS * RET_DV
T_WI = T_RG + RET_HEADS * RET_DV
N_TR = T_WI + IDX_HEADS

TM_PROJ = 512
TQ_MLA = 256
G_MLA = 8
TQ_DSA = 256
CK_DSA = 128
C_RET = 256
TM_MLP = 512
FF_CHUNK = 1024

_NT = (((1,), (1,)), ((), ()))


def _rms(x, g):
    return x * lax.rsqrt(jnp.mean(x * x, axis=-1, keepdims=True) + EPS) * g


def _dot(a, b):
    return jnp.dot(a, b, preferred_element_type=F32)


def _dot_nt(a, b):
    return lax.dot_general(a, b, _NT, preferred_element_type=F32)


def _rope_rows(x1, x2, c, s):
    return x1 * c - x2 * s, x1 * s + x2 * c


def _rope_lanes(x, c, s_signed, half, first_lo):
    lane = lax.broadcasted_iota(jnp.int32, x.shape, 1)
    first = (lane >= first_lo) & (lane < first_lo + half)
    rot = jnp.where(first, pltpu.roll(x, LANE - half, 1), pltpu.roll(x, half, 1))
    return x * c + rot * s_signed


def _proj_kernel(x_ref, gmix_ref, gq_ref, gkv_ref, wnat_ref, wtr_ref, wuq_ref, wuk_ref, wuv_ref,
                 tabt_ref, tabn_ref,
                 qat_ref, ka_ref, vat_ref, qit_ref, wit_ref, ki_ref, qbt_ref, kb_ref, vbt_ref,
                 rqt_ref, rk_ref, rvt_ref, rgt_ref):
    tm = x_ref.shape[0]
    hb = _rms(x_ref[...], gmix_ref[...]).astype(BF16)

    c64t = tabt_ref[0:32, :]
    s64t = tabt_ref[32:64, :]
    c32t = tabt_ref[64:80, :]
    s32t = tabt_ref[80:96, :]

    cq = _dot(hb, wnat_ref[:, N_CQ:N_CQ + MLA_Q_RANK])
    nq = _rms(cq, gq_ref[...]).astype(BF16)
    qt = _dot_nt(wuq_ref[...], nq)
    scale_a = (MLA_NOPE + MLA_ROPE) ** -0.5 * LOG2E
    for h in range(MLA_HEADS):
        r0 = h * LANE
        qat_ref[r0:r0 + 64, :] = (qt[r0:r0 + 64, :] * scale_a).astype(BF16)
        o1, o2 = _rope_rows(qt[r0 + 64:r0 + 80, :], qt[r0 + 80:r0 + 96, :], c32t, s32t)
        qat_ref[r0 + 64:r0 + 80, :] = (o1 * scale_a).astype(BF16)
        qat_ref[r0 + 80:r0 + 96, :] = (o2 * scale_a).astype(BF16)
        qat_ref[r0 + 96:r0 + 128, :] = jnp.zeros((32, tm), BF16)

    ckv = _dot(hb, wnat_ref[:, N_CKV:N_CKV + MLA_KV_RANK])
    nkv = _rms(ckv, gkv_ref[...]).astype(BF16)
    kn = _dot(nkv, wuk_ref[...])
    keys = _dot(hb, wnat_ref[:, N_KR:N_NAT])
    slab = lambda off: keys[:, off - N_KR:off - N_KR + LANE]
    krp = _rope_lanes(slab(N_KR), tabn_ref[0], tabn_ref[1], 16, 64)
    for h in range(MLA_HEADS):
        ka_ref[:, h * LANE:(h + 1) * LANE] = (kn[:, h * LANE:(h + 1) * LANE] + krp).astype(BF16)
    vt = _dot_nt(wuv_ref[...], nkv).astype(BF16)
    for j in range(tm // LANE):
        vat_ref[j] = vt[:, j * LANE:(j + 1) * LANE]

    kb_ref[...] = _rope_lanes(slab(N_AK), tabn_ref[2], tabn_ref[3], 32, 0).astype(BF16)
    ki_ref[...] = _rope_lanes(slab(N_KI), tabn_ref[4], tabn_ref[5], 16, 0).astype(BF16)

    for h in range(RET_HEADS):
        rkp = _rope_lanes(slab(N_RK + h * LANE), tabn_ref[2], tabn_ref[3], 32, 0) * (RET_DK ** -0.5)
        rk_ref[:, h * LANE:(h + 1) * LANE] = rkp.astype(BF16)

    aqt = _dot_nt(wtr_ref[T_AQ:T_AQ + 256, :], hb)
    scale_b = DSA_DIM ** -0.5 * LOG2E
    for h in range(DSA_HEADS):
        o1, o2 = _rope_rows(aqt[h * 64:h * 64 + 32, :], aqt[h * 64 + 32:h * 64 + 64, :], c64t, s64t)
        qbt_ref[h * LANE:h * LANE + 32, :] = (o1 * scale_b).astype(BF16)
        qbt_ref[h * LANE + 32:h * LANE + 64, :] = (o2 * scale_b).astype(BF16)
        qbt_ref[h * LANE + 64:(h + 1) * LANE, :] = jnp.zeros((64, tm), BF16)

    qit = _dot_nt(wtr_ref[T_QI:T_QI + 256, :], hb)
    for h in range(IDX_HEADS):
        o1, o2 = _rope_rows(qit[h * 32:h * 32 + 16, :], qit[h * 32 + 16:h * 32 + 32, :], c32t, s32t)
        qit_ref[h * LANE:h * LANE + 16, :] = o1.astype(BF16)
        qit_ref[h * LANE + 16:h * LANE + 32, :] = o2.astype(BF16)
        qit_ref[h * LANE + 32:(h + 1) * LANE, :] = jnp.zeros((96, tm), BF16)

    idx_scale = (IDX_HEADS ** -0.5) * (IDX_DIM ** -0.5)
    wit_ref[...] = _dot_nt(wtr_ref[T_WI:T_WI + IDX_HEADS, :], hb) * idx_scale

    avt = _dot_nt(wtr_ref[T_AV:T_AV + DSA_DIM, :], hb).astype(BF16)
    for j in range(tm // LANE):
        vbt_ref[j] = avt[:, j * LANE:(j + 1) * LANE]

    rqt = _dot_nt(wtr_ref[T_RQ:T_RQ + 256, :], hb)
    for h in range(RET_HEADS):
        o1, o2 = _rope_rows(rqt[h * 64:h * 64 + 32, :], rqt[h * 64 + 32:h * 64 + 64, :], c64t, s64t)
        rqt_ref[h * LANE:h * LANE + 32, :] = o1.astype(BF16)
        rqt_ref[h * LANE + 32:h * LANE + 64, :] = o2.astype(BF16)
        rqt_ref[h * LANE + 64:(h + 1) * LANE, :] = jnp.zeros((64, tm), BF16)

    rvt_ref[...] = _dot_nt(wtr_ref[T_RV:T_RV + 256, :], hb).astype(BF16)
    rgt_ref[...] = _dot_nt(wtr_ref[T_RG:T_RG + 256, :], hb)


def _full(shape):
    nd = len(shape)
    return pl.BlockSpec(shape, lambda b, i: (0,) * nd)


def _proj(x, gmix, gq, gkv, wnat, wtr, wuq, wuk, wuv, tabt, tabn):
    B, S, D = x.shape
    tm = TM_PROJ
    nch = tm // LANE
    outs = [
        (jax.ShapeDtypeStruct((B, MLA_HEADS * LANE, S), BF16), pl.BlockSpec((None, MLA_HEADS * LANE, tm), lambda b, i: (b, 0, i))),
        (jax.ShapeDtypeStruct((B, S, MLA_HEADS * LANE), BF16), pl.BlockSpec((None, tm, MLA_HEADS * LANE), lambda b, i: (b, i, 0))),
        (jax.ShapeDtypeStruct((B, S // LANE, MLA_HEADS * MLA_V, LANE), BF16), pl.BlockSpec((None, nch, MLA_HEADS * MLA_V, LANE), lambda b, i: (b, i, 0, 0))),
        (jax.ShapeDtypeStruct((B, IDX_HEADS * LANE, S), BF16), pl.BlockSpec((None, IDX_HEADS * LANE, tm), lambda b, i: (b, 0, i))),
        (jax.ShapeDtypeStruct((B, IDX_HEADS, S), F32), pl.BlockSpec((None, IDX_HEADS, tm), lambda b, i: (b, 0, i))),
        (jax.ShapeDtypeStruct((B, S, LANE), BF16), pl.BlockSpec((None, tm, LANE), lambda b, i: (b, i, 0))),
        (jax.ShapeDtypeStruct((B, DSA_HEADS * LANE, S), BF16), pl.BlockSpec((None, DSA_HEADS * LANE, tm), lambda b, i: (b, 0, i))),
        (jax.ShapeDtypeStruct((B, S, LANE), BF16), pl.BlockSpec((None, tm, LANE), lambda b, i: (b, i, 0))),
        (jax.ShapeDtypeStruct((B, S // LANE, DSA_DIM, LANE), BF16), pl.BlockSpec((None, nch, DSA_DIM, LANE), lambda b, i: (b, i, 0, 0))),
        (jax.ShapeDtypeStruct((B, RET_HEADS * LANE, S), BF16), pl.BlockSpec((None, RET_HEADS * LANE, tm), lambda b, i: (b, 0, i))),
        (jax.ShapeDtypeStruct((B, S, RET_HEADS * LANE), BF16), pl.BlockSpec((None, tm, RET_HEADS * LANE), lambda b, i: (b, i, 0))),
        (jax.ShapeDtypeStruct((B, RET_HEADS * RET_DV, S), BF16), pl.BlockSpec((None, RET_HEADS * RET_DV, tm), lambda b, i: (b, 0, i))),
        (jax.ShapeDtypeStruct((B, RET_HEADS * RET_DV, S), F32), pl.BlockSpec((None, RET_HEADS * RET_DV, tm), lambda b, i: (b, 0, i))),
    ]
    in_specs = [
        pl.BlockSpec((None, tm, D), lambda b, i: (b, i, 0)),
        _full(gmix.shape), _full(gq.shape), _full(gkv.shape),
        _full(wnat.shape), _full(wtr.shape), _full(wuq.shape), _full(wuk.shape), _full(wuv.shape),
        pl.BlockSpec((tabt.shape[0], tm), lambda b, i: (0, i)),
        pl.BlockSpec((tabn.shape[0], tm, LANE), lambda b, i: (0, i, 0)),
    ]
    return pl.pallas_call(
        _proj_kernel,
        out_shape=[o[0] for o in outs],
        grid=(B, S // tm),
        in_specs=in_specs,
        out_specs=[o[1] for o in outs],
        compiler_params=pltpu.CompilerParams(
            dimension_semantics=("parallel", "parallel"), vmem_limit_bytes=VMEM_LIMIT),
        name="proj",
    )(x, gmix, gq, gkv, wnat, wtr, wuq, wuk, wuv, tabt, tabn)


def _with_ones_rows(vt):
    return jnp.concatenate([vt, jnp.ones((BF16_ROWS, vt.shape[1]), BF16)], axis=0)


def _mla_kernel(q_ref, k_ref, v_ref, o_ref, s0_sc, s1_sc, m_sc, acc_sc):
    g_heads = acc_sc.shape[0]
    tq = q_ref.shape[1]
    tk = tq
    i = pl.program_id(2)
    nv = tk // LANE
    kidx = lax.broadcasted_iota(jnp.int32, (tk, tq), 0)
    qidx = lax.broadcasted_iota(jnp.int32, (tk, tq), 1)
    bufs = (s0_sc, s1_sc)

    def scores(kt, buf, causal=None):
        for g in range(g_heads):
            s = _dot(k_ref[pl.ds(pl.multiple_of(kt * tk, tk), tk), g * LANE:(g + 1) * LANE],
                     q_ref[g * LANE:(g + 1) * LANE, :])
            if causal is not None:
                s = jnp.where(causal, s, NEG)
            buf[g] = s

    def softmax(kt, buf):
        for g in range(g_heads):
            s = buf[g]
            m_old = m_sc[g]
            m_new = jnp.maximum(m_old, jnp.max(s, axis=0, keepdims=True))
            a = jnp.exp2(m_old - m_new)
            p = jnp.exp2(s - m_new)
            vt = jnp.concatenate([v_ref[kt * nv + j, g * MLA_V:(g + 1) * MLA_V, :] for j in range(nv)],
                                 axis=1)
            m_sc[g] = m_new
            acc_sc[g] = a * acc_sc[g] + _dot(_with_ones_rows(vt), p.astype(BF16))

    m_sc[...] = jnp.full(m_sc.shape, NEG, F32)
    acc_sc[...] = jnp.zeros_like(acc_sc)

    def step(k, parity, diagonal=False):
        scores(k + 1, bufs[1 - parity], (kidx <= qidx) if diagonal else None)
        softmax(k, bufs[parity])

    scores(0, bufs[0], kidx <= i * tq + qidx)

    def pair(p, _):
        step(2 * p, 0)
        step(2 * p + 1, 1)
        return 0

    lax.fori_loop(0, (i - 1) // 2, pair, 0)

    @pl.when(i % 2 == 0)
    def _():
        @pl.when(i >= 2)
        def _():
            step(i - 2, 0)
            step(i - 1, 1, diagonal=True)
        softmax(i, bufs[0])

    @pl.when(i % 2 == 1)
    def _():
        step(i - 1, 0, diagonal=True)
        softmax(i, bufs[1])

    for g in range(g_heads):
        acc = acc_sc[g]
        o_ref[g * MLA_V:(g + 1) * MLA_V, :] = (acc[:MLA_V] / acc[MLA_V:MLA_V + 1]).astype(BF16)


def _mla(qat, ka, vat):
    B, _, S = qat.shape
    tq = TQ_MLA
    g = G_MLA
    return pl.pallas_call(
        _mla_kernel,
        out_shape=jax.ShapeDtypeStruct((B, MLA_HEADS * MLA_V, S), BF16),
        grid=(B, MLA_HEADS // g, S // tq),
        in_specs=[
            pl.BlockSpec((None, g * LANE, tq), lambda b, h, i: (b, h, i)),
            pl.BlockSpec((None, S, g * LANE), lambda b, h, i: (b, 0, h)),
            pl.BlockSpec((None, S // LANE, g * MLA_V, LANE), lambda b, h, i: (b, 0, h, 0)),
        ],
        out_specs=pl.BlockSpec((None, g * MLA_V, tq), lambda b, h, i: (b, h, i)),
        scratch_shapes=[pltpu.VMEM((g, tq, tq), F32), pltpu.VMEM((g, tq, tq), F32),
                        pltpu.VMEM((g, 1, tq), F32),
                        pltpu.VMEM((g, MLA_V + BF16_ROWS, tq), F32)],
        compiler_params=pltpu.CompilerParams(
            dimension_semantics=("parallel", "parallel", "arbitrary"), vmem_limit_bytes=VMEM_LIMIT),
        name="mla",
    )(qat, ka, vat)


def _count_rows(mask):
    r, t = mask.shape
    return jnp.sum(mask.astype(jnp.int32).reshape(r // 8, 8, t), axis=0)


def _dsa_kernel(topk, qit_ref, wit_ref, ki_ref, qbt_ref, kb_ref, vbt_ref, o_ref,
                sc_sc, att0_sc, att1_sc, thr_sc, nge_sc, m_sc, acc_sc):
    tq = o_ref.shape[1]
    ck = CK_DSA
    sub = tq // ck
    i = pl.program_id(1)
    nblk = i + 1

    def rows(ref, r0, n):
        if isinstance(r0, int):
            return ref[r0:r0 + n, :]
        return ref[pl.ds(pl.multiple_of(r0, n), n), :]

    qpos_c = i * tq + lax.broadcasted_iota(jnp.int32, (ck, tq), 1)
    krow_c = lax.broadcasted_iota(jnp.int32, (ck, tq), 0)

    def score_block(cb):
        for j in range(sub):
            r0 = cb * tq + j * ck
            kc = rows(ki_ref, r0, ck)
            acc = jnp.zeros((ck, tq), F32)
            for h in range(IDX_HEADS):
                rel = _dot(kc, qit_ref[h * LANE:(h + 1) * LANE, :])
                acc = acc + wit_ref[h:h + 1, :] * jnp.maximum(rel, 0.0)
            sc_sc[pl.ds(pl.multiple_of(r0, ck), ck), :] = jnp.where(r0 + krow_c <= qpos_c, acc, -jnp.inf)

    def score_pair(p, _):
        score_block(2 * p)
        score_block(2 * p + 1)
        return 0

    lax.fori_loop(0, nblk // 2, score_pair, 0)

    @pl.when(nblk % 2 == 1)
    def _():
        score_block(nblk - 1)

    krow = lax.broadcasted_iota(jnp.int32, (tq, tq), 0)

    def count(pred, nb=None):
        def body(cb, a):
            return a + _count_rows(pred(rows(sc_sc, cb * tq, tq), cb * tq + krow))
        part = jnp.zeros((8, tq), jnp.int32)
        if nb is None:
            part = lax.fori_loop(0, nblk, body, part)
        else:
            for cb in range(nb):
                part = body(cb, part)
        return jnp.sum(part, axis=0, keepdims=True)

    def flip(word):
        return jnp.where(word < 0, word ^ jnp.int32(0x7FFFFFFF), word)

    def key_to_score(key):
        return pltpu.bitcast(flip(key), F32)

    thr_sc[...] = jnp.full((1, tq), jnp.finfo(F32).min, F32)

    def bisect(nb):
        assert tq == 2 * LANE
        la, lb = slice(0, LANE), slice(LANE, 2 * LANE)

        def part_count(lanes, cf):
            part = jnp.zeros((8, LANE), jnp.int32)
            for cb in range(nb):
                part = part + _count_rows(sc_sc[cb * tq:(cb + 1) * tq, lanes] >= cf[0:1, :])
            return part

        def accept(part, cand, ans, n_ans):
            n = jnp.sum(part, axis=0, keepdims=True)
            ok = n >= topk
            return jnp.where(ok, cand, ans), jnp.where(ok, n, n_ans)

        def cand_of(t, ans):
            return jnp.where(t == 0, jnp.zeros_like(ans), ans + jnp.left_shift(jnp.int32(1), 31 - t))

        def body(t, st):
            ans_a, n_a, ans_b, n_b, cand_b, part_b = st
            ans_b, n_b = accept(part_b, cand_b, ans_b, n_b)
            cand_b = cand_of(t, ans_b)
            cand_a = cand_of(t, ans_a)
            ans_a, n_a = accept(part_count(la, key_to_score(cand_a)), cand_a, ans_a, n_a)
            return ans_a, n_a, ans_b, n_b, cand_b, part_count(lb, key_to_score(cand_b))

        start = jnp.full((8, LANE), INT_MIN, jnp.int32)
        zero = jnp.zeros((8, LANE), jnp.int32)
        first = body(jnp.int32(0), (start, zero, start, zero, start, zero))
        ans_a, n_a, ans_b, n_b, cand_b, part_b = lax.fori_loop(1, 32, body, first)
        ans_b, n_b = accept(part_b, cand_b, ans_b, n_b)
        thr_sc[:, la] = key_to_score(ans_a)[0:1, :]
        thr_sc[:, lb] = key_to_score(ans_b)[0:1, :]
        nge_sc[:, la] = n_a[0:1, :]
        nge_sc[:, lb] = n_b[0:1, :]

    for v in range(topk // tq, sc_sc.shape[0] // tq):
        pl.when(i == v)(functools.partial(bisect, v + 1))

    @pl.when(i * tq >= topk)
    def _():
        thr = thr_sc[...]

        @pl.when(jnp.max(nge_sc[...]) > topk)
        def _():
            need = topk - count(lambda s, kidx: s > thr)

            def jbit(t, j):
                cand = j + jnp.left_shift(jnp.int32(1), 10 - t)
                n = count(lambda s, kidx: (s == thr) & (kidx < cand))
                return jnp.where(n < need, cand, j)

            last = lax.fori_loop(0, 11, jbit, jnp.zeros((1, tq), jnp.int32))

            def drop(cb, _):
                s = rows(sc_sc, cb * tq, tq)
                dropped = (s == thr) & (cb * tq + krow > last)
                sc_sc[pl.ds(pl.multiple_of(cb * tq, tq), tq), :] = jnp.where(dropped, -jnp.inf, s)
                return 0

            lax.fori_loop(0, nblk, drop, 0)

    thr = thr_sc[...]
    nh = DSA_HEADS
    qall = jnp.concatenate([qbt_ref[h * LANE:(h + 1) * LANE, :] for h in range(nh)], axis=1)
    bufs = (att0_sc, att1_sc)

    def logits(cb, buf):
        bias = jnp.where(rows(sc_sc, cb * tq, tq) >= thr, 0.0, NEG)
        for j in range(sub):
            b = bias[j * ck:(j + 1) * ck, :]
            buf[j * ck:(j + 1) * ck, :] = (_dot(rows(kb_ref, cb * tq + j * ck, ck), qall)
                                           + jnp.concatenate([b] * nh, axis=1))

    def attend(cb, buf):
        vt = _with_ones_rows(jnp.concatenate([vbt_ref[cb * sub + j] for j in range(sub)], axis=1))
        for h in range(nh):
            cols = slice(h * tq, (h + 1) * tq)
            s = buf[:, cols]
            m_old = m_sc[:, cols]
            m_new = jnp.maximum(m_old, jnp.max(s, axis=0, keepdims=True))
            a = jnp.exp2(m_old - m_new)
            p = jnp.exp2(s - m_new)
            m_sc[:, cols] = m_new
            acc_sc[:, cols] = a * acc_sc[:, cols] + _dot(vt, p.astype(BF16))

    def step(k, parity):
        logits(k + 1, bufs[1 - parity])
        attend(k, bufs[parity])

    m_sc[...] = jnp.full(m_sc.shape, NEG, F32)
    acc_sc[...] = jnp.zeros_like(acc_sc)
    logits(0, bufs[0])

    def pair(p, _):
        step(2 * p, 0)
        step(2 * p + 1, 1)
        return 0

    lax.fori_loop(0, i // 2, pair, 0)

    @pl.when(i % 2 == 1)
    def _():
        step(i - 1, 0)
        attend(i, bufs[1])

    @pl.when(i % 2 == 0)
    def _():
        attend(i, bufs[0])

    out = acc_sc[0:DSA_DIM, :] / acc_sc[DSA_DIM:DSA_DIM + 1, :]
    for h in range(nh):
        o_ref[h * DSA_DIM:(h + 1) * DSA_DIM, :] = out[:, h * tq:(h + 1) * tq].astype(BF16)


def _dsa(qit, wit, ki, qbt, kb, vbt, topk):
    B, _, S = qit.shape
    tq = TQ_DSA
    assert topk % tq == 0 or topk >= S, "query tiles must not straddle the top-k boundary"
    return pl.pallas_call(
        functools.partial(_dsa_kernel, topk),
        out_shape=jax.ShapeDtypeStruct((B, DSA_HEADS * DSA_DIM, S), BF16),
        grid=(B, S // tq),
        in_specs=[
            pl.BlockSpec((None, IDX_HEADS * LANE, tq), lambda b, i: (b, 0, i)),
            pl.BlockSpec((None, IDX_HEADS, tq), lambda b, i: (b, 0, i)),
            pl.BlockSpec((None, S, LANE), lambda b, i: (b, 0, 0)),
            pl.BlockSpec((None, DSA_HEADS * LANE, tq), lambda b, i: (b, 0, i)),
            pl.BlockSpec((None, S, LANE), lambda b, i: (b, 0, 0)),
            pl.BlockSpec((None, S // LANE, DSA_DIM, LANE), lambda b, i: (b, 0, 0, 0)),
        ],
        out_specs=pl.BlockSpec((None, DSA_HEADS * DSA_DIM, tq), lambda b, i: (b, 0, i)),
        scratch_shapes=[
            pltpu.VMEM((S, tq), F32),
            pltpu.VMEM((tq, DSA_HEADS * tq), F32),
            pltpu.VMEM((tq, DSA_HEADS * tq), F32),
            pltpu.VMEM((1, tq), F32),
            pltpu.VMEM((1, tq), jnp.int32),
            pltpu.VMEM((1, DSA_HEADS * tq), F32),
            pltpu.VMEM((DSA_DIM + BF16_ROWS, DSA_HEADS * tq), F32),
        ],
        compiler_params=pltpu.CompilerParams(
            dimension_semantics=("parallel", "arbitrary"), vmem_limit_bytes=VMEM_LIMIT),
        name="dsa",
    )(qit, wit, ki, qbt, kb, vbt)


def _ret_kernel(rqt_ref, rk_ref, rvt_ref, rgt_ref, dt_ref, xi_ref, zeta_ref, cdec_ref, gret_ref, o_ref):
    c = dt_ref.shape[1]
    n_chunks = o_ref.shape[1] // c
    state = [jnp.zeros((RET_DV, LANE), F32) for _ in range(RET_HEADS)]
    for n in range(n_chunks):
        cols = slice(n * c, (n + 1) * c)
        for h in range(RET_HEADS):
            rt = state[h]
            qt = rqt_ref[h * LANE:(h + 1) * LANE, cols]
            k = rk_ref[cols, h * LANE:(h + 1) * LANE]
            vt = rvt_ref[h * RET_DV:(h + 1) * RET_DV, cols]
            inner_t = _dot(k, qt) * dt_ref[h]
            qx = (qt.astype(F32) * xi_ref[h:h + 1, :]).astype(BF16)
            out_t = _dot(vt, inner_t.astype(BF16)) + _dot(rt.astype(BF16), qx)
            vz = (vt.astype(F32) * zeta_ref[h:h + 1, :]).astype(BF16)
            state[h] = rt * cdec_ref[h] + _dot(vz, k)

            mu = jnp.mean(out_t, axis=0, keepdims=True)
            xc = out_t - mu
            var = jnp.mean(xc * xc, axis=0, keepdims=True)
            y = xc * lax.rsqrt(var + EPS) * gret_ref[h * RET_DV:(h + 1) * RET_DV, :]
            g = rgt_ref[h * RET_DV:(h + 1) * RET_DV, cols]
            o_ref[h * RET_DV:(h + 1) * RET_DV, cols] = (g * jax.nn.sigmoid(g) * y).astype(BF16)


def _retention(rqt, rk, rvt, rgt, dt, xi, zeta, cdec, gret):
    B, _, S = rqt.shape

    def full1(shape):
        nd = len(shape)
        return pl.BlockSpec(shape, lambda b: (0,) * nd)

    def per_batch(rows, cols):
        return pl.BlockSpec((None, rows, cols), lambda b: (b, 0, 0))

    return pl.pallas_call(
        _ret_kernel,
        out_shape=jax.ShapeDtypeStruct((B, RET_HEADS * RET_DV, S), BF16),
        grid=(B,),
        in_specs=[
            per_batch(RET_HEADS * LANE, S), per_batch(S, RET_HEADS * LANE),
            per_batch(RET_HEADS * RET_DV, S), per_batch(RET_HEADS * RET_DV, S),
            full1(dt.shape), full1(xi.shape), full1(zeta.shape), full1(cdec.shape), full1(gret.shape),
        ],
        out_specs=per_batch(RET_HEADS * RET_DV, S),
        compiler_params=pltpu.CompilerParams(
            dimension_semantics=("parallel",), vmem_limit_bytes=VMEM_LIMIT),
        name="retention",
    )(rqt, rk, rvt, rgt, dt, xi, zeta, cdec, gret)


def _mlp_kernel(final, x_ref, oat_ref, obt_ref, oct_ref, wout_ref, gmlp_ref, w1_ref, w2_ref, gfin_ref,
                y_ref):
    def tr(ref):
        return ref[...].astype(F32).T.astype(BF16)

    na = MLA_HEADS * MLA_V
    nb = DSA_HEADS * DSA_DIM
    attn = (_dot(tr(oat_ref), wout_ref[0:na, :])
            + _dot(tr(obt_ref), wout_ref[na:na + nb, :])
            + _dot(tr(oct_ref), wout_ref[na + nb:, :]))
    x1 = x_ref[...] + attn
    hf = _rms(x1, gmlp_ref[...]).astype(BF16)
    acc = x1
    for f in range(D_FF // FF_CHUNK):
        u = jnp.maximum(_dot(hf, w1_ref[:, f * FF_CHUNK:(f + 1) * FF_CHUNK]), 0.0)
        acc = acc + _dot((u * u).astype(BF16), w2_ref[f * FF_CHUNK:(f + 1) * FF_CHUNK, :])
    if final:
        acc = _rms(acc, gfin_ref[...])
    y_ref[...] = acc


def _mlp(x, oat, obt, oct, wout, gmlp, w1, w2, gfin, final):
    B, S, D = x.shape
    tm = TM_MLP

    def const(shape):
        nd = len(shape)
        return pl.BlockSpec(shape, lambda b, i: (0,) * nd, pipeline_mode=pl.Buffered(1))

    return pl.pallas_call(
        functools.partial(_mlp_kernel, final),
        out_shape=jax.ShapeDtypeStruct((B, S, D), F32),
        grid=(B, S // tm),
        in_specs=[
            pl.BlockSpec((None, tm, D), lambda b, i: (b, i, 0)),
            pl.BlockSpec((None, oat.shape[1], tm), lambda b, i: (b, 0, i)),
            pl.BlockSpec((None, obt.shape[1], tm), lambda b, i: (b, 0, i)),
            pl.BlockSpec((None, oct.shape[1], tm), lambda b, i: (b, 0, i)),
            const(wout.shape), const(gmlp.shape), const(w1.shape), const(w2.shape), const(gfin.shape),
        ],
        out_specs=pl.BlockSpec((None, tm, D), lambda b, i: (b, i, 0)),
        compiler_params=pltpu.CompilerParams(
            dimension_semantics=("parallel", "parallel"), vmem_limit_bytes=VMEM_LIMIT),
        name="mlp",
    )(x, oat, obt, oct, wout, gmlp, w1, w2, gfin)


def _rope_tables(seq, dim):
    pos = jnp.arange(seq, dtype=F32)
    inv = ROPE_THETA ** (-jnp.arange(0, dim, 2, dtype=F32) / dim)
    ang = pos[:, None] * inv[None, :]
    return jnp.cos(ang), jnp.sin(ang)


def _lane_table(c, s, lo, seq):
    half = c.shape[1]
    ct = jnp.zeros((seq, LANE), F32).at[:, lo:lo + 2 * half].set(jnp.concatenate([c, c], axis=1))
    st = jnp.zeros((seq, LANE), F32).at[:, lo:lo + 2 * half].set(jnp.concatenate([-s, s], axis=1))
    return ct, st


def _pad_cols(w, width):
    return jnp.pad(w, ((0, 0), (0, width - w.shape[1])))


def _prep_layer(w_in, w_uq, w_ukv):
    z = lambda n: jnp.zeros((D_MODEL, n), F32)
    col = lambda a, b: w_in[:, a:b]
    nat = [col(O_CQ, O_CKV), col(O_CKV, O_KR),
           jnp.concatenate([z(64), col(O_KR, O_AQ), z(32)], axis=1),
           _pad_cols(col(O_AK, O_AV), LANE),
           _pad_cols(col(O_KI, O_WI), LANE)]
    for h in range(RET_HEADS):
        nat.append(_pad_cols(col(O_RK + h * RET_DK, O_RK + (h + 1) * RET_DK), LANE))
    wnat = jnp.concatenate(nat, axis=1).astype(BF16)
    wtr = jnp.concatenate([col(O_AQ, O_AK), col(O_QI, O_KI), col(O_AV, O_QI), col(O_RQ, O_RK),
                           col(O_RV, O_RG), col(O_RG, O_END), col(O_WI, O_RQ)], axis=1).T.astype(BF16)
    dq = MLA_NOPE + MLA_ROPE
    wuq = jnp.pad(w_uq.reshape(MLA_Q_RANK, MLA_HEADS, dq), ((0, 0), (0, 0), (0, LANE - dq)))
    wuq = wuq.reshape(MLA_Q_RANK, MLA_HEADS * LANE).T.astype(BF16)
    wkv = w_ukv.reshape(MLA_KV_RANK, MLA_HEADS, MLA_NOPE + MLA_V)
    wuk = jnp.pad(wkv[:, :, :MLA_NOPE], ((0, 0), (0, 0), (0, LANE - MLA_NOPE)))
    wuk = wuk.reshape(MLA_KV_RANK, MLA_HEADS * LANE).astype(BF16)
    wuv = wkv[:, :, MLA_NOPE:].reshape(MLA_KV_RANK, MLA_HEADS * MLA_V).T.astype(BF16)
    return wnat, wtr, wuq, wuk, wuv


def kernel(x, g_mix, w_in, g_q, w_uq, g_kv, w_ukv, g_ret, w_out, g_mlp, w_ff1, w_ff2, g_final):
    B, S, _ = x.shape
    topk = min(DSA_TOPK_MAX, S // 4)

    cos32, sin32 = _rope_tables(S, 32)
    cos64, sin64 = _rope_tables(S, 64)
    tabt = jnp.concatenate([cos64.T, sin64.T, cos32.T, sin32.T], axis=0)
    tabn = jnp.stack(_lane_table(cos32, sin32, 64, S) + _lane_table(cos64, sin64, 0, S)
                     + _lane_table(cos32, sin32, 0, S))

    log_gamma = jnp.log1p(-jnp.exp2(-5.0 - jnp.arange(RET_HEADS, dtype=F32)))
    c = C_RET
    pos = jnp.arange(c, dtype=F32)
    rel = pos[None, :] - pos[:, None]
    dt = jnp.where(rel[None] >= 0, jnp.exp(jnp.maximum(rel, 0.0)[None] * log_gamma[:, None, None]), 0.0)
    xi = jnp.exp((pos + 1.0)[None, :] * log_gamma[:, None])
    zeta = jnp.exp((c - 1.0 - pos)[None, :] * log_gamma[:, None])
    cdec = jnp.broadcast_to(jnp.exp(c * log_gamma)[:, None, None], (RET_HEADS, RET_DV, LANE))

    for l in range(DEPTH):
        wnat, wtr, wuq, wuk, wuv = _prep_layer(w_in[l], w_uq[l], w_ukv[l])
        (qat, ka, vat, qit, wit, ki, qbt, kb, vbt, rqt, rk, rvt, rgt) = _proj(
            x, g_mix[l][None, :], g_q[l][None, :], g_kv[l][None, :], wnat, wtr, wuq, wuk, wuv, tabt, tabn)
        oat = _mla(qat, ka, vat)
        obt = _dsa(qit, wit, ki, qbt, kb, vbt, topk)
        gret = jnp.broadcast_to(g_ret[l][:, None], (RET_HEADS * RET_DV, c))
        oct = _retention(rqt, rk, rvt, rgt, dt, xi, zeta, cdec, gret)
        x = _mlp(x, oat, obt, oct, w_out[l].astype(BF16), g_mlp[l][None, :],
                 w_ff1[l].astype(BF16), w_ff2[l].astype(BF16), g_final[None, :], l == DEPTH - 1)
    return x
```

```python
import functools

import numpy as np
import jax
import jax.numpy as jnp
from jax import lax
from jax.experimental import pallas as pl
from jax.experimental.pallas import tpu as pltpu

D_MODEL = 1024
DEPTH = 2
MLA_HEADS = 8
MLA_NOPE = 64
MLA_ROPE = 32
MLA_V = 64
MLA_Q_RANK = 768
MLA_KV_RANK = 256
DSA_HEADS = 4
DSA_DIM = 64
IDX_HEADS = 8
IDX_DIM = 32
DSA_TOPK_MAX = 256
RET_HEADS = 4
RET_DK = 64
RET_DV = 64
D_FF = 4 * D_MODEL
ROPE_THETA = 10000.0
EPS = 1e-6

MLA_COLS = (MLA_Q_RANK, MLA_KV_RANK, MLA_ROPE)
DSA_COLS = (DSA_HEADS * DSA_DIM, DSA_DIM, DSA_DIM, IDX_HEADS * IDX_DIM, IDX_DIM, IDX_HEADS)
RET_COLS = (RET_HEADS * RET_DK, RET_HEADS * RET_DK, RET_HEADS * RET_DV, RET_HEADS * RET_DV)
SPLITS = MLA_COLS + DSA_COLS + RET_COLS
_OFF = np.concatenate([[0], np.cumsum(SPLITS)]).tolist()
(O_CQ, O_CKV, O_KR, O_AQ, O_AK, O_AV, O_QI, O_KI, O_WI, O_RQ, O_RK, O_RV, O_RG, O_END) = _OFF

LANE = 128
BF16_ROWS = 16
VMEM_LIMIT = 56 * 1024 * 1024

BF16 = jnp.bfloat16
F32 = jnp.float32
NEG = -1e30
LOG2E = float(np.log2(np.e))
INT_MIN = np.int32(-2**31)

N_CQ = 0
N_CKV = N_CQ + MLA_Q_RANK
N_KR = N_CKV + MLA_KV_RANK
N_AK = N_KR + LANE
N_KI = N_AK + LANE
N_RK = N_KI + LANE
N_NAT = N_RK + RET_HEADS * LANE
T_AQ = 0
T_QI = T_AQ + DSA_HEADS * DSA_DIM
T_AV = T_QI + IDX_HEADS * IDX_DIM
T_RQ = T_AV + DSA_DIM
T_RV = T_RQ + RET_HEADS * RET_DK
T_RG = T_RV + RET_HEADS * RET_DV
T_WI = T_RG + RET_HEADS * RET_DV
N_TR = T_WI + IDX_HEADS

TM_PROJ = 512
TQ_MLA = 256
G_MLA = 8
TQ_DSA = 256
CK_DSA = 128
C_RET = 256
TM_MLP = 512
FF_CHUNK = 1024

_NT = (((1,), (1,)), ((), ()))


def _rms(x, g):
    return x * lax.rsqrt(jnp.mean(x * x, axis=-1, keepdims=True) + EPS) * g


def _dot(a, b):
    return jnp.dot(a, b, preferred_element_type=F32)


def _dot_nt(a, b):
    return lax.dot_general(a, b, _NT, preferred_element_type=F32)


def _rope_rows(x1, x2, c, s):
    return x1 * c - x2 * s, x1 * s + x2 * c


def _rope_lanes(x, c, s_signed, half, first_lo):
    lane = lax.broadcasted_iota(jnp.int32, x.shape, 1)
    first = (lane >= first_lo) & (lane < first_lo + half)
    rot = jnp.where(first, pltpu.roll(x, LANE - half, 1), pltpu.roll(x, half, 1))
    return x * c + rot * s_signed


def _proj_kernel(x_ref, gmix_ref, gq_ref, gkv_ref, wnat_ref, wtr_ref, wuq_ref, wuk_ref, wuv_ref,
                 tabt_ref, tabn_ref,
                 qat_ref, ka_ref, vat_ref, qit_ref, wit_ref, ki_ref, qbt_ref, kb_ref, vbt_ref,
                 rqt_ref, rk_ref, rvt_ref, rgt_ref):
    tm = x_ref.shape[0]
    hb = _rms(x_ref[...], gmix_ref[...]).astype(BF16)

    c64t = tabt_ref[0:32, :]
    s64t = tabt_ref[32:64, :]
    c32t = tabt_ref[64:80, :]
    s32t = tabt_ref[80:96, :]

    cq = _dot(hb, wnat_ref[:, N_CQ:N_CQ + MLA_Q_RANK])
    nq = _rms(cq, gq_ref[...]).astype(BF16)
    qt = _dot_nt(wuq_ref[...], nq)
    scale_a = (MLA_NOPE + MLA_ROPE) ** -0.5 * LOG2E
    for h in range(MLA_HEADS):
        r0 = h * LANE
        qat_ref[r0:r0 + 64, :] = (qt[r0:r0 + 64, :] * scale_a).astype(BF16)
        o1, o2 = _rope_rows(qt[r0 + 64:r0 + 80, :], qt[r0 + 80:r0 + 96, :], c32t, s32t)
        qat_ref[r0 + 64:r0 + 80, :] = (o1 * scale_a).astype(BF16)
        qat_ref[r0 + 80:r0 + 96, :] = (o2 * scale_a).astype(BF16)
        qat_ref[r0 + 96:r0 + 128, :] = jnp.zeros((32, tm), BF16)

    ckv = _dot(hb, wnat_ref[:, N_CKV:N_CKV + MLA_KV_RANK])
    nkv = _rms(ckv, gkv_ref[...]).astype(BF16)
    kn = _dot(nkv, wuk_ref[...])
    keys = _dot(hb, wnat_ref[:, N_KR:N_NAT])
    slab = lambda off: keys[:, off - N_KR:off - N_KR + LANE]
    krp = _rope_lanes(slab(N_KR), tabn_ref[0], tabn_ref[1], 16, 64)
    for h in range(MLA_HEADS):
        ka_ref[h] = (kn[:, h * LANE:(h + 1) * LANE] + krp).astype(BF16)
    vt = _dot_nt(wuv_ref[...], nkv).astype(BF16)
    for j in range(tm // LANE):
        vat_ref[j] = vt[:, j * LANE:(j + 1) * LANE]

    kb_ref[...] = _rope_lanes(slab(N_AK), tabn_ref[2], tabn_ref[3], 32, 0).astype(BF16)
    ki_ref[...] = _rope_lanes(slab(N_KI), tabn_ref[4], tabn_ref[5], 16, 0).astype(BF16)

    for h in range(RET_HEADS):
        rkp = _rope_lanes(slab(N_RK + h * LANE), tabn_ref[2], tabn_ref[3], 32, 0) * (RET_DK ** -0.5)
        rk_ref[:, h * LANE:(h + 1) * LANE] = rkp.astype(BF16)

    aqt = _dot_nt(wtr_ref[T_AQ:T_AQ + 256, :], hb)
    scale_b = DSA_DIM ** -0.5 * LOG2E
    for h in range(DSA_HEADS):
        o1, o2 = _rope_rows(aqt[h * 64:h * 64 + 32, :], aqt[h * 64 + 32:h * 64 + 64, :], c64t, s64t)
        qbt_ref[h * LANE:h * LANE + 32, :] = (o1 * scale_b).astype(BF16)
        qbt_ref[h * LANE + 32:h * LANE + 64, :] = (o2 * scale_b).astype(BF16)
        qbt_ref[h * LANE + 64:(h + 1) * LANE, :] = jnp.zeros((64, tm), BF16)

    qit = _dot_nt(wtr_ref[T_QI:T_QI + 256, :], hb)
    for h in range(IDX_HEADS):
        o1, o2 = _rope_rows(qit[h * 32:h * 32 + 16, :], qit[h * 32 + 16:h * 32 + 32, :], c32t, s32t)
        qit_ref[h * LANE:h * LANE + 16, :] = o1.astype(BF16)
        qit_ref[h * LANE + 16:h * LANE + 32, :] = o2.astype(BF16)
        qit_ref[h * LANE + 32:(h + 1) * LANE, :] = jnp.zeros((96, tm), BF16)

    idx_scale = (IDX_HEADS ** -0.5) * (IDX_DIM ** -0.5)
    wit_ref[...] = _dot_nt(wtr_ref[T_WI:T_WI + IDX_HEADS, :], hb) * idx_scale

    avt = _dot_nt(wtr_ref[T_AV:T_AV + DSA_DIM, :], hb).astype(BF16)
    for j in range(tm // LANE):
        vbt_ref[j] = avt[:, j * LANE:(j + 1) * LANE]

    rqt = _dot_nt(wtr_ref[T_RQ:T_RQ + 256, :], hb)
    for h in range(RET_HEADS):
        o1, o2 = _rope_rows(rqt[h * 64:h * 64 + 32, :], rqt[h * 64 + 32:h * 64 + 64, :], c64t, s64t)
        rqt_ref[h * LANE:h * LANE + 32, :] = o1.astype(BF16)
        rqt_ref[h * LANE + 32:h * LANE + 64, :] = o2.astype(BF16)
        rqt_ref[h * LANE + 64:(h + 1) * LANE, :] = jnp.zeros((64, tm), BF16)

    rvt_ref[...] = _dot_nt(wtr_ref[T_RV:T_RV + 256, :], hb).astype(BF16)
    rgt_ref[...] = _dot_nt(wtr_ref[T_RG:T_RG + 256, :], hb)


def _full(shape):
    nd = len(shape)
    return pl.BlockSpec(shape, lambda b, i: (0,) * nd)


def _proj(x, gmix, gq, gkv, wnat, wtr, wuq, wuk, wuv, tabt, tabn):
    B, S, D = x.shape
    tm = TM_PROJ
    nch = tm // LANE
    outs = [
        (jax.ShapeDtypeStruct((B, MLA_HEADS * LANE, S), BF16), pl.BlockSpec((None, MLA_HEADS * LANE, tm), lambda b, i: (b, 0, i))),
        (jax.ShapeDtypeStruct((B, MLA_HEADS, S, LANE), BF16), pl.BlockSpec((None, MLA_HEADS, tm, LANE), lambda b, i: (b, 0, i, 0))),
        (jax.ShapeDtypeStruct((B, S // LANE, MLA_HEADS * MLA_V, LANE), BF16), pl.BlockSpec((None, nch, MLA_HEADS * MLA_V, LANE), lambda b, i: (b, i, 0, 0))),
        (jax.ShapeDtypeStruct((B, IDX_HEADS * LANE, S), BF16), pl.BlockSpec((None, IDX_HEADS * LANE, tm), lambda b, i: (b, 0, i))),
        (jax.ShapeDtypeStruct((B, IDX_HEADS, S), F32), pl.BlockSpec((None, IDX_HEADS, tm), lambda b, i: (b, 0, i))),
        (jax.ShapeDtypeStruct((B, S, LANE), BF16), pl.BlockSpec((None, tm, LANE), lambda b, i: (b, i, 0))),
        (jax.ShapeDtypeStruct((B, DSA_HEADS * LANE, S), BF16), pl.BlockSpec((None, DSA_HEADS * LANE, tm), lambda b, i: (b, 0, i))),
        (jax.ShapeDtypeStruct((B, S, LANE), BF16), pl.BlockSpec((None, tm, LANE), lambda b, i: (b, i, 0))),
        (jax.ShapeDtypeStruct((B, S // LANE, DSA_DIM, LANE), BF16), pl.BlockSpec((None, nch, DSA_DIM, LANE), lambda b, i: (b, i, 0, 0))),
        (jax.ShapeDtypeStruct((B, RET_HEADS * LANE, S), BF16), pl.BlockSpec((None, RET_HEADS * LANE, tm), lambda b, i: (b, 0, i))),
        (jax.ShapeDtypeStruct((B, S, RET_HEADS * LANE), BF16), pl.BlockSpec((None, tm, RET_HEADS * LANE), lambda b, i: (b, i, 0))),
        (jax.ShapeDtypeStruct((B, RET_HEADS * RET_DV, S), BF16), pl.BlockSpec((None, RET_HEADS * RET_DV, tm), lambda b, i: (b, 0, i))),
        (jax.ShapeDtypeStruct((B, RET_HEADS * RET_DV, S), F32), pl.BlockSpec((None, RET_HEADS * RET_DV, tm), lambda b, i: (b, 0, i))),
    ]
    in_specs = [
        pl.BlockSpec((None, tm, D), lambda b, i: (b, i, 0)),
        _full(gmix.shape), _full(gq.shape), _full(gkv.shape),
        _full(wnat.shape), _full(wtr.shape), _full(wuq.shape), _full(wuk.shape), _full(wuv.shape),
        pl.BlockSpec((tabt.shape[0], tm), lambda b, i: (0, i)),
        pl.BlockSpec((tabn.shape[0], tm, LANE), lambda b, i: (0, i, 0)),
    ]
    return pl.pallas_call(
        _proj_kernel,
        out_shape=[o[0] for o in outs],
        grid=(B, S // tm),
        in_specs=in_specs,
        out_specs=[o[1] for o in outs],
        compiler_params=pltpu.CompilerParams(
            dimension_semantics=("parallel", "parallel"), vmem_limit_bytes=VMEM_LIMIT),
        name="proj",
    )(x, gmix, gq, gkv, wnat, wtr, wuq, wuk, wuv, tabt, tabn)


def _with_ones_rows(vt):
    return jnp.concatenate([vt, jnp.ones((BF16_ROWS, vt.shape[1]), BF16)], axis=0)


def _mla_kernel(q_ref, k_ref, v_ref, o_ref, s0_sc, s1_sc, m_sc, acc_sc):
    g_heads = acc_sc.shape[0]
    tq = q_ref.shape[1]
    tk = tq
    i = pl.program_id(2)
    nv = tk // LANE
    kidx = lax.broadcasted_iota(jnp.int32, (tk, tq), 0)
    qidx = lax.broadcasted_iota(jnp.int32, (tk, tq), 1)
    bufs = (s0_sc, s1_sc)

    def scores(kt, buf, causal=None):
        for g in range(g_heads):
            s = _dot(k_ref[g, pl.ds(pl.multiple_of(kt * tk, tk), tk), :],
                     q_ref[g * LANE:(g + 1) * LANE, :])
            if causal is not None:
                s = jnp.where(causal, s, NEG)
            buf[g] = s

    def softmax(kt, buf):
        for g in range(g_heads):
            s = buf[g]
            m_old = m_sc[g]
            m_new = jnp.maximum(m_old, jnp.max(s, axis=0, keepdims=True))
            a = jnp.exp2(m_old - m_new)
            p = jnp.exp2(s - m_new)
            vt = jnp.concatenate([v_ref[kt * nv + j, g * MLA_V:(g + 1) * MLA_V, :] for j in range(nv)],
                                 axis=1)
            m_sc[g] = m_new
            acc_sc[g] = a * acc_sc[g] + _dot(_with_ones_rows(vt), p.astype(BF16))

    m_sc[...] = jnp.full(m_sc.shape, NEG, F32)
    acc_sc[...] = jnp.zeros_like(acc_sc)

    def step(k, parity, diagonal=False):
        scores(k + 1, bufs[1 - parity], (kidx <= qidx) if diagonal else None)
        softmax(k, bufs[parity])

    scores(0, bufs[0], kidx <= i * tq + qidx)

    def pair(p, _):
        step(2 * p, 0)
        step(2 * p + 1, 1)
        return 0

    lax.fori_loop(0, (i - 1) // 2, pair, 0)

    @pl.when(i % 2 == 0)
    def _():
        @pl.when(i >= 2)
        def _():
            step(i - 2, 0)
            step(i - 1, 1, diagonal=True)
        softmax(i, bufs[0])

    @pl.when(i % 2 == 1)
    def _():
        step(i - 1, 0, diagonal=True)
        softmax(i, bufs[1])

    for g in range(g_heads):
        acc = acc_sc[g]
        o_ref[g * MLA_V:(g + 1) * MLA_V, :] = (acc[:MLA_V] / acc[MLA_V:MLA_V + 1]).astype(BF16)


def _mla(qat, ka, vat):
    B, _, S = qat.shape
    tq = TQ_MLA
    g = G_MLA
    return pl.pallas_call(
        _mla_kernel,
        out_shape=jax.ShapeDtypeStruct((B, MLA_HEADS * MLA_V, S), BF16),
        grid=(B, MLA_HEADS // g, S // tq),
        in_specs=[
            pl.BlockSpec((None, g * LANE, tq), lambda b, h, i: (b, h, i)),
            pl.BlockSpec((None, g, S, LANE), lambda b, h, i: (b, h, 0, 0)),
            pl.BlockSpec((None, S // LANE, g * MLA_V, LANE), lambda b, h, i: (b, 0, h, 0)),
        ],
        out_specs=pl.BlockSpec((None, g * MLA_V, tq), lambda b, h, i: (b, h, i)),
        scratch_shapes=[pltpu.VMEM((g, tq, tq), F32), pltpu.VMEM((g, tq, tq), F32),
                        pltpu.VMEM((g, 1, tq), F32),
                        pltpu.VMEM((g, MLA_V + BF16_ROWS, tq), F32)],
        compiler_params=pltpu.CompilerParams(
            dimension_semantics=("parallel", "parallel", "arbitrary"), vmem_limit_bytes=VMEM_LIMIT),
        name="mla",
    )(qat, ka, vat)


def _count_rows(mask):
    r, t = mask.shape
    return jnp.sum(mask.astype(jnp.int32).reshape(r // 8, 8, t), axis=0)


def _dsa_kernel(topk, qit_ref, wit_ref, ki_ref, qbt_ref, kb_ref, vbt_ref, o_ref,
                sc_sc, att0_sc, att1_sc, thr_sc, nge_sc, m_sc, acc_sc):
    tq = o_ref.shape[1]
    ck = CK_DSA
    sub = tq // ck
    i = pl.program_id(1)
    nblk = i + 1

    def rows(ref, r0, n):
        if isinstance(r0, int):
            return ref[r0:r0 + n, :]
        return ref[pl.ds(pl.multiple_of(r0, n), n), :]

    n_lg = tq // LANE

    def sc_load(r0, n):
        return jnp.concatenate([sc_sc[g, pl.ds(pl.multiple_of(r0, n), n), :] for g in range(n_lg)], axis=1)

    def sc_store(r0, n, val):
        for g in range(n_lg):
            sc_sc[g, pl.ds(pl.multiple_of(r0, n), n), :] = val[:, g * LANE:(g + 1) * LANE]

    qpos_c = i * tq + lax.broadcasted_iota(jnp.int32, (ck, tq), 1)
    krow_c = lax.broadcasted_iota(jnp.int32, (ck, tq), 0)

    def score_block(cb):
        for j in range(sub):
            r0 = cb * tq + j * ck
            kc = rows(ki_ref, r0, ck)
            acc = jnp.zeros((ck, tq), F32)
            for h in range(IDX_HEADS):
                rel = _dot(kc, qit_ref[h * LANE:(h + 1) * LANE, :])
                acc = acc + wit_ref[h:h + 1, :] * jnp.maximum(rel, 0.0)
            sc_store(r0, ck, jnp.where(r0 + krow_c <= qpos_c, acc, -jnp.inf))

    def score_pair(p, _):
        score_block(2 * p)
        score_block(2 * p + 1)
        return 0

    lax.fori_loop(0, nblk // 2, score_pair, 0)

    @pl.when(nblk % 2 == 1)
    def _():
        score_block(nblk - 1)

    krow = lax.broadcasted_iota(jnp.int32, (tq, tq), 0)

    def count(pred, nb=None):
        def body(cb, a):
            return a + _count_rows(pred(sc_load(cb * tq, tq), cb * tq + krow))
        part = jnp.zeros((8, tq), jnp.int32)
        if nb is None:
            part = lax.fori_loop(0, nblk, body, part)
        else:
            for cb in range(nb):
                part = body(cb, part)
        return jnp.sum(part, axis=0, keepdims=True)

    def flip(word):
        return jnp.where(word < 0, word ^ jnp.int32(0x7FFFFFFF), word)

    def key_to_score(key):
        return pltpu.bitcast(flip(key), F32)

    thr_sc[...] = jnp.full((1, tq), jnp.finfo(F32).min, F32)

    def bisect(nb):
        assert n_lg == 2
        la, lb = slice(0, LANE), slice(LANE, 2 * LANE)

        def part_count(group, cf):
            part = jnp.zeros((8, LANE), jnp.int32)
            for cb in range(nb):
                part = part + _count_rows(sc_sc[group, cb * tq:(cb + 1) * tq, :] >= cf[0:1, :])
            return part

        def accept(part, cand, ans, n_ans):
            n = jnp.sum(part, axis=0, keepdims=True)
            ok = n >= topk
            return jnp.where(ok, cand, ans), jnp.where(ok, n, n_ans)

        def cand_of(t, ans):
            return jnp.where(t == 0, jnp.zeros_like(ans), ans + jnp.left_shift(jnp.int32(1), 31 - t))

        def body(t, st):
            ans_a, n_a, ans_b, n_b, cand_b, part_b = st
            ans_b, n_b = accept(part_b, cand_b, ans_b, n_b)
            cand_b = cand_of(t, ans_b)
            cand_a = cand_of(t, ans_a)
            ans_a, n_a = accept(part_count(0, key_to_score(cand_a)), cand_a, ans_a, n_a)
            return ans_a, n_a, ans_b, n_b, cand_b, part_count(1, key_to_score(cand_b))

        start = jnp.full((8, LANE), INT_MIN, jnp.int32)
        zero = jnp.zeros((8, LANE), jnp.int32)
        first = body(jnp.int32(0), (start, zero, start, zero, start, zero))
        ans_a, n_a, ans_b, n_b, cand_b, part_b = lax.fori_loop(1, 32, body, first)
        ans_b, n_b = accept(part_b, cand_b, ans_b, n_b)
        thr_sc[:, la] = key_to_score(ans_a)[0:1, :]
        thr_sc[:, lb] = key_to_score(ans_b)[0:1, :]
        nge_sc[:, la] = n_a[0:1, :]
        nge_sc[:, lb] = n_b[0:1, :]

    for v in range(topk // tq, sc_sc.shape[1] // tq):
        pl.when(i == v)(functools.partial(bisect, v + 1))

    @pl.when(i * tq >= topk)
    def _():
        thr = thr_sc[...]

        @pl.when(jnp.max(nge_sc[...]) > topk)
        def _():
            need = topk - count(lambda s, kidx: s > thr)

            def jbit(t, j):
                cand = j + jnp.left_shift(jnp.int32(1), 10 - t)
                n = count(lambda s, kidx: (s == thr) & (kidx < cand))
                return jnp.where(n < need, cand, j)

            last = lax.fori_loop(0, 11, jbit, jnp.zeros((1, tq), jnp.int32))

            def drop(cb, _):
                s = sc_load(cb * tq, tq)
                dropped = (s == thr) & (cb * tq + krow > last)
                sc_store(cb * tq, tq, jnp.where(dropped, -jnp.inf, s))
                return 0

            lax.fori_loop(0, nblk, drop, 0)

    thr = thr_sc[...]
    nh = DSA_HEADS
    qall = jnp.concatenate([qbt_ref[h * LANE:(h + 1) * LANE, :] for h in range(nh)], axis=1)
    bufs = (att0_sc, att1_sc)

    def logits(cb, buf):
        bias = jnp.where(sc_load(cb * tq, tq) >= thr, 0.0, NEG)
        for j in range(sub):
            s = _dot(rows(kb_ref, cb * tq + j * ck, ck), qall)
            b = bias[j * ck:(j + 1) * ck, :]
            for h in range(nh):
                buf[h, j * ck:(j + 1) * ck, :] = s[:, h * tq:(h + 1) * tq] + b

    def attend(cb, buf):
        vt = _with_ones_rows(jnp.concatenate([vbt_ref[cb * sub + j] for j in range(sub)], axis=1))
        for h in range(nh):
            s = buf[h]
            m_old = m_sc[h]
            m_new = jnp.maximum(m_old, jnp.max(s, axis=0, keepdims=True))
            a = jnp.exp2(m_old - m_new)
            p = jnp.exp2(s - m_new)
            m_sc[h] = m_new
            acc_sc[h] = a * acc_sc[h] + _dot(vt, p.astype(BF16))

    def step(k, parity):
        logits(k + 1, bufs[1 - parity])
        attend(k, bufs[parity])

    m_sc[...] = jnp.full(m_sc.shape, NEG, F32)
    acc_sc[...] = jnp.zeros_like(acc_sc)
    logits(0, bufs[0])

    def pair(p, _):
        step(2 * p, 0)
        step(2 * p + 1, 1)
        return 0

    lax.fori_loop(0, i // 2, pair, 0)

    @pl.when(i % 2 == 1)
    def _():
        step(i - 1, 0)
        attend(i, bufs[1])

    @pl.when(i % 2 == 0)
    def _():
        attend(i, bufs[0])

    for h in range(nh):
        acc = acc_sc[h]
        o_ref[h * DSA_DIM:(h + 1) * DSA_DIM, :] = (acc[:DSA_DIM] / acc[DSA_DIM:DSA_DIM + 1]).astype(BF16)


def _dsa(qit, wit, ki, qbt, kb, vbt, topk):
    B, _, S = qit.shape
    tq = TQ_DSA
    assert topk % tq == 0 or topk >= S, "query tiles must not straddle the top-k boundary"
    return pl.pallas_call(
        functools.partial(_dsa_kernel, topk),
        out_shape=jax.ShapeDtypeStruct((B, DSA_HEADS * DSA_DIM, S), BF16),
        grid=(B, S // tq),
        in_specs=[
            pl.BlockSpec((None, IDX_HEADS * LANE, tq), lambda b, i: (b, 0, i)),
            pl.BlockSpec((None, IDX_HEADS, tq), lambda b, i: (b, 0, i)),
            pl.BlockSpec((None, S, LANE), lambda b, i: (b, 0, 0)),
            pl.BlockSpec((None, DSA_HEADS * LANE, tq), lambda b, i: (b, 0, i)),
            pl.BlockSpec((None, S, LANE), lambda b, i: (b, 0, 0)),
            pl.BlockSpec((None, S // LANE, DSA_DIM, LANE), lambda b, i: (b, 0, 0, 0)),
        ],
        out_specs=pl.BlockSpec((None, DSA_HEADS * DSA_DIM, tq), lambda b, i: (b, 0, i)),
        scratch_shapes=[
            pltpu.VMEM((tq // LANE, S, LANE), F32),
            pltpu.VMEM((DSA_HEADS, tq, tq), F32),
            pltpu.VMEM((DSA_HEADS, tq, tq), F32),
            pltpu.VMEM((1, tq), F32),
            pltpu.VMEM((1, tq), jnp.int32),
            pltpu.VMEM((DSA_HEADS, 1, tq), F32),
            pltpu.VMEM((DSA_HEADS, DSA_DIM + BF16_ROWS, tq), F32),
        ],
        compiler_params=pltpu.CompilerParams(
            dimension_semantics=("parallel", "arbitrary"), vmem_limit_bytes=VMEM_LIMIT),
        name="dsa",
    )(qit, wit, ki, qbt, kb, vbt)


def _ret_kernel(rqt_ref, rk_ref, rvt_ref, rgt_ref, dt_ref, xi_ref, zeta_ref, cdec_ref, gret_ref, o_ref):
    c = dt_ref.shape[1]
    n_chunks = o_ref.shape[1] // c
    state = [jnp.zeros((RET_DV, LANE), F32) for _ in range(RET_HEADS)]
    for n in range(n_chunks):
        cols = slice(n * c, (n + 1) * c)
        for h in range(RET_HEADS):
            rt = state[h]
            qt = rqt_ref[h * LANE:(h + 1) * LANE, cols]
            k = rk_ref[cols, h * LANE:(h + 1) * LANE]
            vt = rvt_ref[h * RET_DV:(h + 1) * RET_DV, cols]
            inner_t = _dot(k, qt) * dt_ref[h]
            qx = (qt.astype(F32) * xi_ref[h:h + 1, :]).astype(BF16)
            out_t = _dot(vt, inner_t.astype(BF16)) + _dot(rt.astype(BF16), qx)
            vz = (vt.astype(F32) * zeta_ref[h:h + 1, :]).astype(BF16)
            state[h] = rt * cdec_ref[h] + _dot(vz, k)

            mu = jnp.mean(out_t, axis=0, keepdims=True)
            xc = out_t - mu
            var = jnp.mean(xc * xc, axis=0, keepdims=True)
            y = xc * lax.rsqrt(var + EPS) * gret_ref[h * RET_DV:(h + 1) * RET_DV, :]
            g = rgt_ref[h * RET_DV:(h + 1) * RET_DV, cols]
            o_ref[h * RET_DV:(h + 1) * RET_DV, cols] = (g * jax.nn.sigmoid(g) * y).astype(BF16)


def _retention(rqt, rk, rvt, rgt, dt, xi, zeta, cdec, gret):
    B, _, S = rqt.shape

    def full1(shape):
        nd = len(shape)
        return pl.BlockSpec(shape, lambda b: (0,) * nd)

    def per_batch(rows, cols):
        return pl.BlockSpec((None, rows, cols), lambda b: (b, 0, 0))

    return pl.pallas_call(
        _ret_kernel,
        out_shape=jax.ShapeDtypeStruct((B, RET_HEADS * RET_DV, S), BF16),
        grid=(B,),
        in_specs=[
            per_batch(RET_HEADS * LANE, S), per_batch(S, RET_HEADS * LANE),
            per_batch(RET_HEADS * RET_DV, S), per_batch(RET_HEADS * RET_DV, S),
            full1(dt.shape), full1(xi.shape), full1(zeta.shape), full1(cdec.shape), full1(gret.shape),
        ],
        out_specs=per_batch(RET_HEADS * RET_DV, S),
        compiler_params=pltpu.CompilerParams(
            dimension_semantics=("parallel",), vmem_limit_bytes=VMEM_LIMIT),
        name="retention",
    )(rqt, rk, rvt, rgt, dt, xi, zeta, cdec, gret)


def _mlp_kernel(final, x_ref, oat_ref, obt_ref, oct_ref, wout_ref, gmlp_ref, w1_ref, w2_ref, gfin_ref,
                y_ref):
    def tr(ref):
        return ref[...].astype(F32).T.astype(BF16)

    na = MLA_HEADS * MLA_V
    nb = DSA_HEADS * DSA_DIM
    attn = (_dot(tr(oat_ref), wout_ref[0:na, :])
            + _dot(tr(obt_ref), wout_ref[na:na + nb, :])
            + _dot(tr(oct_ref), wout_ref[na + nb:, :]))
    x1 = x_ref[...] + attn
    hf = _rms(x1, gmlp_ref[...]).astype(BF16)
    acc = x1
    for f in range(D_FF // FF_CHUNK):
        u = jnp.maximum(_dot(hf, w1_ref[:, f * FF_CHUNK:(f + 1) * FF_CHUNK]), 0.0)
        acc = acc + _dot((u * u).astype(BF16), w2_ref[f * FF_CHUNK:(f + 1) * FF_CHUNK, :])
    if final:
        acc = _rms(acc, gfin_ref[...])
    y_ref[...] = acc


def _mlp(x, oat, obt, oct, wout, gmlp, w1, w2, gfin, final):
    B, S, D = x.shape
    tm = TM_MLP

    def const(shape):
        nd = len(shape)
        return pl.BlockSpec(shape, lambda b, i: (0,) * nd, pipeline_mode=pl.Buffered(1))

    return pl.pallas_call(
        functools.partial(_mlp_kernel, final),
        out_shape=jax.ShapeDtypeStruct((B, S, D), F32),
        grid=(B, S // tm),
        in_specs=[
            pl.BlockSpec((None, tm, D), lambda b, i: (b, i, 0)),
            pl.BlockSpec((None, oat.shape[1], tm), lambda b, i: (b, 0, i)),
            pl.BlockSpec((None, obt.shape[1], tm), lambda b, i: (b, 0, i)),
            pl.BlockSpec((None, oct.shape[1], tm), lambda b, i: (b, 0, i)),
            const(wout.shape), const(gmlp.shape), const(w1.shape), const(w2.shape), const(gfin.shape),
        ],
        out_specs=pl.BlockSpec((None, tm, D), lambda b, i: (b, i, 0)),
        compiler_params=pltpu.CompilerParams(
            dimension_semantics=("parallel", "parallel"), vmem_limit_bytes=VMEM_LIMIT),
        name="mlp",
    )(x, oat, obt, oct, wout, gmlp, w1, w2, gfin)


def _rope_tables(seq, dim):
    pos = jnp.arange(seq, dtype=F32)
    inv = ROPE_THETA ** (-jnp.arange(0, dim, 2, dtype=F32) / dim)
    ang = pos[:, None] * inv[None, :]
    return jnp.cos(ang), jnp.sin(ang)


def _lane_table(c, s, lo, seq):
    half = c.shape[1]
    ct = jnp.zeros((seq, LANE), F32).at[:, lo:lo + 2 * half].set(jnp.concatenate([c, c], axis=1))
    st = jnp.zeros((seq, LANE), F32).at[:, lo:lo + 2 * half].set(jnp.concatenate([-s, s], axis=1))
    return ct, st


def _pad_cols(w, width):
    return jnp.pad(w, ((0, 0), (0, width - w.shape[1])))


def _prep_layer(w_in, w_uq, w_ukv):
    z = lambda n: jnp.zeros((D_MODEL, n), F32)
    col = lambda a, b: w_in[:, a:b]
    nat = [col(O_CQ, O_CKV), col(O_CKV, O_KR),
           jnp.concatenate([z(64), col(O_KR, O_AQ), z(32)], axis=1),
           _pad_cols(col(O_AK, O_AV), LANE),
           _pad_cols(col(O_KI, O_WI), LANE)]
    for h in range(RET_HEADS):
        nat.append(_pad_cols(col(O_RK + h * RET_DK, O_RK + (h + 1) * RET_DK), LANE))
    wnat = jnp.concatenate(nat, axis=1).astype(BF16)
    wtr = jnp.concatenate([col(O_AQ, O_AK), col(O_QI, O_KI), col(O_AV, O_QI), col(O_RQ, O_RK),
                           col(O_RV, O_RG), col(O_RG, O_END), col(O_WI, O_RQ)], axis=1).T.astype(BF16)
    dq = MLA_NOPE + MLA_ROPE
    wuq = jnp.pad(w_uq.reshape(MLA_Q_RANK, MLA_HEADS, dq), ((0, 0), (0, 0), (0, LANE - dq)))
    wuq = wuq.reshape(MLA_Q_RANK, MLA_HEADS * LANE).T.astype(BF16)
    wkv = w_ukv.reshape(MLA_KV_RANK, MLA_HEADS, MLA_NOPE + MLA_V)
    wuk = jnp.pad(wkv[:, :, :MLA_NOPE], ((0, 0), (0, 0), (0, LANE - MLA_NOPE)))
    wuk = wuk.reshape(MLA_KV_RANK, MLA_HEADS * LANE).astype(BF16)
    wuv = wkv[:, :, MLA_NOPE:].reshape(MLA_KV_RANK, MLA_HEADS * MLA_V).T.astype(BF16)
    return wnat, wtr, wuq, wuk, wuv


def kernel(x, g_mix, w_in, g_q, w_uq, g_kv, w_ukv, g_ret, w_out, g_mlp, w_ff1, w_ff2, g_final):
    B, S, _ = x.shape
    topk = min(DSA_TOPK_MAX, S // 4)

    cos32, sin32 = _rope_tables(S, 32)
    cos64, sin64 = _rope_tables(S, 64)
    tabt = jnp.concatenate([cos64.T, sin64.T, cos32.T, sin32.T], axis=0)
    tabn = jnp.stack(_lane_table(cos32, sin32, 64, S) + _lane_table(cos64, sin64, 0, S)
                     + _lane_table(cos32, sin32, 0, S))

    log_gamma = jnp.log1p(-jnp.exp2(-5.0 - jnp.arange(RET_HEADS, dtype=F32)))
    c = C_RET
    pos = jnp.arange(c, dtype=F32)
    rel = pos[None, :] - pos[:, None]
    dt = jnp.where(rel[None] >= 0, jnp.exp(jnp.maximum(rel, 0.0)[None] * log_gamma[:, None, None]), 0.0)
    xi = jnp.exp((pos + 1.0)[None, :] * log_gamma[:, None])
    zeta = jnp.exp((c - 1.0 - pos)[None, :] * log_gamma[:, None])
    cdec = jnp.broadcast_to(jnp.exp(c * log_gamma)[:, None, None], (RET_HEADS, RET_DV, LANE))

    for l in range(DEPTH):
        wnat, wtr, wuq, wuk, wuv = _prep_layer(w_in[l], w_uq[l], w_ukv[l])
        (qat, ka, vat, qit, wit, ki, qbt, kb, vbt, rqt, rk, rvt, rgt) = _proj(
            x, g_mix[l][None, :], g_q[l][None, :], g_kv[l][None, :], wnat, wtr, wuq, wuk, wuv, tabt, tabn)
        oat = _mla(qat, ka, vat)
        obt = _dsa(qit, wit, ki, qbt, kb, vbt, topk)
        gret = jnp.broadcast_to(g_ret[l][:, None], (RET_HEADS * RET_DV, c))
        oct = _retention(rqt, rk, rvt, rgt, dt, xi, zeta, cdec, gret)
        x = _mlp(x, oat, obt, oct, w_out[l].astype(BF16), g_mlp[l][None, :],
                 w_ff1[l].astype(BF16), w_ff2[l].astype(BF16), g_final[None, :], l == DEPTH - 1)
    return x
```

```python
import functools

import numpy as np
import jax
import jax.numpy as jnp
from jax import lax
from jax.experimental import pallas as pl
from jax.experimental.pallas import tpu as pltpu

D_MODEL = 1024
DEPTH = 2
MLA_HEADS = 8
MLA_NOPE = 64
MLA_ROPE = 32
MLA_V = 64
MLA_Q_RANK = 768
MLA_KV_RANK = 256
DSA_HEADS = 4
DSA_DIM = 64
IDX_HEADS = 8
IDX_DIM = 32
DSA_TOPK_MAX = 256
RET_HEADS = 4
RET_DK = 64
RET_DV = 64
D_FF = 4 * D_MODEL
ROPE_THETA = 10000.0
EPS = 1e-6

MLA_COLS = (MLA_Q_RANK, MLA_KV_RANK, MLA_ROPE)
DSA_COLS = (DSA_HEADS * DSA_DIM, DSA_DIM, DSA_DIM, IDX_HEADS * IDX_DIM, IDX_DIM, IDX_HEADS)
RET_COLS = (RET_HEADS * RET_DK, RET_HEADS * RET_DK, RET_HEADS * RET_DV, RET_HEADS * RET_DV)
SPLITS = MLA_COLS + DSA_COLS + RET_COLS
_OFF = np.concatenate([[0], np.cumsum(SPLITS)]).tolist()
(O_CQ, O_CKV, O_KR, O_AQ, O_AK, O_AV, O_QI, O_KI, O_WI, O_RQ, O_RK, O_RV, O_RG, O_END) = _OFF

LANE = 128
BF16_ROWS = 16
VMEM_LIMIT = 56 * 1024 * 1024

BF16 = jnp.bfloat16
F32 = jnp.float32
NEG = -1e30
LOG2E = float(np.log2(np.e))
INT_MIN = np.int32(-2**31)

N_CQ = 0
N_CKV = N_CQ + MLA_Q_RANK
N_KR = N_CKV + MLA_KV_RANK
N_AK = N_KR + LANE
N_KI = N_AK + LANE
N_RK = N_KI + LANE
N_NAT = N_RK + RET_HEADS * LANE
T_AQ = 0
T_QI = T_AQ + DSA_HEADS * DSA_DIM
T_AV = T_QI + IDX_HEADS * IDX_DIM
T_RQ = T_AV + DSA_DIM
T_RV = T_RQ + RET_HEADS * RET_DK
T_RG = T_RV + RET_HEADS * RET_DV
T_WI = T_RG + RET_HEADS * RET_DV
N_TR = T_WI + IDX_HEADS

TM_PROJ = 512
TQ_MLA = 256
G_MLA = 8
TQ_DSA = 256
CK_DSA = 128
C_RET = 256
TM_MLP = 512
FF_CHUNK = 1024

_NT = (((1,), (1,)), ((), ()))


def _rms(x, g):
    return x * lax.rsqrt(jnp.mean(x * x, axis=-1, keepdims=True) + EPS) * g


def _dot(a, b):
    return jnp.dot(a, b, preferred_element_type=F32)


def _dot_nt(a, b):
    return lax.dot_general(a, b, _NT, preferred_element_type=F32)


def _rope_rows(x1, x2, c, s):
    return x1 * c - x2 * s, x1 * s + x2 * c


def _rope_lanes(x, c, s_signed, half, first_lo):
    lane = lax.broadcasted_iota(jnp.int32, x.shape, 1)
    first = (lane >= first_lo) & (lane < first_lo + half)
    rot = jnp.where(first, pltpu.roll(x, LANE - half, 1), pltpu.roll(x, half, 1))
    return x * c + rot * s_signed


def _proj_kernel(x_ref, gmix_ref, gq_ref, gkv_ref, wnat_ref, wtr_ref, wuq_ref, wuk_ref, wuv_ref,
                 tabt_ref, tabn_ref,
                 qat_ref, ka_ref, vat_ref, qit_ref, wit_ref, ki_ref, qbt_ref, kb_ref, vbt_ref,
                 rqt_ref, rk_ref, rvt_ref, rgt_ref):
    tm = x_ref.shape[0]
    hb = _rms(x_ref[...], gmix_ref[...]).astype(BF16)

    c64t = tabt_ref[0:32, :]
    s64t = tabt_ref[32:64, :]
    c32t = tabt_ref[64:80, :]
    s32t = tabt_ref[80:96, :]

    cq = _dot(hb, wnat_ref[:, N_CQ:N_CQ + MLA_Q_RANK])
    nq = _rms(cq, gq_ref[...]).astype(BF16)
    qt = _dot_nt(wuq_ref[...], nq)
    scale_a = (MLA_NOPE + MLA_ROPE) ** -0.5 * LOG2E
    for h in range(MLA_HEADS):
        r0 = h * LANE
        qat_ref[r0:r0 + 64, :] = (qt[r0:r0 + 64, :] * scale_a).astype(BF16)
        o1, o2 = _rope_rows(qt[r0 + 64:r0 + 80, :], qt[r0 + 80:r0 + 96, :], c32t, s32t)
        qat_ref[r0 + 64:r0 + 80, :] = (o1 * scale_a).astype(BF16)
        qat_ref[r0 + 80:r0 + 96, :] = (o2 * scale_a).astype(BF16)
        qat_ref[r0 + 96:r0 + 128, :] = jnp.zeros((32, tm), BF16)

    ckv = _dot(hb, wnat_ref[:, N_CKV:N_CKV + MLA_KV_RANK])
    nkv = _rms(ckv, gkv_ref[...]).astype(BF16)
    kn = _dot(nkv, wuk_ref[...])
    keys = _dot(hb, wnat_ref[:, N_KR:N_NAT])
    slab = lambda off: keys[:, off - N_KR:off - N_KR + LANE]
    krp = _rope_lanes(slab(N_KR), tabn_ref[0], tabn_ref[1], 16, 64)
    for h in range(MLA_HEADS):
        ka_ref[h] = (kn[:, h * LANE:(h + 1) * LANE] + krp).astype(BF16)
    vt = _dot_nt(wuv_ref[...], nkv).astype(BF16)
    for j in range(tm // LANE):
        vat_ref[j] = vt[:, j * LANE:(j + 1) * LANE]

    kb_ref[...] = _rope_lanes(slab(N_AK), tabn_ref[2], tabn_ref[3], 32, 0).astype(BF16)
    ki_ref[...] = _rope_lanes(slab(N_KI), tabn_ref[4], tabn_ref[5], 16, 0).astype(BF16)

    for h in range(RET_HEADS):
        rkp = _rope_lanes(slab(N_RK + h * LANE), tabn_ref[2], tabn_ref[3], 32, 0) * (RET_DK ** -0.5)
        rk_ref[:, h * LANE:(h + 1) * LANE] = rkp.astype(BF16)

    aqt = _dot_nt(wtr_ref[T_AQ:T_AQ + 256, :], hb)
    scale_b = DSA_DIM ** -0.5 * LOG2E
    for h in range(DSA_HEADS):
        o1, o2 = _rope_rows(aqt[h * 64:h * 64 + 32, :], aqt[h * 64 + 32:h * 64 + 64, :], c64t, s64t)
        qbt_ref[h * LANE:h * LANE + 32, :] = (o1 * scale_b).astype(BF16)
        qbt_ref[h * LANE + 32:h * LANE + 64, :] = (o2 * scale_b).astype(BF16)
        qbt_ref[h * LANE + 64:(h + 1) * LANE, :] = jnp.zeros((64, tm), BF16)

    qit = _dot_nt(wtr_ref[T_QI:T_QI + 256, :], hb)
    for h in range(IDX_HEADS):
        o1, o2 = _rope_rows(qit[h * 32:h * 32 + 16, :], qit[h * 32 + 16:h * 32 + 32, :], c32t, s32t)
        qit_ref[h * LANE:h * LANE + 16, :] = o1.astype(BF16)
        qit_ref[h * LANE + 16:h * LANE + 32, :] = o2.astype(BF16)
        qit_ref[h * LANE + 32:(h + 1) * LANE, :] = jnp.zeros((96, tm), BF16)

    idx_scale = (IDX_HEADS ** -0.5) * (IDX_DIM ** -0.5)
    wit_ref[...] = _dot_nt(wtr_ref[T_WI:T_WI + IDX_HEADS, :], hb) * idx_scale

    avt = _dot_nt(wtr_ref[T_AV:T_AV + DSA_DIM, :], hb).astype(BF16)
    for j in range(tm // LANE):
        vbt_ref[j] = avt[:, j * LANE:(j + 1) * LANE]

    rqt = _dot_nt(wtr_ref[T_RQ:T_RQ + 256, :], hb)
    for h in range(RET_HEADS):
        o1, o2 = _rope_rows(rqt[h * 64:h * 64 + 32, :], rqt[h * 64 + 32:h * 64 + 64, :], c64t, s64t)
        rqt_ref[h * LANE:h * LANE + 32, :] = o1.astype(BF16)
        rqt_ref[h * LANE + 32:h * LANE + 64, :] = o2.astype(BF16)
        rqt_ref[h * LANE + 64:(h + 1) * LANE, :] = jnp.zeros((64, tm), BF16)

    rvt_ref[...] = _dot_nt(wtr_ref[T_RV:T_RV + 256, :], hb).astype(BF16)
    rgt_ref[...] = _dot_nt(wtr_ref[T_RG:T_RG + 256, :], hb)


def _full(shape):
    nd = len(shape)
    return pl.BlockSpec(shape, lambda b, i: (0,) * nd)


def _proj(x, gmix, gq, gkv, wnat, wtr, wuq, wuk, wuv, tabt, tabn):
    B, S, D = x.shape
    tm = TM_PROJ
    nch = tm // LANE
    outs = [
        (jax.ShapeDtypeStruct((B, MLA_HEADS * LANE, S), BF16), pl.BlockSpec((None, MLA_HEADS * LANE, tm), lambda b, i: (b, 0, i))),
        (jax.ShapeDtypeStruct((B, MLA_HEADS, S, LANE), BF16), pl.BlockSpec((None, MLA_HEADS, tm, LANE), lambda b, i: (b, 0, i, 0))),
        (jax.ShapeDtypeStruct((B, S // LANE, MLA_HEADS * MLA_V, LANE), BF16), pl.BlockSpec((None, nch, MLA_HEADS * MLA_V, LANE), lambda b, i: (b, i, 0, 0))),
        (jax.ShapeDtypeStruct((B, IDX_HEADS * LANE, S), BF16), pl.BlockSpec((None, IDX_HEADS * LANE, tm), lambda b, i: (b, 0, i))),
        (jax.ShapeDtypeStruct((B, IDX_HEADS, S), F32), pl.BlockSpec((None, IDX_HEADS, tm), lambda b, i: (b, 0, i))),
        (jax.ShapeDtypeStruct((B, S, LANE), BF16), pl.BlockSpec((None, tm, LANE), lambda b, i: (b, i, 0))),
        (jax.ShapeDtypeStruct((B, DSA_HEADS * LANE, S), BF16), pl.BlockSpec((None, DSA_HEADS * LANE, tm), lambda b, i: (b, 0, i))),
        (jax.ShapeDtypeStruct((B, S, LANE), BF16), pl.BlockSpec((None, tm, LANE), lambda b, i: (b, i, 0))),
        (jax.ShapeDtypeStruct((B, S // LANE, DSA_DIM, LANE), BF16), pl.BlockSpec((None, nch, DSA_DIM, LANE), lambda b, i: (b, i, 0, 0))),
        (jax.ShapeDtypeStruct((B, RET_HEADS * LANE, S), BF16), pl.BlockSpec((None, RET_HEADS * LANE, tm), lambda b, i: (b, 0, i))),
        (jax.ShapeDtypeStruct((B, S, RET_HEADS * LANE), BF16), pl.BlockSpec((None, tm, RET_HEADS * LANE), lambda b, i: (b, i, 0))),
        (jax.ShapeDtypeStruct((B, RET_HEADS * RET_DV, S), BF16), pl.BlockSpec((None, RET_HEADS * RET_DV, tm), lambda b, i: (b, 0, i))),
        (jax.ShapeDtypeStruct((B, RET_HEADS * RET_DV, S), F32), pl.BlockSpec((None, RET_HEADS * RET_DV, tm), lambda b, i: (b, 0, i))),
    ]
    in_specs = [
        pl.BlockSpec((None, tm, D), lambda b, i: (b, i, 0)),
        _full(gmix.shape), _full(gq.shape), _full(gkv.shape),
        _full(wnat.shape), _full(wtr.shape), _full(wuq.shape), _full(wuk.shape), _full(wuv.shape),
        pl.BlockSpec((tabt.shape[0], tm), lambda b, i: (0, i)),
        pl.BlockSpec((tabn.shape[0], tm, LANE), lambda b, i: (0, i, 0)),
    ]
    return pl.pallas_call(
        _proj_kernel,
        out_shape=[o[0] for o in outs],
        grid=(B, S // tm),
        in_specs=in_specs,
        out_specs=[o[1] for o in outs],
        compiler_params=pltpu.CompilerParams(
            dimension_semantics=("parallel", "parallel"), vmem_limit_bytes=VMEM_LIMIT),
        name="proj",
    )(x, gmix, gq, gkv, wnat, wtr, wuq, wuk, wuv, tabt, tabn)


def _with_ones_rows(vt):
    return jnp.concatenate([vt, jnp.ones((BF16_ROWS, vt.shape[1]), BF16)], axis=0)


def _mla_kernel(q_ref, k_ref, v_ref, o_ref, s0_sc, s1_sc, m_sc, acc_sc):
    g_heads = acc_sc.shape[0]
    tq = q_ref.shape[1]
    tk = tq
    i = pl.program_id(2)
    nv = tk // LANE
    kidx = lax.broadcasted_iota(jnp.int32, (tk, tq), 0)
    qidx = lax.broadcasted_iota(jnp.int32, (tk, tq), 1)
    bufs = (s0_sc, s1_sc)

    def scores(kt, buf, causal=None):
        for g in range(g_heads):
            s = _dot(k_ref[g, pl.ds(pl.multiple_of(kt * tk, tk), tk), :],
                     q_ref[g * LANE:(g + 1) * LANE, :])
            if causal is not None:
                s = jnp.where(causal, s, NEG)
            buf[g] = s

    def softmax(kt, buf):
        for g in range(g_heads):
            s = buf[g]
            m_old = m_sc[g]
            m_new = jnp.maximum(m_old, jnp.max(s, axis=0, keepdims=True))
            a = jnp.exp2(m_old - m_new)
            p = jnp.exp2(s - m_new)
            vt = jnp.concatenate([v_ref[kt * nv + j, g * MLA_V:(g + 1) * MLA_V, :] for j in range(nv)],
                                 axis=1)
            m_sc[g] = m_new
            acc_sc[g] = a * acc_sc[g] + _dot(_with_ones_rows(vt), p.astype(BF16))

    m_sc[...] = jnp.full(m_sc.shape, NEG, F32)
    acc_sc[...] = jnp.zeros_like(acc_sc)

    def step(k, parity, diagonal=False):
        scores(k + 1, bufs[1 - parity], (kidx <= qidx) if diagonal else None)
        softmax(k, bufs[parity])

    scores(0, bufs[0], kidx <= i * tq + qidx)

    def pair(p, _):
        step(2 * p, 0)
        step(2 * p + 1, 1)
        return 0

    lax.fori_loop(0, (i - 1) // 2, pair, 0)

    @pl.when(i % 2 == 0)
    def _():
        @pl.when(i >= 2)
        def _():
            step(i - 2, 0)
            step(i - 1, 1, diagonal=True)
        softmax(i, bufs[0])

    @pl.when(i % 2 == 1)
    def _():
        step(i - 1, 0, diagonal=True)
        softmax(i, bufs[1])

    for g in range(g_heads):
        acc = acc_sc[g]
        o_ref[g * MLA_V:(g + 1) * MLA_V, :] = (acc[:MLA_V] / acc[MLA_V:MLA_V + 1]).astype(BF16)


def _mla(qat, ka, vat):
    B, _, S = qat.shape
    tq = TQ_MLA
    g = G_MLA
    return pl.pallas_call(
        _mla_kernel,
        out_shape=jax.ShapeDtypeStruct((B, MLA_HEADS * MLA_V, S), BF16),
        grid=(B, MLA_HEADS // g, S // tq),
        in_specs=[
            pl.BlockSpec((None, g * LANE, tq), lambda b, h, i: (b, h, i)),
            pl.BlockSpec((None, g, S, LANE), lambda b, h, i: (b, h, 0, 0)),
            pl.BlockSpec((None, S // LANE, g * MLA_V, LANE), lambda b, h, i: (b, 0, h, 0)),
        ],
        out_specs=pl.BlockSpec((None, g * MLA_V, tq), lambda b, h, i: (b, h, i)),
        scratch_shapes=[pltpu.VMEM((g, tq, tq), F32), pltpu.VMEM((g, tq, tq), F32),
                        pltpu.VMEM((g, 1, tq), F32),
                        pltpu.VMEM((g, MLA_V + BF16_ROWS, tq), F32)],
        compiler_params=pltpu.CompilerParams(
            dimension_semantics=("parallel", "parallel", "arbitrary"), vmem_limit_bytes=VMEM_LIMIT),
        name="mla",
    )(qat, ka, vat)


def _count_rows(mask):
    r, t = mask.shape
    return jnp.sum(mask.astype(jnp.int32).reshape(r // 8, 8, t), axis=0)


def _dsa_kernel(topk, qit_ref, wit_ref, ki_ref, qbt_ref, kb_ref, vbt_ref, o_ref,
                sc_sc, att0_sc, att1_sc, thr_sc, nge_sc, m_sc, acc_sc):
    tq = o_ref.shape[1]
    ck = CK_DSA
    sub = tq // ck
    i = pl.program_id(1)
    nblk = i + 1

    def rows(ref, r0, n):
        if isinstance(r0, int):
            return ref[r0:r0 + n, :]
        return ref[pl.ds(pl.multiple_of(r0, n), n), :]

    n_lg = tq // LANE

    def sc_load(r0, n):
        return jnp.concatenate([sc_sc[g, pl.ds(pl.multiple_of(r0, n), n), :] for g in range(n_lg)], axis=1)

    def sc_store(r0, n, val):
        for g in range(n_lg):
            sc_sc[g, pl.ds(pl.multiple_of(r0, n), n), :] = val[:, g * LANE:(g + 1) * LANE]

    qpos_c = i * tq + lax.broadcasted_iota(jnp.int32, (ck, tq), 1)
    krow_c = lax.broadcasted_iota(jnp.int32, (ck, tq), 0)

    def score_block(cb):
        for j in range(sub):
            r0 = cb * tq + j * ck
            kc = rows(ki_ref, r0, ck)
            acc = jnp.zeros((ck, tq), F32)
            for h in range(IDX_HEADS):
                rel = _dot(kc, qit_ref[h * LANE:(h + 1) * LANE, :])
                acc = acc + wit_ref[h:h + 1, :] * jnp.maximum(rel, 0.0)
            sc_store(r0, ck, jnp.where(r0 + krow_c <= qpos_c, acc, -jnp.inf))

    def score_pair(p, _):
        score_block(2 * p)
        score_block(2 * p + 1)
        return 0

    lax.fori_loop(0, nblk // 2, score_pair, 0)

    @pl.when(nblk % 2 == 1)
    def _():
        score_block(nblk - 1)

    krow = lax.broadcasted_iota(jnp.int32, (tq, tq), 0)

    def count(pred, nb=None):
        def body(cb, a):
            return a + _count_rows(pred(sc_load(cb * tq, tq), cb * tq + krow))
        part = jnp.zeros((8, tq), jnp.int32)
        if nb is None:
            part = lax.fori_loop(0, nblk, body, part)
        else:
            for cb in range(nb):
                part = body(cb, part)
        return jnp.sum(part, axis=0, keepdims=True)

    def flip(word):
        return jnp.where(word < 0, word ^ jnp.int32(0x7FFFFFFF), word)

    def key_to_score(key):
        return pltpu.bitcast(flip(key), F32)

    thr_sc[...] = jnp.full((1, tq), jnp.finfo(F32).min, F32)

    def bisect(nb):
        assert n_lg == 2
        la, lb = slice(0, LANE), slice(LANE, 2 * LANE)

        def part_count(group, cf):
            n_rows = (nb - 1) * tq + (group + 1) * LANE
            part = jnp.zeros((8, LANE), jnp.int32)
            for r0 in range(0, n_rows, tq):
                r1 = min(r0 + tq, n_rows)
                part = part + _count_rows(sc_sc[group, r0:r1, :] >= cf[0:1, :])
            return part

        def accept(part, cand, ans, n_ans):
            n = jnp.sum(part, axis=0, keepdims=True)
            ok = n >= topk
            return jnp.where(ok, cand, ans), jnp.where(ok, n, n_ans)

        def cand_of(t, ans):
            return jnp.where(t == 0, jnp.zeros_like(ans), ans + jnp.left_shift(jnp.int32(1), 31 - t))

        def body(t, st):
            ans_a, n_a, ans_b, n_b, cand_b, part_b = st
            ans_b, n_b = accept(part_b, cand_b, ans_b, n_b)
            cand_b = cand_of(t, ans_b)
            cand_a = cand_of(t, ans_a)
            ans_a, n_a = accept(part_count(0, key_to_score(cand_a)), cand_a, ans_a, n_a)
            return ans_a, n_a, ans_b, n_b, cand_b, part_count(1, key_to_score(cand_b))

        start = jnp.full((8, LANE), INT_MIN, jnp.int32)
        zero = jnp.zeros((8, LANE), jnp.int32)
        first = body(jnp.int32(0), (start, zero, start, zero, start, zero))
        ans_a, n_a, ans_b, n_b, cand_b, part_b = lax.fori_loop(1, 32, body, first)
        ans_b, n_b = accept(part_b, cand_b, ans_b, n_b)
        thr_sc[:, la] = key_to_score(ans_a)[0:1, :]
        thr_sc[:, lb] = key_to_score(ans_b)[0:1, :]
        nge_sc[:, la] = n_a[0:1, :]
        nge_sc[:, lb] = n_b[0:1, :]

    for v in range(topk // tq, sc_sc.shape[1] // tq):
        pl.when(i == v)(functools.partial(bisect, v + 1))

    @pl.when(i * tq >= topk)
    def _():
        thr = thr_sc[...]

        @pl.when(jnp.max(nge_sc[...]) > topk)
        def _():
            need = topk - count(lambda s, kidx: s > thr)

            def jbit(t, j):
                cand = j + jnp.left_shift(jnp.int32(1), 10 - t)
                n = count(lambda s, kidx: (s == thr) & (kidx < cand))
                return jnp.where(n < need, cand, j)

            last = lax.fori_loop(0, 11, jbit, jnp.zeros((1, tq), jnp.int32))

            def drop(cb, _):
                s = sc_load(cb * tq, tq)
                dropped = (s == thr) & (cb * tq + krow > last)
                sc_store(cb * tq, tq, jnp.where(dropped, -jnp.inf, s))
                return 0

            lax.fori_loop(0, nblk, drop, 0)

    thr = thr_sc[...]
    nh = DSA_HEADS
    qall = jnp.concatenate([qbt_ref[h * LANE:(h + 1) * LANE, :] for h in range(nh)], axis=1)
    bufs = (att0_sc, att1_sc)

    def logits(cb, buf):
        bias = jnp.where(sc_load(cb * tq, tq) >= thr, 0.0, NEG)
        for j in range(sub):
            s = _dot(rows(kb_ref, cb * tq + j * ck, ck), qall)
            b = bias[j * ck:(j + 1) * ck, :]
            for h in range(nh):
                buf[h, j * ck:(j + 1) * ck, :] = s[:, h * tq:(h + 1) * tq] + b

    def attend(cb, buf):
        vt = _with_ones_rows(jnp.concatenate([vbt_ref[cb * sub + j] for j in range(sub)], axis=1))
        for h in range(nh):
            s = buf[h]
            m_old = m_sc[h]
            m_new = jnp.maximum(m_old, jnp.max(s, axis=0, keepdims=True))
            a = jnp.exp2(m_old - m_new)
            p = jnp.exp2(s - m_new)
            m_sc[h] = m_new
            acc_sc[h] = a * acc_sc[h] + _dot(vt, p.astype(BF16))

    def step(k, parity):
        logits(k + 1, bufs[1 - parity])
        attend(k, bufs[parity])

    m_sc[...] = jnp.full(m_sc.shape, NEG, F32)
    acc_sc[...] = jnp.zeros_like(acc_sc)
    logits(0, bufs[0])

    def pair(p, _):
        step(2 * p, 0)
        step(2 * p + 1, 1)
        return 0

    lax.fori_loop(0, i // 2, pair, 0)

    @pl.when(i % 2 == 1)
    def _():
        step(i - 1, 0)
        attend(i, bufs[1])

    @pl.when(i % 2 == 0)
    def _():
        attend(i, bufs[0])

    for h in range(nh):
        acc = acc_sc[h]
        o_ref[h * DSA_DIM:(h + 1) * DSA_DIM, :] = (acc[:DSA_DIM] / acc[DSA_DIM:DSA_DIM + 1]).astype(BF16)


def _dsa(qit, wit, ki, qbt, kb, vbt, topk):
    B, _, S = qit.shape
    tq = TQ_DSA
    assert topk % tq == 0 or topk >= S, "query tiles must not straddle the top-k boundary"
    return pl.pallas_call(
        functools.partial(_dsa_kernel, topk),
        out_shape=jax.ShapeDtypeStruct((B, DSA_HEADS * DSA_DIM, S), BF16),
        grid=(B, S // tq),
        in_specs=[
            pl.BlockSpec((None, IDX_HEADS * LANE, tq), lambda b, i: (b, 0, i)),
            pl.BlockSpec((None, IDX_HEADS, tq), lambda b, i: (b, 0, i)),
            pl.BlockSpec((None, S, LANE), lambda b, i: (b, 0, 0)),
            pl.BlockSpec((None, DSA_HEADS * LANE, tq), lambda b, i: (b, 0, i)),
            pl.BlockSpec((None, S, LANE), lambda b, i: (b, 0, 0)),
            pl.BlockSpec((None, S // LANE, DSA_DIM, LANE), lambda b, i: (b, 0, 0, 0)),
        ],
        out_specs=pl.BlockSpec((None, DSA_HEADS * DSA_DIM, tq), lambda b, i: (b, 0, i)),
        scratch_shapes=[
            pltpu.VMEM((tq // LANE, S, LANE), F32),
            pltpu.VMEM((DSA_HEADS, tq, tq), F32),
            pltpu.VMEM((DSA_HEADS, tq, tq), F32),
            pltpu.VMEM((1, tq), F32),
            pltpu.VMEM((1, tq), jnp.int32),
            pltpu.VMEM((DSA_HEADS, 1, tq), F32),
            pltpu.VMEM((DSA_HEADS, DSA_DIM + BF16_ROWS, tq), F32),
        ],
        compiler_params=pltpu.CompilerParams(
            dimension_semantics=("parallel", "arbitrary"), vmem_limit_bytes=VMEM_LIMIT),
        name="dsa",
    )(qit, wit, ki, qbt, kb, vbt)


def _ret_kernel(rqt_ref, rk_ref, rvt_ref, rgt_ref, dt_ref, xi_ref, zeta_ref, cdec_ref, gret_ref, o_ref):
    c = dt_ref.shape[1]
    n_chunks = o_ref.shape[1] // c
    state = [jnp.zeros((RET_DV, LANE), F32) for _ in range(RET_HEADS)]
    for n in range(n_chunks):
        cols = slice(n * c, (n + 1) * c)
        for h in range(RET_HEADS):
            rt = state[h]
            qt = rqt_ref[h * LANE:(h + 1) * LANE, cols]
            k = rk_ref[cols, h * LANE:(h + 1) * LANE]
            vt = rvt_ref[h * RET_DV:(h + 1) * RET_DV, cols]
            inner_t = _dot(k, qt) * dt_ref[h]
            qx = (qt.astype(F32) * xi_ref[h:h + 1, :]).astype(BF16)
            out_t = _dot(vt, inner_t.astype(BF16)) + _dot(rt.astype(BF16), qx)
            vz = (vt.astype(F32) * zeta_ref[h:h + 1, :]).astype(BF16)
            state[h] = rt * cdec_ref[h] + _dot(vz, k)

            mu = jnp.mean(out_t, axis=0, keepdims=True)
            xc = out_t - mu
            var = jnp.mean(xc * xc, axis=0, keepdims=True)
            y = xc * lax.rsqrt(var + EPS) * gret_ref[h * RET_DV:(h + 1) * RET_DV, :]
            g = rgt_ref[h * RET_DV:(h + 1) * RET_DV, cols]
            o_ref[h * RET_DV:(h + 1) * RET_DV, cols] = (g * jax.nn.sigmoid(g) * y).astype(BF16)


def _retention(rqt, rk, rvt, rgt, dt, xi, zeta, cdec, gret):
    B, _, S = rqt.shape

    def full1(shape):
        nd = len(shape)
        return pl.BlockSpec(shape, lambda b: (0,) * nd)

    def per_batch(rows, cols):
        return pl.BlockSpec((None, rows, cols), lambda b: (b, 0, 0))

    return pl.pallas_call(
        _ret_kernel,
        out_shape=jax.ShapeDtypeStruct((B, RET_HEADS * RET_DV, S), BF16),
        grid=(B,),
        in_specs=[
            per_batch(RET_HEADS * LANE, S), per_batch(S, RET_HEADS * LANE),
            per_batch(RET_HEADS * RET_DV, S), per_batch(RET_HEADS * RET_DV, S),
            full1(dt.shape), full1(xi.shape), full1(zeta.shape), full1(cdec.shape), full1(gret.shape),
        ],
        out_specs=per_batch(RET_HEADS * RET_DV, S),
        compiler_params=pltpu.CompilerParams(
            dimension_semantics=("parallel",), vmem_limit_bytes=VMEM_LIMIT),
        name="retention",
    )(rqt, rk, rvt, rgt, dt, xi, zeta, cdec, gret)


def _mlp_kernel(final, x_ref, oat_ref, obt_ref, oct_ref, wout_ref, gmlp_ref, w1_ref, w2_ref, gfin_ref,
                y_ref):
    def tr(ref):
        return ref[...].astype(F32).T.astype(BF16)

    na = MLA_HEADS * MLA_V
    nb = DSA_HEADS * DSA_DIM
    attn = (_dot(tr(oat_ref), wout_ref[0:na, :])
            + _dot(tr(obt_ref), wout_ref[na:na + nb, :])
            + _dot(tr(oct_ref), wout_ref[na + nb:, :]))
    x1 = x_ref[...] + attn
    hf = _rms(x1, gmlp_ref[...]).astype(BF16)
    acc = x1
    for f in range(D_FF // FF_CHUNK):
        u = jnp.maximum(_dot(hf, w1_ref[:, f * FF_CHUNK:(f + 1) * FF_CHUNK]), 0.0)
        acc = acc + _dot((u * u).astype(BF16), w2_ref[f * FF_CHUNK:(f + 1) * FF_CHUNK, :])
    if final:
        acc = _rms(acc, gfin_ref[...])
    y_ref[...] = acc


def _mlp(x, oat, obt, oct, wout, gmlp, w1, w2, gfin, final):
    B, S, D = x.shape
    tm = TM_MLP

    def const(shape):
        nd = len(shape)
        return pl.BlockSpec(shape, lambda b, i: (0,) * nd, pipeline_mode=pl.Buffered(1))

    return pl.pallas_call(
        functools.partial(_mlp_kernel, final),
        out_shape=jax.ShapeDtypeStruct((B, S, D), F32),
        grid=(B, S // tm),
        in_specs=[
            pl.BlockSpec((None, tm, D), lambda b, i: (b, i, 0)),
            pl.BlockSpec((None, oat.shape[1], tm), lambda b, i: (b, 0, i)),
            pl.BlockSpec((None, obt.shape[1], tm), lambda b, i: (b, 0, i)),
            pl.BlockSpec((None, oct.shape[1], tm), lambda b, i: (b, 0, i)),
            const(wout.shape), const(gmlp.shape), const(w1.shape), const(w2.shape), const(gfin.shape),
        ],
        out_specs=pl.BlockSpec((None, tm, D), lambda b, i: (b, i, 0)),
        compiler_params=pltpu.CompilerParams(
            dimension_semantics=("parallel", "parallel"), vmem_limit_bytes=VMEM_LIMIT),
        name="mlp",
    )(x, oat, obt, oct, wout, gmlp, w1, w2, gfin)


def _rope_tables(seq, dim):
    pos = jnp.arange(seq, dtype=F32)
    inv = ROPE_THETA ** (-jnp.arange(0, dim, 2, dtype=F32) / dim)
    ang = pos[:, None] * inv[None, :]
    return jnp.cos(ang), jnp.sin(ang)


def _lane_table(c, s, lo, seq):
    half = c.shape[1]
    ct = jnp.zeros((seq, LANE), F32).at[:, lo:lo + 2 * half].set(jnp.concatenate([c, c], axis=1))
    st = jnp.zeros((seq, LANE), F32).at[:, lo:lo + 2 * half].set(jnp.concatenate([-s, s], axis=1))
    return ct, st


def _pad_cols(w, width):
    return jnp.pad(w, ((0, 0), (0, width - w.shape[1])))


def _prep_layer(w_in, w_uq, w_ukv):
    z = lambda n: jnp.zeros((D_MODEL, n), F32)
    col = lambda a, b: w_in[:, a:b]
    nat = [col(O_CQ, O_CKV), col(O_CKV, O_KR),
           jnp.concatenate([z(64), col(O_KR, O_AQ), z(32)], axis=1),
           _pad_cols(col(O_AK, O_AV), LANE),
           _pad_cols(col(O_KI, O_WI), LANE)]
    for h in range(RET_HEADS):
        nat.append(_pad_cols(col(O_RK + h * RET_DK, O_RK + (h + 1) * RET_DK), LANE))
    wnat = jnp.concatenate(nat, axis=1).astype(BF16)
    wtr = jnp.concatenate([col(O_AQ, O_AK), col(O_QI, O_KI), col(O_AV, O_QI), col(O_RQ, O_RK),
                           col(O_RV, O_RG), col(O_RG, O_END), col(O_WI, O_RQ)], axis=1).T.astype(BF16)
    dq = MLA_NOPE + MLA_ROPE
    wuq = jnp.pad(w_uq.reshape(MLA_Q_RANK, MLA_HEADS, dq), ((0, 0), (0, 0), (0, LANE - dq)))
    wuq = wuq.reshape(MLA_Q_RANK, MLA_HEADS * LANE).T.astype(BF16)
    wkv = w_ukv.reshape(MLA_KV_RANK, MLA_HEADS, MLA_NOPE + MLA_V)
    wuk = jnp.pad(wkv[:, :, :MLA_NOPE], ((0, 0), (0, 0), (0, LANE - MLA_NOPE)))
    wuk = wuk.reshape(MLA_KV_RANK, MLA_HEADS * LANE).astype(BF16)
    wuv = wkv[:, :, MLA_NOPE:].reshape(MLA_KV_RANK, MLA_HEADS * MLA_V).T.astype(BF16)
    return wnat, wtr, wuq, wuk, wuv


def kernel(x, g_mix, w_in, g_q, w_uq, g_kv, w_ukv, g_ret, w_out, g_mlp, w_ff1, w_ff2, g_final):
    B, S, _ = x.shape
    topk = min(DSA_TOPK_MAX, S // 4)

    cos32, sin32 = _rope_tables(S, 32)
    cos64, sin64 = _rope_tables(S, 64)
    tabt = jnp.concatenate([cos64.T, sin64.T, cos32.T, sin32.T], axis=0)
    tabn = jnp.stack(_lane_table(cos32, sin32, 64, S) + _lane_table(cos64, sin64, 0, S)
                     + _lane_table(cos32, sin32, 0, S))

    log_gamma = jnp.log1p(-jnp.exp2(-5.0 - jnp.arange(RET_HEADS, dtype=F32)))
    c = C_RET
    pos = jnp.arange(c, dtype=F32)
    rel = pos[None, :] - pos[:, None]
    dt = jnp.where(rel[None] >= 0, jnp.exp(jnp.maximum(rel, 0.0)[None] * log_gamma[:, None, None]), 0.0)
    xi = jnp.exp((pos + 1.0)[None, :] * log_gamma[:, None])
    zeta = jnp.exp((c - 1.0 - pos)[None, :] * log_gamma[:, None])
    cdec = jnp.broadcast_to(jnp.exp(c * log_gamma)[:, None, None], (RET_HEADS, RET_DV, LANE))

    for l in range(DEPTH):
        wnat, wtr, wuq, wuk, wuv = _prep_layer(w_in[l], w_uq[l], w_ukv[l])
        (qat, ka, vat, qit, wit, ki, qbt, kb, vbt, rqt, rk, rvt, rgt) = _proj(
            x, g_mix[l][None, :], g_q[l][None, :], g_kv[l][None, :], wnat, wtr, wuq, wuk, wuv, tabt, tabn)
        oat = _mla(qat, ka, vat)
        obt = _dsa(qit, wit, ki, qbt, kb, vbt, topk)
        gret = jnp.broadcast_to(g_ret[l][:, None], (RET_HEADS * RET_DV, c))
        oct = _retention(rqt, rk, rvt, rgt, dt, xi, zeta, cdec, gret)
        x = _mlp(x, oat, obt, oct, w_out[l].astype(BF16), g_mlp[l][None, :],
                 w_ff1[l].astype(BF16), w_ff2[l].astype(BF16), g_final[None, :], l == DEPTH - 1)
    return x
```

```python
import functools

import numpy as np
import jax
import jax.numpy as jnp
from jax import lax
from jax.experimental import pallas as pl
from jax.experimental.pallas import tpu as pltpu

D_MODEL = 1024
DEPTH = 2
MLA_HEADS = 8
MLA_NOPE = 64
MLA_ROPE = 32
MLA_V = 64
MLA_Q_RANK = 768
MLA_KV_RANK = 256
DSA_HEADS = 4
DSA_DIM = 64
IDX_HEADS = 8
IDX_DIM = 32
DSA_TOPK_MAX = 256
RET_HEADS = 4
RET_DK = 64
RET_DV = 64
D_FF = 4 * D_MODEL
ROPE_THETA = 10000.0
EPS = 1e-6

MLA_COLS = (MLA_Q_RANK, MLA_KV_RANK, MLA_ROPE)
DSA_COLS = (DSA_HEADS * DSA_DIM, DSA_DIM, DSA_DIM, IDX_HEADS * IDX_DIM, IDX_DIM, IDX_HEADS)
RET_COLS = (RET_HEADS * RET_DK, RET_HEADS * RET_DK, RET_HEADS * RET_DV, RET_HEADS * RET_DV)
SPLITS = MLA_COLS + DSA_COLS + RET_COLS
_OFF = np.concatenate([[0], np.cumsum(SPLITS)]).tolist()
(O_CQ, O_CKV, O_KR, O_AQ, O_AK, O_AV, O_QI, O_KI, O_WI, O_RQ, O_RK, O_RV, O_RG, O_END) = _OFF

LANE = 128
BF16_ROWS = 16
VMEM_LIMIT = 56 * 1024 * 1024

BF16 = jnp.bfloat16
F32 = jnp.float32
NEG = -1e30
LOG2E = float(np.log2(np.e))
INT_MIN = np.int32(-2**31)

N_CQ = 0
N_CKV = N_CQ + MLA_Q_RANK
N_KR = N_CKV + MLA_KV_RANK
N_AK = N_KR + LANE
N_KI = N_AK + LANE
N_RK = N_KI + LANE
N_NAT = N_RK + RET_HEADS * LANE
T_AQ = 0
T_QI = T_AQ + DSA_HEADS * DSA_DIM
T_AV = T_QI + IDX_HEADS * IDX_DIM
T_RQ = T_AV + DSA_DIM
T_RV = T_RQ + RET_HEADS * RET_DK
T_RG = T_RV + RET_HEADS * RET_DV
T_WI = T_RG + RET_HEADS * RET_DV
N_TR = T_WI + IDX_HEADS

TM_PROJ = 512
TQ_MLA = 256
G_MLA = 8
TQ_DSA = 512
KB_DSA = 256
UW_DSA = 256
CK_DSA = 128
C_RET = 256
TM_MLP = 512
FF_CHUNK = 1024

_NT = (((1,), (1,)), ((), ()))


def _rms(x, g):
    return x * lax.rsqrt(jnp.mean(x * x, axis=-1, keepdims=True) + EPS) * g


def _dot(a, b):
    return jnp.dot(a, b, preferred_element_type=F32)


def _dot_nt(a, b):
    return lax.dot_general(a, b, _NT, preferred_element_type=F32)


def _rope_rows(x1, x2, c, s):
    return x1 * c - x2 * s, x1 * s + x2 * c


def _rope_lanes(x, c, s_signed, half, first_lo):
    lane = lax.broadcasted_iota(jnp.int32, x.shape, 1)
    first = (lane >= first_lo) & (lane < first_lo + half)
    rot = jnp.where(first, pltpu.roll(x, LANE - half, 1), pltpu.roll(x, half, 1))
    return x * c + rot * s_signed


def _proj_kernel(x_ref, gmix_ref, gq_ref, gkv_ref, wnat_ref, wtr_ref, wuq_ref, wuk_ref, wuv_ref,
                 tabt_ref, tabn_ref,
                 qat_ref, ka_ref, vat_ref, qit_ref, wit_ref, ki_ref, qbt_ref, kb_ref, vbt_ref,
                 rqt_ref, rk_ref, rvt_ref, rgt_ref):
    tm = x_ref.shape[0]
    hb = _rms(x_ref[...], gmix_ref[...]).astype(BF16)

    c64t = tabt_ref[0:32, :]
    s64t = tabt_ref[32:64, :]
    c32t = tabt_ref[64:80, :]
    s32t = tabt_ref[80:96, :]

    cq = _dot(hb, wnat_ref[:, N_CQ:N_CQ + MLA_Q_RANK])
    nq = _rms(cq, gq_ref[...]).astype(BF16)
    qt = _dot_nt(wuq_ref[...], nq)
    scale_a = (MLA_NOPE + MLA_ROPE) ** -0.5 * LOG2E
    for h in range(MLA_HEADS):
        r0 = h * LANE
        qat_ref[r0:r0 + 64, :] = (qt[r0:r0 + 64, :] * scale_a).astype(BF16)
        o1, o2 = _rope_rows(qt[r0 + 64:r0 + 80, :], qt[r0 + 80:r0 + 96, :], c32t, s32t)
        qat_ref[r0 + 64:r0 + 80, :] = (o1 * scale_a).astype(BF16)
        qat_ref[r0 + 80:r0 + 96, :] = (o2 * scale_a).astype(BF16)
        qat_ref[r0 + 96:r0 + 128, :] = jnp.zeros((32, tm), BF16)

    ckv = _dot(hb, wnat_ref[:, N_CKV:N_CKV + MLA_KV_RANK])
    nkv = _rms(ckv, gkv_ref[...]).astype(BF16)
    kn = _dot(nkv, wuk_ref[...])
    keys = _dot(hb, wnat_ref[:, N_KR:N_NAT])
    slab = lambda off: keys[:, off - N_KR:off - N_KR + LANE]
    krp = _rope_lanes(slab(N_KR), tabn_ref[0], tabn_ref[1], 16, 64)
    for h in range(MLA_HEADS):
        ka_ref[h] = (kn[:, h * LANE:(h + 1) * LANE] + krp).astype(BF16)
    vt = _dot_nt(wuv_ref[...], nkv).astype(BF16)
    for j in range(tm // LANE):
        vat_ref[j] = vt[:, j * LANE:(j + 1) * LANE]

    kb_ref[...] = _rope_lanes(slab(N_AK), tabn_ref[2], tabn_ref[3], 32, 0).astype(BF16)
    ki_ref[...] = _rope_lanes(slab(N_KI), tabn_ref[4], tabn_ref[5], 16, 0).astype(BF16)

    for h in range(RET_HEADS):
        rkp = _rope_lanes(slab(N_RK + h * LANE), tabn_ref[2], tabn_ref[3], 32, 0) * (RET_DK ** -0.5)
        rk_ref[:, h * LANE:(h + 1) * LANE] = rkp.astype(BF16)

    aqt = _dot_nt(wtr_ref[T_AQ:T_AQ + 256, :], hb)
    scale_b = DSA_DIM ** -0.5 * LOG2E
    for h in range(DSA_HEADS):
        o1, o2 = _rope_rows(aqt[h * 64:h * 64 + 32, :], aqt[h * 64 + 32:h * 64 + 64, :], c64t, s64t)
        qbt_ref[h * LANE:h * LANE + 32, :] = (o1 * scale_b).astype(BF16)
        qbt_ref[h * LANE + 32:h * LANE + 64, :] = (o2 * scale_b).astype(BF16)
        qbt_ref[h * LANE + 64:(h + 1) * LANE, :] = jnp.zeros((64, tm), BF16)

    qit = _dot_nt(wtr_ref[T_QI:T_QI + 256, :], hb)
    for h in range(IDX_HEADS):
        o1, o2 = _rope_rows(qit[h * 32:h * 32 + 16, :], qit[h * 32 + 16:h * 32 + 32, :], c32t, s32t)
        qit_ref[h * LANE:h * LANE + 16, :] = o1.astype(BF16)
        qit_ref[h * LANE + 16:h * LANE + 32, :] = o2.astype(BF16)
        qit_ref[h * LANE + 32:(h + 1) * LANE, :] = jnp.zeros((96, tm), BF16)

    idx_scale = (IDX_HEADS ** -0.5) * (IDX_DIM ** -0.5)
    wit_ref[...] = _dot_nt(wtr_ref[T_WI:T_WI + IDX_HEADS, :], hb) * idx_scale

    avt = _dot_nt(wtr_ref[T_AV:T_AV + DSA_DIM, :], hb).astype(BF16)
    for j in range(tm // LANE):
        vbt_ref[j] = avt[:, j * LANE:(j + 1) * LANE]

    rqt = _dot_nt(wtr_ref[T_RQ:T_RQ + 256, :], hb)
    for h in range(RET_HEADS):
        o1, o2 = _rope_rows(rqt[h * 64:h * 64 + 32, :], rqt[h * 64 + 32:h * 64 + 64, :], c64t, s64t)
        rqt_ref[h * LANE:h * LANE + 32, :] = o1.astype(BF16)
        rqt_ref[h * LANE + 32:h * LANE + 64, :] = o2.astype(BF16)
        rqt_ref[h * LANE + 64:(h + 1) * LANE, :] = jnp.zeros((64, tm), BF16)

    rvt_ref[...] = _dot_nt(wtr_ref[T_RV:T_RV + 256, :], hb).astype(BF16)
    rgt_ref[...] = _dot_nt(wtr_ref[T_RG:T_RG + 256, :], hb)


def _full(shape):
    nd = len(shape)
    return pl.BlockSpec(shape, lambda b, i: (0,) * nd)


def _proj(x, gmix, gq, gkv, wnat, wtr, wuq, wuk, wuv, tabt, tabn):
    B, S, D = x.shape
    tm = TM_PROJ
    nch = tm // LANE
    outs = [
        (jax.ShapeDtypeStruct((B, MLA_HEADS * LANE, S), BF16), pl.BlockSpec((None, MLA_HEADS * LANE, tm), lambda b, i: (b, 0, i))),
        (jax.ShapeDtypeStruct((B, MLA_HEADS, S, LANE), BF16), pl.BlockSpec((None, MLA_HEADS, tm, LANE), lambda b, i: (b, 0, i, 0))),
        (jax.ShapeDtypeStruct((B, S // LANE, MLA_HEADS * MLA_V, LANE), BF16), pl.BlockSpec((None, nch, MLA_HEADS * MLA_V, LANE), lambda b, i: (b, i, 0, 0))),
        (jax.ShapeDtypeStruct((B, IDX_HEADS * LANE, S), BF16), pl.BlockSpec((None, IDX_HEADS * LANE, tm), lambda b, i: (b, 0, i))),
        (jax.ShapeDtypeStruct((B, IDX_HEADS, S), F32), pl.BlockSpec((None, IDX_HEADS, tm), lambda b, i: (b, 0, i))),
        (jax.ShapeDtypeStruct((B, S, LANE), BF16), pl.BlockSpec((None, tm, LANE), lambda b, i: (b, i, 0))),
        (jax.ShapeDtypeStruct((B, DSA_HEADS * LANE, S), BF16), pl.BlockSpec((None, DSA_HEADS * LANE, tm), lambda b, i: (b, 0, i))),
        (jax.ShapeDtypeStruct((B, S, LANE), BF16), pl.BlockSpec((None, tm, LANE), lambda b, i: (b, i, 0))),
        (jax.ShapeDtypeStruct((B, S // LANE, DSA_DIM, LANE), BF16), pl.BlockSpec((None, nch, DSA_DIM, LANE), lambda b, i: (b, i, 0, 0))),
        (jax.ShapeDtypeStruct((B, RET_HEADS * LANE, S), BF16), pl.BlockSpec((None, RET_HEADS * LANE, tm), lambda b, i: (b, 0, i))),
        (jax.ShapeDtypeStruct((B, S, RET_HEADS * LANE), BF16), pl.BlockSpec((None, tm, RET_HEADS * LANE), lambda b, i: (b, i, 0))),
        (jax.ShapeDtypeStruct((B, RET_HEADS * RET_DV, S), BF16), pl.BlockSpec((None, RET_HEADS * RET_DV, tm), lambda b, i: (b, 0, i))),
        (jax.ShapeDtypeStruct((B, RET_HEADS * RET_DV, S), F32), pl.BlockSpec((None, RET_HEADS * RET_DV, tm), lambda b, i: (b, 0, i))),
    ]
    in_specs = [
        pl.BlockSpec((None, tm, D), lambda b, i: (b, i, 0)),
        _full(gmix.shape), _full(gq.shape), _full(gkv.shape),
        _full(wnat.shape), _full(wtr.shape), _full(wuq.shape), _full(wuk.shape), _full(wuv.shape),
        pl.BlockSpec((tabt.shape[0], tm), lambda b, i: (0, i)),
        pl.BlockSpec((tabn.shape[0], tm, LANE), lambda b, i: (0, i, 0)),
    ]
    return pl.pallas_call(
        _proj_kernel,
        out_shape=[o[0] for o in outs],
        grid=(B, S // tm),
        in_specs=in_specs,
        out_specs=[o[1] for o in outs],
        compiler_params=pltpu.CompilerParams(
            dimension_semantics=("parallel", "parallel"), vmem_limit_bytes=VMEM_LIMIT),
        name="proj",
    )(x, gmix, gq, gkv, wnat, wtr, wuq, wuk, wuv, tabt, tabn)


def _with_ones_rows(vt):
    return jnp.concatenate([vt, jnp.ones((BF16_ROWS, vt.shape[1]), BF16)], axis=0)


def _mla_kernel(q_ref, k_ref, v_ref, o_ref, s0_sc, s1_sc, m_sc, acc_sc):
    g_heads = acc_sc.shape[0]
    tq = q_ref.shape[1]
    tk = tq
    i = pl.program_id(2)
    nv = tk // LANE
    kidx = lax.broadcasted_iota(jnp.int32, (tk, tq), 0)
    qidx = lax.broadcasted_iota(jnp.int32, (tk, tq), 1)
    bufs = (s0_sc, s1_sc)

    def scores(kt, buf, causal=None):
        for g in range(g_heads):
            s = _dot(k_ref[g, pl.ds(pl.multiple_of(kt * tk, tk), tk), :],
                     q_ref[g * LANE:(g + 1) * LANE, :])
            if causal is not None:
                s = jnp.where(causal, s, NEG)
            buf[g] = s

    def softmax(kt, buf):
        for g in range(g_heads):
            s = buf[g]
            m_old = m_sc[g]
            m_new = jnp.maximum(m_old, jnp.max(s, axis=0, keepdims=True))
            a = jnp.exp2(m_old - m_new)
            p = jnp.exp2(s - m_new)
            vt = jnp.concatenate([v_ref[kt * nv + j, g * MLA_V:(g + 1) * MLA_V, :] for j in range(nv)],
                                 axis=1)
            m_sc[g] = m_new
            acc_sc[g] = a * acc_sc[g] + _dot(_with_ones_rows(vt), p.astype(BF16))

    m_sc[...] = jnp.full(m_sc.shape, NEG, F32)
    acc_sc[...] = jnp.zeros_like(acc_sc)

    def step(k, parity, diagonal=False):
        scores(k + 1, bufs[1 - parity], (kidx <= qidx) if diagonal else None)
        softmax(k, bufs[parity])

    scores(0, bufs[0], kidx <= i * tq + qidx)

    def pair(p, _):
        step(2 * p, 0)
        step(2 * p + 1, 1)
        return 0

    lax.fori_loop(0, (i - 1) // 2, pair, 0)

    @pl.when(i % 2 == 0)
    def _():
        @pl.when(i >= 2)
        def _():
            step(i - 2, 0)
            step(i - 1, 1, diagonal=True)
        softmax(i, bufs[0])

    @pl.when(i % 2 == 1)
    def _():
        step(i - 1, 0, diagonal=True)
        softmax(i, bufs[1])

    for g in range(g_heads):
        acc = acc_sc[g]
        o_ref[g * MLA_V:(g + 1) * MLA_V, :] = (acc[:MLA_V] / acc[MLA_V:MLA_V + 1]).astype(BF16)


def _mla(qat, ka, vat):
    B, _, S = qat.shape
    tq = TQ_MLA
    g = G_MLA
    return pl.pallas_call(
        _mla_kernel,
        out_shape=jax.ShapeDtypeStruct((B, MLA_HEADS * MLA_V, S), BF16),
        grid=(B, MLA_HEADS // g, S // tq),
        in_specs=[
            pl.BlockSpec((None, g * LANE, tq), lambda b, h, i: (b, h, i)),
            pl.BlockSpec((None, g, S, LANE), lambda b, h, i: (b, h, 0, 0)),
            pl.BlockSpec((None, S // LANE, g * MLA_V, LANE), lambda b, h, i: (b, 0, h, 0)),
        ],
        out_specs=pl.BlockSpec((None, g * MLA_V, tq), lambda b, h, i: (b, h, i)),
        scratch_shapes=[pltpu.VMEM((g, tq, tq), F32), pltpu.VMEM((g, tq, tq), F32),
                        pltpu.VMEM((g, 1, tq), F32),
                        pltpu.VMEM((g, MLA_V + BF16_ROWS, tq), F32)],
        compiler_params=pltpu.CompilerParams(
            dimension_semantics=("parallel", "parallel", "arbitrary"), vmem_limit_bytes=VMEM_LIMIT),
        name="mla",
    )(qat, ka, vat)


def _count_rows(mask):
    r, t = mask.shape
    return jnp.sum(mask.astype(jnp.int32).reshape(r // 8, 8, t), axis=0)


def _dsa_kernel(topk, qit_ref, wit_ref, ki_ref, qbt_ref, kb_ref, vbt_ref, o_ref,
                sc_sc, att0_sc, att1_sc, thr_sc, nge_sc, m_sc, acc_sc):
    tq = o_ref.shape[1]
    ck = CK_DSA
    kbk = KB_DSA
    sub = kbk // ck
    i = pl.program_id(1)
    nblk = (i + 1) * (tq // kbk)

    def rows(ref, r0, n):
        if isinstance(r0, int):
            return ref[r0:r0 + n, :]
        return ref[pl.ds(pl.multiple_of(r0, n), n), :]

    n_lg = tq // LANE

    def sc_load(r0, n):
        return jnp.concatenate([sc_sc[g, pl.ds(pl.multiple_of(r0, n), n), :] for g in range(n_lg)], axis=1)

    def sc_store(r0, n, val):
        for g in range(n_lg):
            sc_sc[g, pl.ds(pl.multiple_of(r0, n), n), :] = val[:, g * LANE:(g + 1) * LANE]

    qpos_c = i * tq + lax.broadcasted_iota(jnp.int32, (ck, tq), 1)
    krow_c = lax.broadcasted_iota(jnp.int32, (ck, tq), 0)

    def score_block(cb):
        for j in range(sub):
            r0 = cb * kbk + j * ck
            kc = rows(ki_ref, r0, ck)
            acc = jnp.zeros((ck, tq), F32)
            for h in range(IDX_HEADS):
                rel = _dot(kc, qit_ref[h * LANE:(h + 1) * LANE, :])
                acc = acc + wit_ref[h:h + 1, :] * jnp.maximum(rel, 0.0)
            sc_store(r0, ck, jnp.where(r0 + krow_c <= qpos_c, acc, -jnp.inf))

    def score_pair(p, _):
        score_block(2 * p)
        score_block(2 * p + 1)
        return 0

    lax.fori_loop(0, nblk // 2, score_pair, 0)

    @pl.when(nblk % 2 == 1)
    def _():
        score_block(nblk - 1)

    krow = lax.broadcasted_iota(jnp.int32, (kbk, tq), 0)

    def count(pred):
        def body(cb, a):
            return a + _count_rows(pred(sc_load(cb * kbk, kbk), cb * kbk + krow))
        part = lax.fori_loop(0, nblk, body, jnp.zeros((8, tq), jnp.int32))
        return jnp.sum(part, axis=0, keepdims=True)

    def flip(word):
        return jnp.where(word < 0, word ^ jnp.int32(0x7FFFFFFF), word)

    def key_to_score(key):
        return pltpu.bitcast(flip(key), F32)

    thr_sc[...] = jnp.full((1, tq), jnp.finfo(F32).min, F32)
    nge_sc[...] = jnp.zeros((1, tq), jnp.int32)

    def bisect(tile, groups):
        def part_count(group, cf):
            n_rows = tile * tq + (group + 1) * LANE
            part = jnp.zeros((8, LANE), jnp.int32)
            for r0 in range(0, n_rows, kbk):
                r1 = min(r0 + kbk, n_rows)
                part = part + _count_rows(sc_sc[group, r0:r1, :] >= cf[0:1, :])
            return part

        def accept(part, cand, ans, n_ans):
            n = jnp.sum(part, axis=0, keepdims=True)
            ok = n >= topk
            return jnp.where(ok, cand, ans), jnp.where(ok, n, n_ans)

        def cand_of(t, ans):
            return jnp.where(t == 0, jnp.zeros_like(ans), ans + jnp.left_shift(jnp.int32(1), 31 - t))

        def body(t, st):
            ans, n, cand_z, part_z = st
            ans, n = list(ans), list(n)
            ans[-1], n[-1] = accept(part_z, cand_z, ans[-1], n[-1])
            cand_z = cand_of(t, ans[-1])
            for x, g in enumerate(groups[:-1]):
                cand = cand_of(t, ans[x])
                ans[x], n[x] = accept(part_count(g, key_to_score(cand)), cand, ans[x], n[x])
            return tuple(ans), tuple(n), cand_z, part_count(groups[-1], key_to_score(cand_z))

        start = jnp.full((8, LANE), INT_MIN, jnp.int32)
        zero = jnp.zeros((8, LANE), jnp.int32)
        first = body(jnp.int32(0), ((start,) * len(groups), (zero,) * len(groups), start, zero))
        ans, n, cand_z, part_z = lax.fori_loop(1, 32, body, first)
        ans, n = list(ans), list(n)
        ans[-1], n[-1] = accept(part_z, cand_z, ans[-1], n[-1])
        for x, g in enumerate(groups):
            thr_sc[:, g * LANE:(g + 1) * LANE] = key_to_score(ans[x])[0:1, :]
            nge_sc[:, g * LANE:(g + 1) * LANE] = n[x][0:1, :]

    for v in range(sc_sc.shape[1] // tq):
        groups = [g for g in range(n_lg) if v * tq + g * LANE >= topk]
        if groups:
            pl.when(i == v)(functools.partial(bisect, v, groups))

    @pl.when(jnp.max(nge_sc[...]) > topk)
    def _():
        thr = thr_sc[...]
        need = topk - count(lambda s, kidx: s > thr)

        idx_bits = (sc_sc.shape[1] - 1).bit_length()

        def jbit(t, j):
            cand = j + jnp.left_shift(jnp.int32(1), idx_bits - 1 - t)
            n = count(lambda s, kidx: (s == thr) & (kidx < cand))
            return jnp.where(n < need, cand, j)

        last = lax.fori_loop(0, idx_bits, jbit, jnp.zeros((1, tq), jnp.int32))

        def drop(cb, _):
            s = sc_load(cb * kbk, kbk)
            dropped = (s == thr) & (cb * kbk + krow > last)
            sc_store(cb * kbk, kbk, jnp.where(dropped, -jnp.inf, s))
            return 0

        lax.fori_loop(0, nblk, drop, 0)

    thr = thr_sc[...]
    nh = DSA_HEADS
    bufs = (att0_sc, att1_sc)
    uw = att0_sc.shape[2]
    n_units = nh * tq // uw

    def logits(cb, buf):
        for j in range(sub):
            kc = rows(kb_ref, cb * kbk + j * ck, ck)
            b = jnp.where(sc_load(cb * kbk + j * ck, ck) >= thr, 0.0, NEG)
            for h in range(nh):
                s = _dot(kc, qbt_ref[h * LANE:(h + 1) * LANE, :]) + b
                for c0 in range(0, tq, uw):
                    buf[(h * tq + c0) // uw, j * ck:(j + 1) * ck, :] = s[:, c0:c0 + uw]

    def attend(cb, buf):
        vt = _with_ones_rows(jnp.concatenate([vbt_ref[cb * sub + j] for j in range(sub)], axis=1))
        for u in range(n_units):
            s = buf[u]
            m_old = m_sc[u]
            m_new = jnp.maximum(m_old, jnp.max(s, axis=0, keepdims=True))
            a = jnp.exp2(m_old - m_new)
            p = jnp.exp2(s - m_new)
            m_sc[u] = m_new
            acc_sc[u] = a * acc_sc[u] + _dot(vt, p.astype(BF16))

    def step(k, parity):
        logits(k + 1, bufs[1 - parity])
        attend(k, bufs[parity])

    m_sc[...] = jnp.full(m_sc.shape, NEG, F32)
    acc_sc[...] = jnp.zeros_like(acc_sc)
    logits(0, bufs[0])

    def pair(p, _):
        step(2 * p, 0)
        step(2 * p + 1, 1)
        return 0

    assert (tq // kbk) % 2 == 0
    lax.fori_loop(0, nblk // 2 - 1, pair, 0)
    step(nblk - 2, 0)
    attend(nblk - 1, bufs[1])

    for u in range(n_units):
        h, c0 = divmod(u * uw, tq)
        acc = acc_sc[u]
        o_ref[h * DSA_DIM:(h + 1) * DSA_DIM, c0:c0 + uw] = (acc[:DSA_DIM] / acc[DSA_DIM:DSA_DIM + 1]).astype(BF16)


def _dsa(qit, wit, ki, qbt, kb, vbt, topk):
    B, _, S = qit.shape
    tq = TQ_DSA
    assert topk % LANE == 0, "128-lane query groups must not straddle the top-k boundary"
    units = DSA_HEADS * tq // UW_DSA
    return pl.pallas_call(
        functools.partial(_dsa_kernel, topk),
        out_shape=jax.ShapeDtypeStruct((B, DSA_HEADS * DSA_DIM, S), BF16),
        grid=(B, S // tq),
        in_specs=[
            pl.BlockSpec((None, IDX_HEADS * LANE, tq), lambda b, i: (b, 0, i)),
            pl.BlockSpec((None, IDX_HEADS, tq), lambda b, i: (b, 0, i)),
            pl.BlockSpec((None, S, LANE), lambda b, i: (b, 0, 0)),
            pl.BlockSpec((None, DSA_HEADS * LANE, tq), lambda b, i: (b, 0, i)),
            pl.BlockSpec((None, S, LANE), lambda b, i: (b, 0, 0)),
            pl.BlockSpec((None, S // LANE, DSA_DIM, LANE), lambda b, i: (b, 0, 0, 0)),
        ],
        out_specs=pl.BlockSpec((None, DSA_HEADS * DSA_DIM, tq), lambda b, i: (b, 0, i)),
        scratch_shapes=[
            pltpu.VMEM((tq // LANE, S, LANE), F32),
            pltpu.VMEM((units, KB_DSA, UW_DSA), F32),
            pltpu.VMEM((units, KB_DSA, UW_DSA), F32),
            pltpu.VMEM((1, tq), F32),
            pltpu.VMEM((1, tq), jnp.int32),
            pltpu.VMEM((units, 1, UW_DSA), F32),
            pltpu.VMEM((units, DSA_DIM + BF16_ROWS, UW_DSA), F32),
        ],
        compiler_params=pltpu.CompilerParams(
            dimension_semantics=("parallel", "arbitrary"), vmem_limit_bytes=VMEM_LIMIT),
        name="dsa",
    )(qit, wit, ki, qbt, kb, vbt)


def _ret_kernel(rqt_ref, rk_ref, rvt_ref, rgt_ref, dt_ref, xi_ref, zeta_ref, cdec_ref, gret_ref, o_ref):
    c = dt_ref.shape[1]
    n_chunks = o_ref.shape[1] // c
    state = [jnp.zeros((RET_DV, LANE), F32) for _ in range(RET_HEADS)]
    for n in range(n_chunks):
        cols = slice(n * c, (n + 1) * c)
        for h in range(RET_HEADS):
            rt = state[h]
            qt = rqt_ref[h * LANE:(h + 1) * LANE, cols]
            k = rk_ref[cols, h * LANE:(h + 1) * LANE]
            vt = rvt_ref[h * RET_DV:(h + 1) * RET_DV, cols]
            inner_t = _dot(k, qt) * dt_ref[h]
            qx = (qt.astype(F32) * xi_ref[h:h + 1, :]).astype(BF16)
            out_t = _dot(vt, inner_t.astype(BF16)) + _dot(rt.astype(BF16), qx)
            vz = (vt.astype(F32) * zeta_ref[h:h + 1, :]).astype(BF16)
            state[h] = rt * cdec_ref[h] + _dot(vz, k)

            mu = jnp.mean(out_t, axis=0, keepdims=True)
            xc = out_t - mu
            var = jnp.mean(xc * xc, axis=0, keepdims=True)
            y = xc * lax.rsqrt(var + EPS) * gret_ref[h * RET_DV:(h + 1) * RET_DV, :]
            g = rgt_ref[h * RET_DV:(h + 1) * RET_DV, cols]
            o_ref[h * RET_DV:(h + 1) * RET_DV, cols] = (g * jax.nn.sigmoid(g) * y).astype(BF16)


def _retention(rqt, rk, rvt, rgt, dt, xi, zeta, cdec, gret):
    B, _, S = rqt.shape

    def full1(shape):
        nd = len(shape)
        return pl.BlockSpec(shape, lambda b: (0,) * nd)

    def per_batch(rows, cols):
        return pl.BlockSpec((None, rows, cols), lambda b: (b, 0, 0))

    return pl.pallas_call(
        _ret_kernel,
        out_shape=jax.ShapeDtypeStruct((B, RET_HEADS * RET_DV, S), BF16),
        grid=(B,),
        in_specs=[
            per_batch(RET_HEADS * LANE, S), per_batch(S, RET_HEADS * LANE),
            per_batch(RET_HEADS * RET_DV, S), per_batch(RET_HEADS * RET_DV, S),
            full1(dt.shape), full1(xi.shape), full1(zeta.shape), full1(cdec.shape), full1(gret.shape),
        ],
        out_specs=per_batch(RET_HEADS * RET_DV, S),
        compiler_params=pltpu.CompilerParams(
            dimension_semantics=("parallel",), vmem_limit_bytes=VMEM_LIMIT),
        name="retention",
    )(rqt, rk, rvt, rgt, dt, xi, zeta, cdec, gret)


def _mlp_kernel(final, x_ref, oat_ref, obt_ref, oct_ref, wout_ref, gmlp_ref, w1_ref, w2_ref, gfin_ref,
                y_ref):
    def tr(ref):
        return ref[...].astype(F32).T.astype(BF16)

    na = MLA_HEADS * MLA_V
    nb = DSA_HEADS * DSA_DIM
    attn = (_dot(tr(oat_ref), wout_ref[0:na, :])
            + _dot(tr(obt_ref), wout_ref[na:na + nb, :])
            + _dot(tr(oct_ref), wout_ref[na + nb:, :]))
    x1 = x_ref[...] + attn
    hf = _rms(x1, gmlp_ref[...]).astype(BF16)
    acc = x1
    for f in range(D_FF // FF_CHUNK):
        u = jnp.maximum(_dot(hf, w1_ref[:, f * FF_CHUNK:(f + 1) * FF_CHUNK]), 0.0)
        acc = acc + _dot((u * u).astype(BF16), w2_ref[f * FF_CHUNK:(f + 1) * FF_CHUNK, :])
    if final:
        acc = _rms(acc, gfin_ref[...])
    y_ref[...] = acc


def _mlp(x, oat, obt, oct, wout, gmlp, w1, w2, gfin, final):
    B, S, D = x.shape
    tm = TM_MLP

    def const(shape):
        nd = len(shape)
        return pl.BlockSpec(shape, lambda b, i: (0,) * nd, pipeline_mode=pl.Buffered(1))

    return pl.pallas_call(
        functools.partial(_mlp_kernel, final),
        out_shape=jax.ShapeDtypeStruct((B, S, D), F32),
        grid=(B, S // tm),
        in_specs=[
            pl.BlockSpec((None, tm, D), lambda b, i: (b, i, 0)),
            pl.BlockSpec((None, oat.shape[1], tm), lambda b, i: (b, 0, i)),
            pl.BlockSpec((None, obt.shape[1], tm), lambda b, i: (b, 0, i)),
            pl.BlockSpec((None, oct.shape[1], tm), lambda b, i: (b, 0, i)),
            const(wout.shape), const(gmlp.shape), const(w1.shape), const(w2.shape), const(gfin.shape),
        ],
        out_specs=pl.BlockSpec((None, tm, D), lambda b, i: (b, i, 0)),
        compiler_params=pltpu.CompilerParams(
            dimension_semantics=("parallel", "parallel"), vmem_limit_bytes=VMEM_LIMIT),
        name="mlp",
    )(x, oat, obt, oct, wout, gmlp, w1, w2, gfin)


def _rope_tables(seq, dim):
    pos = jnp.arange(seq, dtype=F32)
    inv = ROPE_THETA ** (-jnp.arange(0, dim, 2, dtype=F32) / dim)
    ang = pos[:, None] * inv[None, :]
    return jnp.cos(ang), jnp.sin(ang)


def _lane_table(c, s, lo, seq):
    half = c.shape[1]
    ct = jnp.zeros((seq, LANE), F32).at[:, lo:lo + 2 * half].set(jnp.concatenate([c, c], axis=1))
    st = jnp.zeros((seq, LANE), F32).at[:, lo:lo + 2 * half].set(jnp.concatenate([-s, s], axis=1))
    return ct, st


def _pad_cols(w, width):
    return jnp.pad(w, ((0, 0), (0, width - w.shape[1])))


def _prep_layer(w_in, w_uq, w_ukv):
    z = lambda n: jnp.zeros((D_MODEL, n), F32)
    col = lambda a, b: w_in[:, a:b]
    nat = [col(O_CQ, O_CKV), col(O_CKV, O_KR),
           jnp.concatenate([z(64), col(O_KR, O_AQ), z(32)], axis=1),
           _pad_cols(col(O_AK, O_AV), LANE),
           _pad_cols(col(O_KI, O_WI), LANE)]
    for h in range(RET_HEADS):
        nat.append(_pad_cols(col(O_RK + h * RET_DK, O_RK + (h + 1) * RET_DK), LANE))
    wnat = jnp.concatenate(nat, axis=1).astype(BF16)
    wtr = jnp.concatenate([col(O_AQ, O_AK), col(O_QI, O_KI), col(O_AV, O_QI), col(O_RQ, O_RK),
                           col(O_RV, O_RG), col(O_RG, O_END), col(O_WI, O_RQ)], axis=1).T.astype(BF16)
    dq = MLA_NOPE + MLA_ROPE
    wuq = jnp.pad(w_uq.reshape(MLA_Q_RANK, MLA_HEADS, dq), ((0, 0), (0, 0), (0, LANE - dq)))
    wuq = wuq.reshape(MLA_Q_RANK, MLA_HEADS * LANE).T.astype(BF16)
    wkv = w_ukv.reshape(MLA_KV_RANK, MLA_HEADS, MLA_NOPE + MLA_V)
    wuk = jnp.pad(wkv[:, :, :MLA_NOPE], ((0, 0), (0, 0), (0, LANE - MLA_NOPE)))
    wuk = wuk.reshape(MLA_KV_RANK, MLA_HEADS * LANE).astype(BF16)
    wuv = wkv[:, :, MLA_NOPE:].reshape(MLA_KV_RANK, MLA_HEADS * MLA_V).T.astype(BF16)
    return wnat, wtr, wuq, wuk, wuv


def kernel(x, g_mix, w_in, g_q, w_uq, g_kv, w_ukv, g_ret, w_out, g_mlp, w_ff1, w_ff2, g_final):
    B, S, _ = x.shape
    topk = min(DSA_TOPK_MAX, S // 4)

    cos32, sin32 = _rope_tables(S, 32)
    cos64, sin64 = _rope_tables(S, 64)
    tabt = jnp.concatenate([cos64.T, sin64.T, cos32.T, sin32.T], axis=0)
    tabn = jnp.stack(_lane_table(cos32, sin32, 64, S) + _lane_table(cos64, sin64, 0, S)
                     + _lane_table(cos32, sin32, 0, S))

    log_gamma = jnp.log1p(-jnp.exp2(-5.0 - jnp.arange(RET_HEADS, dtype=F32)))
    c = C_RET
    pos = jnp.arange(c, dtype=F32)
    rel = pos[None, :] - pos[:, None]
    dt = jnp.where(rel[None] >= 0, jnp.exp(jnp.maximum(rel, 0.0)[None] * log_gamma[:, None, None]), 0.0)
    xi = jnp.exp((pos + 1.0)[None, :] * log_gamma[:, None])
    zeta = jnp.exp((c - 1.0 - pos)[None, :] * log_gamma[:, None])
    cdec = jnp.broadcast_to(jnp.exp(c * log_gamma)[:, None, None], (RET_HEADS, RET_DV, LANE))

    for l in range(DEPTH):
        wnat, wtr, wuq, wuk, wuv = _prep_layer(w_in[l], w_uq[l], w_ukv[l])
        (qat, ka, vat, qit, wit, ki, qbt, kb, vbt, rqt, rk, rvt, rgt) = _proj(
            x, g_mix[l][None, :], g_q[l][None, :], g_kv[l][None, :], wnat, wtr, wuq, wuk, wuv, tabt, tabn)
        oat = _mla(qat, ka, vat)
        obt = _dsa(qit, wit, ki, qbt, kb, vbt, topk)
        gret = jnp.broadcast_to(g_ret[l][:, None], (RET_HEADS * RET_DV, c))
        oct = _retention(rqt, rk, rvt, rgt, dt, xi, zeta, cdec, gret)
        x = _mlp(x, oat, obt, oct, w_out[l].astype(BF16), g_mlp[l][None, :],
                 w_ff1[l].astype(BF16), w_ff2[l].astype(BF16), g_final[None, :], l == DEPTH - 1)
    return x
```

```python
import functools

import numpy as np
import jax
import jax.numpy as jnp
from jax import lax
from jax.experimental import pallas as pl
from jax.experimental.pallas import tpu as pltpu

D_MODEL = 1024
DEPTH = 2
MLA_HEADS = 8
MLA_NOPE = 64
MLA_ROPE = 32
MLA_V = 64
MLA_Q_RANK = 768
MLA_KV_RANK = 256
DSA_HEADS = 4
DSA_DIM = 64
IDX_HEADS = 8
IDX_DIM = 32
DSA_TOPK_MAX = 256
RET_HEADS = 4
RET_DK = 64
RET_DV = 64
D_FF = 4 * D_MODEL
ROPE_THETA = 10000.0
EPS = 1e-6

MLA_COLS = (MLA_Q_RANK, MLA_KV_RANK, MLA_ROPE)
DSA_COLS = (DSA_HEADS * DSA_DIM, DSA_DIM, DSA_DIM, IDX_HEADS * IDX_DIM, IDX_DIM, IDX_HEADS)
RET_COLS = (RET_HEADS * RET_DK, RET_HEADS * RET_DK, RET_HEADS * RET_DV, RET_HEADS * RET_DV)
SPLITS = MLA_COLS + DSA_COLS + RET_COLS
_OFF = np.concatenate([[0], np.cumsum(SPLITS)]).tolist()
(O_CQ, O_CKV, O_KR, O_AQ, O_AK, O_AV, O_QI, O_KI, O_WI, O_RQ, O_RK, O_RV, O_RG, O_END) = _OFF

LANE = 128
BF16_ROWS = 16
VMEM_LIMIT = 56 * 1024 * 1024

BF16 = jnp.bfloat16
F32 = jnp.float32
NEG = -1e30
LOG2E = float(np.log2(np.e))
INT_MIN = np.int32(-2**31)

N_CQ = 0
N_CKV = N_CQ + MLA_Q_RANK
N_KR = N_CKV + MLA_KV_RANK
N_AK = N_KR + LANE
N_KI = N_AK + LANE
N_RK = N_KI + LANE
N_NAT = N_RK + RET_HEADS * LANE
T_AQ = 0
T_QI = T_AQ + DSA_HEADS * DSA_DIM
T_AV = T_QI + IDX_HEADS * IDX_DIM
T_RQ = T_AV + DSA_DIM
T_RV = T_RQ + RET_HEADS * RET_DK
T_RG = T_RV + RET_HEADS * RET_DV
T_WI = T_RG + RET_HEADS * RET_DV
N_TR = T_WI + IDX_HEADS

TM_PROJ = 512
TQ_MLA = 256
G_MLA = 8
TQ_DSA = 256
CK_DSA = 128
C_RET = 256
TM_MLP = 512
FF_CHUNK = 1024

_NT = (((1,), (1,)), ((), ()))


def _rms(x, g):
    return x * lax.rsqrt(jnp.mean(x * x, axis=-1, keepdims=True) + EPS) * g


def _dot(a, b):
    return jnp.dot(a, b, preferred_element_type=F32)


def _dot_nt(a, b):
    return lax.dot_general(a, b, _NT, preferred_element_type=F32)


def _rope_rows(x1, x2, c, s):
    return x1 * c - x2 * s, x1 * s + x2 * c


def _rope_lanes(x, c, s_signed, half, first_lo):
    lane = lax.broadcasted_iota(jnp.int32, x.shape, 1)
    first = (lane >= first_lo) & (lane < first_lo + half)
    rot = jnp.where(first, pltpu.roll(x, LANE - half, 1), pltpu.roll(x, half, 1))
    return x * c + rot * s_signed


def _proj_kernel(x_ref, gmix_ref, gq_ref, gkv_ref, wnat_ref, wtr_ref, wuq_ref, wuk_ref, wuv_ref,
                 tabt_ref, tabn_ref,
                 qat_ref, ka_ref, vat_ref, qit_ref, wit_ref, ki_ref, qbt_ref, kb_ref, vbt_ref,
                 rqt_ref, rk_ref, rvt_ref, rgt_ref):
    tm = x_ref.shape[0]
    hb = _rms(x_ref[...], gmix_ref[...]).astype(BF16)

    c64t = tabt_ref[0:32, :]
    s64t = tabt_ref[32:64, :]
    c32t = tabt_ref[64:80, :]
    s32t = tabt_ref[80:96, :]

    cq = _dot(hb, wnat_ref[:, N_CQ:N_CQ + MLA_Q_RANK])
    nq = _rms(cq, gq_ref[...]).astype(BF16)
    qt = _dot_nt(wuq_ref[...], nq)
    scale_a = (MLA_NOPE + MLA_ROPE) ** -0.5 * LOG2E
    for h in range(MLA_HEADS):
        r0 = h * LANE
        qat_ref[r0:r0 + 64, :] = (qt[r0:r0 + 64, :] * scale_a).astype(BF16)
        o1, o2 = _rope_rows(qt[r0 + 64:r0 + 80, :], qt[r0 + 80:r0 + 96, :], c32t, s32t)
        qat_ref[r0 + 64:r0 + 80, :] = (o1 * scale_a).astype(BF16)
        qat_ref[r0 + 80:r0 + 96, :] = (o2 * scale_a).astype(BF16)
        qat_ref[r0 + 96:r0 + 128, :] = jnp.zeros((32, tm), BF16)

    ckv = _dot(hb, wnat_ref[:, N_CKV:N_CKV + MLA_KV_RANK])
    nkv = _rms(ckv, gkv_ref[...]).astype(BF16)
    kn = _dot(nkv, wuk_ref[...])
    keys = _dot(hb, wnat_ref[:, N_KR:N_NAT])
    slab = lambda off: keys[:, off - N_KR:off - N_KR + LANE]
    krp = _rope_lanes(slab(N_KR), tabn_ref[0], tabn_ref[1], 16, 64)
    for h in range(MLA_HEADS):
        ka_ref[h] = (kn[:, h * LANE:(h + 1) * LANE] + krp).astype(BF16)
    vt = _dot_nt(wuv_ref[...], nkv).astype(BF16)
    for j in range(tm // LANE):
        vat_ref[j] = vt[:, j * LANE:(j + 1) * LANE]

    kb_ref[...] = _rope_lanes(slab(N_AK), tabn_ref[2], tabn_ref[3], 32, 0).astype(BF16)
    ki_ref[...] = _rope_lanes(slab(N_KI), tabn_ref[4], tabn_ref[5], 16, 0).astype(BF16)

    for h in range(RET_HEADS):
        rkp = _rope_lanes(slab(N_RK + h * LANE), tabn_ref[2], tabn_ref[3], 32, 0) * (RET_DK ** -0.5)
        rk_ref[:, h * LANE:(h + 1) * LANE] = rkp.astype(BF16)

    aqt = _dot_nt(wtr_ref[T_AQ:T_AQ + 256, :], hb)
    scale_b = DSA_DIM ** -0.5 * LOG2E
    for h in range(DSA_HEADS):
        o1, o2 = _rope_rows(aqt[h * 64:h * 64 + 32, :], aqt[h * 64 + 32:h * 64 + 64, :], c64t, s64t)
        qbt_ref[h * LANE:h * LANE + 32, :] = (o1 * scale_b).astype(BF16)
        qbt_ref[h * LANE + 32:h * LANE + 64, :] = (o2 * scale_b).astype(BF16)
        qbt_ref[h * LANE + 64:(h + 1) * LANE, :] = jnp.zeros((64, tm), BF16)

    qit = _dot_nt(wtr_ref[T_QI:T_QI + 256, :], hb)
    for h in range(IDX_HEADS):
        o1, o2 = _rope_rows(qit[h * 32:h * 32 + 16, :], qit[h * 32 + 16:h * 32 + 32, :], c32t, s32t)
        qit_ref[h * LANE:h * LANE + 16, :] = o1.astype(BF16)
        qit_ref[h * LANE + 16:h * LANE + 32, :] = o2.astype(BF16)
        qit_ref[h * LANE + 32:(h + 1) * LANE, :] = jnp.zeros((96, tm), BF16)

    idx_scale = (IDX_HEADS ** -0.5) * (IDX_DIM ** -0.5)
    wit_ref[...] = _dot_nt(wtr_ref[T_WI:T_WI + IDX_HEADS, :], hb) * idx_scale

    avt = _dot_nt(wtr_ref[T_AV:T_AV + DSA_DIM, :], hb).astype(BF16)
    for j in range(tm // LANE):
        vbt_ref[j] = avt[:, j * LANE:(j + 1) * LANE]

    rqt = _dot_nt(wtr_ref[T_RQ:T_RQ + 256, :], hb)
    for h in range(RET_HEADS):
        o1, o2 = _rope_rows(rqt[h * 64:h * 64 + 32, :], rqt[h * 64 + 32:h * 64 + 64, :], c64t, s64t)
        rqt_ref[h * LANE:h * LANE + 32, :] = o1.astype(BF16)
        rqt_ref[h * LANE + 32:h * LANE + 64, :] = o2.astype(BF16)
        rqt_ref[h * LANE + 64:(h + 1) * LANE, :] = jnp.zeros((64, tm), BF16)

    rvt_ref[...] = _dot_nt(wtr_ref[T_RV:T_RV + 256, :], hb).astype(BF16)
    rgt_ref[...] = _dot_nt(wtr_ref[T_RG:T_RG + 256, :], hb)


def _full(shape):
    nd = len(shape)
    return pl.BlockSpec(shape, lambda b, i: (0,) * nd)


def _proj(x, gmix, gq, gkv, wnat, wtr, wuq, wuk, wuv, tabt, tabn):
    B, S, D = x.shape
    tm = TM_PROJ
    nch = tm // LANE
    outs = [
        (jax.ShapeDtypeStruct((B, MLA_HEADS * LANE, S), BF16), pl.BlockSpec((None, MLA_HEADS * LANE, tm), lambda b, i: (b, 0, i))),
        (jax.ShapeDtypeStruct((B, MLA_HEADS, S, LANE), BF16), pl.BlockSpec((None, MLA_HEADS, tm, LANE), lambda b, i: (b, 0, i, 0))),
        (jax.ShapeDtypeStruct((B, S // LANE, MLA_HEADS * MLA_V, LANE), BF16), pl.BlockSpec((None, nch, MLA_HEADS * MLA_V, LANE), lambda b, i: (b, i, 0, 0))),
        (jax.ShapeDtypeStruct((B, IDX_HEADS * LANE, S), BF16), pl.BlockSpec((None, IDX_HEADS * LANE, tm), lambda b, i: (b, 0, i))),
        (jax.ShapeDtypeStruct((B, IDX_HEADS, S), F32), pl.BlockSpec((None, IDX_HEADS, tm), lambda b, i: (b, 0, i))),
        (jax.ShapeDtypeStruct((B, S, LANE), BF16), pl.BlockSpec((None, tm, LANE), lambda b, i: (b, i, 0))),
        (jax.ShapeDtypeStruct((B, DSA_HEADS * LANE, S), BF16), pl.BlockSpec((None, DSA_HEADS * LANE, tm), lambda b, i: (b, 0, i))),
        (jax.ShapeDtypeStruct((B, S, LANE), BF16), pl.BlockSpec((None, tm, LANE), lambda b, i: (b, i, 0))),
        (jax.ShapeDtypeStruct((B, S // LANE, DSA_DIM, LANE), BF16), pl.BlockSpec((None, nch, DSA_DIM, LANE), lambda b, i: (b, i, 0, 0))),
        (jax.ShapeDtypeStruct((B, RET_HEADS * LANE, S), BF16), pl.BlockSpec((None, RET_HEADS * LANE, tm), lambda b, i: (b, 0, i))),
        (jax.ShapeDtypeStruct((B, S, RET_HEADS * LANE), BF16), pl.BlockSpec((None, tm, RET_HEADS * LANE), lambda b, i: (b, i, 0))),
        (jax.ShapeDtypeStruct((B, RET_HEADS * RET_DV, S), BF16), pl.BlockSpec((None, RET_HEADS * RET_DV, tm), lambda b, i: (b, 0, i))),
        (jax.ShapeDtypeStruct((B, RET_HEADS * RET_DV, S), F32), pl.BlockSpec((None, RET_HEADS * RET_DV, tm), lambda b, i: (b, 0, i))),
    ]
    in_specs = [
        pl.BlockSpec((None, tm, D), lambda b, i: (b, i, 0)),
        _full(gmix.shape), _full(gq.shape), _full(gkv.shape),
        _full(wnat.shape), _full(wtr.shape), _full(wuq.shape), _full(wuk.shape), _full(wuv.shape),
        pl.BlockSpec((tabt.shape[0], tm), lambda b, i: (0, i)),
        pl.BlockSpec((tabn.shape[0], tm, LANE), lambda b, i: (0, i, 0)),
    ]
    return pl.pallas_call(
        _proj_kernel,
        out_shape=[o[0] for o in outs],
        grid=(B, S // tm),
        in_specs=in_specs,
        out_specs=[o[1] for o in outs],
        compiler_params=pltpu.CompilerParams(
            dimension_semantics=("parallel", "parallel"), vmem_limit_bytes=VMEM_LIMIT),
        name="proj",
    )(x, gmix, gq, gkv, wnat, wtr, wuq, wuk, wuv, tabt, tabn)


def _with_ones_rows(vt):
    return jnp.concatenate([vt, jnp.ones((BF16_ROWS, vt.shape[1]), BF16)], axis=0)


def _mla_kernel(q_ref, k_ref, v_ref, o_ref, s0_sc, s1_sc, mt_sc, m_sc, acc_sc):
    g_heads = acc_sc.shape[0]
    tq = q_ref.shape[1]
    tk = tq
    i = pl.program_id(2)
    nv = tk // LANE
    kidx = lax.broadcasted_iota(jnp.int32, (tk, tq), 0)
    qidx = lax.broadcasted_iota(jnp.int32, (tk, tq), 1)
    bufs = (s0_sc, s1_sc)

    def scores(kt, parity, causal=None):
        for g in range(g_heads):
            s = _dot(k_ref[g, pl.ds(pl.multiple_of(kt * tk, tk), tk), :],
                     q_ref[g * LANE:(g + 1) * LANE, :])
            if causal is not None:
                s = jnp.where(causal, s, NEG)
            bufs[parity][g] = s
            mt_sc[parity, g] = jnp.max(s, axis=0, keepdims=True)

    def softmax(kt, parity):
        for g in range(g_heads):
            m_old = m_sc[g]
            m_new = jnp.maximum(m_old, mt_sc[parity, g])
            a = jnp.exp2(m_old - m_new)
            p = jnp.exp2(bufs[parity][g] - m_new)
            vt = jnp.concatenate([v_ref[kt * nv + j, g * MLA_V:(g + 1) * MLA_V, :] for j in range(nv)],
                                 axis=1)
            m_sc[g] = m_new
            acc_sc[g] = a * acc_sc[g] + _dot(_with_ones_rows(vt), p.astype(BF16))

    m_sc[...] = jnp.full(m_sc.shape, NEG, F32)
    acc_sc[...] = jnp.zeros_like(acc_sc)

    def step(k, parity, diagonal=False):
        scores(k + 1, 1 - parity, (kidx <= qidx) if diagonal else None)
        softmax(k, parity)

    scores(0, 0, kidx <= i * tq + qidx)

    def pair(p, _):
        step(2 * p, 0)
        step(2 * p + 1, 1)
        return 0

    lax.fori_loop(0, (i - 1) // 2, pair, 0)

    @pl.when(i % 2 == 0)
    def _():
        @pl.when(i >= 2)
        def _():
            step(i - 2, 0)
            step(i - 1, 1, diagonal=True)
        softmax(i, 0)

    @pl.when(i % 2 == 1)
    def _():
        step(i - 1, 0, diagonal=True)
        softmax(i, 1)

    for g in range(g_heads):
        acc = acc_sc[g]
        o_ref[g * MLA_V:(g + 1) * MLA_V, :] = (acc[:MLA_V] / acc[MLA_V:MLA_V + 1]).astype(BF16)


def _mla(qat, ka, vat):
    B, _, S = qat.shape
    tq = TQ_MLA
    g = G_MLA
    return pl.pallas_call(
        _mla_kernel,
        out_shape=jax.ShapeDtypeStruct((B, MLA_HEADS * MLA_V, S), BF16),
        grid=(B, MLA_HEADS // g, S // tq),
        in_specs=[
            pl.BlockSpec((None, g * LANE, tq), lambda b, h, i: (b, h, i)),
            pl.BlockSpec((None, g, S, LANE), lambda b, h, i: (b, h, 0, 0)),
            pl.BlockSpec((None, S // LANE, g * MLA_V, LANE), lambda b, h, i: (b, 0, h, 0)),
        ],
        out_specs=pl.BlockSpec((None, g * MLA_V, tq), lambda b, h, i: (b, h, i)),
        scratch_shapes=[pltpu.VMEM((g, tq, tq), F32), pltpu.VMEM((g, tq, tq), F32),
                        pltpu.VMEM((2, g, 1, tq), F32), pltpu.VMEM((g, 1, tq), F32),
                        pltpu.VMEM((g, MLA_V + BF16_ROWS, tq), F32)],
        compiler_params=pltpu.CompilerParams(
            dimension_semantics=("parallel", "parallel", "arbitrary"), vmem_limit_bytes=VMEM_LIMIT),
        name="mla",
    )(qat, ka, vat)


def _count_rows(mask):
    r, t = mask.shape
    return jnp.sum(mask.astype(jnp.int32).reshape(r // 8, 8, t), axis=0)


def _dsa_kernel(topk, qit_ref, wit_ref, ki_ref, qbt_ref, kb_ref, vbt_ref, o_ref,
                sc_sc, att0_sc, att1_sc, thr_sc, nge_sc, mt_sc, m_sc, acc_sc):
    tq = o_ref.shape[1]
    ck = CK_DSA
    sub = tq // ck
    i = pl.program_id(1)
    nblk = i + 1

    def rows(ref, r0, n):
        if isinstance(r0, int):
            return ref[r0:r0 + n, :]
        return ref[pl.ds(pl.multiple_of(r0, n), n), :]

    n_lg = tq // LANE

    def sc_load(r0, n):
        return jnp.concatenate([sc_sc[g, pl.ds(pl.multiple_of(r0, n), n), :] for g in range(n_lg)], axis=1)

    def sc_store(r0, n, val):
        for g in range(n_lg):
            sc_sc[g, pl.ds(pl.multiple_of(r0, n), n), :] = val[:, g * LANE:(g + 1) * LANE]

    qpos_c = i * tq + lax.broadcasted_iota(jnp.int32, (ck, tq), 1)
    krow_c = lax.broadcasted_iota(jnp.int32, (ck, tq), 0)

    def score_block(cb):
        for j in range(sub):
            r0 = cb * tq + j * ck
            kc = rows(ki_ref, r0, ck)
            acc = jnp.zeros((ck, tq), F32)
            for h in range(IDX_HEADS):
                rel = _dot(kc, qit_ref[h * LANE:(h + 1) * LANE, :])
                acc = acc + wit_ref[h:h + 1, :] * jnp.maximum(rel, 0.0)
            sc_store(r0, ck, jnp.where(r0 + krow_c <= qpos_c, acc, -jnp.inf))

    def score_pair(p, _):
        score_block(2 * p)
        score_block(2 * p + 1)
        return 0

    lax.fori_loop(0, nblk // 2, score_pair, 0)

    @pl.when(nblk % 2 == 1)
    def _():
        score_block(nblk - 1)

    krow = lax.broadcasted_iota(jnp.int32, (tq, tq), 0)

    def count(pred, nb=None):
        def body(cb, a):
            return a + _count_rows(pred(sc_load(cb * tq, tq), cb * tq + krow))
        part = jnp.zeros((8, tq), jnp.int32)
        if nb is None:
            part = lax.fori_loop(0, nblk, body, part)
        else:
            for cb in range(nb):
                part = body(cb, part)
        return jnp.sum(part, axis=0, keepdims=True)

    def flip(word):
        return jnp.where(word < 0, word ^ jnp.int32(0x7FFFFFFF), word)

    def key_to_score(key):
        return pltpu.bitcast(flip(key), F32)

    thr_sc[...] = jnp.full((1, tq), jnp.finfo(F32).min, F32)

    def bisect(nb):
        assert n_lg == 2
        la, lb = slice(0, LANE), slice(LANE, 2 * LANE)

        def part_count(group, cf):
            n_rows = (nb - 1) * tq + (group + 1) * LANE
            part = jnp.zeros((8, LANE), jnp.int32)
            for r0 in range(0, n_rows, tq):
                r1 = min(r0 + tq, n_rows)
                part = part + _count_rows(sc_sc[group, r0:r1, :] >= cf[0:1, :])
            return part

        def accept(part, cand, ans, n_ans):
            n = jnp.sum(part, axis=0, keepdims=True)
            ok = n >= topk
            return jnp.where(ok, cand, ans), jnp.where(ok, n, n_ans)

        def cand_of(t, ans):
            return jnp.where(t == 0, jnp.zeros_like(ans), ans + jnp.left_shift(jnp.int32(1), 31 - t))

        def body(t, st):
            ans_a, n_a, ans_b, n_b, cand_b, part_b = st
            ans_b, n_b = accept(part_b, cand_b, ans_b, n_b)
            cand_b = cand_of(t, ans_b)
            cand_a = cand_of(t, ans_a)
            ans_a, n_a = accept(part_count(0, key_to_score(cand_a)), cand_a, ans_a, n_a)
            return ans_a, n_a, ans_b, n_b, cand_b, part_count(1, key_to_score(cand_b))

        start = jnp.full((8, LANE), INT_MIN, jnp.int32)
        zero = jnp.zeros((8, LANE), jnp.int32)
        first = body(jnp.int32(0), (start, zero, start, zero, start, zero))
        ans_a, n_a, ans_b, n_b, cand_b, part_b = lax.fori_loop(1, 32, body, first)
        ans_b, n_b = accept(part_b, cand_b, ans_b, n_b)
        thr_sc[:, la] = key_to_score(ans_a)[0:1, :]
        thr_sc[:, lb] = key_to_score(ans_b)[0:1, :]
        nge_sc[:, la] = n_a[0:1, :]
        nge_sc[:, lb] = n_b[0:1, :]

    for v in range(topk // tq, sc_sc.shape[1] // tq):
        pl.when(i == v)(functools.partial(bisect, v + 1))

    @pl.when(i * tq >= topk)
    def _():
        thr = thr_sc[...]

        @pl.when(jnp.max(nge_sc[...]) > topk)
        def _():
            need = topk - count(lambda s, kidx: s > thr)

            def jbit(t, j):
                cand = j + jnp.left_shift(jnp.int32(1), 10 - t)
                n = count(lambda s, kidx: (s == thr) & (kidx < cand))
                return jnp.where(n < need, cand, j)

            last = lax.fori_loop(0, 11, jbit, jnp.zeros((1, tq), jnp.int32))

            def drop(cb, _):
                s = sc_load(cb * tq, tq)
                dropped = (s == thr) & (cb * tq + krow > last)
                sc_store(cb * tq, tq, jnp.where(dropped, -jnp.inf, s))
                return 0

            lax.fori_loop(0, nblk, drop, 0)

    thr = thr_sc[...]
    nh = DSA_HEADS
    qall = jnp.concatenate([qbt_ref[h * LANE:(h + 1) * LANE, :] for h in range(nh)], axis=1)
    bufs = (att0_sc, att1_sc)

    def logits(cb, parity):
        bias = jnp.where(sc_load(cb * tq, tq) >= thr, 0.0, NEG)
        mt = [jnp.full((1, tq), NEG, F32)] * nh
        for j in range(sub):
            s = _dot(rows(kb_ref, cb * tq + j * ck, ck), qall)
            b = bias[j * ck:(j + 1) * ck, :]
            for h in range(nh):
                sh = s[:, h * tq:(h + 1) * tq] + b
                bufs[parity][h, j * ck:(j + 1) * ck, :] = sh
                mt[h] = jnp.maximum(mt[h], jnp.max(sh, axis=0, keepdims=True))
        for h in range(nh):
            mt_sc[parity, h] = mt[h]

    def attend(cb, parity):
        vt = _with_ones_rows(jnp.concatenate([vbt_ref[cb * sub + j] for j in range(sub)], axis=1))
        for h in range(nh):
            m_old = m_sc[h]
            m_new = jnp.maximum(m_old, mt_sc[parity, h])
            a = jnp.exp2(m_old - m_new)
            p = jnp.exp2(bufs[parity][h] - m_new)
            m_sc[h] = m_new
            acc_sc[h] = a * acc_sc[h] + _dot(vt, p.astype(BF16))

    def step(k, parity):
        logits(k + 1, 1 - parity)
        attend(k, parity)

    m_sc[...] = jnp.full(m_sc.shape, NEG, F32)
    acc_sc[...] = jnp.zeros_like(acc_sc)
    logits(0, 0)

    def pair(p, _):
        step(2 * p, 0)
        step(2 * p + 1, 1)
        return 0

    lax.fori_loop(0, i // 2, pair, 0)

    @pl.when(i % 2 == 1)
    def _():
        step(i - 1, 0)
        attend(i, 1)

    @pl.when(i % 2 == 0)
    def _():
        attend(i, 0)

    for h in range(nh):
        acc = acc_sc[h]
        o_ref[h * DSA_DIM:(h + 1) * DSA_DIM, :] = (acc[:DSA_DIM] / acc[DSA_DIM:DSA_DIM + 1]).astype(BF16)


def _dsa(qit, wit, ki, qbt, kb, vbt, topk):
    B, _, S = qit.shape
    tq = TQ_DSA
    assert topk % tq == 0 or topk >= S, "query tiles must not straddle the top-k boundary"
    return pl.pallas_call(
        functools.partial(_dsa_kernel, topk),
        out_shape=jax.ShapeDtypeStruct((B, DSA_HEADS * DSA_DIM, S), BF16),
        grid=(B, S // tq),
        in_specs=[
            pl.BlockSpec((None, IDX_HEADS * LANE, tq), lambda b, i: (b, 0, i)),
            pl.BlockSpec((None, IDX_HEADS, tq), lambda b, i: (b, 0, i)),
            pl.BlockSpec((None, S, LANE), lambda b, i: (b, 0, 0)),
            pl.BlockSpec((None, DSA_HEADS * LANE, tq), lambda b, i: (b, 0, i)),
            pl.BlockSpec((None, S, LANE), lambda b, i: (b, 0, 0)),
            pl.BlockSpec((None, S // LANE, DSA_DIM, LANE), lambda b, i: (b, 0, 0, 0)),
        ],
        out_specs=pl.BlockSpec((None, DSA_HEADS * DSA_DIM, tq), lambda b, i: (b, 0, i)),
        scratch_shapes=[
            pltpu.VMEM((tq // LANE, S, LANE), F32),
            pltpu.VMEM((DSA_HEADS, tq, tq), F32),
            pltpu.VMEM((DSA_HEADS, tq, tq), F32),
            pltpu.VMEM((1, tq), F32),
            pltpu.VMEM((1, tq), jnp.int32),
            pltpu.VMEM((2, DSA_HEADS, 1, tq), F32),
            pltpu.VMEM((DSA_HEADS, 1, tq), F32),
            pltpu.VMEM((DSA_HEADS, DSA_DIM + BF16_ROWS, tq), F32),
        ],
        compiler_params=pltpu.CompilerParams(
            dimension_semantics=("parallel", "arbitrary"), vmem_limit_bytes=VMEM_LIMIT),
        name="dsa",
    )(qit, wit, ki, qbt, kb, vbt)


def _ret_kernel(rqt_ref, rk_ref, rvt_ref, rgt_ref, dt_ref, xi_ref, zeta_ref, cdec_ref, gret_ref, o_ref):
    c = dt_ref.shape[1]
    n_chunks = o_ref.shape[1] // c
    state = [jnp.zeros((RET_DV, LANE), F32) for _ in range(RET_HEADS)]
    for n in range(n_chunks):
        cols = slice(n * c, (n + 1) * c)
        for h in range(RET_HEADS):
            rt = state[h]
            qt = rqt_ref[h * LANE:(h + 1) * LANE, cols]
            k = rk_ref[cols, h * LANE:(h + 1) * LANE]
            vt = rvt_ref[h * RET_DV:(h + 1) * RET_DV, cols]
            inner_t = _dot(k, qt) * dt_ref[h]
            qx = (qt.astype(F32) * xi_ref[h:h + 1, :]).astype(BF16)
            out_t = _dot(vt, inner_t.astype(BF16)) + _dot(rt.astype(BF16), qx)
            vz = (vt.astype(F32) * zeta_ref[h:h + 1, :]).astype(BF16)
            state[h] = rt * cdec_ref[h] + _dot(vz, k)

            mu = jnp.mean(out_t, axis=0, keepdims=True)
            xc = out_t - mu
            var = jnp.mean(xc * xc, axis=0, keepdims=True)
            y = xc * lax.rsqrt(var + EPS) * gret_ref[h * RET_DV:(h + 1) * RET_DV, :]
            g = rgt_ref[h * RET_DV:(h + 1) * RET_DV, cols]
            o_ref[h * RET_DV:(h + 1) * RET_DV, cols] = (g * jax.nn.sigmoid(g) * y).astype(BF16)


def _retention(rqt, rk, rvt, rgt, dt, xi, zeta, cdec, gret):
    B, _, S = rqt.shape

    def full1(shape):
        nd = len(shape)
        return pl.BlockSpec(shape, lambda b: (0,) * nd)

    def per_batch(rows, cols):
        return pl.BlockSpec((None, rows, cols), lambda b: (b, 0, 0))

    return pl.pallas_call(
        _ret_kernel,
        out_shape=jax.ShapeDtypeStruct((B, RET_HEADS * RET_DV, S), BF16),
        grid=(B,),
        in_specs=[
            per_batch(RET_HEADS * LANE, S), per_batch(S, RET_HEADS * LANE),
            per_batch(RET_HEADS * RET_DV, S), per_batch(RET_HEADS * RET_DV, S),
            full1(dt.shape), full1(xi.shape), full1(zeta.shape), full1(cdec.shape), full1(gret.shape),
        ],
        out_specs=per_batch(RET_HEADS * RET_DV, S),
        compiler_params=pltpu.CompilerParams(
            dimension_semantics=("parallel",), vmem_limit_bytes=VMEM_LIMIT),
        name="retention",
    )(rqt, rk, rvt, rgt, dt, xi, zeta, cdec, gret)


def _mlp_kernel(final, x_ref, oat_ref, obt_ref, oct_ref, wout_ref, gmlp_ref, w1_ref, w2_ref, gfin_ref,
                y_ref):
    def tr(ref):
        return ref[...].astype(F32).T.astype(BF16)

    na = MLA_HEADS * MLA_V
    nb = DSA_HEADS * DSA_DIM
    attn = (_dot(tr(oat_ref), wout_ref[0:na, :])
            + _dot(tr(obt_ref), wout_ref[na:na + nb, :])
            + _dot(tr(oct_ref), wout_ref[na + nb:, :]))
    x1 = x_ref[...] + attn
    hf = _rms(x1, gmlp_ref[...]).astype(BF16)
    acc = x1
    for f in range(D_FF // FF_CHUNK):
        u = jnp.maximum(_dot(hf, w1_ref[:, f * FF_CHUNK:(f + 1) * FF_CHUNK]), 0.0)
        acc = acc + _dot((u * u).astype(BF16), w2_ref[f * FF_CHUNK:(f + 1) * FF_CHUNK, :])
    if final:
        acc = _rms(acc, gfin_ref[...])
    y_ref[...] = acc


def _mlp(x, oat, obt, oct, wout, gmlp, w1, w2, gfin, final):
    B, S, D = x.shape
    tm = TM_MLP

    def const(shape):
        nd = len(shape)
        return pl.BlockSpec(shape, lambda b, i: (0,) * nd, pipeline_mode=pl.Buffered(1))

    return pl.pallas_call(
        functools.partial(_mlp_kernel, final),
        out_shape=jax.ShapeDtypeStruct((B, S, D), F32),
        grid=(B, S // tm),
        in_specs=[
            pl.BlockSpec((None, tm, D), lambda b, i: (b, i, 0)),
            pl.BlockSpec((None, oat.shape[1], tm), lambda b, i: (b, 0, i)),
            pl.BlockSpec((None, obt.shape[1], tm), lambda b, i: (b, 0, i)),
            pl.BlockSpec((None, oct.shape[1], tm), lambda b, i: (b, 0, i)),
            const(wout.shape), const(gmlp.shape), const(w1.shape), const(w2.shape), const(gfin.shape),
        ],
        out_specs=pl.BlockSpec((None, tm, D), lambda b, i: (b, i, 0)),
        compiler_params=pltpu.CompilerParams(
            dimension_semantics=("parallel", "parallel"), vmem_limit_bytes=VMEM_LIMIT),
        name="mlp",
    )(x, oat, obt, oct, wout, gmlp, w1, w2, gfin)


def _rope_tables(seq, dim):
    pos = jnp.arange(seq, dtype=F32)
    inv = ROPE_THETA ** (-jnp.arange(0, dim, 2, dtype=F32) / dim)
    ang = pos[:, None] * inv[None, :]
    return jnp.cos(ang), jnp.sin(ang)


def _lane_table(c, s, lo, seq):
    half = c.shape[1]
    ct = jnp.zeros((seq, LANE), F32).at[:, lo:lo + 2 * half].set(jnp.concatenate([c, c], axis=1))
    st = jnp.zeros((seq, LANE), F32).at[:, lo:lo + 2 * half].set(jnp.concatenate([-s, s], axis=1))
    return ct, st


def _pad_cols(w, width):
    return jnp.pad(w, ((0, 0), (0, width - w.shape[1])))


def _prep_layer(w_in, w_uq, w_ukv):
    z = lambda n: jnp.zeros((D_MODEL, n), F32)
    col = lambda a, b: w_in[:, a:b]
    nat = [col(O_CQ, O_CKV), col(O_CKV, O_KR),
           jnp.concatenate([z(64), col(O_KR, O_AQ), z(32)], axis=1),
           _pad_cols(col(O_AK, O_AV), LANE),
           _pad_cols(col(O_KI, O_WI), LANE)]
    for h in range(RET_HEADS):
        nat.append(_pad_cols(col(O_RK + h * RET_DK, O_RK + (h + 1) * RET_DK), LANE))
    wnat = jnp.concatenate(nat, axis=1).astype(BF16)
    wtr = jnp.concatenate([col(O_AQ, O_AK), col(O_QI, O_KI), col(O_AV, O_QI), col(O_RQ, O_RK),
                           col(O_RV, O_RG), col(O_RG, O_END), col(O_WI, O_RQ)], axis=1).T.astype(BF16)
    dq = MLA_NOPE + MLA_ROPE
    wuq = jnp.pad(w_uq.reshape(MLA_Q_RANK, MLA_HEADS, dq), ((0, 0), (0, 0), (0, LANE - dq)))
    wuq = wuq.reshape(MLA_Q_RANK, MLA_HEADS * LANE).T.astype(BF16)
    wkv = w_ukv.reshape(MLA_KV_RANK, MLA_HEADS, MLA_NOPE + MLA_V)
    wuk = jnp.pad(wkv[:, :, :MLA_NOPE], ((0, 0), (0, 0), (0, LANE - MLA_NOPE)))
    wuk = wuk.reshape(MLA_KV_RANK, MLA_HEADS * LANE).astype(BF16)
    wuv = wkv[:, :, MLA_NOPE:].reshape(MLA_KV_RANK, MLA_HEADS * MLA_V).T.astype(BF16)
    return wnat, wtr, wuq, wuk, wuv


def kernel(x, g_mix, w_in, g_q, w_uq, g_kv, w_ukv, g_ret, w_out, g_mlp, w_ff1, w_ff2, g_final):
    B, S, _ = x.shape
    topk = min(DSA_TOPK_MAX, S // 4)

    cos32, sin32 = _rope_tables(S, 32)
    cos64, sin64 = _rope_tables(S, 64)
    tabt = jnp.concatenate([cos64.T, sin64.T, cos32.T, sin32.T], axis=0)
    tabn = jnp.stack(_lane_table(cos32, sin32, 64, S) + _lane_table(cos64, sin64, 0, S)
                     + _lane_table(cos32, sin32, 0, S))

    log_gamma = jnp.log1p(-jnp.exp2(-5.0 - jnp.arange(RET_HEADS, dtype=F32)))
    c = C_RET
    pos = jnp.arange(c, dtype=F32)
    rel = pos[None, :] - pos[:, None]
    dt = jnp.where(rel[None] >= 0, jnp.exp(jnp.maximum(rel, 0.0)[None] * log_gamma[:, None, None]), 0.0)
    xi = jnp.exp((pos + 1.0)[None, :] * log_gamma[:, None])
    zeta = jnp.exp((c - 1.0 - pos)[None, :] * log_gamma[:, None])
    cdec = jnp.broadcast_to(jnp.exp(c * log_gamma)[:, None, None], (RET_HEADS, RET_DV, LANE))

    for l in range(DEPTH):
        wnat, wtr, wuq, wuk, wuv = _prep_layer(w_in[l], w_uq[l], w_ukv[l])
        (qat, ka, vat, qit, wit, ki, qbt, kb, vbt, rqt, rk, rvt, rgt) = _proj(
            x, g_mix[l][None, :], g_q[l][None, :], g_kv[l][None, :], wnat, wtr, wuq, wuk, wuv, tabt, tabn)
        oat = _mla(qat, ka, vat)
        obt = _dsa(qit, wit, ki, qbt, kb, vbt, topk)
        gret = jnp.broadcast_to(g_ret[l][:, None], (RET_HEADS * RET_DV, c))
        oct = _retention(rqt, rk, rvt, rgt, dt, xi, zeta, cdec, gret)
        x = _mlp(x, oat, obt, oct, w_out[l].astype(BF16), g_mlp[l][None, :],
                 w_ff1[l].astype(BF16), w_ff2[l].astype(BF16), g_final[None, :], l == DEPTH - 1)
    return x
```

```python
import functools

import numpy as np
import jax
import jax.numpy as jnp
from jax import lax
from jax.experimental import pallas as pl
from jax.experimental.pallas import tpu as pltpu

D_MODEL = 1024
DEPTH = 2
MLA_HEADS = 8
MLA_NOPE = 64
MLA_ROPE = 32
MLA_V = 64
MLA_Q_RANK = 768
MLA_KV_RANK = 256
DSA_HEADS = 4
DSA_DIM = 64
IDX_HEADS = 8
IDX_DIM = 32
DSA_TOPK_MAX = 256
RET_HEADS = 4
RET_DK = 64
RET_DV = 64
D_FF = 4 * D_MODEL
ROPE_THETA = 10000.0
EPS = 1e-6

MLA_COLS = (MLA_Q_RANK, MLA_KV_RANK, MLA_ROPE)
DSA_COLS = (DSA_HEADS * DSA_DIM, DSA_DIM, DSA_DIM, IDX_HEADS * IDX_DIM, IDX_DIM, IDX_HEADS)
RET_COLS = (RET_HEADS * RET_DK, RET_HEADS * RET_DK, RET_HEADS * RET_DV, RET_HEADS * RET_DV)
SPLITS = MLA_COLS + DSA_COLS + RET_COLS
_OFF = np.concatenate([[0], np.cumsum(SPLITS)]).tolist()
(O_CQ, O_CKV, O_KR, O_AQ, O_AK, O_AV, O_QI, O_KI, O_WI, O_RQ, O_RK, O_RV, O_RG, O_END) = _OFF

LANE = 128
BF16_ROWS = 16
VMEM_LIMIT = 56 * 1024 * 1024

BF16 = jnp.bfloat16
F32 = jnp.float32
NEG = -1e30
LOG2E = float(np.log2(np.e))
INT_MIN = np.int32(-2**31)

N_CQ = 0
N_CKV = N_CQ + MLA_Q_RANK
N_KR = N_CKV + MLA_KV_RANK
N_AK = N_KR + LANE
N_KI = N_AK + LANE
N_RK = N_KI + LANE
N_NAT = N_RK + RET_HEADS * LANE
T_AQ = 0
T_QI = T_AQ + DSA_HEADS * DSA_DIM
T_AV = T_QI + IDX_HEADS * IDX_DIM
T_RQ = T_AV + DSA_DIM
T_RV = T_RQ + RET_HEADS * RET_DK
T_RG = T_RV + RET_HEADS * RET_DV
T_WI = T_RG + RET_HEADS * RET_DV
N_TR = T_WI + IDX_HEADS

TM_PROJ = 512
TQ_MLA = 256
G_MLA = 8
TQ_DSA = 256
CK_DSA = 128
C_RET = 256
TM_MLP = 512
FF_CHUNK = 1024

_NT = (((1,), (1,)), ((), ()))


def _rms(x, g):
    return x * lax.rsqrt(jnp.mean(x * x, axis=-1, keepdims=True) + EPS) * g


def _dot(a, b):
    return jnp.dot(a, b, preferred_element_type=F32)


def _dot_nt(a, b):
    return lax.dot_general(a, b, _NT, preferred_element_type=F32)


def _rope_rows(x1, x2, c, s):
    return x1 * c - x2 * s, x1 * s + x2 * c


def _rope_lanes(x, c, s_signed, half, first_lo):
    lane = lax.broadcasted_iota(jnp.int32, x.shape, 1)
    first = (lane >= first_lo) & (lane < first_lo + half)
    rot = jnp.where(first, pltpu.roll(x, LANE - half, 1), pltpu.roll(x, half, 1))
    return x * c + rot * s_signed


def _proj_kernel(x_ref, gmix_ref, gq_ref, gkv_ref, wnat_ref, wtr_ref, wuq_ref, wuk_ref, wuv_ref,
                 tabt_ref, tabn_ref,
                 qat_ref, ka_ref, vat_ref, qit_ref, wit_ref, ki_ref, qbt_ref, kb_ref, vbt_ref,
                 rqt_ref, rk_ref, rvt_ref, rgt_ref):
    tm = x_ref.shape[0]
    hb = _rms(x_ref[...], gmix_ref[...]).astype(BF16)

    c64t = tabt_ref[0:32, :]
    s64t = tabt_ref[32:64, :]
    c32t = tabt_ref[64:80, :]
    s32t = tabt_ref[80:96, :]

    cq = _dot(hb, wnat_ref[:, N_CQ:N_CQ + MLA_Q_RANK])
    nq = _rms(cq, gq_ref[...]).astype(BF16)
    qt = _dot_nt(wuq_ref[...], nq)
    scale_a = (MLA_NOPE + MLA_ROPE) ** -0.5 * LOG2E
    for h in range(MLA_HEADS):
        r0 = h * LANE
        qat_ref[r0:r0 + 64, :] = (qt[r0:r0 + 64, :] * scale_a).astype(BF16)
        o1, o2 = _rope_rows(qt[r0 + 64:r0 + 80, :], qt[r0 + 80:r0 + 96, :], c32t, s32t)
        qat_ref[r0 + 64:r0 + 80, :] = (o1 * scale_a).astype(BF16)
        qat_ref[r0 + 80:r0 + 96, :] = (o2 * scale_a).astype(BF16)
        qat_ref[r0 + 96:r0 + 128, :] = jnp.zeros((32, tm), BF16)

    ckv = _dot(hb, wnat_ref[:, N_CKV:N_CKV + MLA_KV_RANK])
    nkv = _rms(ckv, gkv_ref[...]).astype(BF16)
    kn = _dot(nkv, wuk_ref[...])
    keys = _dot(hb, wnat_ref[:, N_KR:N_NAT])
    slab = lambda off: keys[:, off - N_KR:off - N_KR + LANE]
    krp = _rope_lanes(slab(N_KR), tabn_ref[0], tabn_ref[1], 16, 64)
    for h in range(MLA_HEADS):
        ka_ref[h] = (kn[:, h * LANE:(h + 1) * LANE] + krp).astype(BF16)
    vt = _dot_nt(wuv_ref[...], nkv).astype(BF16)
    for j in range(tm // LANE):
        vat_ref[j] = vt[:, j * LANE:(j + 1) * LANE]

    kb_ref[...] = _rope_lanes(slab(N_AK), tabn_ref[2], tabn_ref[3], 32, 0).astype(BF16)
    ki_ref[...] = _rope_lanes(slab(N_KI), tabn_ref[4], tabn_ref[5], 16, 0).astype(BF16)

    for h in range(RET_HEADS):
        rkp = _rope_lanes(slab(N_RK + h * LANE), tabn_ref[2], tabn_ref[3], 32, 0) * (RET_DK ** -0.5)
        rk_ref[:, h * LANE:(h + 1) * LANE] = rkp.astype(BF16)

    aqt = _dot_nt(wtr_ref[T_AQ:T_AQ + 256, :], hb)
    scale_b = DSA_DIM ** -0.5 * LOG2E
    for h in range(DSA_HEADS):
        o1, o2 = _rope_rows(aqt[h * 64:h * 64 + 32, :], aqt[h * 64 + 32:h * 64 + 64, :], c64t, s64t)
        qbt_ref[h * LANE:h * LANE + 32, :] = (o1 * scale_b).astype(BF16)
        qbt_ref[h * LANE + 32:h * LANE + 64, :] = (o2 * scale_b).astype(BF16)
        qbt_ref[h * LANE + 64:(h + 1) * LANE, :] = jnp.zeros((64, tm), BF16)

    qit = _dot_nt(wtr_ref[T_QI:T_QI + 256, :], hb)
    for h in range(IDX_HEADS):
        o1, o2 = _rope_rows(qit[h * 32:h * 32 + 16, :], qit[h * 32 + 16:h * 32 + 32, :], c32t, s32t)
        qit_ref[h * LANE:h * LANE + 16, :] = o1.astype(BF16)
        qit_ref[h * LANE + 16:h * LANE + 32, :] = o2.astype(BF16)
        qit_ref[h * LANE + 32:(h + 1) * LANE, :] = jnp.zeros((96, tm), BF16)

    idx_scale = (IDX_HEADS ** -0.5) * (IDX_DIM ** -0.5)
    wit_ref[...] = _dot_nt(wtr_ref[T_WI:T_WI + IDX_HEADS, :], hb) * idx_scale

    avt = _dot_nt(wtr_ref[T_AV:T_AV + DSA_DIM, :], hb).astype(BF16)
    for j in range(tm // LANE):
        vbt_ref[j] = avt[:, j * LANE:(j + 1) * LANE]

    rqt = _dot_nt(wtr_ref[T_RQ:T_RQ + 256, :], hb)
    for h in range(RET_HEADS):
        o1, o2 = _rope_rows(rqt[h * 64:h * 64 + 32, :], rqt[h * 64 + 32:h * 64 + 64, :], c64t, s64t)
        rqt_ref[h * LANE:h * LANE + 32, :] = o1.astype(BF16)
        rqt_ref[h * LANE + 32:h * LANE + 64, :] = o2.astype(BF16)
        rqt_ref[h * LANE + 64:(h + 1) * LANE, :] = jnp.zeros((64, tm), BF16)

    rvt_ref[...] = _dot_nt(wtr_ref[T_RV:T_RV + 256, :], hb).astype(BF16)
    rgt_ref[...] = _dot_nt(wtr_ref[T_RG:T_RG + 256, :], hb)


def _full(shape):
    nd = len(shape)
    return pl.BlockSpec(shape, lambda b, i: (0,) * nd)


def _proj(x, gmix, gq, gkv, wnat, wtr, wuq, wuk, wuv, tabt, tabn):
    B, S, D = x.shape
    tm = TM_PROJ
    nch = tm // LANE
    outs = [
        (jax.ShapeDtypeStruct((B, MLA_HEADS * LANE, S), BF16), pl.BlockSpec((None, MLA_HEADS * LANE, tm), lambda b, i: (b, 0, i))),
        (jax.ShapeDtypeStruct((B, MLA_HEADS, S, LANE), BF16), pl.BlockSpec((None, MLA_HEADS, tm, LANE), lambda b, i: (b, 0, i, 0))),
        (jax.ShapeDtypeStruct((B, S // LANE, MLA_HEADS * MLA_V, LANE), BF16), pl.BlockSpec((None, nch, MLA_HEADS * MLA_V, LANE), lambda b, i: (b, i, 0, 0))),
        (jax.ShapeDtypeStruct((B, IDX_HEADS * LANE, S), BF16), pl.BlockSpec((None, IDX_HEADS * LANE, tm), lambda b, i: (b, 0, i))),
        (jax.ShapeDtypeStruct((B, IDX_HEADS, S), F32), pl.BlockSpec((None, IDX_HEADS, tm), lambda b, i: (b, 0, i))),
        (jax.ShapeDtypeStruct((B, S, LANE), BF16), pl.BlockSpec((None, tm, LANE), lambda b, i: (b, i, 0))),
        (jax.ShapeDtypeStruct((B, DSA_HEADS * LANE, S), BF16), pl.BlockSpec((None, DSA_HEADS * LANE, tm), lambda b, i: (b, 0, i))),
        (jax.ShapeDtypeStruct((B, S, LANE), BF16), pl.BlockSpec((None, tm, LANE), lambda b, i: (b, i, 0))),
        (jax.ShapeDtypeStruct((B, S // LANE, DSA_DIM, LANE), BF16), pl.BlockSpec((None, nch, DSA_DIM, LANE), lambda b, i: (b, i, 0, 0))),
        (jax.ShapeDtypeStruct((B, RET_HEADS * LANE, S), BF16), pl.BlockSpec((None, RET_HEADS * LANE, tm), lambda b, i: (b, 0, i))),
        (jax.ShapeDtypeStruct((B, S, RET_HEADS * LANE), BF16), pl.BlockSpec((None, tm, RET_HEADS * LANE), lambda b, i: (b, i, 0))),
        (jax.ShapeDtypeStruct((B, RET_HEADS * RET_DV, S), BF16), pl.BlockSpec((None, RET_HEADS * RET_DV, tm), lambda b, i: (b, 0, i))),
        (jax.ShapeDtypeStruct((B, RET_HEADS * RET_DV, S), F32), pl.BlockSpec((None, RET_HEADS * RET_DV, tm), lambda b, i: (b, 0, i))),
    ]
    in_specs = [
        pl.BlockSpec((None, tm, D), lambda b, i: (b, i, 0)),
        _full(gmix.shape), _full(gq.shape), _full(gkv.shape),
        _full(wnat.shape), _full(wtr.shape), _full(wuq.shape), _full(wuk.shape), _full(wuv.shape),
        pl.BlockSpec((tabt.shape[0], tm), lambda b, i: (0, i)),
        pl.BlockSpec((tabn.shape[0], tm, LANE), lambda b, i: (0, i, 0)),
    ]
    return pl.pallas_call(
        _proj_kernel,
        out_shape=[o[0] for o in outs],
        grid=(B, S // tm),
        in_specs=in_specs,
        out_specs=[o[1] for o in outs],
        compiler_params=pltpu.CompilerParams(
            dimension_semantics=("parallel", "parallel"), vmem_limit_bytes=VMEM_LIMIT),
        name="proj",
    )(x, gmix, gq, gkv, wnat, wtr, wuq, wuk, wuv, tabt, tabn)


def _with_ones_rows(vt):
    return jnp.concatenate([vt, jnp.ones((BF16_ROWS, vt.shape[1]), BF16)], axis=0)


def _mla_kernel(q_ref, k_ref, v_ref, o_ref, s0_sc, s1_sc, mt_sc, m_sc, acc_sc):
    g_heads = acc_sc.shape[0]
    tq = q_ref.shape[1]
    tk = tq
    i = pl.program_id(2)
    nv = tk // LANE
    kidx = lax.broadcasted_iota(jnp.int32, (tk, tq), 0)
    qidx = lax.broadcasted_iota(jnp.int32, (tk, tq), 1)
    bufs = (s0_sc, s1_sc)

    def scores(kt, parity, causal=None):
        for g in range(g_heads):
            s = _dot(k_ref[g, pl.ds(pl.multiple_of(kt * tk, tk), tk), :],
                     q_ref[g * LANE:(g + 1) * LANE, :])
            if causal is not None:
                s = jnp.where(causal, s, NEG)
            bufs[parity][g] = s
            mt_sc[parity, g] = jnp.max(s, axis=0, keepdims=True)

    def softmax(kt, parity):
        for g in range(g_heads):
            m_old = m_sc[g]
            m_new = jnp.maximum(m_old, mt_sc[parity, g])
            a = jnp.exp2(m_old - m_new)
            p = jnp.exp2(bufs[parity][g] - m_new)
            vt = jnp.concatenate([v_ref[kt * nv + j, g * MLA_V:(g + 1) * MLA_V, :] for j in range(nv)],
                                 axis=1)
            m_sc[g] = m_new
            acc_sc[g] = a * acc_sc[g] + _dot(_with_ones_rows(vt), p.astype(BF16))

    m_sc[...] = jnp.full(m_sc.shape, NEG, F32)
    acc_sc[...] = jnp.zeros_like(acc_sc)

    def step(k, parity, diagonal=False):
        scores(k + 1, 1 - parity, (kidx <= qidx) if diagonal else None)
        softmax(k, parity)

    scores(0, 0, kidx <= i * tq + qidx)

    def pair(p, _):
        step(2 * p, 0)
        step(2 * p + 1, 1)
        return 0

    lax.fori_loop(0, (i - 1) // 2, pair, 0)

    @pl.when(i % 2 == 0)
    def _():
        @pl.when(i >= 2)
        def _():
            step(i - 2, 0)
            step(i - 1, 1, diagonal=True)
        softmax(i, 0)

    @pl.when(i % 2 == 1)
    def _():
        step(i - 1, 0, diagonal=True)
        softmax(i, 1)

    for g in range(g_heads):
        acc = acc_sc[g]
        o_ref[g * MLA_V:(g + 1) * MLA_V, :] = (acc[:MLA_V] / acc[MLA_V:MLA_V + 1]).astype(BF16)


def _mla(qat, ka, vat):
    B, _, S = qat.shape
    tq = TQ_MLA
    g = G_MLA
    return pl.pallas_call(
        _mla_kernel,
        out_shape=jax.ShapeDtypeStruct((B, MLA_HEADS * MLA_V, S), BF16),
        grid=(B, MLA_HEADS // g, S // tq),
        in_specs=[
            pl.BlockSpec((None, g * LANE, tq), lambda b, h, i: (b, h, i)),
            pl.BlockSpec((None, g, S, LANE), lambda b, h, i: (b, h, 0, 0)),
            pl.BlockSpec((None, S // LANE, g * MLA_V, LANE), lambda b, h, i: (b, 0, h, 0)),
        ],
        out_specs=pl.BlockSpec((None, g * MLA_V, tq), lambda b, h, i: (b, h, i)),
        scratch_shapes=[pltpu.VMEM((g, tq, tq), F32), pltpu.VMEM((g, tq, tq), F32),
                        pltpu.VMEM((2, g, 1, tq), F32), pltpu.VMEM((g, 1, tq), F32),
                        pltpu.VMEM((g, MLA_V + BF16_ROWS, tq), F32)],
        compiler_params=pltpu.CompilerParams(
            dimension_semantics=("parallel", "parallel", "arbitrary"), vmem_limit_bytes=VMEM_LIMIT),
        name="mla",
    )(qat, ka, vat)


def _count_rows(mask):
    r, t = mask.shape
    return jnp.sum(mask.astype(jnp.int32).reshape(r // 8, 8, t), axis=0)


def _dsa_kernel(topk, qit_ref, wit_ref, ki_ref, qbt_ref, kb_ref, vbt_ref, o_ref,
                sc_sc, sb_sc, att0_sc, att1_sc, thr_sc, nge_sc, mt_sc, m_sc, acc_sc):
    tq = o_ref.shape[1]
    ck = CK_DSA
    sub = tq // ck
    i = pl.program_id(1)
    nblk = i + 1

    def rows(ref, r0, n):
        if isinstance(r0, int):
            return ref[r0:r0 + n, :]
        return ref[pl.ds(pl.multiple_of(r0, n), n), :]

    n_lg = tq // LANE

    def sc_load(r0, n):
        return jnp.concatenate([sc_sc[g, pl.ds(pl.multiple_of(r0, n), n), :] for g in range(n_lg)], axis=1)

    def sc_store(r0, n, val):
        for g in range(n_lg):
            sc_sc[g, pl.ds(pl.multiple_of(r0, n), n), :] = val[:, g * LANE:(g + 1) * LANE]

    qpos_c = i * tq + lax.broadcasted_iota(jnp.int32, (ck, tq), 1)
    krow_c = lax.broadcasted_iota(jnp.int32, (ck, tq), 0)

    def score_block(cb):
        for j in range(sub):
            r0 = cb * tq + j * ck
            kc = rows(ki_ref, r0, ck)
            acc = jnp.zeros((ck, tq), F32)
            for h in range(IDX_HEADS):
                rel = _dot(kc, qit_ref[h * LANE:(h + 1) * LANE, :])
                acc = acc + wit_ref[h:h + 1, :] * jnp.maximum(rel, 0.0)
            sc_store(r0, ck, jnp.where(r0 + krow_c <= qpos_c, acc, -jnp.inf))

    def score_pair(p, _):
        score_block(2 * p)
        score_block(2 * p + 1)
        return 0

    lax.fori_loop(0, nblk // 2, score_pair, 0)

    @pl.when(nblk % 2 == 1)
    def _():
        score_block(nblk - 1)

    krow = lax.broadcasted_iota(jnp.int32, (tq, tq), 0)

    def count(pred, nb=None):
        def body(cb, a):
            return a + _count_rows(pred(sc_load(cb * tq, tq), cb * tq + krow))
        part = jnp.zeros((8, tq), jnp.int32)
        if nb is None:
            part = lax.fori_loop(0, nblk, body, part)
        else:
            for cb in range(nb):
                part = body(cb, part)
        return jnp.sum(part, axis=0, keepdims=True)

    def flip(word):
        return jnp.where(word < 0, word ^ jnp.int32(0x7FFFFFFF), word)

    def key_to_score(key):
        return pltpu.bitcast(flip(key), F32)

    thr_sc[...] = jnp.full((1, tq), jnp.finfo(F32).min, F32)

    def bisect(nb):
        assert n_lg == 2
        la, lb = slice(0, LANE), slice(LANE, 2 * LANE)

        def blocks(group):
            n_rows = (nb - 1) * tq + (group + 1) * LANE
            return [(r0, min(r0 + tq, n_rows)) for r0 in range(0, n_rows, tq)]

        for group in range(2):
            for r0, r1 in blocks(group):
                sb_sc[group, r0:r1, :] = sc_sc[group, r0:r1, :].astype(BF16)

        def grid_bits(key16):
            return jnp.left_shift(jnp.where(key16 < 0, key16 ^ jnp.int32(0x7FFF), key16), 16)

        def count16(group, key16):
            c16 = pltpu.bitcast(grid_bits(key16), F32)[0:1, :].astype(BF16)
            parts = [jnp.zeros((BF16_ROWS, LANE), BF16) for _ in range(4)]
            for r0, r1 in blocks(group):
                hit = jnp.where(sb_sc[group, r0:r1, :] >= c16, jnp.ones((), BF16), jnp.zeros((), BF16))
                h3 = hit.reshape((r1 - r0) // BF16_ROWS, BF16_ROWS, LANE)
                for q in range(h3.shape[0]):
                    parts[q % 4] = parts[q % 4] + h3[q]
            p32 = ((parts[0] + parts[1]) + (parts[2] + parts[3])).astype(F32)
            return (p32[0:8] + p32[8:16]).astype(jnp.int32)

        def count32(group, key):
            cf = key_to_score(key)[0:1, :]
            part = jnp.zeros((8, LANE), jnp.int32)
            for r0, r1 in blocks(group):
                part = part + _count_rows(sc_sc[group, r0:r1, :] >= cf)
            return part

        def enough(part):
            n = jnp.sum(part, axis=0, keepdims=True)
            return n >= topk, n

        def pipelined(steps, propose, update, count, st_a, st_b, idle_b):
            def body(t, carry):
                st_a, st_b, cand_b, part_b = carry
                st_b = update(st_b, cand_b, part_b)
                cand_b = propose(t, st_b)
                cand_a = propose(t, st_a)
                st_a = update(st_a, cand_a, count(0, cand_a))
                return st_a, st_b, cand_b, count(1, cand_b)

            first = body(jnp.int32(0), (st_a, st_b, idle_b, jnp.zeros((8, LANE), jnp.int32)))
            st_a, st_b, cand_b, part_b = lax.fori_loop(1, steps, body, first)
            return st_a, update(st_b, cand_b, part_b)

        def propose16(t, ans):
            return jnp.where(t == 0, jnp.zeros_like(ans), ans + jnp.left_shift(jnp.int32(1), 15 - t))

        def update16(ans, cand, part):
            return jnp.where(enough(part)[0], cand, ans)

        low16 = jnp.full((8, LANE), -(1 << 15), jnp.int32)
        g_a, g_b = pipelined(16, propose16, update16, count16, low16, low16, low16)

        def propose32(t, st):
            lo, hi, _ = st
            return lo + jnp.right_shift(hi - lo, 1)

        def update32(st, cand, part):
            lo, hi, n_lo = st
            ok, n = enough(part)
            return jnp.where(ok, cand, lo), jnp.where(ok, hi, cand), jnp.where(ok, n, n_lo)

        def bracket(g16):
            key = flip(grid_bits(g16))
            return key - np.int32((1 << 15) + 2), key + np.int32((1 << 16) + 2), jnp.full((8, LANE), topk + 1, jnp.int32)

        st_a, st_b = bracket(g_a), bracket(g_b)
        (lo_a, _, n_a), (lo_b, _, n_b) = pipelined(17, propose32, update32, count32, st_a, st_b, st_b[1])
        thr_sc[:, la] = key_to_score(lo_a)[0:1, :]
        thr_sc[:, lb] = key_to_score(lo_b)[0:1, :]
        nge_sc[:, la] = n_a[0:1, :]
        nge_sc[:, lb] = n_b[0:1, :]

    for v in range(topk // tq, sc_sc.shape[1] // tq):
        pl.when(i == v)(functools.partial(bisect, v + 1))

    @pl.when(i * tq >= topk)
    def _():
        thr = thr_sc[...]

        @pl.when(jnp.max(nge_sc[...]) > topk)
        def _():
            need = topk - count(lambda s, kidx: s > thr)

            def jbit(t, j):
                cand = j + jnp.left_shift(jnp.int32(1), 10 - t)
                n = count(lambda s, kidx: (s == thr) & (kidx < cand))
                return jnp.where(n < need, cand, j)

            last = lax.fori_loop(0, 11, jbit, jnp.zeros((1, tq), jnp.int32))

            def drop(cb, _):
                s = sc_load(cb * tq, tq)
                dropped = (s == thr) & (cb * tq + krow > last)
                sc_store(cb * tq, tq, jnp.where(dropped, -jnp.inf, s))
                return 0

            lax.fori_loop(0, nblk, drop, 0)

    thr = thr_sc[...]
    nh = DSA_HEADS
    qall = jnp.concatenate([qbt_ref[h * LANE:(h + 1) * LANE, :] for h in range(nh)], axis=1)
    bufs = (att0_sc, att1_sc)

    def logits(cb, parity):
        bias = jnp.where(sc_load(cb * tq, tq) >= thr, 0.0, NEG)
        mt = [jnp.full((1, tq), NEG, F32)] * nh
        for j in range(sub):
            s = _dot(rows(kb_ref, cb * tq + j * ck, ck), qall)
            b = bias[j * ck:(j + 1) * ck, :]
            for h in range(nh):
                sh = s[:, h * tq:(h + 1) * tq] + b
                bufs[parity][h, j * ck:(j + 1) * ck, :] = sh
                mt[h] = jnp.maximum(mt[h], jnp.max(sh, axis=0, keepdims=True))
        for h in range(nh):
            mt_sc[parity, h] = mt[h]

    def attend(cb, parity):
        vt = _with_ones_rows(jnp.concatenate([vbt_ref[cb * sub + j] for j in range(sub)], axis=1))
        for h in range(nh):
            m_old = m_sc[h]
            m_new = jnp.maximum(m_old, mt_sc[parity, h])
            a = jnp.exp2(m_old - m_new)
            p = jnp.exp2(bufs[parity][h] - m_new)
            m_sc[h] = m_new
            acc_sc[h] = a * acc_sc[h] + _dot(vt, p.astype(BF16))

    def step(k, parity):
        logits(k + 1, 1 - parity)
        attend(k, parity)

    m_sc[...] = jnp.full(m_sc.shape, NEG, F32)
    acc_sc[...] = jnp.zeros_like(acc_sc)
    logits(0, 0)

    def pair(p, _):
        step(2 * p, 0)
        step(2 * p + 1, 1)
        return 0

    lax.fori_loop(0, i // 2, pair, 0)

    @pl.when(i % 2 == 1)
    def _():
        step(i - 1, 0)
        attend(i, 1)

    @pl.when(i % 2 == 0)
    def _():
        attend(i, 0)

    for h in range(nh):
        acc = acc_sc[h]
        o_ref[h * DSA_DIM:(h + 1) * DSA_DIM, :] = (acc[:DSA_DIM] / acc[DSA_DIM:DSA_DIM + 1]).astype(BF16)


def _dsa(qit, wit, ki, qbt, kb, vbt, topk):
    B, _, S = qit.shape
    tq = TQ_DSA
    assert topk % tq == 0 or topk >= S, "query tiles must not straddle the top-k boundary"
    return pl.pallas_call(
        functools.partial(_dsa_kernel, topk),
        out_shape=jax.ShapeDtypeStruct((B, DSA_HEADS * DSA_DIM, S), BF16),
        grid=(B, S // tq),
        in_specs=[
            pl.BlockSpec((None, IDX_HEADS * LANE, tq), lambda b, i: (b, 0, i)),
            pl.BlockSpec((None, IDX_HEADS, tq), lambda b, i: (b, 0, i)),
            pl.BlockSpec((None, S, LANE), lambda b, i: (b, 0, 0)),
            pl.BlockSpec((None, DSA_HEADS * LANE, tq), lambda b, i: (b, 0, i)),
            pl.BlockSpec((None, S, LANE), lambda b, i: (b, 0, 0)),
            pl.BlockSpec((None, S // LANE, DSA_DIM, LANE), lambda b, i: (b, 0, 0, 0)),
        ],
        out_specs=pl.BlockSpec((None, DSA_HEADS * DSA_DIM, tq), lambda b, i: (b, 0, i)),
        scratch_shapes=[
            pltpu.VMEM((tq // LANE, S, LANE), F32),
            pltpu.VMEM((tq // LANE, S, LANE), BF16),
            pltpu.VMEM((DSA_HEADS, tq, tq), F32),
            pltpu.VMEM((DSA_HEADS, tq, tq), F32),
            pltpu.VMEM((1, tq), F32),
            pltpu.VMEM((1, tq), jnp.int32),
            pltpu.VMEM((2, DSA_HEADS, 1, tq), F32),
            pltpu.VMEM((DSA_HEADS, 1, tq), F32),
            pltpu.VMEM((DSA_HEADS, DSA_DIM + BF16_ROWS, tq), F32),
        ],
        compiler_params=pltpu.CompilerParams(
            dimension_semantics=("parallel", "arbitrary"), vmem_limit_bytes=VMEM_LIMIT),
        name="dsa",
    )(qit, wit, ki, qbt, kb, vbt)


def _ret_kernel(rqt_ref, rk_ref, rvt_ref, rgt_ref, dt_ref, xi_ref, zeta_ref, cdec_ref, gret_ref, o_ref):
    c = dt_ref.shape[1]
    n_chunks = o_ref.shape[1] // c
    state = [jnp.zeros((RET_DV, LANE), F32) for _ in range(RET_HEADS)]
    for n in range(n_chunks):
        cols = slice(n * c, (n + 1) * c)
        for h in range(RET_HEADS):
            rt = state[h]
            qt = rqt_ref[h * LANE:(h + 1) * LANE, cols]
            k = rk_ref[cols, h * LANE:(h + 1) * LANE]
            vt = rvt_ref[h * RET_DV:(h + 1) * RET_DV, cols]
            inner_t = _dot(k, qt) * dt_ref[h]
            qx = (qt.astype(F32) * xi_ref[h:h + 1, :]).astype(BF16)
            out_t = _dot(vt, inner_t.astype(BF16)) + _dot(rt.astype(BF16), qx)
            vz = (vt.astype(F32) * zeta_ref[h:h + 1, :]).astype(BF16)
            state[h] = rt * cdec_ref[h] + _dot(vz, k)

            mu = jnp.mean(out_t, axis=0, keepdims=True)
            xc = out_t - mu
            var = jnp.mean(xc * xc, axis=0, keepdims=True)
            y = xc * lax.rsqrt(var + EPS) * gret_ref[h * RET_DV:(h + 1) * RET_DV, :]
            g = rgt_ref[h * RET_DV:(h + 1) * RET_DV, cols]
            o_ref[h * RET_DV:(h + 1) * RET_DV, cols] = (g * jax.nn.sigmoid(g) * y).astype(BF16)


def _retention(rqt, rk, rvt, rgt, dt, xi, zeta, cdec, gret):
    B, _, S = rqt.shape

    def full1(shape):
        nd = len(shape)
        return pl.BlockSpec(shape, lambda b: (0,) * nd)

    def per_batch(rows, cols):
        return pl.BlockSpec((None, rows, cols), lambda b: (b, 0, 0))

    return pl.pallas_call(
        _ret_kernel,
        out_shape=jax.ShapeDtypeStruct((B, RET_HEADS * RET_DV, S), BF16),
        grid=(B,),
        in_specs=[
            per_batch(RET_HEADS * LANE, S), per_batch(S, RET_HEADS * LANE),
            per_batch(RET_HEADS * RET_DV, S), per_batch(RET_HEADS * RET_DV, S),
            full1(dt.shape), full1(xi.shape), full1(zeta.shape), full1(cdec.shape), full1(gret.shape),
        ],
        out_specs=per_batch(RET_HEADS * RET_DV, S),
        compiler_params=pltpu.CompilerParams(
            dimension_semantics=("parallel",), vmem_limit_bytes=VMEM_LIMIT),
        name="retention",
    )(rqt, rk, rvt, rgt, dt, xi, zeta, cdec, gret)


def _mlp_kernel(final, x_ref, oat_ref, obt_ref, oct_ref, wout_ref, gmlp_ref, w1_ref, w2_ref, gfin_ref,
                y_ref):
    def tr(ref):
        return ref[...].astype(F32).T.astype(BF16)

    na = MLA_HEADS * MLA_V
    nb = DSA_HEADS * DSA_DIM
    attn = (_dot(tr(oat_ref), wout_ref[0:na, :])
            + _dot(tr(obt_ref), wout_ref[na:na + nb, :])
            + _dot(tr(oct_ref), wout_ref[na + nb:, :]))
    x1 = x_ref[...] + attn
    hf = _rms(x1, gmlp_ref[...]).astype(BF16)
    acc = x1
    for f in range(D_FF // FF_CHUNK):
        u = jnp.maximum(_dot(hf, w1_ref[:, f * FF_CHUNK:(f + 1) * FF_CHUNK]), 0.0)
        acc = acc + _dot((u * u).astype(BF16), w2_ref[f * FF_CHUNK:(f + 1) * FF_CHUNK, :])
    if final:
        acc = _rms(acc, gfin_ref[...])
    y_ref[...] = acc


def _mlp(x, oat, obt, oct, wout, gmlp, w1, w2, gfin, final):
    B, S, D = x.shape
    tm = TM_MLP

    def const(shape):
        nd = len(shape)
        return pl.BlockSpec(shape, lambda b, i: (0,) * nd, pipeline_mode=pl.Buffered(1))

    return pl.pallas_call(
        functools.partial(_mlp_kernel, final),
        out_shape=jax.ShapeDtypeStruct((B, S, D), F32),
        grid=(B, S // tm),
        in_specs=[
            pl.BlockSpec((None, tm, D), lambda b, i: (b, i, 0)),
            pl.BlockSpec((None, oat.shape[1], tm), lambda b, i: (b, 0, i)),
            pl.BlockSpec((None, obt.shape[1], tm), lambda b, i: (b, 0, i)),
            pl.BlockSpec((None, oct.shape[1], tm), lambda b, i: (b, 0, i)),
            const(wout.shape), const(gmlp.shape), const(w1.shape), const(w2.shape), const(gfin.shape),
        ],
        out_specs=pl.BlockSpec((None, tm, D), lambda b, i: (b, i, 0)),
        compiler_params=pltpu.CompilerParams(
            dimension_semantics=("parallel", "parallel"), vmem_limit_bytes=VMEM_LIMIT),
        name="mlp",
    )(x, oat, obt, oct, wout, gmlp, w1, w2, gfin)


def _rope_tables(seq, dim):
    pos = jnp.arange(seq, dtype=F32)
    inv = ROPE_THETA ** (-jnp.arange(0, dim, 2, dtype=F32) / dim)
    ang = pos[:, None] * inv[None, :]
    return jnp.cos(ang), jnp.sin(ang)


def _lane_table(c, s, lo, seq):
    half = c.shape[1]
    ct = jnp.zeros((seq, LANE), F32).at[:, lo:lo + 2 * half].set(jnp.concatenate([c, c], axis=1))
    st = jnp.zeros((seq, LANE), F32).at[:, lo:lo + 2 * half].set(jnp.concatenate([-s, s], axis=1))
    return ct, st


def _pad_cols(w, width):
    return jnp.pad(w, ((0, 0), (0, width - w.shape[1])))


def _prep_layer(w_in, w_uq, w_ukv):
    z = lambda n: jnp.zeros((D_MODEL, n), F32)
    col = lambda a, b: w_in[:, a:b]
    nat = [col(O_CQ, O_CKV), col(O_CKV, O_KR),
           jnp.concatenate([z(64), col(O_KR, O_AQ), z(32)], axis=1),
           _pad_cols(col(O_AK, O_AV), LANE),
           _pad_cols(col(O_KI, O_WI), LANE)]
    for h in range(RET_HEADS):
        nat.append(_pad_cols(col(O_RK + h * RET_DK, O_RK + (h + 1) * RET_DK), LANE))
    wnat = jnp.concatenate(nat, axis=1).astype(BF16)
    wtr = jnp.concatenate([col(O_AQ, O_AK), col(O_QI, O_KI), col(O_AV, O_QI), col(O_RQ, O_RK),
                           col(O_RV, O_RG), col(O_RG, O_END), col(O_WI, O_RQ)], axis=1).T.astype(BF16)
    dq = MLA_NOPE + MLA_ROPE
    wuq = jnp.pad(w_uq.reshape(MLA_Q_RANK, MLA_HEADS, dq), ((0, 0), (0, 0), (0, LANE - dq)))
    wuq = wuq.reshape(MLA_Q_RANK, MLA_HEADS * LANE).T.astype(BF16)
    wkv = w_ukv.reshape(MLA_KV_RANK, MLA_HEADS, MLA_NOPE + MLA_V)
    wuk = jnp.pad(wkv[:, :, :MLA_NOPE], ((0, 0), (0, 0), (0, LANE - MLA_NOPE)))
    wuk = wuk.reshape(MLA_KV_RANK, MLA_HEADS * LANE).astype(BF16)
    wuv = wkv[:, :, MLA_NOPE:].reshape(MLA_KV_RANK, MLA_HEADS * MLA_V).T.astype(BF16)
    return wnat, wtr, wuq, wuk, wuv


def kernel(x, g_mix, w_in, g_q, w_uq, g_kv, w_ukv, g_ret, w_out, g_mlp, w_ff1, w_ff2, g_final):
    B, S, _ = x.shape
    topk = min(DSA_TOPK_MAX, S // 4)

    cos32, sin32 = _rope_tables(S, 32)
    cos64, sin64 = _rope_tables(S, 64)
    tabt = jnp.concatenate([cos64.T, sin64.T, cos32.T, sin32.T], axis=0)
    tabn = jnp.stack(_lane_table(cos32, sin32, 64, S) + _lane_table(cos64, sin64, 0, S)
                     + _lane_table(cos32, sin32, 0, S))

    log_gamma = jnp.log1p(-jnp.exp2(-5.0 - jnp.arange(RET_HEADS, dtype=F32)))
    c = C_RET
    pos = jnp.arange(c, dtype=F32)
    rel = pos[None, :] - pos[:, None]
    dt = jnp.where(rel[None] >= 0, jnp.exp(jnp.maximum(rel, 0.0)[None] * log_gamma[:, None, None]), 0.0)
    xi = jnp.exp((pos + 1.0)[None, :] * log_gamma[:, None])
    zeta = jnp.exp((c - 1.0 - pos)[None, :] * log_gamma[:, None])
    cdec = jnp.broadcast_to(jnp.exp(c * log_gamma)[:, None, None], (RET_HEADS, RET_DV, LANE))

    for l in range(DEPTH):
        wnat, wtr, wuq, wuk, wuv = _prep_layer(w_in[l], w_uq[l], w_ukv[l])
        (qat, ka, vat, qit, wit, ki, qbt, kb, vbt, rqt, rk, rvt, rgt) = _proj(
            x, g_mix[l][None, :], g_q[l][None, :], g_kv[l][None, :], wnat, wtr, wuq, wuk, wuv, tabt, tabn)
        oat = _mla(qat, ka, vat)
        obt = _dsa(qit, wit, ki, qbt, kb, vbt, topk)
        gret = jnp.broadcast_to(g_ret[l][:, None], (RET_HEADS * RET_DV, c))
        oct = _retention(rqt, rk, rvt, rgt, dt, xi, zeta, cdec, gret)
        x = _mlp(x, oat, obt, oct, w_out[l].astype(BF16), g_mlp[l][None, :],
                 w_ff1[l].astype(BF16), w_ff2[l].astype(BF16), g_final[None, :], l == DEPTH - 1)
    return x
```

```python
import functools

import numpy as np
import jax
import jax.numpy as jnp
from jax import lax
from jax.experimental import pallas as pl
from jax.experimental.pallas import tpu as pltpu

D_MODEL = 1024
DEPTH = 2
MLA_HEADS = 8
MLA_NOPE = 64
MLA_ROPE = 32
MLA_V = 64
MLA_Q_RANK = 768
MLA_KV_RANK = 256
DSA_HEADS = 4
DSA_DIM = 64
IDX_HEADS = 8
IDX_DIM = 32
DSA_TOPK_MAX = 256
RET_HEADS = 4
RET_DK = 64
RET_DV = 64
D_FF = 4 * D_MODEL
ROPE_THETA = 10000.0
EPS = 1e-6

MLA_COLS = (MLA_Q_RANK, MLA_KV_RANK, MLA_ROPE)
DSA_COLS = (DSA_HEADS * DSA_DIM, DSA_DIM, DSA_DIM, IDX_HEADS * IDX_DIM, IDX_DIM, IDX_HEADS)
RET_COLS = (RET_HEADS * RET_DK, RET_HEADS * RET_DK, RET_HEADS * RET_DV, RET_HEADS * RET_DV)
SPLITS = MLA_COLS + DSA_COLS + RET_COLS
_OFF = np.concatenate([[0], np.cumsum(SPLITS)]).tolist()
(O_CQ, O_CKV, O_KR, O_AQ, O_AK, O_AV, O_QI, O_KI, O_WI, O_RQ, O_RK, O_RV, O_RG, O_END) = _OFF

LANE = 128
BF16_ROWS = 16
VMEM_LIMIT = 56 * 1024 * 1024

BF16 = jnp.bfloat16
F32 = jnp.float32
NEG = -1e30
LOG2E = float(np.log2(np.e))
INT_MIN = np.int32(-2**31)

N_CQ = 0
N_CKV = N_CQ + MLA_Q_RANK
N_KR = N_CKV + MLA_KV_RANK
N_AK = N_KR + LANE
N_KI = N_AK + LANE
N_RK = N_KI + LANE
N_NAT = N_RK + RET_HEADS * LANE
T_AQ = 0
T_QI = T_AQ + DSA_HEADS * DSA_DIM
T_AV = T_QI + IDX_HEADS * IDX_DIM
T_RQ = T_AV + DSA_DIM
T_RV = T_RQ + RET_HEADS * RET_DK
T_RG = T_RV + RET_HEADS * RET_DV
T_WI = T_RG + RET_HEADS * RET_DV
N_TR = T_WI + IDX_HEADS

TM_PROJ = 512
TQ_MLA = 256
G_MLA = 8
TQ_DSA = 256
CK_DSA = 128
C_RET = 256
TM_MLP = 512
FF_CHUNK = 1024

_NT = (((1,), (1,)), ((), ()))


def _rms(x, g):
    return x * lax.rsqrt(jnp.mean(x * x, axis=-1, keepdims=True) + EPS) * g


def _dot(a, b):
    return jnp.dot(a, b, preferred_element_type=F32)


def _dot_nt(a, b):
    return lax.dot_general(a, b, _NT, preferred_element_type=F32)


def _rope_rows(x1, x2, c, s):
    return x1 * c - x2 * s, x1 * s + x2 * c


def _rope_lanes(x, c, s_signed, half, first_lo):
    lane = lax.broadcasted_iota(jnp.int32, x.shape, 1)
    first = (lane >= first_lo) & (lane < first_lo + half)
    rot = jnp.where(first, pltpu.roll(x, LANE - half, 1), pltpu.roll(x, half, 1))
    return x * c + rot * s_signed


def _proj_kernel(x_ref, gmix_ref, gq_ref, gkv_ref, wnat_ref, wtr_ref, wuq_ref, wuk_ref, wuv_ref,
                 tabt_ref, tabn_ref,
                 qat_ref, ka_ref, vat_ref, qit_ref, wit_ref, ki_ref, qbt_ref, kb_ref, vbt_ref,
                 rqt_ref, rk_ref, rvt_ref, rgt_ref):
    tm = x_ref.shape[0]
    hb = _rms(x_ref[...], gmix_ref[...]).astype(BF16)

    c64t = tabt_ref[0:32, :]
    s64t = tabt_ref[32:64, :]
    c32t = tabt_ref[64:80, :]
    s32t = tabt_ref[80:96, :]

    cq = _dot(hb, wnat_ref[:, N_CQ:N_CQ + MLA_Q_RANK])
    nq = _rms(cq, gq_ref[...]).astype(BF16)
    qt = _dot_nt(wuq_ref[...], nq)
    scale_a = (MLA_NOPE + MLA_ROPE) ** -0.5 * LOG2E
    for h in range(MLA_HEADS):
        r0 = h * LANE
        s0 = h * (MLA_NOPE + MLA_ROPE)
        qat_ref[r0:r0 + 64, :] = (qt[s0:s0 + 64, :] * scale_a).astype(BF16)
        o1, o2 = _rope_rows(qt[s0 + 64:s0 + 80, :], qt[s0 + 80:s0 + 96, :], c32t, s32t)
        qat_ref[r0 + 64:r0 + 80, :] = (o1 * scale_a).astype(BF16)
        qat_ref[r0 + 80:r0 + 96, :] = (o2 * scale_a).astype(BF16)
        qat_ref[r0 + 96:r0 + 128, :] = jnp.zeros((32, tm), BF16)

    ckv = _dot(hb, wnat_ref[:, N_CKV:N_CKV + MLA_KV_RANK])
    nkv = _rms(ckv, gkv_ref[...]).astype(BF16)
    kn = _dot(nkv, wuk_ref[...])
    keys = _dot(hb, wnat_ref[:, N_KR:N_NAT])
    slab = lambda off: keys[:, off - N_KR:off - N_KR + LANE]
    krp = _rope_lanes(slab(N_KR), tabn_ref[0], tabn_ref[1], 16, 64)
    for h in range(MLA_HEADS):
        ka_ref[h] = (kn[:, h * LANE:(h + 1) * LANE] + krp).astype(BF16)
    vt = _dot_nt(wuv_ref[...], nkv).astype(BF16)
    for j in range(tm // LANE):
        vat_ref[j] = vt[:, j * LANE:(j + 1) * LANE]

    kb_ref[...] = _rope_lanes(slab(N_AK), tabn_ref[2], tabn_ref[3], 32, 0).astype(BF16)
    ki_ref[...] = _rope_lanes(slab(N_KI), tabn_ref[4], tabn_ref[5], 16, 0).astype(BF16)

    for h in range(RET_HEADS):
        rkp = _rope_lanes(slab(N_RK + h * LANE), tabn_ref[2], tabn_ref[3], 32, 0) * (RET_DK ** -0.5)
        rk_ref[:, h * LANE:(h + 1) * LANE] = rkp.astype(BF16)

    aqt = _dot_nt(wtr_ref[T_AQ:T_AQ + 256, :], hb)
    scale_b = DSA_DIM ** -0.5 * LOG2E
    for h in range(DSA_HEADS):
        o1, o2 = _rope_rows(aqt[h * 64:h * 64 + 32, :], aqt[h * 64 + 32:h * 64 + 64, :], c64t, s64t)
        qbt_ref[h * LANE:h * LANE + 32, :] = (o1 * scale_b).astype(BF16)
        qbt_ref[h * LANE + 32:h * LANE + 64, :] = (o2 * scale_b).astype(BF16)
        qbt_ref[h * LANE + 64:(h + 1) * LANE, :] = jnp.zeros((64, tm), BF16)

    qit = _dot_nt(wtr_ref[T_QI:T_QI + 256, :], hb)
    for h in range(IDX_HEADS):
        o1, o2 = _rope_rows(qit[h * 32:h * 32 + 16, :], qit[h * 32 + 16:h * 32 + 32, :], c32t, s32t)
        qit_ref[h * LANE:h * LANE + 16, :] = o1.astype(BF16)
        qit_ref[h * LANE + 16:h * LANE + 32, :] = o2.astype(BF16)
        qit_ref[h * LANE + 32:(h + 1) * LANE, :] = jnp.zeros((96, tm), BF16)

    idx_scale = (IDX_HEADS ** -0.5) * (IDX_DIM ** -0.5)
    wit_ref[...] = _dot_nt(wtr_ref[T_WI:T_WI + IDX_HEADS, :], hb) * idx_scale

    avt = _dot_nt(wtr_ref[T_AV:T_AV + DSA_DIM, :], hb).astype(BF16)
    for j in range(tm // LANE):
        vbt_ref[j] = avt[:, j * LANE:(j + 1) * LANE]

    rqt = _dot_nt(wtr_ref[T_RQ:T_RQ + 256, :], hb)
    for h in range(RET_HEADS):
        o1, o2 = _rope_rows(rqt[h * 64:h * 64 + 32, :], rqt[h * 64 + 32:h * 64 + 64, :], c64t, s64t)
        rqt_ref[h * LANE:h * LANE + 32, :] = o1.astype(BF16)
        rqt_ref[h * LANE + 32:h * LANE + 64, :] = o2.astype(BF16)
        rqt_ref[h * LANE + 64:(h + 1) * LANE, :] = jnp.zeros((64, tm), BF16)

    rvt_ref[...] = _dot_nt(wtr_ref[T_RV:T_RV + 256, :], hb).astype(BF16)
    rgt_ref[...] = _dot_nt(wtr_ref[T_RG:T_RG + 256, :], hb)


def _full(shape):
    nd = len(shape)
    return pl.BlockSpec(shape, lambda b, i: (0,) * nd)


def _proj(x, gmix, gq, gkv, wnat, wtr, wuq, wuk, wuv, tabt, tabn):
    B, S, D = x.shape
    tm = TM_PROJ
    nch = tm // LANE
    outs = [
        (jax.ShapeDtypeStruct((B, MLA_HEADS * LANE, S), BF16), pl.BlockSpec((None, MLA_HEADS * LANE, tm), lambda b, i: (b, 0, i))),
        (jax.ShapeDtypeStruct((B, MLA_HEADS, S, LANE), BF16), pl.BlockSpec((None, MLA_HEADS, tm, LANE), lambda b, i: (b, 0, i, 0))),
        (jax.ShapeDtypeStruct((B, S // LANE, MLA_HEADS * MLA_V, LANE), BF16), pl.BlockSpec((None, nch, MLA_HEADS * MLA_V, LANE), lambda b, i: (b, i, 0, 0))),
        (jax.ShapeDtypeStruct((B, IDX_HEADS * LANE, S), BF16), pl.BlockSpec((None, IDX_HEADS * LANE, tm), lambda b, i: (b, 0, i))),
        (jax.ShapeDtypeStruct((B, IDX_HEADS, S), F32), pl.BlockSpec((None, IDX_HEADS, tm), lambda b, i: (b, 0, i))),
        (jax.ShapeDtypeStruct((B, S, LANE), BF16), pl.BlockSpec((None, tm, LANE), lambda b, i: (b, i, 0))),
        (jax.ShapeDtypeStruct((B, DSA_HEADS * LANE, S), BF16), pl.BlockSpec((None, DSA_HEADS * LANE, tm), lambda b, i: (b, 0, i))),
        (jax.ShapeDtypeStruct((B, S, LANE), BF16), pl.BlockSpec((None, tm, LANE), lambda b, i: (b, i, 0))),
        (jax.ShapeDtypeStruct((B, S // LANE, DSA_DIM, LANE), BF16), pl.BlockSpec((None, nch, DSA_DIM, LANE), lambda b, i: (b, i, 0, 0))),
        (jax.ShapeDtypeStruct((B, RET_HEADS * LANE, S), BF16), pl.BlockSpec((None, RET_HEADS * LANE, tm), lambda b, i: (b, 0, i))),
        (jax.ShapeDtypeStruct((B, S, RET_HEADS * LANE), BF16), pl.BlockSpec((None, tm, RET_HEADS * LANE), lambda b, i: (b, i, 0))),
        (jax.ShapeDtypeStruct((B, RET_HEADS * RET_DV, S), BF16), pl.BlockSpec((None, RET_HEADS * RET_DV, tm), lambda b, i: (b, 0, i))),
        (jax.ShapeDtypeStruct((B, RET_HEADS * RET_DV, S), F32), pl.BlockSpec((None, RET_HEADS * RET_DV, tm), lambda b, i: (b, 0, i))),
    ]
    in_specs = [
        pl.BlockSpec((None, tm, D), lambda b, i: (b, i, 0)),
        _full(gmix.shape), _full(gq.shape), _full(gkv.shape),
        _full(wnat.shape), _full(wtr.shape), _full(wuq.shape), _full(wuk.shape), _full(wuv.shape),
        pl.BlockSpec((tabt.shape[0], tm), lambda b, i: (0, i)),
        pl.BlockSpec((tabn.shape[0], tm, LANE), lambda b, i: (0, i, 0)),
    ]
    return pl.pallas_call(
        _proj_kernel,
        out_shape=[o[0] for o in outs],
        grid=(B, S // tm),
        in_specs=in_specs,
        out_specs=[o[1] for o in outs],
        compiler_params=pltpu.CompilerParams(
            dimension_semantics=("parallel", "parallel"), vmem_limit_bytes=VMEM_LIMIT),
        name="proj",
    )(x, gmix, gq, gkv, wnat, wtr, wuq, wuk, wuv, tabt, tabn)


def _with_ones_rows(vt):
    return jnp.concatenate([vt, jnp.ones((BF16_ROWS, vt.shape[1]), BF16)], axis=0)


def _mla_kernel(q_ref, k_ref, v_ref, o_ref, s0_sc, s1_sc, mt_sc, m_sc, acc_sc):
    g_heads = acc_sc.shape[0]
    tq = q_ref.shape[1]
    tk = tq
    i = pl.program_id(2)
    nv = tk // LANE
    kidx = lax.broadcasted_iota(jnp.int32, (tk, tq), 0)
    qidx = lax.broadcasted_iota(jnp.int32, (tk, tq), 1)
    bufs = (s0_sc, s1_sc)

    def scores(kt, parity, causal=None):
        for g in range(g_heads):
            s = _dot(k_ref[g, pl.ds(pl.multiple_of(kt * tk, tk), tk), :],
                     q_ref[g * LANE:(g + 1) * LANE, :])
            if causal is not None:
                s = jnp.where(causal, s, NEG)
            bufs[parity][g] = s
            mt_sc[parity, g] = jnp.max(s, axis=0, keepdims=True)

    def softmax(kt, parity):
        for g in range(g_heads):
            m_old = m_sc[g]
            m_new = jnp.maximum(m_old, mt_sc[parity, g])
            a = jnp.exp2(m_old - m_new)
            p = jnp.exp2(bufs[parity][g] - m_new)
            vt = jnp.concatenate([v_ref[kt * nv + j, g * MLA_V:(g + 1) * MLA_V, :] for j in range(nv)],
                                 axis=1)
            m_sc[g] = m_new
            acc_sc[g] = a * acc_sc[g] + _dot(_with_ones_rows(vt), p.astype(BF16))

    m_sc[...] = jnp.full(m_sc.shape, NEG, F32)
    acc_sc[...] = jnp.zeros_like(acc_sc)

    def step(k, parity, diagonal=False):
        scores(k + 1, 1 - parity, (kidx <= qidx) if diagonal else None)
        softmax(k, parity)

    scores(0, 0, kidx <= i * tq + qidx)

    def pair(p, _):
        step(2 * p, 0)
        step(2 * p + 1, 1)
        return 0

    lax.fori_loop(0, (i - 1) // 2, pair, 0)

    @pl.when(i % 2 == 0)
    def _():
        @pl.when(i >= 2)
        def _():
            step(i - 2, 0)
            step(i - 1, 1, diagonal=True)
        softmax(i, 0)

    @pl.when(i % 2 == 1)
    def _():
        step(i - 1, 0, diagonal=True)
        softmax(i, 1)

    for g in range(g_heads):
        acc = acc_sc[g]
        o_ref[g * MLA_V:(g + 1) * MLA_V, :] = (acc[:MLA_V] / acc[MLA_V:MLA_V + 1]).astype(BF16)


def _mla(qat, ka, vat):
    B, _, S = qat.shape
    tq = TQ_MLA
    g = G_MLA
    return pl.pallas_call(
        _mla_kernel,
        out_shape=jax.ShapeDtypeStruct((B, MLA_HEADS * MLA_V, S), BF16),
        grid=(B, MLA_HEADS // g, S // tq),
        in_specs=[
            pl.BlockSpec((None, g * LANE, tq), lambda b, h, i: (b, h, i)),
            pl.BlockSpec((None, g, S, LANE), lambda b, h, i: (b, h, 0, 0)),
            pl.BlockSpec((None, S // LANE, g * MLA_V, LANE), lambda b, h, i: (b, 0, h, 0)),
        ],
        out_specs=pl.BlockSpec((None, g * MLA_V, tq), lambda b, h, i: (b, h, i)),
        scratch_shapes=[pltpu.VMEM((g, tq, tq), F32), pltpu.VMEM((g, tq, tq), F32),
                        pltpu.VMEM((2, g, 1, tq), F32), pltpu.VMEM((g, 1, tq), F32),
                        pltpu.VMEM((g, MLA_V + BF16_ROWS, tq), F32)],
        compiler_params=pltpu.CompilerParams(
            dimension_semantics=("parallel", "parallel", "arbitrary"), vmem_limit_bytes=VMEM_LIMIT),
        name="mla",
    )(qat, ka, vat)


def _count_rows(mask):
    r, t = mask.shape
    return jnp.sum(mask.astype(jnp.int32).reshape(r // 8, 8, t), axis=0)


def _dsa_kernel(topk, qit_ref, wit_ref, ki_ref, qbt_ref, kb_ref, vbt_ref, o_ref,
                sc_sc, att0_sc, att1_sc, thr_sc, nge_sc, m_sc, acc_sc):
    tq = o_ref.shape[1]
    ck = CK_DSA
    sub = tq // ck
    i = pl.program_id(1)
    nblk = i + 1

    def rows(ref, r0, n):
        if isinstance(r0, int):
            return ref[r0:r0 + n, :]
        return ref[pl.ds(pl.multiple_of(r0, n), n), :]

    n_lg = tq // LANE

    def sc_load(r0, n):
        return jnp.concatenate([sc_sc[g, pl.ds(pl.multiple_of(r0, n), n), :] for g in range(n_lg)], axis=1)

    def sc_store(r0, n, val):
        for g in range(n_lg):
            sc_sc[g, pl.ds(pl.multiple_of(r0, n), n), :] = val[:, g * LANE:(g + 1) * LANE]

    qpos_c = i * tq + lax.broadcasted_iota(jnp.int32, (ck, tq), 1)
    krow_c = lax.broadcasted_iota(jnp.int32, (ck, tq), 0)

    def score_block(cb):
        for j in range(sub):
            r0 = cb * tq + j * ck
            kc = rows(ki_ref, r0, ck)
            acc = jnp.zeros((ck, tq), F32)
            for h in range(IDX_HEADS):
                rel = _dot(kc, qit_ref[h * LANE:(h + 1) * LANE, :])
                acc = acc + wit_ref[h:h + 1, :] * jnp.maximum(rel, 0.0)
            sc_store(r0, ck, jnp.where(r0 + krow_c <= qpos_c, acc, -jnp.inf))

    def score_pair(p, _):
        score_block(2 * p)
        score_block(2 * p + 1)
        return 0

    lax.fori_loop(0, nblk // 2, score_pair, 0)

    @pl.when(nblk % 2 == 1)
    def _():
        score_block(nblk - 1)

    krow = lax.broadcasted_iota(jnp.int32, (tq, tq), 0)

    def count(pred):
        def body(cb, a):
            return a + _count_rows(pred(sc_load(cb * tq, tq), cb * tq + krow))
        part = lax.fori_loop(0, nblk, body, jnp.zeros((8, tq), jnp.int32))
        return jnp.sum(part, axis=0, keepdims=True)

    def flip(word):
        return jnp.where(word < 0, word ^ jnp.int32(0x7FFFFFFF), word)

    def key_to_score(key):
        return pltpu.bitcast(flip(key), F32)

    thr_sc[...] = jnp.full((1, tq), jnp.finfo(F32).min, F32)

    def bisect(nb):
        assert n_lg == 2
        la, lb = slice(0, LANE), slice(LANE, 2 * LANE)

        def part_count(group, cf):
            n_rows = (nb - 1) * tq + (group + 1) * LANE
            part = jnp.zeros((8, LANE), jnp.int32)
            for r0 in range(0, n_rows, tq):
                r1 = min(r0 + tq, n_rows)
                part = part + _count_rows(sc_sc[group, r0:r1, :] >= cf[0:1, :])
            return part

        def accept(part, cand, ans, n_ans):
            n = jnp.sum(part, axis=0, keepdims=True)
            ok = n >= topk
            return jnp.where(ok, cand, ans), jnp.where(ok, n, n_ans)

        def cand_of(t, ans):
            return jnp.where(t == 0, jnp.zeros_like(ans), ans + jnp.left_shift(jnp.int32(1), 31 - t))

        def body(t, st):
            ans_a, n_a, ans_b, n_b, cand_b, part_b = st
            ans_b, n_b = accept(part_b, cand_b, ans_b, n_b)
            cand_b = cand_of(t, ans_b)
            cand_a = cand_of(t, ans_a)
            ans_a, n_a = accept(part_count(0, key_to_score(cand_a)), cand_a, ans_a, n_a)
            return ans_a, n_a, ans_b, n_b, cand_b, part_count(1, key_to_score(cand_b))

        start = jnp.full((8, LANE), INT_MIN, jnp.int32)
        zero = jnp.zeros((8, LANE), jnp.int32)
        first = body(jnp.int32(0), (start, zero, start, zero, start, zero))
        ans_a, n_a, ans_b, n_b, cand_b, part_b = lax.fori_loop(1, 32, body, first)
        ans_b, n_b = accept(part_b, cand_b, ans_b, n_b)
        thr_sc[:, la] = key_to_score(ans_a)[0:1, :]
        thr_sc[:, lb] = key_to_score(ans_b)[0:1, :]
        nge_sc[:, la] = n_a[0:1, :]
        nge_sc[:, lb] = n_b[0:1, :]

    for v in range(topk // tq, sc_sc.shape[1] // tq):
        pl.when(i == v)(functools.partial(bisect, v + 1))

    @pl.when(i * tq >= topk)
    def _():
        thr = thr_sc[...]

        @pl.when(jnp.max(nge_sc[...]) > topk)
        def _():
            need = topk - count(lambda s, kidx: s > thr)

            idx_bits = (sc_sc.shape[1] - 1).bit_length()

            def jbit(t, j):
                cand = j + jnp.left_shift(jnp.int32(1), idx_bits - 1 - t)
                n = count(lambda s, kidx: (s == thr) & (kidx < cand))
                return jnp.where(n < need, cand, j)

            last = lax.fori_loop(0, idx_bits, jbit, jnp.zeros((1, tq), jnp.int32))

            def drop(cb, _):
                s = sc_load(cb * tq, tq)
                dropped = (s == thr) & (cb * tq + krow > last)
                sc_store(cb * tq, tq, jnp.where(dropped, -jnp.inf, s))
                return 0

            lax.fori_loop(0, nblk, drop, 0)

    thr = thr_sc[...]
    nh = DSA_HEADS
    qall = jnp.concatenate([qbt_ref[h * LANE:(h + 1) * LANE, :] for h in range(nh)], axis=1)
    bufs = (att0_sc, att1_sc)

    def logits(cb, buf):
        bias = jnp.where(sc_load(cb * tq, tq) >= thr, 0.0, NEG)
        for j in range(sub):
            s = _dot(rows(kb_ref, cb * tq + j * ck, ck), qall)
            b = bias[j * ck:(j + 1) * ck, :]
            for h in range(nh):
                buf[h, j * ck:(j + 1) * ck, :] = s[:, h * tq:(h + 1) * tq] + b

    def attend(cb, buf):
        vt = _with_ones_rows(jnp.concatenate([vbt_ref[cb * sub + j] for j in range(sub)], axis=1))
        for h in range(nh):
            s = buf[h]
            m_old = m_sc[h]
            m_new = jnp.maximum(m_old, jnp.max(s, axis=0, keepdims=True))
            a = jnp.exp2(m_old - m_new)
            p = jnp.exp2(s - m_new)
            m_sc[h] = m_new
            acc_sc[h] = a * acc_sc[h] + _dot(vt, p.astype(BF16))

    def step(k, parity):
        logits(k + 1, bufs[1 - parity])
        attend(k, bufs[parity])

    m_sc[...] = jnp.full(m_sc.shape, NEG, F32)
    acc_sc[...] = jnp.zeros_like(acc_sc)
    logits(0, bufs[0])

    def pair(p, _):
        step(2 * p, 0)
        step(2 * p + 1, 1)
        return 0

    lax.fori_loop(0, i // 2, pair, 0)

    @pl.when(i % 2 == 1)
    def _():
        step(i - 1, 0)
        attend(i, bufs[1])

    @pl.when(i % 2 == 0)
    def _():
        attend(i, bufs[0])

    for h in range(nh):
        acc = acc_sc[h]
        o_ref[h * DSA_DIM:(h + 1) * DSA_DIM, :] = (acc[:DSA_DIM] / acc[DSA_DIM:DSA_DIM + 1]).astype(BF16)


def _dsa(qit, wit, ki, qbt, kb, vbt, topk):
    B, _, S = qit.shape
    tq = TQ_DSA
    assert topk % tq == 0 or topk >= S, "query tiles must not straddle the top-k boundary"
    return pl.pallas_call(
        functools.partial(_dsa_kernel, topk),
        out_shape=jax.ShapeDtypeStruct((B, DSA_HEADS * DSA_DIM, S), BF16),
        grid=(B, S // tq),
        in_specs=[
            pl.BlockSpec((None, IDX_HEADS * LANE, tq), lambda b, i: (b, 0, i)),
            pl.BlockSpec((None, IDX_HEADS, tq), lambda b, i: (b, 0, i)),
            pl.BlockSpec((None, S, LANE), lambda b, i: (b, 0, 0)),
            pl.BlockSpec((None, DSA_HEADS * LANE, tq), lambda b, i: (b, 0, i)),
            pl.BlockSpec((None, S, LANE), lambda b, i: (b, 0, 0)),
            pl.BlockSpec((None, S // LANE, DSA_DIM, LANE), lambda b, i: (b, 0, 0, 0)),
        ],
        out_specs=pl.BlockSpec((None, DSA_HEADS * DSA_DIM, tq), lambda b, i: (b, 0, i)),
        scratch_shapes=[
            pltpu.VMEM((tq // LANE, S, LANE), F32),
            pltpu.VMEM((DSA_HEADS, tq, tq), F32),
            pltpu.VMEM((DSA_HEADS, tq, tq), F32),
            pltpu.VMEM((1, tq), F32),
            pltpu.VMEM((1, tq), jnp.int32),
            pltpu.VMEM((DSA_HEADS, 1, tq), F32),
            pltpu.VMEM((DSA_HEADS, DSA_DIM + BF16_ROWS, tq), F32),
        ],
        compiler_params=pltpu.CompilerParams(
            dimension_semantics=("parallel", "arbitrary"), vmem_limit_bytes=VMEM_LIMIT),
        name="dsa",
    )(qit, wit, ki, qbt, kb, vbt)


def _ret_kernel(rqt_ref, rk_ref, rvt_ref, rgt_ref, dt_ref, xi_ref, zeta_ref, cdec_ref, gret_ref, o_ref):
    c = dt_ref.shape[1]
    n_chunks = o_ref.shape[1] // c
    state = [jnp.zeros((RET_DV, LANE), F32) for _ in range(RET_HEADS)]
    for n in range(n_chunks):
        cols = slice(n * c, (n + 1) * c)
        for h in range(RET_HEADS):
            rt = state[h]
            qt = rqt_ref[h * LANE:(h + 1) * LANE, cols]
            k = rk_ref[cols, h * LANE:(h + 1) * LANE]
            vt = rvt_ref[h * RET_DV:(h + 1) * RET_DV, cols]
            inner_t = _dot(k, qt) * dt_ref[h]
            qx = (qt.astype(F32) * xi_ref[h:h + 1, :]).astype(BF16)
            out_t = _dot(vt, inner_t.astype(BF16)) + _dot(rt.astype(BF16), qx)
            vz = (vt.astype(F32) * zeta_ref[h:h + 1, :]).astype(BF16)
            state[h] = rt * cdec_ref[h] + _dot(vz, k)

            mu = jnp.mean(out_t, axis=0, keepdims=True)
            xc = out_t - mu
            var = jnp.mean(xc * xc, axis=0, keepdims=True)
            y = xc * lax.rsqrt(var + EPS) * gret_ref[h * RET_DV:(h + 1) * RET_DV, :]
            g = rgt_ref[h * RET_DV:(h + 1) * RET_DV, cols]
            o_ref[h * RET_DV:(h + 1) * RET_DV, cols] = (g * jax.nn.sigmoid(g) * y).astype(BF16)


def _retention(rqt, rk, rvt, rgt, dt, xi, zeta, cdec, gret):
    B, _, S = rqt.shape

    def full1(shape):
        nd = len(shape)
        return pl.BlockSpec(shape, lambda b: (0,) * nd)

    def per_batch(rows, cols):
        return pl.BlockSpec((None, rows, cols), lambda b: (b, 0, 0))

    return pl.pallas_call(
        _ret_kernel,
        out_shape=jax.ShapeDtypeStruct((B, RET_HEADS * RET_DV, S), BF16),
        grid=(B,),
        in_specs=[
            per_batch(RET_HEADS * LANE, S), per_batch(S, RET_HEADS * LANE),
            per_batch(RET_HEADS * RET_DV, S), per_batch(RET_HEADS * RET_DV, S),
            full1(dt.shape), full1(xi.shape), full1(zeta.shape), full1(cdec.shape), full1(gret.shape),
        ],
        out_specs=per_batch(RET_HEADS * RET_DV, S),
        compiler_params=pltpu.CompilerParams(
            dimension_semantics=("parallel",), vmem_limit_bytes=VMEM_LIMIT),
        name="retention",
    )(rqt, rk, rvt, rgt, dt, xi, zeta, cdec, gret)


def _mlp_kernel(final, x_ref, oat_ref, obt_ref, oct_ref, wout_ref, gmlp_ref, w1_ref, w2_ref, gfin_ref,
                y_ref):
    def tr(ref):
        return ref[...].astype(F32).T.astype(BF16)

    na = MLA_HEADS * MLA_V
    nb = DSA_HEADS * DSA_DIM
    attn = (_dot(tr(oat_ref), wout_ref[0:na, :])
            + _dot(tr(obt_ref), wout_ref[na:na + nb, :])
            + _dot(tr(oct_ref), wout_ref[na + nb:, :]))
    x1 = x_ref[...] + attn
    hf = _rms(x1, gmlp_ref[...]).astype(BF16)
    acc = x1
    for f in range(D_FF // FF_CHUNK):
        u = jnp.maximum(_dot(hf, w1_ref[:, f * FF_CHUNK:(f + 1) * FF_CHUNK]), 0.0)
        acc = acc + _dot((u * u).astype(BF16), w2_ref[f * FF_CHUNK:(f + 1) * FF_CHUNK, :])
    if final:
        acc = _rms(acc, gfin_ref[...])
    y_ref[...] = acc


def _mlp(x, oat, obt, oct, wout, gmlp, w1, w2, gfin, final):
    B, S, D = x.shape
    tm = TM_MLP

    def const(shape):
        nd = len(shape)
        return pl.BlockSpec(shape, lambda b, i: (0,) * nd, pipeline_mode=pl.Buffered(1))

    return pl.pallas_call(
        functools.partial(_mlp_kernel, final),
        out_shape=jax.ShapeDtypeStruct((B, S, D), F32),
        grid=(B, S // tm),
        in_specs=[
            pl.BlockSpec((None, tm, D), lambda b, i: (b, i, 0)),
            pl.BlockSpec((None, oat.shape[1], tm), lambda b, i: (b, 0, i)),
            pl.BlockSpec((None, obt.shape[1], tm), lambda b, i: (b, 0, i)),
            pl.BlockSpec((None, oct.shape[1], tm), lambda b, i: (b, 0, i)),
            const(wout.shape), const(gmlp.shape), const(w1.shape), const(w2.shape), const(gfin.shape),
        ],
        out_specs=pl.BlockSpec((None, tm, D), lambda b, i: (b, i, 0)),
        compiler_params=pltpu.CompilerParams(
            dimension_semantics=("parallel", "parallel"), vmem_limit_bytes=VMEM_LIMIT),
        name="mlp",
    )(x, oat, obt, oct, wout, gmlp, w1, w2, gfin)


def _rope_tables(seq, dim):
    pos = jnp.arange(seq, dtype=F32)
    inv = ROPE_THETA ** (-jnp.arange(0, dim, 2, dtype=F32) / dim)
    ang = pos[:, None] * inv[None, :]
    return jnp.cos(ang), jnp.sin(ang)


def _lane_table(c, s, lo, seq):
    half = c.shape[1]
    ct = jnp.zeros((seq, LANE), F32).at[:, lo:lo + 2 * half].set(jnp.concatenate([c, c], axis=1))
    st = jnp.zeros((seq, LANE), F32).at[:, lo:lo + 2 * half].set(jnp.concatenate([-s, s], axis=1))
    return ct, st


def _pad_cols(w, width):
    return jnp.pad(w, ((0, 0), (0, width - w.shape[1])))


def _prep_layer(w_in, w_uq, w_ukv):
    z = lambda n: jnp.zeros((D_MODEL, n), F32)
    col = lambda a, b: w_in[:, a:b]
    nat = [col(O_CQ, O_CKV), col(O_CKV, O_KR),
           jnp.concatenate([z(64), col(O_KR, O_AQ), z(32)], axis=1),
           _pad_cols(col(O_AK, O_AV), LANE),
           _pad_cols(col(O_KI, O_WI), LANE)]
    for h in range(RET_HEADS):
        nat.append(_pad_cols(col(O_RK + h * RET_DK, O_RK + (h + 1) * RET_DK), LANE))
    wnat = jnp.concatenate(nat, axis=1).astype(BF16)
    wtr = jnp.concatenate([col(O_AQ, O_AK), col(O_QI, O_KI), col(O_AV, O_QI), col(O_RQ, O_RK),
                           col(O_RV, O_RG), col(O_RG, O_END), col(O_WI, O_RQ)], axis=1).T.astype(BF16)
    wuq = w_uq.T.astype(BF16)
    wkv = w_ukv.reshape(MLA_KV_RANK, MLA_HEADS, MLA_NOPE + MLA_V)
    wuk = jnp.pad(wkv[:, :, :MLA_NOPE], ((0, 0), (0, 0), (0, LANE - MLA_NOPE)))
    wuk = wuk.reshape(MLA_KV_RANK, MLA_HEADS * LANE).astype(BF16)
    wuv = wkv[:, :, MLA_NOPE:].reshape(MLA_KV_RANK, MLA_HEADS * MLA_V).T.astype(BF16)
    return wnat, wtr, wuq, wuk, wuv


def kernel(x, g_mix, w_in, g_q, w_uq, g_kv, w_ukv, g_ret, w_out, g_mlp, w_ff1, w_ff2, g_final):
    B, S, _ = x.shape
    topk = min(DSA_TOPK_MAX, S // 4)

    cos32, sin32 = _rope_tables(S, 32)
    cos64, sin64 = _rope_tables(S, 64)
    tabt = jnp.concatenate([cos64.T, sin64.T, cos32.T, sin32.T], axis=0)
    tabn = jnp.stack(_lane_table(cos32, sin32, 64, S) + _lane_table(cos64, sin64, 0, S)
                     + _lane_table(cos32, sin32, 0, S))

    log_gamma = jnp.log1p(-jnp.exp2(-5.0 - jnp.arange(RET_HEADS, dtype=F32)))
    c = C_RET
    pos = jnp.arange(c, dtype=F32)
    rel = pos[None, :] - pos[:, None]
    dt = jnp.where(rel[None] >= 0, jnp.exp(jnp.maximum(rel, 0.0)[None] * log_gamma[:, None, None]), 0.0)
    xi = jnp.exp((pos + 1.0)[None, :] * log_gamma[:, None])
    zeta = jnp.exp((c - 1.0 - pos)[None, :] * log_gamma[:, None])
    cdec = jnp.broadcast_to(jnp.exp(c * log_gamma)[:, None, None], (RET_HEADS, RET_DV, LANE))

    for l in range(DEPTH):
        wnat, wtr, wuq, wuk, wuv = _prep_layer(w_in[l], w_uq[l], w_ukv[l])
        (qat, ka, vat, qit, wit, ki, qbt, kb, vbt, rqt, rk, rvt, rgt) = _proj(
            x, g_mix[l][None, :], g_q[l][None, :], g_kv[l][None, :], wnat, wtr, wuq, wuk, wuv, tabt, tabn)
        oat = _mla(qat, ka, vat)
        obt = _dsa(qit, wit, ki, qbt, kb, vbt, topk)
        gret = jnp.broadcast_to(g_ret[l][:, None], (RET_HEADS * RET_DV, c))
        oct = _retention(rqt, rk, rvt, rgt, dt, xi, zeta, cdec, gret)
        x = _mlp(x, oat, obt, oct, w_out[l].astype(BF16), g_mlp[l][None, :],
                 w_ff1[l].astype(BF16), w_ff2[l].astype(BF16), g_final[None, :], l == DEPTH - 1)
    return x
```

```python
import functools

import numpy as np
import jax
import jax.numpy as jnp
from jax import lax
from jax.experimental import pallas as pl
from jax.experimental.pallas import tpu as pltpu

D_MODEL = 1024
DEPTH = 2
MLA_HEADS = 8
MLA_NOPE = 64
MLA_ROPE = 32
MLA_V = 64
MLA_Q_RANK = 768
MLA_KV_RANK = 256
DSA_HEADS = 4
DSA_DIM = 64
IDX_HEADS = 8
IDX_DIM = 32
DSA_TOPK_MAX = 256
RET_HEADS = 4
RET_DK = 64
RET_DV = 64
D_FF = 4 * D_MODEL
ROPE_THETA = 10000.0
EPS = 1e-6

MLA_COLS = (MLA_Q_RANK, MLA_KV_RANK, MLA_ROPE)
DSA_COLS = (DSA_HEADS * DSA_DIM, DSA_DIM, DSA_DIM, IDX_HEADS * IDX_DIM, IDX_DIM, IDX_HEADS)
RET_COLS = (RET_HEADS * RET_DK, RET_HEADS * RET_DK, RET_HEADS * RET_DV, RET_HEADS * RET_DV)
SPLITS = MLA_COLS + DSA_COLS + RET_COLS
_OFF = np.concatenate([[0], np.cumsum(SPLITS)]).tolist()
(O_CQ, O_CKV, O_KR, O_AQ, O_AK, O_AV, O_QI, O_KI, O_WI, O_RQ, O_RK, O_RV, O_RG, O_END) = _OFF

LANE = 128
BF16_ROWS = 16
VMEM_LIMIT = 56 * 1024 * 1024

BF16 = jnp.bfloat16
F32 = jnp.float32
NEG = -1e30
LOG2E = float(np.log2(np.e))
INT_MIN = np.int32(-2**31)

N_CQ = 0
N_CKV = N_CQ + MLA_Q_RANK
N_KX = N_CKV + MLA_KV_RANK
N_AK = N_KX + LANE
N_RK = N_AK + LANE
N_NAT = N_RK + RET_HEADS * RET_DK
T_AQ = 0
T_QI = T_AQ + DSA_HEADS * DSA_DIM
T_AV = T_QI + IDX_HEADS * IDX_DIM
T_RQ = T_AV + DSA_DIM
T_RV = T_RQ + RET_HEADS * RET_DK
T_RG = T_RV + RET_HEADS * RET_DV
T_WI = T_RG + RET_HEADS * RET_DV
N_TR = T_WI + IDX_HEADS

TM_PROJ = 512
TQ_MLA = 256
G_MLA = 8
TQ_DSA = 256
CK_DSA = 128
C_RET = 256
TM_MLP = 512
FF_CHUNK = 1024

_NT = (((1,), (1,)), ((), ()))


def _rms(x, g):
    return x * lax.rsqrt(jnp.mean(x * x, axis=-1, keepdims=True) + EPS) * g


def _dot(a, b):
    return jnp.dot(a, b, preferred_element_type=F32)


def _dot_nt(a, b):
    return lax.dot_general(a, b, _NT, preferred_element_type=F32)


def _rope_rows(x1, x2, c, s):
    return x1 * c - x2 * s, x1 * s + x2 * c


def _rope_lanes(x, c, s_signed, half, first_los):
    lane = lax.broadcasted_iota(jnp.int32, x.shape, 1)
    first = (lane >= first_los[0]) & (lane < first_los[0] + half)
    for lo in first_los[1:]:
        first = first | ((lane >= lo) & (lane < lo + half))
    rot = jnp.where(first, pltpu.roll(x, LANE - half, 1), pltpu.roll(x, half, 1))
    return x * c + rot * s_signed


def _proj_kernel(x_ref, gmix_ref, gq_ref, gkv_ref, wnat_ref, wtr_ref, wuq_ref, wuk_ref, wuv_ref,
                 tabt_ref, tabn_ref,
                 qat_ref, ka_ref, vat_ref, qit_ref, wit_ref, ki_ref, qbt_ref, kb_ref, vbt_ref,
                 rqt_ref, rk_ref, rvt_ref, rgt_ref):
    tm = x_ref.shape[0]
    hb = _rms(x_ref[...], gmix_ref[...]).astype(BF16)

    c64t = tabt_ref[0:32, :]
    s64t = tabt_ref[32:64, :]
    c32t = tabt_ref[64:80, :]
    s32t = tabt_ref[80:96, :]

    cq = _dot(hb, wnat_ref[:, N_CQ:N_CQ + MLA_Q_RANK])
    nq = _rms(cq, gq_ref[...]).astype(BF16)
    qt = _dot_nt(wuq_ref[...], nq)
    scale_a = (MLA_NOPE + MLA_ROPE) ** -0.5 * LOG2E
    for h in range(MLA_HEADS):
        r0 = h * LANE
        s0 = h * (MLA_NOPE + MLA_ROPE)
        qat_ref[r0:r0 + 64, :] = (qt[s0:s0 + 64, :] * scale_a).astype(BF16)
        o1, o2 = _rope_rows(qt[s0 + 64:s0 + 80, :], qt[s0 + 80:s0 + 96, :], c32t, s32t)
        qat_ref[r0 + 64:r0 + 80, :] = (o1 * scale_a).astype(BF16)
        qat_ref[r0 + 80:r0 + 96, :] = (o2 * scale_a).astype(BF16)
        qat_ref[r0 + 96:r0 + 128, :] = jnp.zeros((32, tm), BF16)

    ckv = _dot(hb, wnat_ref[:, N_CKV:N_CKV + MLA_KV_RANK])
    nkv = _rms(ckv, gkv_ref[...]).astype(BF16)
    kn = _dot(nkv, wuk_ref[...])
    keys = _dot(hb, wnat_ref[:, N_KX:N_NAT])
    slab = lambda off: keys[:, off - N_KX:off - N_KX + LANE]
    kx = _rope_lanes(slab(N_KX), tabn_ref[0], tabn_ref[1], 16, (0, 64))
    ki_ref[...] = kx.astype(BF16)
    krp = jnp.where(lax.broadcasted_iota(jnp.int32, kx.shape, 1) >= 64, kx, 0.0)
    for h in range(MLA_HEADS):
        ka_ref[h] = (kn[:, h * LANE:(h + 1) * LANE] + krp).astype(BF16)
    vt = _dot_nt(wuv_ref[...], nkv).astype(BF16)
    for j in range(tm // LANE):
        vat_ref[j] = vt[:, j * LANE:(j + 1) * LANE]

    kb_ref[...] = _rope_lanes(slab(N_AK), tabn_ref[2], tabn_ref[3], 32, (0,)).astype(BF16)

    for p in range(RET_HEADS // 2):
        rkp = _rope_lanes(slab(N_RK + p * LANE), tabn_ref[4], tabn_ref[5], 32, (0, 64)) * (RET_DK ** -0.5)
        rk_ref[:, p * LANE:(p + 1) * LANE] = rkp.astype(BF16)

    aqt = _dot_nt(wtr_ref[T_AQ:T_AQ + 256, :], hb)
    scale_b = DSA_DIM ** -0.5 * LOG2E
    for h in range(DSA_HEADS):
        o1, o2 = _rope_rows(aqt[h * 64:h * 64 + 32, :], aqt[h * 64 + 32:h * 64 + 64, :], c64t, s64t)
        qbt_ref[h * LANE:h * LANE + 32, :] = (o1 * scale_b).astype(BF16)
        qbt_ref[h * LANE + 32:h * LANE + 64, :] = (o2 * scale_b).astype(BF16)
        qbt_ref[h * LANE + 64:(h + 1) * LANE, :] = jnp.zeros((64, tm), BF16)

    qit = _dot_nt(wtr_ref[T_QI:T_QI + 256, :], hb)
    for h in range(IDX_HEADS):
        o1, o2 = _rope_rows(qit[h * 32:h * 32 + 16, :], qit[h * 32 + 16:h * 32 + 32, :], c32t, s32t)
        qit_ref[h * LANE:h * LANE + 16, :] = o1.astype(BF16)
        qit_ref[h * LANE + 16:h * LANE + 32, :] = o2.astype(BF16)
        qit_ref[h * LANE + 32:(h + 1) * LANE, :] = jnp.zeros((96, tm), BF16)

    idx_scale = (IDX_HEADS ** -0.5) * (IDX_DIM ** -0.5)
    wit_ref[...] = _dot_nt(wtr_ref[T_WI:T_WI + IDX_HEADS, :], hb) * idx_scale

    avt = _dot_nt(wtr_ref[T_AV:T_AV + DSA_DIM, :], hb).astype(BF16)
    for j in range(tm // LANE):
        vbt_ref[j] = avt[:, j * LANE:(j + 1) * LANE]

    rqt = _dot_nt(wtr_ref[T_RQ:T_RQ + 256, :], hb)
    for h in range(RET_HEADS):
        o1, o2 = _rope_rows(rqt[h * 64:h * 64 + 32, :], rqt[h * 64 + 32:h * 64 + 64, :], c64t, s64t)
        r0 = h * LANE + (h % 2) * RET_DK
        z0 = h * LANE + (1 - h % 2) * RET_DK
        rqt_ref[r0:r0 + 32, :] = o1.astype(BF16)
        rqt_ref[r0 + 32:r0 + 64, :] = o2.astype(BF16)
        rqt_ref[z0:z0 + 64, :] = jnp.zeros((64, tm), BF16)

    rvt_ref[...] = _dot_nt(wtr_ref[T_RV:T_RV + 256, :], hb).astype(BF16)
    rgt_ref[...] = _dot_nt(wtr_ref[T_RG:T_RG + 256, :], hb)


def _full(shape):
    nd = len(shape)
    return pl.BlockSpec(shape, lambda b, i: (0,) * nd)


def _proj(x, gmix, gq, gkv, wnat, wtr, wuq, wuk, wuv, tabt, tabn):
    B, S, D = x.shape
    tm = TM_PROJ
    nch = tm // LANE
    outs = [
        (jax.ShapeDtypeStruct((B, MLA_HEADS * LANE, S), BF16), pl.BlockSpec((None, MLA_HEADS * LANE, tm), lambda b, i: (b, 0, i))),
        (jax.ShapeDtypeStruct((B, MLA_HEADS, S, LANE), BF16), pl.BlockSpec((None, MLA_HEADS, tm, LANE), lambda b, i: (b, 0, i, 0))),
        (jax.ShapeDtypeStruct((B, S // LANE, MLA_HEADS * MLA_V, LANE), BF16), pl.BlockSpec((None, nch, MLA_HEADS * MLA_V, LANE), lambda b, i: (b, i, 0, 0))),
        (jax.ShapeDtypeStruct((B, IDX_HEADS * LANE, S), BF16), pl.BlockSpec((None, IDX_HEADS * LANE, tm), lambda b, i: (b, 0, i))),
        (jax.ShapeDtypeStruct((B, IDX_HEADS, S), F32), pl.BlockSpec((None, IDX_HEADS, tm), lambda b, i: (b, 0, i))),
        (jax.ShapeDtypeStruct((B, S, LANE), BF16), pl.BlockSpec((None, tm, LANE), lambda b, i: (b, i, 0))),
        (jax.ShapeDtypeStruct((B, DSA_HEADS * LANE, S), BF16), pl.BlockSpec((None, DSA_HEADS * LANE, tm), lambda b, i: (b, 0, i))),
        (jax.ShapeDtypeStruct((B, S, LANE), BF16), pl.BlockSpec((None, tm, LANE), lambda b, i: (b, i, 0))),
        (jax.ShapeDtypeStruct((B, S // LANE, DSA_DIM, LANE), BF16), pl.BlockSpec((None, nch, DSA_DIM, LANE), lambda b, i: (b, i, 0, 0))),
        (jax.ShapeDtypeStruct((B, RET_HEADS * LANE, S), BF16), pl.BlockSpec((None, RET_HEADS * LANE, tm), lambda b, i: (b, 0, i))),
        (jax.ShapeDtypeStruct((B, S, RET_HEADS * RET_DK), BF16), pl.BlockSpec((None, tm, RET_HEADS * RET_DK), lambda b, i: (b, i, 0))),
        (jax.ShapeDtypeStruct((B, RET_HEADS * RET_DV, S), BF16), pl.BlockSpec((None, RET_HEADS * RET_DV, tm), lambda b, i: (b, 0, i))),
        (jax.ShapeDtypeStruct((B, RET_HEADS * RET_DV, S), F32), pl.BlockSpec((None, RET_HEADS * RET_DV, tm), lambda b, i: (b, 0, i))),
    ]
    in_specs = [
        pl.BlockSpec((None, tm, D), lambda b, i: (b, i, 0)),
        _full(gmix.shape), _full(gq.shape), _full(gkv.shape),
        _full(wnat.shape), _full(wtr.shape), _full(wuq.shape), _full(wuk.shape), _full(wuv.shape),
        pl.BlockSpec((tabt.shape[0], tm), lambda b, i: (0, i)),
        pl.BlockSpec((tabn.shape[0], tm, LANE), lambda b, i: (0, i, 0)),
    ]
    return pl.pallas_call(
        _proj_kernel,
        out_shape=[o[0] for o in outs],
        grid=(B, S // tm),
        in_specs=in_specs,
        out_specs=[o[1] for o in outs],
        compiler_params=pltpu.CompilerParams(
            dimension_semantics=("parallel", "parallel"), vmem_limit_bytes=VMEM_LIMIT),
        name="proj",
    )(x, gmix, gq, gkv, wnat, wtr, wuq, wuk, wuv, tabt, tabn)


def _with_ones_rows(vt):
    return jnp.concatenate([vt, jnp.ones((BF16_ROWS, vt.shape[1]), BF16)], axis=0)


def _mla_kernel(q_ref, k_ref, v_ref, o_ref, s0_sc, s1_sc, mt_sc, m_sc, acc_sc):
    g_heads = acc_sc.shape[0]
    tq = q_ref.shape[1]
    tk = tq
    i = pl.program_id(2)
    nv = tk // LANE
    kidx = lax.broadcasted_iota(jnp.int32, (tk, tq), 0)
    qidx = lax.broadcasted_iota(jnp.int32, (tk, tq), 1)
    bufs = (s0_sc, s1_sc)

    def scores(kt, parity, causal=None):
        for g in range(g_heads):
            s = _dot(k_ref[g, pl.ds(pl.multiple_of(kt * tk, tk), tk), :],
                     q_ref[g * LANE:(g + 1) * LANE, :])
            if causal is not None:
                s = jnp.where(causal, s, NEG)
            bufs[parity][g] = s
            mt_sc[parity, g] = jnp.max(s, axis=0, keepdims=True)

    def softmax(kt, parity):
        for g in range(g_heads):
            m_old = m_sc[g]
            m_new = jnp.maximum(m_old, mt_sc[parity, g])
            a = jnp.exp2(m_old - m_new)
            p = jnp.exp2(bufs[parity][g] - m_new)
            vt = jnp.concatenate([v_ref[kt * nv + j, g * MLA_V:(g + 1) * MLA_V, :] for j in range(nv)],
                                 axis=1)
            m_sc[g] = m_new
            acc_sc[g] = a * acc_sc[g] + _dot(_with_ones_rows(vt), p.astype(BF16))

    m_sc[...] = jnp.full(m_sc.shape, NEG, F32)
    acc_sc[...] = jnp.zeros_like(acc_sc)

    def step(k, parity, diagonal=False):
        scores(k + 1, 1 - parity, (kidx <= qidx) if diagonal else None)
        softmax(k, parity)

    @pl.when(i == 0)
    def _():
        scores(0, 0, kidx <= qidx)

    @pl.when(i > 0)
    def _():
        scores(0, 0)

    def pair(p, _):
        step(2 * p, 0)
        step(2 * p + 1, 1)
        return 0

    lax.fori_loop(0, (i - 1) // 2, pair, 0)

    @pl.when(i % 2 == 0)
    def _():
        @pl.when(i >= 2)
        def _():
            step(i - 2, 0)
            step(i - 1, 1, diagonal=True)
        softmax(i, 0)

    @pl.when(i % 2 == 1)
    def _():
        step(i - 1, 0, diagonal=True)
        softmax(i, 1)

    for g in range(g_heads):
        acc = acc_sc[g]
        o_ref[g * MLA_V:(g + 1) * MLA_V, :] = (acc[:MLA_V] / acc[MLA_V:MLA_V + 1]).astype(BF16)


def _mla(qat, ka, vat):
    B, _, S = qat.shape
    tq = TQ_MLA
    g = G_MLA
    return pl.pallas_call(
        _mla_kernel,
        out_shape=jax.ShapeDtypeStruct((B, MLA_HEADS * MLA_V, S), BF16),
        grid=(B, MLA_HEADS // g, S // tq),
        in_specs=[
            pl.BlockSpec((None, g * LANE, tq), lambda b, h, i: (b, h, i)),
            pl.BlockSpec((None, g, S, LANE), lambda b, h, i: (b, h, 0, 0)),
            pl.BlockSpec((None, S // LANE, g * MLA_V, LANE), lambda b, h, i: (b, 0, h, 0)),
        ],
        out_specs=pl.BlockSpec((None, g * MLA_V, tq), lambda b, h, i: (b, h, i)),
        scratch_shapes=[pltpu.VMEM((g, tq, tq), F32), pltpu.VMEM((g, tq, tq), F32),
                        pltpu.VMEM((2, g, 1, tq), F32), pltpu.VMEM((g, 1, tq), F32),
                        pltpu.VMEM((g, MLA_V + BF16_ROWS, tq), F32)],
        compiler_params=pltpu.CompilerParams(
            dimension_semantics=("parallel", "parallel", "arbitrary"), vmem_limit_bytes=VMEM_LIMIT),
        name="mla",
    )(qat, ka, vat)


def _count_rows(mask):
    r, t = mask.shape
    return jnp.sum(mask.astype(jnp.int32).reshape(r // 8, 8, t), axis=0)


def _dsa_kernel(topk, qit_ref, wit_ref, ki_ref, qbt_ref, kb_ref, vbt_ref, o_ref,
                sc_sc, att0_sc, att1_sc, thr_sc, nge_sc, m_sc, acc_sc):
    tq = o_ref.shape[1]
    ck = CK_DSA
    sub = tq // ck
    i = pl.program_id(1)
    nblk = i + 1

    def rows(ref, r0, n):
        if isinstance(r0, int):
            return ref[r0:r0 + n, :]
        return ref[pl.ds(pl.multiple_of(r0, n), n), :]

    n_lg = tq // LANE

    def sc_load(r0, n):
        return jnp.concatenate([sc_sc[g, pl.ds(pl.multiple_of(r0, n), n), :] for g in range(n_lg)], axis=1)

    def sc_store(r0, n, val):
        for g in range(n_lg):
            sc_sc[g, pl.ds(pl.multiple_of(r0, n), n), :] = val[:, g * LANE:(g + 1) * LANE]

    qpos_c = i * tq + lax.broadcasted_iota(jnp.int32, (ck, tq), 1)
    krow_c = lax.broadcasted_iota(jnp.int32, (ck, tq), 0)

    def score_block(cb):
        for j in range(sub):
            r0 = cb * tq + j * ck
            kc = rows(ki_ref, r0, ck)
            acc = jnp.zeros((ck, tq), F32)
            for h in range(IDX_HEADS):
                rel = _dot(kc, qit_ref[h * LANE:(h + 1) * LANE, :])
                acc = acc + wit_ref[h:h + 1, :] * jnp.maximum(rel, 0.0)
            sc_store(r0, ck, jnp.where(r0 + krow_c <= qpos_c, acc, -jnp.inf))

    def score_pair(p, _):
        score_block(2 * p)
        score_block(2 * p + 1)
        return 0

    lax.fori_loop(0, nblk // 2, score_pair, 0)

    @pl.when(nblk % 2 == 1)
    def _():
        score_block(nblk - 1)

    krow = lax.broadcasted_iota(jnp.int32, (tq, tq), 0)

    def count(pred):
        def body(cb, a):
            return a + _count_rows(pred(sc_load(cb * tq, tq), cb * tq + krow))
        part = lax.fori_loop(0, nblk, body, jnp.zeros((8, tq), jnp.int32))
        return jnp.sum(part, axis=0, keepdims=True)

    def flip(word):
        return jnp.where(word < 0, word ^ jnp.int32(0x7FFFFFFF), word)

    def key_to_score(key):
        return pltpu.bitcast(flip(key), F32)

    thr_sc[...] = jnp.full((1, tq), jnp.finfo(F32).min, F32)

    def bisect(nb):
        assert n_lg == 2
        la, lb = slice(0, LANE), slice(LANE, 2 * LANE)

        def part_count(group, cf):
            n_rows = (nb - 1) * tq + (group + 1) * LANE
            part = jnp.zeros((8, LANE), jnp.int32)
            for r0 in range(0, n_rows, tq):
                r1 = min(r0 + tq, n_rows)
                part = part + _count_rows(sc_sc[group, r0:r1, :] >= cf[0:1, :])
            return part

        def accept(part, cand, ans, n_ans):
            n = jnp.sum(part, axis=0, keepdims=True)
            ok = n >= topk
            return jnp.where(ok, cand, ans), jnp.where(ok, n, n_ans)

        def cand_of(t, ans):
            return jnp.where(t == 0, jnp.zeros_like(ans), ans + jnp.left_shift(jnp.int32(1), 31 - t))

        def body(t, st):
            ans_a, n_a, ans_b, n_b, cand_b, part_b = st
            ans_b, n_b = accept(part_b, cand_b, ans_b, n_b)
            cand_b = cand_of(t, ans_b)
            cand_a = cand_of(t, ans_a)
            ans_a, n_a = accept(part_count(0, key_to_score(cand_a)), cand_a, ans_a, n_a)
            return ans_a, n_a, ans_b, n_b, cand_b, part_count(1, key_to_score(cand_b))

        start = jnp.full((8, LANE), INT_MIN, jnp.int32)
        zero = jnp.zeros((8, LANE), jnp.int32)
        first = body(jnp.int32(0), (start, zero, start, zero, start, zero))
        ans_a, n_a, ans_b, n_b, cand_b, part_b = lax.fori_loop(1, 32, body, first)
        ans_b, n_b = accept(part_b, cand_b, ans_b, n_b)
        thr_sc[:, la] = key_to_score(ans_a)[0:1, :]
        thr_sc[:, lb] = key_to_score(ans_b)[0:1, :]
        nge_sc[:, la] = n_a[0:1, :]
        nge_sc[:, lb] = n_b[0:1, :]

    for v in range(topk // tq, sc_sc.shape[1] // tq):
        pl.when(i == v)(functools.partial(bisect, v + 1))

    @pl.when(i * tq >= topk)
    def _():
        thr = thr_sc[...]

        @pl.when(jnp.max(nge_sc[...]) > topk)
        def _():
            need = topk - count(lambda s, kidx: s > thr)

            idx_bits = (sc_sc.shape[1] - 1).bit_length()

            def jbit(t, j):
                cand = j + jnp.left_shift(jnp.int32(1), idx_bits - 1 - t)
                n = count(lambda s, kidx: (s == thr) & (kidx < cand))
                return jnp.where(n < need, cand, j)

            last = lax.fori_loop(0, idx_bits, jbit, jnp.zeros((1, tq), jnp.int32))

            def drop(cb, _):
                s = sc_load(cb * tq, tq)
                dropped = (s == thr) & (cb * tq + krow > last)
                sc_store(cb * tq, tq, jnp.where(dropped, -jnp.inf, s))
                return 0

            lax.fori_loop(0, nblk, drop, 0)

    thr = thr_sc[...]
    nh = DSA_HEADS
    qall = jnp.concatenate([qbt_ref[h * LANE:(h + 1) * LANE, :] for h in range(nh)], axis=1)
    bufs = (att0_sc, att1_sc)

    def logits(cb, buf):
        bias = jnp.where(sc_load(cb * tq, tq) >= thr, 0.0, NEG)
        for j in range(sub):
            s = _dot(rows(kb_ref, cb * tq + j * ck, ck), qall)
            b = bias[j * ck:(j + 1) * ck, :]
            for h in range(nh):
                buf[h, j * ck:(j + 1) * ck, :] = s[:, h * tq:(h + 1) * tq] + b

    def attend(cb, buf):
        vt = _with_ones_rows(jnp.concatenate([vbt_ref[cb * sub + j] for j in range(sub)], axis=1))
        for h in range(nh):
            s = buf[h]
            m_old = m_sc[h]
            m_new = jnp.maximum(m_old, jnp.max(s, axis=0, keepdims=True))
            a = jnp.exp2(m_old - m_new)
            p = jnp.exp2(s - m_new)
            m_sc[h] = m_new
            acc_sc[h] = a * acc_sc[h] + _dot(vt, p.astype(BF16))

    def step(k, parity):
        logits(k + 1, bufs[1 - parity])
        attend(k, bufs[parity])

    m_sc[...] = jnp.full(m_sc.shape, NEG, F32)
    acc_sc[...] = jnp.zeros_like(acc_sc)
    logits(0, bufs[0])

    def pair(p, _):
        step(2 * p, 0)
        step(2 * p + 1, 1)
        return 0

    lax.fori_loop(0, i // 2, pair, 0)

    @pl.when(i % 2 == 1)
    def _():
        step(i - 1, 0)
        attend(i, bufs[1])

    @pl.when(i % 2 == 0)
    def _():
        attend(i, bufs[0])

    for h in range(nh):
        acc = acc_sc[h]
        o_ref[h * DSA_DIM:(h + 1) * DSA_DIM, :] = (acc[:DSA_DIM] / acc[DSA_DIM:DSA_DIM + 1]).astype(BF16)


def _dsa(qit, wit, ki, qbt, kb, vbt, topk):
    B, _, S = qit.shape
    tq = TQ_DSA
    assert topk % tq == 0 or topk >= S, "query tiles must not straddle the top-k boundary"
    return pl.pallas_call(
        functools.partial(_dsa_kernel, topk),
        out_shape=jax.ShapeDtypeStruct((B, DSA_HEADS * DSA_DIM, S), BF16),
        grid=(B, S // tq),
        in_specs=[
            pl.BlockSpec((None, IDX_HEADS * LANE, tq), lambda b, i: (b, 0, i)),
            pl.BlockSpec((None, IDX_HEADS, tq), lambda b, i: (b, 0, i)),
            pl.BlockSpec((None, S, LANE), lambda b, i: (b, 0, 0)),
            pl.BlockSpec((None, DSA_HEADS * LANE, tq), lambda b, i: (b, 0, i)),
            pl.BlockSpec((None, S, LANE), lambda b, i: (b, 0, 0)),
            pl.BlockSpec((None, S // LANE, DSA_DIM, LANE), lambda b, i: (b, 0, 0, 0)),
        ],
        out_specs=pl.BlockSpec((None, DSA_HEADS * DSA_DIM, tq), lambda b, i: (b, 0, i)),
        scratch_shapes=[
            pltpu.VMEM((tq // LANE, S, LANE), F32),
            pltpu.VMEM((DSA_HEADS, tq, tq), F32),
            pltpu.VMEM((DSA_HEADS, tq, tq), F32),
            pltpu.VMEM((1, tq), F32),
            pltpu.VMEM((1, tq), jnp.int32),
            pltpu.VMEM((DSA_HEADS, 1, tq), F32),
            pltpu.VMEM((DSA_HEADS, DSA_DIM + BF16_ROWS, tq), F32),
        ],
        compiler_params=pltpu.CompilerParams(
            dimension_semantics=("parallel", "arbitrary"), vmem_limit_bytes=VMEM_LIMIT),
        name="dsa",
    )(qit, wit, ki, qbt, kb, vbt)


def _ret_kernel(rqt_ref, rk_ref, rvt_ref, rgt_ref, dt_ref, xi_ref, zeta_ref, cdec_ref, gret_ref, o_ref):
    c = dt_ref.shape[1]
    n_chunks = o_ref.shape[1] // c
    state = [jnp.zeros((RET_DV, LANE), F32) for _ in range(RET_HEADS)]
    for n in range(n_chunks):
        cols = slice(n * c, (n + 1) * c)
        for h in range(RET_HEADS):
            rt = state[h]
            qt = rqt_ref[h * LANE:(h + 1) * LANE, cols]
            k = rk_ref[cols, (h // 2) * LANE:(h // 2 + 1) * LANE]
            vt = rvt_ref[h * RET_DV:(h + 1) * RET_DV, cols]
            inner_t = _dot(k, qt) * dt_ref[h]
            qx = (qt.astype(F32) * xi_ref[h:h + 1, :]).astype(BF16)
            out_t = _dot(vt, inner_t.astype(BF16)) + _dot(rt.astype(BF16), qx)
            vz = (vt.astype(F32) * zeta_ref[h:h + 1, :]).astype(BF16)
            state[h] = rt * cdec_ref[h] + _dot(vz, k)

            mu = jnp.mean(out_t, axis=0, keepdims=True)
            xc = out_t - mu
            var = jnp.mean(xc * xc, axis=0, keepdims=True)
            y = xc * lax.rsqrt(var + EPS) * gret_ref[h * RET_DV:(h + 1) * RET_DV, :]
            g = rgt_ref[h * RET_DV:(h + 1) * RET_DV, cols]
            o_ref[h * RET_DV:(h + 1) * RET_DV, cols] = (g * jax.nn.sigmoid(g) * y).astype(BF16)


def _retention(rqt, rk, rvt, rgt, dt, xi, zeta, cdec, gret):
    B, _, S = rqt.shape

    def full1(shape):
        nd = len(shape)
        return pl.BlockSpec(shape, lambda b: (0,) * nd)

    def per_batch(rows, cols):
        return pl.BlockSpec((None, rows, cols), lambda b: (b, 0, 0))

    return pl.pallas_call(
        _ret_kernel,
        out_shape=jax.ShapeDtypeStruct((B, RET_HEADS * RET_DV, S), BF16),
        grid=(B,),
        in_specs=[
            per_batch(RET_HEADS * LANE, S), per_batch(S, RET_HEADS * RET_DK),
            per_batch(RET_HEADS * RET_DV, S), per_batch(RET_HEADS * RET_DV, S),
            full1(dt.shape), full1(xi.shape), full1(zeta.shape), full1(cdec.shape), full1(gret.shape),
        ],
        out_specs=per_batch(RET_HEADS * RET_DV, S),
        compiler_params=pltpu.CompilerParams(
            dimension_semantics=("parallel",), vmem_limit_bytes=VMEM_LIMIT),
        name="retention",
    )(rqt, rk, rvt, rgt, dt, xi, zeta, cdec, gret)


def _mlp_kernel(final, x_ref, oat_ref, obt_ref, oct_ref, wout_ref, gmlp_ref, w1_ref, w2_ref, gfin_ref,
                y_ref):
    def tr(ref):
        return ref[...].astype(F32).T.astype(BF16)

    na = MLA_HEADS * MLA_V
    nb = DSA_HEADS * DSA_DIM
    attn = (_dot(tr(oat_ref), wout_ref[0:na, :])
            + _dot(tr(obt_ref), wout_ref[na:na + nb, :])
            + _dot(tr(oct_ref), wout_ref[na + nb:, :]))
    x1 = x_ref[...] + attn
    hf = _rms(x1, gmlp_ref[...]).astype(BF16)
    acc = x1
    for f in range(D_FF // FF_CHUNK):
        u = jnp.maximum(_dot(hf, w1_ref[:, f * FF_CHUNK:(f + 1) * FF_CHUNK]), 0.0)
        acc = acc + _dot((u * u).astype(BF16), w2_ref[f * FF_CHUNK:(f + 1) * FF_CHUNK, :])
    if final:
        acc = _rms(acc, gfin_ref[...])
    y_ref[...] = acc


def _mlp(x, oat, obt, oct, wout, gmlp, w1, w2, gfin, final):
    B, S, D = x.shape
    tm = TM_MLP

    def const(shape):
        nd = len(shape)
        return pl.BlockSpec(shape, lambda b, i: (0,) * nd, pipeline_mode=pl.Buffered(1))

    return pl.pallas_call(
        functools.partial(_mlp_kernel, final),
        out_shape=jax.ShapeDtypeStruct((B, S, D), F32),
        grid=(B, S // tm),
        in_specs=[
            pl.BlockSpec((None, tm, D), lambda b, i: (b, i, 0)),
            pl.BlockSpec((None, oat.shape[1], tm), lambda b, i: (b, 0, i)),
            pl.BlockSpec((None, obt.shape[1], tm), lambda b, i: (b, 0, i)),
            pl.BlockSpec((None, oct.shape[1], tm), lambda b, i: (b, 0, i)),
            const(wout.shape), const(gmlp.shape), const(w1.shape), const(w2.shape), const(gfin.shape),
        ],
        out_specs=pl.BlockSpec((None, tm, D), lambda b, i: (b, i, 0)),
        compiler_params=pltpu.CompilerParams(
            dimension_semantics=("parallel", "parallel"), vmem_limit_bytes=VMEM_LIMIT),
        name="mlp",
    )(x, oat, obt, oct, wout, gmlp, w1, w2, gfin)


def _rope_tables(seq, dim):
    pos = jnp.arange(seq, dtype=F32)
    inv = ROPE_THETA ** (-jnp.arange(0, dim, 2, dtype=F32) / dim)
    ang = pos[:, None] * inv[None, :]
    return jnp.cos(ang), jnp.sin(ang)


def _lane_table(c, s, lo, seq):
    half = c.shape[1]
    ct = jnp.zeros((seq, LANE), F32).at[:, lo:lo + 2 * half].set(jnp.concatenate([c, c], axis=1))
    st = jnp.zeros((seq, LANE), F32).at[:, lo:lo + 2 * half].set(jnp.concatenate([-s, s], axis=1))
    return ct, st


def _pad_cols(w, width):
    return jnp.pad(w, ((0, 0), (0, width - w.shape[1])))


def _prep_layer(w_in, w_uq, w_ukv):
    z = lambda n: jnp.zeros((D_MODEL, n), F32)
    col = lambda a, b: w_in[:, a:b]
    nat = [col(O_CQ, O_CKV), col(O_CKV, O_KR),
           jnp.concatenate([col(O_KI, O_WI), z(32), col(O_KR, O_AQ), z(32)], axis=1),
           _pad_cols(col(O_AK, O_AV), LANE),
           col(O_RK, O_RV)]
    wnat = jnp.concatenate(nat, axis=1).astype(BF16)
    wtr = jnp.concatenate([col(O_AQ, O_AK), col(O_QI, O_KI), col(O_AV, O_QI), col(O_RQ, O_RK),
                           col(O_RV, O_RG), col(O_RG, O_END), col(O_WI, O_RQ)], axis=1).T.astype(BF16)
    wuq = w_uq.T.astype(BF16)
    wkv = w_ukv.reshape(MLA_KV_RANK, MLA_HEADS, MLA_NOPE + MLA_V)
    wuk = jnp.pad(wkv[:, :, :MLA_NOPE], ((0, 0), (0, 0), (0, LANE - MLA_NOPE)))
    wuk = wuk.reshape(MLA_KV_RANK, MLA_HEADS * LANE).astype(BF16)
    wuv = wkv[:, :, MLA_NOPE:].reshape(MLA_KV_RANK, MLA_HEADS * MLA_V).T.astype(BF16)
    return wnat, wtr, wuq, wuk, wuv


def kernel(x, g_mix, w_in, g_q, w_uq, g_kv, w_ukv, g_ret, w_out, g_mlp, w_ff1, w_ff2, g_final):
    B, S, _ = x.shape
    topk = min(DSA_TOPK_MAX, S // 4)

    cos32, sin32 = _rope_tables(S, 32)
    cos64, sin64 = _rope_tables(S, 64)
    tabt = jnp.concatenate([cos64.T, sin64.T, cos32.T, sin32.T], axis=0)
    both = lambda t1, t2: (t1[0] + t2[0], t1[1] + t2[1])
    tabn = jnp.stack(both(_lane_table(cos32, sin32, 0, S), _lane_table(cos32, sin32, 64, S))
                     + _lane_table(cos64, sin64, 0, S)
                     + both(_lane_table(cos64, sin64, 0, S), _lane_table(cos64, sin64, 64, S)))

    log_gamma = jnp.log1p(-jnp.exp2(-5.0 - jnp.arange(RET_HEADS, dtype=F32)))
    c = C_RET
    pos = jnp.arange(c, dtype=F32)
    rel = pos[None, :] - pos[:, None]
    dt = jnp.where(rel[None] >= 0, jnp.exp(jnp.maximum(rel, 0.0)[None] * log_gamma[:, None, None]), 0.0)
    xi = jnp.exp((pos + 1.0)[None, :] * log_gamma[:, None])
    zeta = jnp.exp((c - 1.0 - pos)[None, :] * log_gamma[:, None])
    cdec = jnp.broadcast_to(jnp.exp(c * log_gamma)[:, None, None], (RET_HEADS, RET_DV, LANE))

    for l in range(DEPTH):
        wnat, wtr, wuq, wuk, wuv = _prep_layer(w_in[l], w_uq[l], w_ukv[l])
        (qat, ka, vat, qit, wit, ki, qbt, kb, vbt, rqt, rk, rvt, rgt) = _proj(
            x, g_mix[l][None, :], g_q[l][None, :], g_kv[l][None, :], wnat, wtr, wuq, wuk, wuv, tabt, tabn)
        oat = _mla(qat, ka, vat)
        obt = _dsa(qit, wit, ki, qbt, kb, vbt, topk)
        gret = jnp.broadcast_to(g_ret[l][:, None], (RET_HEADS * RET_DV, c))
        oct = _retention(rqt, rk, rvt, rgt, dt, xi, zeta, cdec, gret)
        x = _mlp(x, oat, obt, oct, w_out[l].astype(BF16), g_mlp[l][None, :],
                 w_ff1[l].astype(BF16), w_ff2[l].astype(BF16), g_final[None, :], l == DEPTH - 1)
    return x
```

```python
import functools

import numpy as np
import jax
import jax.numpy as jnp
from jax import lax
from jax.experimental import pallas as pl
from jax.experimental.pallas import tpu as pltpu

D_MODEL = 1024
DEPTH = 2
MLA_HEADS = 8
MLA_NOPE = 64
MLA_ROPE = 32
MLA_V = 64
MLA_Q_RANK = 768
MLA_KV_RANK = 256
DSA_HEADS = 4
DSA_DIM = 64
IDX_HEADS = 8
IDX_DIM = 32
DSA_TOPK_MAX = 256
RET_HEADS = 4
RET_DK = 64
RET_DV = 64
D_FF = 4 * D_MODEL
ROPE_THETA = 10000.0
EPS = 1e-6

MLA_COLS = (MLA_Q_RANK, MLA_KV_RANK, MLA_ROPE)
DSA_COLS = (DSA_HEADS * DSA_DIM, DSA_DIM, DSA_DIM, IDX_HEADS * IDX_DIM, IDX_DIM, IDX_HEADS)
RET_COLS = (RET_HEADS * RET_DK, RET_HEADS * RET_DK, RET_HEADS * RET_DV, RET_HEADS * RET_DV)
SPLITS = MLA_COLS + DSA_COLS + RET_COLS
_OFF = np.concatenate([[0], np.cumsum(SPLITS)]).tolist()
(O_CQ, O_CKV, O_KR, O_AQ, O_AK, O_AV, O_QI, O_KI, O_WI, O_RQ, O_RK, O_RV, O_RG, O_END) = _OFF

LANE = 128
BF16_ROWS = 16
VMEM_LIMIT = 56 * 1024 * 1024

BF16 = jnp.bfloat16
F32 = jnp.float32
NEG = -1e30
LOG2E = float(np.log2(np.e))
INT_MIN = np.int32(-2**31)

N_CQ = 0
N_CKV = N_CQ + MLA_Q_RANK
N_KX = N_CKV + MLA_KV_RANK
N_AK = N_KX + LANE
N_RK = N_AK + LANE
N_NAT = N_RK + RET_HEADS * RET_DK
T_AQ = 0
T_QI = T_AQ + DSA_HEADS * DSA_DIM
T_AV = T_QI + IDX_HEADS * IDX_DIM
T_RQ = T_AV + DSA_DIM
T_RV = T_RQ + RET_HEADS * RET_DK
T_RG = T_RV + RET_HEADS * RET_DV
T_WI = T_RG + RET_HEADS * RET_DV
N_TR = T_WI + IDX_HEADS

TM_PROJ = 512
TQ_MLA = 256
G_MLA = 8
TQ_DSA = 256
CK_DSA = 128
C_RET = 256
TM_MLP = 512
FF_CHUNK = 1024

_NT = (((1,), (1,)), ((), ()))


def _rms(x, g):
    return x * lax.rsqrt(jnp.mean(x * x, axis=-1, keepdims=True) + EPS) * g


def _dot(a, b):
    return jnp.dot(a, b, preferred_element_type=F32)


def _dot_nt(a, b):
    return lax.dot_general(a, b, _NT, preferred_element_type=F32)


def _rope_rows(x1, x2, c, s):
    return x1 * c - x2 * s, x1 * s + x2 * c


def _rope_lanes(x, c, s_signed, half, first_los):
    lane = lax.broadcasted_iota(jnp.int32, x.shape, 1)
    first = (lane >= first_los[0]) & (lane < first_los[0] + half)
    for lo in first_los[1:]:
        first = first | ((lane >= lo) & (lane < lo + half))
    rot = jnp.where(first, pltpu.roll(x, LANE - half, 1), pltpu.roll(x, half, 1))
    return x * c + rot * s_signed


def _proj_kernel(x_ref, gmix_ref, gq_ref, gkv_ref, wnat_ref, wtr_ref, wuq_ref, wuk_ref, wuv_ref,
                 tabt_ref, tabn_ref,
                 qat_ref, ka_ref, vat_ref, qit_ref, wit_ref, ki_ref, qbt_ref, kb_ref, vbt_ref,
                 rqt_ref, rk_ref, rvt_ref, rgt_ref):
    tm = x_ref.shape[0]
    hb = _rms(x_ref[...], gmix_ref[...]).astype(BF16)

    c64t = tabt_ref[0:32, :]
    s64t = tabt_ref[32:64, :]
    c32t = tabt_ref[64:80, :]
    s32t = tabt_ref[80:96, :]

    cq = _dot(hb, wnat_ref[:, N_CQ:N_CQ + MLA_Q_RANK])
    nq = _rms(cq, gq_ref[...]).astype(BF16)
    qt = _dot_nt(wuq_ref[...], nq)
    scale_a = (MLA_NOPE + MLA_ROPE) ** -0.5 * LOG2E
    for h in range(MLA_HEADS):
        r0 = h * LANE
        s0 = h * (MLA_NOPE + MLA_ROPE)
        qat_ref[r0:r0 + 64, :] = (qt[s0:s0 + 64, :] * scale_a).astype(BF16)
        o1, o2 = _rope_rows(qt[s0 + 64:s0 + 80, :], qt[s0 + 80:s0 + 96, :], c32t, s32t)
        qat_ref[r0 + 64:r0 + 80, :] = (o1 * scale_a).astype(BF16)
        qat_ref[r0 + 80:r0 + 96, :] = (o2 * scale_a).astype(BF16)
        qat_ref[r0 + 96:r0 + 128, :] = jnp.zeros((32, tm), BF16)

    ckv = _dot(hb, wnat_ref[:, N_CKV:N_CKV + MLA_KV_RANK])
    nkv = _rms(ckv, gkv_ref[...]).astype(BF16)
    kn = _dot(nkv, wuk_ref[...])
    keys = _dot(hb, wnat_ref[:, N_KX:N_NAT])
    slab = lambda off: keys[:, off - N_KX:off - N_KX + LANE]
    kx = _rope_lanes(slab(N_KX), tabn_ref[0], tabn_ref[1], 16, (0, 64))
    ki_ref[...] = kx.astype(BF16)
    krp = jnp.where(lax.broadcasted_iota(jnp.int32, kx.shape, 1) >= 64, kx, 0.0)
    for h in range(MLA_HEADS):
        ka_ref[h] = (kn[:, h * LANE:(h + 1) * LANE] + krp).astype(BF16)
    vt = _dot_nt(wuv_ref[...], nkv).astype(BF16)
    for j in range(tm // LANE):
        vat_ref[j] = vt[:, j * LANE:(j + 1) * LANE]

    kb_ref[...] = _rope_lanes(slab(N_AK), tabn_ref[2], tabn_ref[3], 32, (0,)).astype(BF16)

    for p in range(RET_HEADS // 2):
        rkp = _rope_lanes(slab(N_RK + p * LANE), tabn_ref[4], tabn_ref[5], 32, (0, 64)) * (RET_DK ** -0.5)
        rk_ref[:, p * LANE:(p + 1) * LANE] = rkp.astype(BF16)

    aqt = _dot_nt(wtr_ref[T_AQ:T_AQ + 256, :], hb)
    scale_b = DSA_DIM ** -0.5 * LOG2E
    for h in range(DSA_HEADS):
        o1, o2 = _rope_rows(aqt[h * 64:h * 64 + 32, :], aqt[h * 64 + 32:h * 64 + 64, :], c64t, s64t)
        qbt_ref[h * LANE:h * LANE + 32, :] = (o1 * scale_b).astype(BF16)
        qbt_ref[h * LANE + 32:h * LANE + 64, :] = (o2 * scale_b).astype(BF16)
        qbt_ref[h * LANE + 64:(h + 1) * LANE, :] = jnp.zeros((64, tm), BF16)

    qit = _dot_nt(wtr_ref[T_QI:T_QI + 256, :], hb)
    for h in range(IDX_HEADS):
        o1, o2 = _rope_rows(qit[h * 32:h * 32 + 16, :], qit[h * 32 + 16:h * 32 + 32, :], c32t, s32t)
        qit_ref[h * LANE:h * LANE + 16, :] = o1.astype(BF16)
        qit_ref[h * LANE + 16:h * LANE + 32, :] = o2.astype(BF16)
        qit_ref[h * LANE + 32:(h + 1) * LANE, :] = jnp.zeros((96, tm), BF16)

    idx_scale = (IDX_HEADS ** -0.5) * (IDX_DIM ** -0.5)
    wit_ref[...] = _dot_nt(wtr_ref[T_WI:T_WI + IDX_HEADS, :], hb) * idx_scale

    avt = _dot_nt(wtr_ref[T_AV:T_AV + DSA_DIM, :], hb).astype(BF16)
    for j in range(tm // LANE):
        vbt_ref[j] = avt[:, j * LANE:(j + 1) * LANE]

    rqt = _dot_nt(wtr_ref[T_RQ:T_RQ + 256, :], hb)
    for h in range(RET_HEADS):
        o1, o2 = _rope_rows(rqt[h * 64:h * 64 + 32, :], rqt[h * 64 + 32:h * 64 + 64, :], c64t, s64t)
        r0 = h * LANE + (h % 2) * RET_DK
        z0 = h * LANE + (1 - h % 2) * RET_DK
        rqt_ref[r0:r0 + 32, :] = o1.astype(BF16)
        rqt_ref[r0 + 32:r0 + 64, :] = o2.astype(BF16)
        rqt_ref[z0:z0 + 64, :] = jnp.zeros((64, tm), BF16)

    rvt_ref[...] = _dot_nt(wtr_ref[T_RV:T_RV + 256, :], hb).astype(BF16)
    rgt_ref[...] = _dot_nt(wtr_ref[T_RG:T_RG + 256, :], hb)


def _full(shape):
    nd = len(shape)
    return pl.BlockSpec(shape, lambda b, i: (0,) * nd)


def _proj(x, gmix, gq, gkv, wnat, wtr, wuq, wuk, wuv, tabt, tabn):
    B, S, D = x.shape
    tm = TM_PROJ
    nch = tm // LANE
    outs = [
        (jax.ShapeDtypeStruct((B, MLA_HEADS * LANE, S), BF16), pl.BlockSpec((None, MLA_HEADS * LANE, tm), lambda b, i: (b, 0, i))),
        (jax.ShapeDtypeStruct((B, MLA_HEADS, S, LANE), BF16), pl.BlockSpec((None, MLA_HEADS, tm, LANE), lambda b, i: (b, 0, i, 0))),
        (jax.ShapeDtypeStruct((B, S // LANE, MLA_HEADS * MLA_V, LANE), BF16), pl.BlockSpec((None, nch, MLA_HEADS * MLA_V, LANE), lambda b, i: (b, i, 0, 0))),
        (jax.ShapeDtypeStruct((B, IDX_HEADS * LANE, S), BF16), pl.BlockSpec((None, IDX_HEADS * LANE, tm), lambda b, i: (b, 0, i))),
        (jax.ShapeDtypeStruct((B, IDX_HEADS, S), F32), pl.BlockSpec((None, IDX_HEADS, tm), lambda b, i: (b, 0, i))),
        (jax.ShapeDtypeStruct((B, S, LANE), BF16), pl.BlockSpec((None, tm, LANE), lambda b, i: (b, i, 0))),
        (jax.ShapeDtypeStruct((B, DSA_HEADS * LANE, S), BF16), pl.BlockSpec((None, DSA_HEADS * LANE, tm), lambda b, i: (b, 0, i))),
        (jax.ShapeDtypeStruct((B, S, LANE), BF16), pl.BlockSpec((None, tm, LANE), lambda b, i: (b, i, 0))),
        (jax.ShapeDtypeStruct((B, S // LANE, DSA_DIM, LANE), BF16), pl.BlockSpec((None, nch, DSA_DIM, LANE), lambda b, i: (b, i, 0, 0))),
        (jax.ShapeDtypeStruct((B, RET_HEADS * LANE, S), BF16), pl.BlockSpec((None, RET_HEADS * LANE, tm), lambda b, i: (b, 0, i))),
        (jax.ShapeDtypeStruct((B, S, RET_HEADS * RET_DK), BF16), pl.BlockSpec((None, tm, RET_HEADS * RET_DK), lambda b, i: (b, i, 0))),
        (jax.ShapeDtypeStruct((B, RET_HEADS * RET_DV, S), BF16), pl.BlockSpec((None, RET_HEADS * RET_DV, tm), lambda b, i: (b, 0, i))),
        (jax.ShapeDtypeStruct((B, RET_HEADS * RET_DV, S), F32), pl.BlockSpec((None, RET_HEADS * RET_DV, tm), lambda b, i: (b, 0, i))),
    ]
    in_specs = [
        pl.BlockSpec((None, tm, D), lambda b, i: (b, i, 0)),
        _full(gmix.shape), _full(gq.shape), _full(gkv.shape),
        _full(wnat.shape), _full(wtr.shape), _full(wuq.shape), _full(wuk.shape), _full(wuv.shape),
        pl.BlockSpec((tabt.shape[0], tm), lambda b, i: (0, i)),
        pl.BlockSpec((tabn.shape[0], tm, LANE), lambda b, i: (0, i, 0)),
    ]
    return pl.pallas_call(
        _proj_kernel,
        out_shape=[o[0] for o in outs],
        grid=(B, S // tm),
        in_specs=in_specs,
        out_specs=[o[1] for o in outs],
        compiler_params=pltpu.CompilerParams(
            dimension_semantics=("parallel", "parallel"), vmem_limit_bytes=VMEM_LIMIT),
        name="proj",
    )(x, gmix, gq, gkv, wnat, wtr, wuq, wuk, wuv, tabt, tabn)


def _with_ones_rows(vt):
    return jnp.concatenate([vt, jnp.ones((BF16_ROWS, vt.shape[1]), BF16)], axis=0)


def _mla_kernel(q_ref, k_ref, v_ref, o_ref, s0_sc, s1_sc, mt_sc, m_sc, acc_sc):
    g_heads = acc_sc.shape[0]
    tq = q_ref.shape[1]
    tk = tq
    i = pl.program_id(2)
    nv = tk // LANE
    kidx = lax.broadcasted_iota(jnp.int32, (tk, tq), 0)
    qidx = lax.broadcasted_iota(jnp.int32, (tk, tq), 1)
    bufs = (s0_sc, s1_sc)

    def scores(kt, parity, causal=None):
        for g in range(g_heads):
            s = _dot(k_ref[g, pl.ds(pl.multiple_of(kt * tk, tk), tk), :],
                     q_ref[g * LANE:(g + 1) * LANE, :])
            if causal is not None:
                s = jnp.where(causal, s, NEG)
            bufs[parity][g] = s
            mt_sc[parity, g] = jnp.max(s, axis=0, keepdims=True)

    def softmax(kt, parity):
        for g in range(g_heads):
            m_old = m_sc[g]
            m_new = jnp.maximum(m_old, mt_sc[parity, g])
            a = jnp.exp2(m_old - m_new)
            p = jnp.exp2(bufs[parity][g] - m_new)
            vt = jnp.concatenate([v_ref[kt * nv + j, g * MLA_V:(g + 1) * MLA_V, :] for j in range(nv)],
                                 axis=1)
            m_sc[g] = m_new
            acc_sc[g] = a * acc_sc[g] + _dot(_with_ones_rows(vt), p.astype(BF16))

    m_sc[...] = jnp.full(m_sc.shape, NEG, F32)
    acc_sc[...] = jnp.zeros_like(acc_sc)

    def step(k, parity, diagonal=False):
        scores(k + 1, 1 - parity, (kidx <= qidx) if diagonal else None)
        softmax(k, parity)

    scores(0, 0, kidx <= i * tq + qidx)

    def pair(p, _):
        step(2 * p, 0)
        step(2 * p + 1, 1)
        return 0

    lax.fori_loop(0, (i - 1) // 2, pair, 0)

    @pl.when(i % 2 == 0)
    def _():
        @pl.when(i >= 2)
        def _():
            step(i - 2, 0)
            step(i - 1, 1, diagonal=True)
        softmax(i, 0)

    @pl.when(i % 2 == 1)
    def _():
        step(i - 1, 0, diagonal=True)
        softmax(i, 1)

    for g in range(g_heads):
        acc = acc_sc[g]
        o_ref[g * MLA_V:(g + 1) * MLA_V, :] = (acc[:MLA_V] / acc[MLA_V:MLA_V + 1]).astype(BF16)


def _mla(qat, ka, vat):
    B, _, S = qat.shape
    tq = TQ_MLA
    g = G_MLA
    return pl.pallas_call(
        _mla_kernel,
        out_shape=jax.ShapeDtypeStruct((B, MLA_HEADS * MLA_V, S), BF16),
        grid=(B, MLA_HEADS // g, S // tq),
        in_specs=[
            pl.BlockSpec((None, g * LANE, tq), lambda b, h, i: (b, h, i)),
            pl.BlockSpec((None, g, S, LANE), lambda b, h, i: (b, h, 0, 0)),
            pl.BlockSpec((None, S // LANE, g * MLA_V, LANE), lambda b, h, i: (b, 0, h, 0)),
        ],
        out_specs=pl.BlockSpec((None, g * MLA_V, tq), lambda b, h, i: (b, h, i)),
        scratch_shapes=[pltpu.VMEM((g, tq, tq), F32), pltpu.VMEM((g, tq, tq), F32),
                        pltpu.VMEM((2, g, 1, tq), F32), pltpu.VMEM((g, 1, tq), F32),
                        pltpu.VMEM((g, MLA_V + BF16_ROWS, tq), F32)],
        compiler_params=pltpu.CompilerParams(
            dimension_semantics=("parallel", "parallel", "arbitrary"), vmem_limit_bytes=VMEM_LIMIT),
        name="mla",
    )(qat, ka, vat)


def _count_rows(mask):
    r, t = mask.shape
    return jnp.sum(mask.astype(jnp.int32).reshape(r // 8, 8, t), axis=0)


def _dsa_kernel(topk, qit_ref, wit_ref, ki_ref, qbt_ref, kb_ref, vbt_ref, o_ref,
                sc_sc, att0_sc, att1_sc, thr_sc, nge_sc, m_sc, acc_sc):
    tq = o_ref.shape[1]
    ck = CK_DSA
    sub = tq // ck
    i = pl.program_id(1)
    nblk = i + 1

    def rows(ref, r0, n):
        if isinstance(r0, int):
            return ref[r0:r0 + n, :]
        return ref[pl.ds(pl.multiple_of(r0, n), n), :]

    n_lg = tq // LANE

    def sc_load(r0, n):
        return jnp.concatenate([sc_sc[g, pl.ds(pl.multiple_of(r0, n), n), :] for g in range(n_lg)], axis=1)

    def sc_store(r0, n, val):
        for g in range(n_lg):
            sc_sc[g, pl.ds(pl.multiple_of(r0, n), n), :] = val[:, g * LANE:(g + 1) * LANE]

    qpos_c = i * tq + lax.broadcasted_iota(jnp.int32, (ck, tq), 1)
    krow_c = lax.broadcasted_iota(jnp.int32, (ck, tq), 0)

    def score_block(cb):
        for j in range(sub):
            r0 = cb * tq + j * ck
            kc = rows(ki_ref, r0, ck)
            acc = jnp.zeros((ck, tq), F32)
            for h in range(IDX_HEADS):
                rel = _dot(kc, qit_ref[h * LANE:(h + 1) * LANE, :])
                acc = acc + wit_ref[h:h + 1, :] * jnp.maximum(rel, 0.0)
            sc_store(r0, ck, jnp.where(r0 + krow_c <= qpos_c, acc, -jnp.inf))

    def score_pair(p, _):
        score_block(2 * p)
        score_block(2 * p + 1)
        return 0

    lax.fori_loop(0, nblk // 2, score_pair, 0)

    @pl.when(nblk % 2 == 1)
    def _():
        score_block(nblk - 1)

    krow = lax.broadcasted_iota(jnp.int32, (tq, tq), 0)

    def count(pred):
        def body(cb, a):
            return a + _count_rows(pred(sc_load(cb * tq, tq), cb * tq + krow))
        part = lax.fori_loop(0, nblk, body, jnp.zeros((8, tq), jnp.int32))
        return jnp.sum(part, axis=0, keepdims=True)

    def flip(word):
        return jnp.where(word < 0, word ^ jnp.int32(0x7FFFFFFF), word)

    def key_to_score(key):
        return pltpu.bitcast(flip(key), F32)

    thr_sc[...] = jnp.full((1, tq), jnp.finfo(F32).min, F32)

    def bisect(nb):
        assert n_lg == 2
        la, lb = slice(0, LANE), slice(LANE, 2 * LANE)

        def part_count(group, cf):
            n_rows = (nb - 1) * tq + (group + 1) * LANE
            part = jnp.zeros((8, LANE), jnp.int32)
            for r0 in range(0, n_rows, tq):
                r1 = min(r0 + tq, n_rows)
                part = part + _count_rows(sc_sc[group, r0:r1, :] >= cf[0:1, :])
            return part

        def accept(part, cand, ans, n_ans):
            n = jnp.sum(part, axis=0, keepdims=True)
            ok = n >= topk
            return jnp.where(ok, cand, ans), jnp.where(ok, n, n_ans)

        def cand_of(t, ans):
            return jnp.where(t == 0, jnp.zeros_like(ans), ans + jnp.left_shift(jnp.int32(1), 31 - t))

        def body(t, st):
            ans_a, n_a, ans_b, n_b, cand_b, part_b = st
            ans_b, n_b = accept(part_b, cand_b, ans_b, n_b)
            cand_b = cand_of(t, ans_b)
            cand_a = cand_of(t, ans_a)
            ans_a, n_a = accept(part_count(0, key_to_score(cand_a)), cand_a, ans_a, n_a)
            return ans_a, n_a, ans_b, n_b, cand_b, part_count(1, key_to_score(cand_b))

        start = jnp.full((8, LANE), INT_MIN, jnp.int32)
        zero = jnp.zeros((8, LANE), jnp.int32)
        first = body(jnp.int32(0), (start, zero, start, zero, start, zero))
        ans_a, n_a, ans_b, n_b, cand_b, part_b = lax.fori_loop(1, 32, body, first)
        ans_b, n_b = accept(part_b, cand_b, ans_b, n_b)
        thr_sc[:, la] = key_to_score(ans_a)[0:1, :]
        thr_sc[:, lb] = key_to_score(ans_b)[0:1, :]
        nge_sc[:, la] = n_a[0:1, :]
        nge_sc[:, lb] = n_b[0:1, :]

    def dispatch(tiles):
        if len(tiles) == 1:
            bisect(tiles[0] + 1)
        else:
            mid = len(tiles) // 2
            lax.cond(i < tiles[mid], lambda: dispatch(tiles[:mid]), lambda: dispatch(tiles[mid:]))

    pl.when(i * tq >= topk)(lambda: dispatch(list(range(topk // tq, sc_sc.shape[1] // tq))))

    @pl.when(i * tq >= topk)
    def _():
        thr = thr_sc[...]

        @pl.when(jnp.max(nge_sc[...]) > topk)
        def _():
            need = topk - count(lambda s, kidx: s > thr)

            idx_bits = (sc_sc.shape[1] - 1).bit_length()

            def jbit(t, j):
                cand = j + jnp.left_shift(jnp.int32(1), idx_bits - 1 - t)
                n = count(lambda s, kidx: (s == thr) & (kidx < cand))
                return jnp.where(n < need, cand, j)

            last = lax.fori_loop(0, idx_bits, jbit, jnp.zeros((1, tq), jnp.int32))

            def drop(cb, _):
                s = sc_load(cb * tq, tq)
                dropped = (s == thr) & (cb * tq + krow > last)
                sc_store(cb * tq, tq, jnp.where(dropped, -jnp.inf, s))
                return 0

            lax.fori_loop(0, nblk, drop, 0)

    thr = thr_sc[...]
    nh = DSA_HEADS
    qall = jnp.concatenate([qbt_ref[h * LANE:(h + 1) * LANE, :] for h in range(nh)], axis=1)
    bufs = (att0_sc, att1_sc)

    def logits(cb, buf):
        bias = jnp.where(sc_load(cb * tq, tq) >= thr, 0.0, NEG)
        for j in range(sub):
            s = _dot(rows(kb_ref, cb * tq + j * ck, ck), qall)
            b = bias[j * ck:(j + 1) * ck, :]
            for h in range(nh):
                buf[h, j * ck:(j + 1) * ck, :] = s[:, h * tq:(h + 1) * tq] + b

    def attend(cb, buf):
        vt = _with_ones_rows(jnp.concatenate([vbt_ref[cb * sub + j] for j in range(sub)], axis=1))
        for h in range(nh):
            s = buf[h]
            m_old = m_sc[h]
            m_new = jnp.maximum(m_old, jnp.max(s, axis=0, keepdims=True))
            a = jnp.exp2(m_old - m_new)
            p = jnp.exp2(s - m_new)
            m_sc[h] = m_new
            acc_sc[h] = a * acc_sc[h] + _dot(vt, p.astype(BF16))

    def step(k, parity):
        logits(k + 1, bufs[1 - parity])
        attend(k, bufs[parity])

    m_sc[...] = jnp.full(m_sc.shape, NEG, F32)
    acc_sc[...] = jnp.zeros_like(acc_sc)
    logits(0, bufs[0])

    def pair(p, _):
        step(2 * p, 0)
        step(2 * p + 1, 1)
        return 0

    lax.fori_loop(0, i // 2, pair, 0)

    @pl.when(i % 2 == 1)
    def _():
        step(i - 1, 0)
        attend(i, bufs[1])

    @pl.when(i % 2 == 0)
    def _():
        attend(i, bufs[0])

    for h in range(nh):
        acc = acc_sc[h]
        o_ref[h * DSA_DIM:(h + 1) * DSA_DIM, :] = (acc[:DSA_DIM] / acc[DSA_DIM:DSA_DIM + 1]).astype(BF16)


def _dsa(qit, wit, ki, qbt, kb, vbt, topk):
    B, _, S = qit.shape
    tq = TQ_DSA
    assert topk % tq == 0 or topk >= S, "query tiles must not straddle the top-k boundary"
    return pl.pallas_call(
        functools.partial(_dsa_kernel, topk),
        out_shape=jax.ShapeDtypeStruct((B, DSA_HEADS * DSA_DIM, S), BF16),
        grid=(B, S // tq),
        in_specs=[
            pl.BlockSpec((None, IDX_HEADS * LANE, tq), lambda b, i: (b, 0, i)),
            pl.BlockSpec((None, IDX_HEADS, tq), lambda b, i: (b, 0, i)),
            pl.BlockSpec((None, S, LANE), lambda b, i: (b, 0, 0)),
            pl.BlockSpec((None, DSA_HEADS * LANE, tq), lambda b, i: (b, 0, i)),
            pl.BlockSpec((None, S, LANE), lambda b, i: (b, 0, 0)),
            pl.BlockSpec((None, S // LANE, DSA_DIM, LANE), lambda b, i: (b, 0, 0, 0)),
        ],
        out_specs=pl.BlockSpec((None, DSA_HEADS * DSA_DIM, tq), lambda b, i: (b, 0, i)),
        scratch_shapes=[
            pltpu.VMEM((tq // LANE, S, LANE), F32),
            pltpu.VMEM((DSA_HEADS, tq, tq), F32),
            pltpu.VMEM((DSA_HEADS, tq, tq), F32),
            pltpu.VMEM((1, tq), F32),
            pltpu.VMEM((1, tq), jnp.int32),
            pltpu.VMEM((DSA_HEADS, 1, tq), F32),
            pltpu.VMEM((DSA_HEADS, DSA_DIM + BF16_ROWS, tq), F32),
        ],
        compiler_params=pltpu.CompilerParams(
            dimension_semantics=("parallel", "arbitrary"), vmem_limit_bytes=VMEM_LIMIT),
        name="dsa",
    )(qit, wit, ki, qbt, kb, vbt)


def _ret_kernel(rqt_ref, rk_ref, rvt_ref, rgt_ref, dt_ref, xi_ref, zeta_ref, cdec_ref, gret_ref, o_ref):
    c = dt_ref.shape[1]
    n_chunks = o_ref.shape[1] // c
    state = [jnp.zeros((RET_DV, LANE), F32) for _ in range(RET_HEADS)]
    for n in range(n_chunks):
        cols = slice(n * c, (n + 1) * c)
        for h in range(RET_HEADS):
            rt = state[h]
            qt = rqt_ref[h * LANE:(h + 1) * LANE, cols]
            k = rk_ref[cols, (h // 2) * LANE:(h // 2 + 1) * LANE]
            vt = rvt_ref[h * RET_DV:(h + 1) * RET_DV, cols]
            inner_t = _dot(k, qt) * dt_ref[h]
            qx = (qt.astype(F32) * xi_ref[h:h + 1, :]).astype(BF16)
            out_t = _dot(vt, inner_t.astype(BF16)) + _dot(rt.astype(BF16), qx)
            vz = (vt.astype(F32) * zeta_ref[h:h + 1, :]).astype(BF16)
            state[h] = rt * cdec_ref[h] + _dot(vz, k)

            mu = jnp.mean(out_t, axis=0, keepdims=True)
            xc = out_t - mu
            var = jnp.mean(xc * xc, axis=0, keepdims=True)
            y = xc * lax.rsqrt(var + EPS) * gret_ref[h * RET_DV:(h + 1) * RET_DV, :]
            g = rgt_ref[h * RET_DV:(h + 1) * RET_DV, cols]
            o_ref[h * RET_DV:(h + 1) * RET_DV, cols] = (g * jax.nn.sigmoid(g) * y).astype(BF16)


def _retention(rqt, rk, rvt, rgt, dt, xi, zeta, cdec, gret):
    B, _, S = rqt.shape

    def full1(shape):
        nd = len(shape)
        return pl.BlockSpec(shape, lambda b: (0,) * nd)

    def per_batch(rows, cols):
        return pl.BlockSpec((None, rows, cols), lambda b: (b, 0, 0))

    return pl.pallas_call(
        _ret_kernel,
        out_shape=jax.ShapeDtypeStruct((B, RET_HEADS * RET_DV, S), BF16),
        grid=(B,),
        in_specs=[
            per_batch(RET_HEADS * LANE, S), per_batch(S, RET_HEADS * RET_DK),
            per_batch(RET_HEADS * RET_DV, S), per_batch(RET_HEADS * RET_DV, S),
            full1(dt.shape), full1(xi.shape), full1(zeta.shape), full1(cdec.shape), full1(gret.shape),
        ],
        out_specs=per_batch(RET_HEADS * RET_DV, S),
        compiler_params=pltpu.CompilerParams(
            dimension_semantics=("parallel",), vmem_limit_bytes=VMEM_LIMIT),
        name="retention",
    )(rqt, rk, rvt, rgt, dt, xi, zeta, cdec, gret)


def _mlp_kernel(final, x_ref, oat_ref, obt_ref, oct_ref, wout_ref, gmlp_ref, w1_ref, w2_ref, gfin_ref,
                y_ref):
    def tr(ref):
        return ref[...].astype(F32).T.astype(BF16)

    na = MLA_HEADS * MLA_V
    nb = DSA_HEADS * DSA_DIM
    attn = (_dot(tr(oat_ref), wout_ref[0:na, :])
            + _dot(tr(obt_ref), wout_ref[na:na + nb, :])
            + _dot(tr(oct_ref), wout_ref[na + nb:, :]))
    x1 = x_ref[...] + attn
    hf = _rms(x1, gmlp_ref[...]).astype(BF16)
    acc = x1
    for f in range(D_FF // FF_CHUNK):
        u = jnp.maximum(_dot(hf, w1_ref[:, f * FF_CHUNK:(f + 1) * FF_CHUNK]), 0.0)
        acc = acc + _dot((u * u).astype(BF16), w2_ref[f * FF_CHUNK:(f + 1) * FF_CHUNK, :])
    if final:
        acc = _rms(acc, gfin_ref[...])
    y_ref[...] = acc


def _mlp(x, oat, obt, oct, wout, gmlp, w1, w2, gfin, final):
    B, S, D = x.shape
    tm = TM_MLP

    def const(shape):
        nd = len(shape)
        return pl.BlockSpec(shape, lambda b, i: (0,) * nd, pipeline_mode=pl.Buffered(1))

    return pl.pallas_call(
        functools.partial(_mlp_kernel, final),
        out_shape=jax.ShapeDtypeStruct((B, S, D), F32),
        grid=(B, S // tm),
        in_specs=[
            pl.BlockSpec((None, tm, D), lambda b, i: (b, i, 0)),
            pl.BlockSpec((None, oat.shape[1], tm), lambda b, i: (b, 0, i)),
            pl.BlockSpec((None, obt.shape[1], tm), lambda b, i: (b, 0, i)),
            pl.BlockSpec((None, oct.shape[1], tm), lambda b, i: (b, 0, i)),
            const(wout.shape), const(gmlp.shape), const(w1.shape), const(w2.shape), const(gfin.shape),
        ],
        out_specs=pl.BlockSpec((None, tm, D), lambda b, i: (b, i, 0)),
        compiler_params=pltpu.CompilerParams(
            dimension_semantics=("parallel", "parallel"), vmem_limit_bytes=VMEM_LIMIT),
        name="mlp",
    )(x, oat, obt, oct, wout, gmlp, w1, w2, gfin)


def _rope_tables(seq, dim):
    pos = jnp.arange(seq, dtype=F32)
    inv = ROPE_THETA ** (-jnp.arange(0, dim, 2, dtype=F32) / dim)
    ang = pos[:, None] * inv[None, :]
    return jnp.cos(ang), jnp.sin(ang)


def _lane_table(c, s, lo, seq):
    half = c.shape[1]
    ct = jnp.zeros((seq, LANE), F32).at[:, lo:lo + 2 * half].set(jnp.concatenate([c, c], axis=1))
    st = jnp.zeros((seq, LANE), F32).at[:, lo:lo + 2 * half].set(jnp.concatenate([-s, s], axis=1))
    return ct, st


def _pad_cols(w, width):
    return jnp.pad(w, ((0, 0), (0, width - w.shape[1])))


def _prep_layer(w_in, w_uq, w_ukv):
    z = lambda n: jnp.zeros((D_MODEL, n), F32)
    col = lambda a, b: w_in[:, a:b]
    nat = [col(O_CQ, O_CKV), col(O_CKV, O_KR),
           jnp.concatenate([col(O_KI, O_WI), z(32), col(O_KR, O_AQ), z(32)], axis=1),
           _pad_cols(col(O_AK, O_AV), LANE),
           col(O_RK, O_RV)]
    wnat = jnp.concatenate(nat, axis=1).astype(BF16)
    wtr = jnp.concatenate([col(O_AQ, O_AK), col(O_QI, O_KI), col(O_AV, O_QI), col(O_RQ, O_RK),
                           col(O_RV, O_RG), col(O_RG, O_END), col(O_WI, O_RQ)], axis=1).T.astype(BF16)
    wuq = w_uq.T.astype(BF16)
    wkv = w_ukv.reshape(MLA_KV_RANK, MLA_HEADS, MLA_NOPE + MLA_V)
    wuk = jnp.pad(wkv[:, :, :MLA_NOPE], ((0, 0), (0, 0), (0, LANE - MLA_NOPE)))
    wuk = wuk.reshape(MLA_KV_RANK, MLA_HEADS * LANE).astype(BF16)
    wuv = wkv[:, :, MLA_NOPE:].reshape(MLA_KV_RANK, MLA_HEADS * MLA_V).T.astype(BF16)
    return wnat, wtr, wuq, wuk, wuv


def kernel(x, g_mix, w_in, g_q, w_uq, g_kv, w_ukv, g_ret, w_out, g_mlp, w_ff1, w_ff2, g_final):
    B, S, _ = x.shape
    topk = min(DSA_TOPK_MAX, S // 4)

    cos32, sin32 = _rope_tables(S, 32)
    cos64, sin64 = _rope_tables(S, 64)
    tabt = jnp.concatenate([cos64.T, sin64.T, cos32.T, sin32.T], axis=0)
    both = lambda t1, t2: (t1[0] + t2[0], t1[1] + t2[1])
    tabn = jnp.stack(both(_lane_table(cos32, sin32, 0, S), _lane_table(cos32, sin32, 64, S))
                     + _lane_table(cos64, sin64, 0, S)
                     + both(_lane_table(cos64, sin64, 0, S), _lane_table(cos64, sin64, 64, S)))

    log_gamma = jnp.log1p(-jnp.exp2(-5.0 - jnp.arange(RET_HEADS, dtype=F32)))
    c = C_RET
    pos = jnp.arange(c, dtype=F32)
    rel = pos[None, :] - pos[:, None]
    dt = jnp.where(rel[None] >= 0, jnp.exp(jnp.maximum(rel, 0.0)[None] * log_gamma[:, None, None]), 0.0)
    xi = jnp.exp((pos + 1.0)[None, :] * log_gamma[:, None])
    zeta = jnp.exp((c - 1.0 - pos)[None, :] * log_gamma[:, None])
    cdec = jnp.broadcast_to(jnp.exp(c * log_gamma)[:, None, None], (RET_HEADS, RET_DV, LANE))

    for l in range(DEPTH):
        wnat, wtr, wuq, wuk, wuv = _prep_layer(w_in[l], w_uq[l], w_ukv[l])
        (qat, ka, vat, qit, wit, ki, qbt, kb, vbt, rqt, rk, rvt, rgt) = _proj(
            x, g_mix[l][None, :], g_q[l][None, :], g_kv[l][None, :], wnat, wtr, wuq, wuk, wuv, tabt, tabn)
        oat = _mla(qat, ka, vat)
        obt = _dsa(qit, wit, ki, qbt, kb, vbt, topk)
        gret = jnp.broadcast_to(g_ret[l][:, None], (RET_HEADS * RET_DV, c))
        oct = _retention(rqt, rk, rvt, rgt, dt, xi, zeta, cdec, gret)
        x = _mlp(x, oat, obt, oct, w_out[l].astype(BF16), g_mlp[l][None, :],
                 w_ff1[l].astype(BF16), w_ff2[l].astype(BF16), g_final[None, :], l == DEPTH - 1)
    return x
```

```python
import functools

import numpy as np
import jax
import jax.numpy as jnp
from jax import lax
from jax.experimental import pallas as pl
from jax.experimental.pallas import tpu as pltpu

D_MODEL = 1024
DEPTH = 2
MLA_HEADS = 8
MLA_NOPE = 64
MLA_ROPE = 32
MLA_V = 64
MLA_Q_RANK = 768
MLA_KV_RANK = 256
DSA_HEADS = 4
DSA_DIM = 64
IDX_HEADS = 8
IDX_DIM = 32
DSA_TOPK_MAX = 256
RET_HEADS = 4
RET_DK = 64
RET_DV = 64
D_FF = 4 * D_MODEL
ROPE_THETA = 10000.0
EPS = 1e-6

MLA_COLS = (MLA_Q_RANK, MLA_KV_RANK, MLA_ROPE)
DSA_COLS = (DSA_HEADS * DSA_DIM, DSA_DIM, DSA_DIM, IDX_HEADS * IDX_DIM, IDX_DIM, IDX_HEADS)
RET_COLS = (RET_HEADS * RET_DK, RET_HEADS * RET_DK, RET_HEADS * RET_DV, RET_HEADS * RET_DV)
SPLITS = MLA_COLS + DSA_COLS + RET_COLS
_OFF = np.concatenate([[0], np.cumsum(SPLITS)]).tolist()
(O_CQ, O_CKV, O_KR, O_AQ, O_AK, O_AV, O_QI, O_KI, O_WI, O_RQ, O_RK, O_RV, O_RG, O_END) = _OFF

LANE = 128
BF16_ROWS = 16
VMEM_LIMIT = 56 * 1024 * 1024

BF16 = jnp.bfloat16
F32 = jnp.float32
NEG = -1e30
LOG2E = float(np.log2(np.e))
INT_MIN = np.int32(-2**31)

N_CQ = 0
N_CKV = N_CQ + MLA_Q_RANK
N_KX = N_CKV + MLA_KV_RANK
N_AK = N_KX + LANE
N_RK = N_AK + LANE
N_NAT = N_RK + RET_HEADS * RET_DK
T_AQ = 0
T_QI = T_AQ + DSA_HEADS * DSA_DIM
T_AV = T_QI + IDX_HEADS * IDX_DIM
T_RQ = T_AV + DSA_DIM
T_RV = T_RQ + RET_HEADS * RET_DK
T_RG = T_RV + RET_HEADS * RET_DV
T_WI = T_RG + RET_HEADS * RET_DV
N_TR = T_WI + IDX_HEADS

TM_PROJ = 1024
TQ_MLA = 256
G_MLA = 8
TQ_DSA = 256
CK_DSA = 128
C_RET = 256
TM_MLP = 512
FF_CHUNK = 1024

_NT = (((1,), (1,)), ((), ()))


def _rms(x, g):
    return x * lax.rsqrt(jnp.mean(x * x, axis=-1, keepdims=True) + EPS) * g


def _dot(a, b):
    return jnp.dot(a, b, preferred_element_type=F32)


def _dot_nt(a, b):
    return lax.dot_general(a, b, _NT, preferred_element_type=F32)


def _rope_rows(x1, x2, c, s):
    return x1 * c - x2 * s, x1 * s + x2 * c


def _rope_lanes(x, c, s_signed, half, first_los):
    lane = lax.broadcasted_iota(jnp.int32, x.shape, 1)
    first = (lane >= first_los[0]) & (lane < first_los[0] + half)
    for lo in first_los[1:]:
        first = first | ((lane >= lo) & (lane < lo + half))
    rot = jnp.where(first, pltpu.roll(x, LANE - half, 1), pltpu.roll(x, half, 1))
    return x * c + rot * s_signed


def _proj_kernel(x_ref, gmix_ref, gq_ref, gkv_ref, wnat_ref, wtr_ref, wuq_ref, wuk_ref, wuv_ref,
                 tabt_ref, tabn_ref,
                 qat_ref, ka_ref, vat_ref, qit_ref, wit_ref, ki_ref, qbt_ref, kb_ref, vbt_ref,
                 rqt_ref, rk_ref, rvt_ref, rgt_ref):
    tm = x_ref.shape[0]
    hb = _rms(x_ref[...], gmix_ref[...]).astype(BF16)

    c64t = tabt_ref[0:32, :]
    s64t = tabt_ref[32:64, :]
    c32t = tabt_ref[64:80, :]
    s32t = tabt_ref[80:96, :]

    cq = _dot(hb, wnat_ref[:, N_CQ:N_CQ + MLA_Q_RANK])
    nq = _rms(cq, gq_ref[...]).astype(BF16)
    qt = _dot_nt(wuq_ref[...], nq)
    scale_a = (MLA_NOPE + MLA_ROPE) ** -0.5 * LOG2E
    for h in range(MLA_HEADS):
        r0 = h * LANE
        s0 = h * (MLA_NOPE + MLA_ROPE)
        qat_ref[r0:r0 + 64, :] = (qt[s0:s0 + 64, :] * scale_a).astype(BF16)
        o1, o2 = _rope_rows(qt[s0 + 64:s0 + 80, :], qt[s0 + 80:s0 + 96, :], c32t, s32t)
        qat_ref[r0 + 64:r0 + 80, :] = (o1 * scale_a).astype(BF16)
        qat_ref[r0 + 80:r0 + 96, :] = (o2 * scale_a).astype(BF16)
        qat_ref[r0 + 96:r0 + 128, :] = jnp.zeros((32, tm), BF16)

    ckv = _dot(hb, wnat_ref[:, N_CKV:N_CKV + MLA_KV_RANK])
    nkv = _rms(ckv, gkv_ref[...]).astype(BF16)
    kn = _dot(nkv, wuk_ref[...])
    keys = _dot(hb, wnat_ref[:, N_KX:N_NAT])
    slab = lambda off: keys[:, off - N_KX:off - N_KX + LANE]
    kx = _rope_lanes(slab(N_KX), tabn_ref[0], tabn_ref[1], 16, (0, 64))
    ki_ref[...] = kx.astype(BF16)
    krp = jnp.where(lax.broadcasted_iota(jnp.int32, kx.shape, 1) >= 64, kx, 0.0)
    for h in range(MLA_HEADS):
        ka_ref[h] = (kn[:, h * LANE:(h + 1) * LANE] + krp).astype(BF16)
    vt = _dot_nt(wuv_ref[...], nkv).astype(BF16)
    for j in range(tm // LANE):
        vat_ref[j] = vt[:, j * LANE:(j + 1) * LANE]

    kb_ref[...] = _rope_lanes(slab(N_AK), tabn_ref[2], tabn_ref[3], 32, (0,)).astype(BF16)

    for p in range(RET_HEADS // 2):
        rkp = _rope_lanes(slab(N_RK + p * LANE), tabn_ref[4], tabn_ref[5], 32, (0, 64)) * (RET_DK ** -0.5)
        rk_ref[:, p * LANE:(p + 1) * LANE] = rkp.astype(BF16)

    aqt = _dot_nt(wtr_ref[T_AQ:T_AQ + 256, :], hb)
    scale_b = DSA_DIM ** -0.5 * LOG2E
    for h in range(DSA_HEADS):
        o1, o2 = _rope_rows(aqt[h * 64:h * 64 + 32, :], aqt[h * 64 + 32:h * 64 + 64, :], c64t, s64t)
        qbt_ref[h * LANE:h * LANE + 32, :] = (o1 * scale_b).astype(BF16)
        qbt_ref[h * LANE + 32:h * LANE + 64, :] = (o2 * scale_b).astype(BF16)
        qbt_ref[h * LANE + 64:(h + 1) * LANE, :] = jnp.zeros((64, tm), BF16)

    qit = _dot_nt(wtr_ref[T_QI:T_QI + 256, :], hb)
    for h in range(IDX_HEADS):
        o1, o2 = _rope_rows(qit[h * 32:h * 32 + 16, :], qit[h * 32 + 16:h * 32 + 32, :], c32t, s32t)
        qit_ref[h * LANE:h * LANE + 16, :] = o1.astype(BF16)
        qit_ref[h * LANE + 16:h * LANE + 32, :] = o2.astype(BF16)
        qit_ref[h * LANE + 32:(h + 1) * LANE, :] = jnp.zeros((96, tm), BF16)

    idx_scale = (IDX_HEADS ** -0.5) * (IDX_DIM ** -0.5)
    wit_ref[...] = _dot_nt(wtr_ref[T_WI:T_WI + IDX_HEADS, :], hb) * idx_scale

    avt = _dot_nt(wtr_ref[T_AV:T_AV + DSA_DIM, :], hb).astype(BF16)
    for j in range(tm // LANE):
        vbt_ref[j] = avt[:, j * LANE:(j + 1) * LANE]

    rqt = _dot_nt(wtr_ref[T_RQ:T_RQ + 256, :], hb)
    for h in range(RET_HEADS):
        o1, o2 = _rope_rows(rqt[h * 64:h * 64 + 32, :], rqt[h * 64 + 32:h * 64 + 64, :], c64t, s64t)
        r0 = h * LANE + (h % 2) * RET_DK
        z0 = h * LANE + (1 - h % 2) * RET_DK
        rqt_ref[r0:r0 + 32, :] = o1.astype(BF16)
        rqt_ref[r0 + 32:r0 + 64, :] = o2.astype(BF16)
        rqt_ref[z0:z0 + 64, :] = jnp.zeros((64, tm), BF16)

    rvt_ref[...] = _dot_nt(wtr_ref[T_RV:T_RV + 256, :], hb).astype(BF16)
    rgt_ref[...] = _dot_nt(wtr_ref[T_RG:T_RG + 256, :], hb)


def _full(shape):
    nd = len(shape)
    return pl.BlockSpec(shape, lambda b, i: (0,) * nd)


def _proj(x, gmix, gq, gkv, wnat, wtr, wuq, wuk, wuv, tabt, tabn):
    B, S, D = x.shape
    tm = TM_PROJ
    nch = tm // LANE
    outs = [
        (jax.ShapeDtypeStruct((B, MLA_HEADS * LANE, S), BF16), pl.BlockSpec((None, MLA_HEADS * LANE, tm), lambda b, i: (b, 0, i))),
        (jax.ShapeDtypeStruct((B, MLA_HEADS, S, LANE), BF16), pl.BlockSpec((None, MLA_HEADS, tm, LANE), lambda b, i: (b, 0, i, 0))),
        (jax.ShapeDtypeStruct((B, S // LANE, MLA_HEADS * MLA_V, LANE), BF16), pl.BlockSpec((None, nch, MLA_HEADS * MLA_V, LANE), lambda b, i: (b, i, 0, 0))),
        (jax.ShapeDtypeStruct((B, IDX_HEADS * LANE, S), BF16), pl.BlockSpec((None, IDX_HEADS * LANE, tm), lambda b, i: (b, 0, i))),
        (jax.ShapeDtypeStruct((B, IDX_HEADS, S), F32), pl.BlockSpec((None, IDX_HEADS, tm), lambda b, i: (b, 0, i))),
        (jax.ShapeDtypeStruct((B, S, LANE), BF16), pl.BlockSpec((None, tm, LANE), lambda b, i: (b, i, 0))),
        (jax.ShapeDtypeStruct((B, DSA_HEADS * LANE, S), BF16), pl.BlockSpec((None, DSA_HEADS * LANE, tm), lambda b, i: (b, 0, i))),
        (jax.ShapeDtypeStruct((B, S, LANE), BF16), pl.BlockSpec((None, tm, LANE), lambda b, i: (b, i, 0))),
        (jax.ShapeDtypeStruct((B, S // LANE, DSA_DIM, LANE), BF16), pl.BlockSpec((None, nch, DSA_DIM, LANE), lambda b, i: (b, i, 0, 0))),
        (jax.ShapeDtypeStruct((B, RET_HEADS * LANE, S), BF16), pl.BlockSpec((None, RET_HEADS * LANE, tm), lambda b, i: (b, 0, i))),
        (jax.ShapeDtypeStruct((B, S, RET_HEADS * RET_DK), BF16), pl.BlockSpec((None, tm, RET_HEADS * RET_DK), lambda b, i: (b, i, 0))),
        (jax.ShapeDtypeStruct((B, RET_HEADS * RET_DV, S), BF16), pl.BlockSpec((None, RET_HEADS * RET_DV, tm), lambda b, i: (b, 0, i))),
        (jax.ShapeDtypeStruct((B, RET_HEADS * RET_DV, S), F32), pl.BlockSpec((None, RET_HEADS * RET_DV, tm), lambda b, i: (b, 0, i))),
    ]
    in_specs = [
        pl.BlockSpec((None, tm, D), lambda b, i: (b, i, 0)),
        _full(gmix.shape), _full(gq.shape), _full(gkv.shape),
        _full(wnat.shape), _full(wtr.shape), _full(wuq.shape), _full(wuk.shape), _full(wuv.shape),
        pl.BlockSpec((tabt.shape[0], tm), lambda b, i: (0, i)),
        pl.BlockSpec((tabn.shape[0], tm, LANE), lambda b, i: (0, i, 0)),
    ]
    return pl.pallas_call(
        _proj_kernel,
        out_shape=[o[0] for o in outs],
        grid=(B, S // tm),
        in_specs=in_specs,
        out_specs=[o[1] for o in outs],
        compiler_params=pltpu.CompilerParams(
            dimension_semantics=("parallel", "parallel"), vmem_limit_bytes=VMEM_LIMIT),
        name="proj",
    )(x, gmix, gq, gkv, wnat, wtr, wuq, wuk, wuv, tabt, tabn)


def _with_ones_rows(vt):
    return jnp.concatenate([vt, jnp.ones((BF16_ROWS, vt.shape[1]), BF16)], axis=0)


def _mla_kernel(q_ref, k_ref, v_ref, o_ref, s0_sc, s1_sc, mt_sc, m_sc, acc_sc):
    g_heads = acc_sc.shape[0]
    tq = q_ref.shape[1]
    tk = tq
    i = pl.program_id(2)
    nv = tk // LANE
    kidx = lax.broadcasted_iota(jnp.int32, (tk, tq), 0)
    qidx = lax.broadcasted_iota(jnp.int32, (tk, tq), 1)
    bufs = (s0_sc, s1_sc)

    def scores(kt, parity, causal=None):
        for g in range(g_heads):
            s = _dot(k_ref[g, pl.ds(pl.multiple_of(kt * tk, tk), tk), :],
                     q_ref[g * LANE:(g + 1) * LANE, :])
            if causal is not None:
                s = jnp.where(causal, s, NEG)
            bufs[parity][g] = s
            mt_sc[parity, g] = jnp.max(s, axis=0, keepdims=True)

    def softmax(kt, parity):
        for g in range(g_heads):
            m_old = m_sc[g]
            m_new = jnp.maximum(m_old, mt_sc[parity, g])
            a = jnp.exp2(m_old - m_new)
            p = jnp.exp2(bufs[parity][g] - m_new)
            vt = jnp.concatenate([v_ref[kt * nv + j, g * MLA_V:(g + 1) * MLA_V, :] for j in range(nv)],
                                 axis=1)
            m_sc[g] = m_new
            acc_sc[g] = a * acc_sc[g] + _dot(_with_ones_rows(vt), p.astype(BF16))

    m_sc[...] = jnp.full(m_sc.shape, NEG, F32)
    acc_sc[...] = jnp.zeros_like(acc_sc)

    def step(k, parity, diagonal=False):
        scores(k + 1, 1 - parity, (kidx <= qidx) if diagonal else None)
        softmax(k, parity)

    scores(0, 0, kidx <= i * tq + qidx)

    def pair(p, _):
        step(2 * p, 0)
        step(2 * p + 1, 1)
        return 0

    lax.fori_loop(0, (i - 1) // 2, pair, 0)

    @pl.when(i % 2 == 0)
    def _():
        @pl.when(i >= 2)
        def _():
            step(i - 2, 0)
            step(i - 1, 1, diagonal=True)
        softmax(i, 0)

    @pl.when(i % 2 == 1)
    def _():
        step(i - 1, 0, diagonal=True)
        softmax(i, 1)

    for g in range(g_heads):
        acc = acc_sc[g]
        o_ref[g * MLA_V:(g + 1) * MLA_V, :] = (acc[:MLA_V] / acc[MLA_V:MLA_V + 1]).astype(BF16)


def _mla(qat, ka, vat):
    B, _, S = qat.shape
    tq = TQ_MLA
    g = G_MLA
    return pl.pallas_call(
        _mla_kernel,
        out_shape=jax.ShapeDtypeStruct((B, MLA_HEADS * MLA_V, S), BF16),
        grid=(B, MLA_HEADS // g, S // tq),
        in_specs=[
            pl.BlockSpec((None, g * LANE, tq), lambda b, h, i: (b, h, i)),
            pl.BlockSpec((None, g, S, LANE), lambda b, h, i: (b, h, 0, 0)),
            pl.BlockSpec((None, S // LANE, g * MLA_V, LANE), lambda b, h, i: (b, 0, h, 0)),
        ],
        out_specs=pl.BlockSpec((None, g * MLA_V, tq), lambda b, h, i: (b, h, i)),
        scratch_shapes=[pltpu.VMEM((g, tq, tq), F32), pltpu.VMEM((g, tq, tq), F32),
                        pltpu.VMEM((2, g, 1, tq), F32), pltpu.VMEM((g, 1, tq), F32),
                        pltpu.VMEM((g, MLA_V + BF16_ROWS, tq), F32)],
        compiler_params=pltpu.CompilerParams(
            dimension_semantics=("parallel", "parallel", "arbitrary"), vmem_limit_bytes=VMEM_LIMIT),
        name="mla",
    )(qat, ka, vat)


def _count_rows(mask):
    r, t = mask.shape
    return jnp.sum(mask.astype(jnp.int32).reshape(r // 8, 8, t), axis=0)


def _dsa_kernel(topk, qit_ref, wit_ref, ki_ref, qbt_ref, kb_ref, vbt_ref, o_ref,
                sc_sc, att0_sc, att1_sc, thr_sc, nge_sc, m_sc, acc_sc):
    tq = o_ref.shape[1]
    ck = CK_DSA
    sub = tq // ck
    i = pl.program_id(1)
    nblk = i + 1

    def rows(ref, r0, n):
        if isinstance(r0, int):
            return ref[r0:r0 + n, :]
        return ref[pl.ds(pl.multiple_of(r0, n), n), :]

    n_lg = tq // LANE

    def sc_load(r0, n):
        return jnp.concatenate([sc_sc[g, pl.ds(pl.multiple_of(r0, n), n), :] for g in range(n_lg)], axis=1)

    def sc_store(r0, n, val):
        for g in range(n_lg):
            sc_sc[g, pl.ds(pl.multiple_of(r0, n), n), :] = val[:, g * LANE:(g + 1) * LANE]

    qpos_c = i * tq + lax.broadcasted_iota(jnp.int32, (ck, tq), 1)
    krow_c = lax.broadcasted_iota(jnp.int32, (ck, tq), 0)

    def score_block(cb):
        for j in range(sub):
            r0 = cb * tq + j * ck
            kc = rows(ki_ref, r0, ck)
            acc = jnp.zeros((ck, tq), F32)
            for h in range(IDX_HEADS):
                rel = _dot(kc, qit_ref[h * LANE:(h + 1) * LANE, :])
                acc = acc + wit_ref[h:h + 1, :] * jnp.maximum(rel, 0.0)
            sc_store(r0, ck, jnp.where(r0 + krow_c <= qpos_c, acc, -jnp.inf))

    def score_pair(p, _):
        score_block(2 * p)
        score_block(2 * p + 1)
        return 0

    lax.fori_loop(0, nblk // 2, score_pair, 0)

    @pl.when(nblk % 2 == 1)
    def _():
        score_block(nblk - 1)

    krow = lax.broadcasted_iota(jnp.int32, (tq, tq), 0)

    def count(pred):
        def body(cb, a):
            return a + _count_rows(pred(sc_load(cb * tq, tq), cb * tq + krow))
        part = lax.fori_loop(0, nblk, body, jnp.zeros((8, tq), jnp.int32))
        return jnp.sum(part, axis=0, keepdims=True)

    def flip(word):
        return jnp.where(word < 0, word ^ jnp.int32(0x7FFFFFFF), word)

    def key_to_score(key):
        return pltpu.bitcast(flip(key), F32)

    thr_sc[...] = jnp.full((1, tq), jnp.finfo(F32).min, F32)

    def bisect(nb):
        assert n_lg == 2
        la, lb = slice(0, LANE), slice(LANE, 2 * LANE)

        def part_count(group, cf):
            n_rows = (nb - 1) * tq + (group + 1) * LANE
            part = jnp.zeros((8, LANE), jnp.int32)
            for r0 in range(0, n_rows, tq):
                r1 = min(r0 + tq, n_rows)
                part = part + _count_rows(sc_sc[group, r0:r1, :] >= cf[0:1, :])
            return part

        def accept(part, cand, ans, n_ans):
            n = jnp.sum(part, axis=0, keepdims=True)
            ok = n >= topk
            return jnp.where(ok, cand, ans), jnp.where(ok, n, n_ans)

        def cand_of(t, ans):
            return jnp.where(t == 0, jnp.zeros_like(ans), ans + jnp.left_shift(jnp.int32(1), 31 - t))

        def body(t, st):
            ans_a, n_a, ans_b, n_b, cand_b, part_b = st
            ans_b, n_b = accept(part_b, cand_b, ans_b, n_b)
            cand_b = cand_of(t, ans_b)
            cand_a = cand_of(t, ans_a)
            ans_a, n_a = accept(part_count(0, key_to_score(cand_a)), cand_a, ans_a, n_a)
            return ans_a, n_a, ans_b, n_b, cand_b, part_count(1, key_to_score(cand_b))

        start = jnp.full((8, LANE), INT_MIN, jnp.int32)
        zero = jnp.zeros((8, LANE), jnp.int32)
        first = body(jnp.int32(0), (start, zero, start, zero, start, zero))
        ans_a, n_a, ans_b, n_b, cand_b, part_b = lax.fori_loop(1, 32, body, first)
        ans_b, n_b = accept(part_b, cand_b, ans_b, n_b)
        thr_sc[:, la] = key_to_score(ans_a)[0:1, :]
        thr_sc[:, lb] = key_to_score(ans_b)[0:1, :]
        nge_sc[:, la] = n_a[0:1, :]
        nge_sc[:, lb] = n_b[0:1, :]

    for v in range(topk // tq, sc_sc.shape[1] // tq):
        pl.when(i == v)(functools.partial(bisect, v + 1))

    @pl.when(i * tq >= topk)
    def _():
        thr = thr_sc[...]

        @pl.when(jnp.max(nge_sc[...]) > topk)
        def _():
            need = topk - count(lambda s, kidx: s > thr)

            idx_bits = (sc_sc.shape[1] - 1).bit_length()

            def jbit(t, j):
                cand = j + jnp.left_shift(jnp.int32(1), idx_bits - 1 - t)
                n = count(lambda s, kidx: (s == thr) & (kidx < cand))
                return jnp.where(n < need, cand, j)

            last = lax.fori_loop(0, idx_bits, jbit, jnp.zeros((1, tq), jnp.int32))

            def drop(cb, _):
                s = sc_load(cb * tq, tq)
                dropped = (s == thr) & (cb * tq + krow > last)
                sc_store(cb * tq, tq, jnp.where(dropped, -jnp.inf, s))
                return 0

            lax.fori_loop(0, nblk, drop, 0)

    thr = thr_sc[...]
    nh = DSA_HEADS
    qall = jnp.concatenate([qbt_ref[h * LANE:(h + 1) * LANE, :] for h in range(nh)], axis=1)
    bufs = (att0_sc, att1_sc)

    def logits(cb, buf):
        bias = jnp.where(sc_load(cb * tq, tq) >= thr, 0.0, NEG)
        for j in range(sub):
            s = _dot(rows(kb_ref, cb * tq + j * ck, ck), qall)
            b = bias[j * ck:(j + 1) * ck, :]
            for h in range(nh):
                buf[h, j * ck:(j + 1) * ck, :] = s[:, h * tq:(h + 1) * tq] + b

    def attend(cb, buf):
        vt = _with_ones_rows(jnp.concatenate([vbt_ref[cb * sub + j] for j in range(sub)], axis=1))
        for h in range(nh):
            s = buf[h]
            m_old = m_sc[h]
            m_new = jnp.maximum(m_old, jnp.max(s, axis=0, keepdims=True))
            a = jnp.exp2(m_old - m_new)
            p = jnp.exp2(s - m_new)
            m_sc[h] = m_new
            acc_sc[h] = a * acc_sc[h] + _dot(vt, p.astype(BF16))

    def step(k, parity):
        logits(k + 1, bufs[1 - parity])
        attend(k, bufs[parity])

    m_sc[...] = jnp.full(m_sc.shape, NEG, F32)
    acc_sc[...] = jnp.zeros_like(acc_sc)
    logits(0, bufs[0])

    def pair(p, _):
        step(2 * p, 0)
        step(2 * p + 1, 1)
        return 0

    lax.fori_loop(0, i // 2, pair, 0)

    @pl.when(i % 2 == 1)
    def _():
        step(i - 1, 0)
        attend(i, bufs[1])

    @pl.when(i % 2 == 0)
    def _():
        attend(i, bufs[0])

    for h in range(nh):
        acc = acc_sc[h]
        o_ref[h * DSA_DIM:(h + 1) * DSA_DIM, :] = (acc[:DSA_DIM] / acc[DSA_DIM:DSA_DIM + 1]).astype(BF16)


def _dsa(qit, wit, ki, qbt, kb, vbt, topk):
    B, _, S = qit.shape
    tq = TQ_DSA
    assert topk % tq == 0 or topk >= S, "query tiles must not straddle the top-k boundary"
    return pl.pallas_call(
        functools.partial(_dsa_kernel, topk),
        out_shape=jax.ShapeDtypeStruct((B, DSA_HEADS * DSA_DIM, S), BF16),
        grid=(B, S // tq),
        in_specs=[
            pl.BlockSpec((None, IDX_HEADS * LANE, tq), lambda b, i: (b, 0, i)),
            pl.BlockSpec((None, IDX_HEADS, tq), lambda b, i: (b, 0, i)),
            pl.BlockSpec((None, S, LANE), lambda b, i: (b, 0, 0)),
            pl.BlockSpec((None, DSA_HEADS * LANE, tq), lambda b, i: (b, 0, i)),
            pl.BlockSpec((None, S, LANE), lambda b, i: (b, 0, 0)),
            pl.BlockSpec((None, S // LANE, DSA_DIM, LANE), lambda b, i: (b, 0, 0, 0)),
        ],
        out_specs=pl.BlockSpec((None, DSA_HEADS * DSA_DIM, tq), lambda b, i: (b, 0, i)),
        scratch_shapes=[
            pltpu.VMEM((tq // LANE, S, LANE), F32),
            pltpu.VMEM((DSA_HEADS, tq, tq), F32),
            pltpu.VMEM((DSA_HEADS, tq, tq), F32),
            pltpu.VMEM((1, tq), F32),
            pltpu.VMEM((1, tq), jnp.int32),
            pltpu.VMEM((DSA_HEADS, 1, tq), F32),
            pltpu.VMEM((DSA_HEADS, DSA_DIM + BF16_ROWS, tq), F32),
        ],
        compiler_params=pltpu.CompilerParams(
            dimension_semantics=("parallel", "arbitrary"), vmem_limit_bytes=VMEM_LIMIT),
        name="dsa",
    )(qit, wit, ki, qbt, kb, vbt)


def _ret_kernel(rqt_ref, rk_ref, rvt_ref, rgt_ref, dt_ref, xi_ref, zeta_ref, cdec_ref, gret_ref, o_ref):
    c = dt_ref.shape[1]
    n_chunks = o_ref.shape[1] // c
    state = [jnp.zeros((RET_DV, LANE), F32) for _ in range(RET_HEADS)]
    for n in range(n_chunks):
        cols = slice(n * c, (n + 1) * c)
        for h in range(RET_HEADS):
            rt = state[h]
            qt = rqt_ref[h * LANE:(h + 1) * LANE, cols]
            k = rk_ref[cols, (h // 2) * LANE:(h // 2 + 1) * LANE]
            vt = rvt_ref[h * RET_DV:(h + 1) * RET_DV, cols]
            inner_t = _dot(k, qt) * dt_ref[h]
            qx = (qt.astype(F32) * xi_ref[h:h + 1, :]).astype(BF16)
            out_t = _dot(vt, inner_t.astype(BF16)) + _dot(rt.astype(BF16), qx)
            vz = (vt.astype(F32) * zeta_ref[h:h + 1, :]).astype(BF16)
            state[h] = rt * cdec_ref[h] + _dot(vz, k)

            mu = jnp.mean(out_t, axis=0, keepdims=True)
            xc = out_t - mu
            var = jnp.mean(xc * xc, axis=0, keepdims=True)
            y = xc * lax.rsqrt(var + EPS) * gret_ref[h * RET_DV:(h + 1) * RET_DV, :]
            g = rgt_ref[h * RET_DV:(h + 1) * RET_DV, cols]
            o_ref[h * RET_DV:(h + 1) * RET_DV, cols] = (g * jax.nn.sigmoid(g) * y).astype(BF16)


def _retention(rqt, rk, rvt, rgt, dt, xi, zeta, cdec, gret):
    B, _, S = rqt.shape

    def full1(shape):
        nd = len(shape)
        return pl.BlockSpec(shape, lambda b: (0,) * nd)

    def per_batch(rows, cols):
        return pl.BlockSpec((None, rows, cols), lambda b: (b, 0, 0))

    return pl.pallas_call(
        _ret_kernel,
        out_shape=jax.ShapeDtypeStruct((B, RET_HEADS * RET_DV, S), BF16),
        grid=(B,),
        in_specs=[
            per_batch(RET_HEADS * LANE, S), per_batch(S, RET_HEADS * RET_DK),
            per_batch(RET_HEADS * RET_DV, S), per_batch(RET_HEADS * RET_DV, S),
            full1(dt.shape), full1(xi.shape), full1(zeta.shape), full1(cdec.shape), full1(gret.shape),
        ],
        out_specs=per_batch(RET_HEADS * RET_DV, S),
        compiler_params=pltpu.CompilerParams(
            dimension_semantics=("parallel",), vmem_limit_bytes=VMEM_LIMIT),
        name="retention",
    )(rqt, rk, rvt, rgt, dt, xi, zeta, cdec, gret)


def _mlp_kernel(final, x_ref, oat_ref, obt_ref, oct_ref, wout_ref, gmlp_ref, w1_ref, w2_ref, gfin_ref,
                y_ref):
    def tr(ref):
        return ref[...].astype(F32).T.astype(BF16)

    na = MLA_HEADS * MLA_V
    nb = DSA_HEADS * DSA_DIM
    attn = (_dot(tr(oat_ref), wout_ref[0:na, :])
            + _dot(tr(obt_ref), wout_ref[na:na + nb, :])
            + _dot(tr(oct_ref), wout_ref[na + nb:, :]))
    x1 = x_ref[...] + attn
    hf = _rms(x1, gmlp_ref[...]).astype(BF16)
    acc = x1
    for f in range(D_FF // FF_CHUNK):
        u = jnp.maximum(_dot(hf, w1_ref[:, f * FF_CHUNK:(f + 1) * FF_CHUNK]), 0.0)
        acc = acc + _dot((u * u).astype(BF16), w2_ref[f * FF_CHUNK:(f + 1) * FF_CHUNK, :])
    if final:
        acc = _rms(acc, gfin_ref[...])
    y_ref[...] = acc


def _mlp(x, oat, obt, oct, wout, gmlp, w1, w2, gfin, final):
    B, S, D = x.shape
    tm = TM_MLP

    def const(shape):
        nd = len(shape)
        return pl.BlockSpec(shape, lambda b, i: (0,) * nd, pipeline_mode=pl.Buffered(1))

    return pl.pallas_call(
        functools.partial(_mlp_kernel, final),
        out_shape=jax.ShapeDtypeStruct((B, S, D), F32),
        grid=(B, S // tm),
        in_specs=[
            pl.BlockSpec((None, tm, D), lambda b, i: (b, i, 0)),
            pl.BlockSpec((None, oat.shape[1], tm), lambda b, i: (b, 0, i)),
            pl.BlockSpec((None, obt.shape[1], tm), lambda b, i: (b, 0, i)),
            pl.BlockSpec((None, oct.shape[1], tm), lambda b, i: (b, 0, i)),
            const(wout.shape), const(gmlp.shape), const(w1.shape), const(w2.shape), const(gfin.shape),
        ],
        out_specs=pl.BlockSpec((None, tm, D), lambda b, i: (b, i, 0)),
        compiler_params=pltpu.CompilerParams(
            dimension_semantics=("parallel", "parallel"), vmem_limit_bytes=VMEM_LIMIT),
        name="mlp",
    )(x, oat, obt, oct, wout, gmlp, w1, w2, gfin)


def _rope_tables(seq, dim):
    pos = jnp.arange(seq, dtype=F32)
    inv = ROPE_THETA ** (-jnp.arange(0, dim, 2, dtype=F32) / dim)
    ang = pos[:, None] * inv[None, :]
    return jnp.cos(ang), jnp.sin(ang)


def _lane_table(c, s, lo, seq):
    half = c.shape[1]
    ct = jnp.zeros((seq, LANE), F32).at[:, lo:lo + 2 * half].set(jnp.concatenate([c, c], axis=1))
    st = jnp.zeros((seq, LANE), F32).at[:, lo:lo + 2 * half].set(jnp.concatenate([-s, s], axis=1))
    return ct, st


def _pad_cols(w, width):
    return jnp.pad(w, ((0, 0), (0, width - w.shape[1])))


def _prep_layer(w_in, w_uq, w_ukv):
    z = lambda n: jnp.zeros((D_MODEL, n), F32)
    col = lambda a, b: w_in[:, a:b]
    nat = [col(O_CQ, O_CKV), col(O_CKV, O_KR),
           jnp.concatenate([col(O_KI, O_WI), z(32), col(O_KR, O_AQ), z(32)], axis=1),
           _pad_cols(col(O_AK, O_AV), LANE),
           col(O_RK, O_RV)]
    wnat = jnp.concatenate(nat, axis=1).astype(BF16)
    wtr = jnp.concatenate([col(O_AQ, O_AK), col(O_QI, O_KI), col(O_AV, O_QI), col(O_RQ, O_RK),
                           col(O_RV, O_RG), col(O_RG, O_END), col(O_WI, O_RQ)], axis=1).T.astype(BF16)
    wuq = w_uq.T.astype(BF16)
    wkv = w_ukv.reshape(MLA_KV_RANK, MLA_HEADS, MLA_NOPE + MLA_V)
    wuk = jnp.pad(wkv[:, :, :MLA_NOPE], ((0, 0), (0, 0), (0, LANE - MLA_NOPE)))
    wuk = wuk.reshape(MLA_KV_RANK, MLA_HEADS * LANE).astype(BF16)
    wuv = wkv[:, :, MLA_NOPE:].reshape(MLA_KV_RANK, MLA_HEADS * MLA_V).T.astype(BF16)
    return wnat, wtr, wuq, wuk, wuv


def kernel(x, g_mix, w_in, g_q, w_uq, g_kv, w_ukv, g_ret, w_out, g_mlp, w_ff1, w_ff2, g_final):
    B, S, _ = x.shape
    topk = min(DSA_TOPK_MAX, S // 4)

    cos32, sin32 = _rope_tables(S, 32)
    cos64, sin64 = _rope_tables(S, 64)
    tabt = jnp.concatenate([cos64.T, sin64.T, cos32.T, sin32.T], axis=0)
    both = lambda t1, t2: (t1[0] + t2[0], t1[1] + t2[1])
    tabn = jnp.stack(both(_lane_table(cos32, sin32, 0, S), _lane_table(cos32, sin32, 64, S))
                     + _lane_table(cos64, sin64, 0, S)
                     + both(_lane_table(cos64, sin64, 0, S), _lane_table(cos64, sin64, 64, S)))

    log_gamma = jnp.log1p(-jnp.exp2(-5.0 - jnp.arange(RET_HEADS, dtype=F32)))
    c = C_RET
    pos = jnp.arange(c, dtype=F32)
    rel = pos[None, :] - pos[:, None]
    dt = jnp.where(rel[None] >= 0, jnp.exp(jnp.maximum(rel, 0.0)[None] * log_gamma[:, None, None]), 0.0)
    xi = jnp.exp((pos + 1.0)[None, :] * log_gamma[:, None])
    zeta = jnp.exp((c - 1.0 - pos)[None, :] * log_gamma[:, None])
    cdec = jnp.broadcast_to(jnp.exp(c * log_gamma)[:, None, None], (RET_HEADS, RET_DV, LANE))

    for l in range(DEPTH):
        wnat, wtr, wuq, wuk, wuv = _prep_layer(w_in[l], w_uq[l], w_ukv[l])
        (qat, ka, vat, qit, wit, ki, qbt, kb, vbt, rqt, rk, rvt, rgt) = _proj(
            x, g_mix[l][None, :], g_q[l][None, :], g_kv[l][None, :], wnat, wtr, wuq, wuk, wuv, tabt, tabn)
        oat = _mla(qat, ka, vat)
        obt = _dsa(qit, wit, ki, qbt, kb, vbt, topk)
        gret = jnp.broadcast_to(g_ret[l][:, None], (RET_HEADS * RET_DV, c))
        oct = _retention(rqt, rk, rvt, rgt, dt, xi, zeta, cdec, gret)
        x = _mlp(x, oat, obt, oct, w_out[l].astype(BF16), g_mlp[l][None, :],
                 w_ff1[l].astype(BF16), w_ff2[l].astype(BF16), g_final[None, :], l == DEPTH - 1)
    return x
```
